```python
import jax, jax.numpy as jnp
from jax import lax
import numpy as np

D_MODEL = 2048
BATCH = 8
SEQ = 8192
DEPTH = 4

CHUNK = 64
D_MIX = D_MODEL
ATTN_HEAD_DIM = 128
ATTN_HEADS = (D_MIX // 2) // ATTN_HEAD_DIM
D_ATTN = ATTN_HEADS * ATTN_HEAD_DIM
D_CONF = D_MIX // 4
D_SCONV = D_MIX - D_ATTN - D_CONF
CONF_WIDTH = 31
SCONV_WIDTH = 3
FFN_WIDTH = 3
D_FF = 5632
Q_BLOCK = 128
N_ADA = 6
RMS_EPS = 1e-6
LN_EPS = 1e-5
IN_COLS = 3 * D_ATTN + ATTN_HEADS + 2 * D_CONF + 3 * D_SCONV

kernel_name = "hybrid_fox_conformer_shortconv_trunk"


def rms_norm(x, g):
    xf = x.astype(jnp.float32)
    y = xf * lax.rsqrt(jnp.mean(xf * xf, axis=-1, keepdims=True) + RMS_EPS)
    return (y * g.astype(jnp.float32)).astype(x.dtype)


def layer_norm(x, g, b):
    xf = x.astype(jnp.float32)
    mu = jnp.mean(xf, axis=-1, keepdims=True)
    xc = xf - mu
    y = xc * lax.rsqrt(jnp.mean(xc * xc, axis=-1, keepdims=True) + LN_EPS)
    return (y * g.astype(jnp.float32) + b.astype(jnp.float32)).astype(x.dtype)


def modulate(h, shift, scale):
    return h * (1 + scale[:, None, :]) + shift[:, None, :]


def causal_dwconv(x, w, b=None):
    K, C = w.shape
    xp = jnp.pad(x, ((0, 0), (K - 1, 0), (0, 0)))
    y = lax.conv_general_dilated(xp, w[:, None, :], window_strides=(1,), padding='VALID',
                                 dimension_numbers=('NWC', 'WIO', 'NWC'),
                                 feature_group_count=C)
    if b is not None:
        y = y + b
    return y


def forgetting_attention(q, k, v, log_f):
    B, S, H, Dh = q.shape
    nb = S // Q_BLOCK
    F = jnp.cumsum(log_f, axis=1)
    Fk = jnp.transpose(F, (0, 2, 1))[:, :, None, :]
    qb = jnp.transpose(q.reshape(B, nb, Q_BLOCK, H, Dh), (1, 0, 2, 3, 4))
    Fqb = jnp.transpose(F.reshape(B, nb, Q_BLOCK, H), (1, 0, 3, 2))
    qpos = jnp.arange(S).reshape(nb, Q_BLOCK)
    kpos = jnp.arange(S)
    scale = Dh ** -0.5

    def one_block(args):
        q_blk, Fq_blk, qp = args
        s = jnp.einsum('bqhd,bkhd->bhqk', q_blk, k,
                       preferred_element_type=jnp.float32) * scale
        s = s + Fq_blk[..., None] - Fk
        mask = kpos[None, :] <= qp[:, None]
        s = jnp.where(mask, s, -jnp.inf)
        p = jax.nn.softmax(s, axis=-1)
        return jnp.einsum('bhqk,bkhd->bqhd', p.astype(v.dtype), v)

    out = lax.map(one_block, (qb, Fqb, qpos))
    return jnp.transpose(out, (1, 0, 2, 3, 4)).reshape(B, S, H * Dh)


def _fwd_setup_inputs(seed: int = 0) -> dict:
    key = jax.random.key(seed)
    ks = jax.random.split(key, 24)
    L, D = DEPTH, D_MODEL
    nrm = jax.random.normal
    x = nrm(ks[0], (BATCH, SEQ, D), jnp.float32)
    c = nrm(ks[1], (BATCH, D), jnp.float32)
    ada_w = nrm(ks[2], (L, D, N_ADA * D), jnp.float32) * (0.5 * D ** -0.5)
    ada_b = nrm(ks[3], (L, N_ADA * D), jnp.float32) * 0.02
    mix_norm_g = 1.0 + 0.05 * nrm(ks[4], (L, D), jnp.float32)
    w_in = nrm(ks[5], (L, D, IN_COLS), jnp.float32) * D ** -0.5
    b_forget = jax.random.uniform(ks[6], (L, ATTN_HEADS), jnp.float32, 1.0, 4.0)
    conf_dw_w = nrm(ks[7], (L, CONF_WIDTH, D_CONF), jnp.float32) * CONF_WIDTH ** -0.5
    conf_dw_b = nrm(ks[8], (L, D_CONF), jnp.float32) * 0.02
    conf_ln_g = 1.0 + 0.05 * nrm(ks[9], (L, D_CONF), jnp.float32)
    conf_ln_b = nrm(ks[10], (L, D_CONF), jnp.float32) * 0.02
    sc_dw_w = nrm(ks[11], (L, SCONV_WIDTH, D_SCONV), jnp.float32) * SCONV_WIDTH ** -0.5
    w_out = nrm(ks[12], (L, D_MIX, D), jnp.float32) * D_MIX ** -0.5
    ffn_norm_g = 1.0 + 0.05 * nrm(ks[13], (L, D), jnp.float32)
    w_up = nrm(ks[14], (L, D, 2 * D_FF), jnp.float32) * D ** -0.5
    ffn_dw_w = nrm(ks[15], (L, FFN_WIDTH, 2 * D_FF), jnp.float32) * FFN_WIDTH ** -0.5
    ffn_dw_b = nrm(ks[16], (L, 2 * D_FF), jnp.float32) * 0.02
    w_down = nrm(ks[17], (L, D_FF, D), jnp.float32) * D_FF ** -0.5
    final_norm_g = 1.0 + 0.05 * nrm(ks[18], (D,), jnp.float32)
    return {"x": x, "c": c, "ada_w": ada_w, "ada_b": ada_b, "mix_norm_g": mix_norm_g,
            "w_in": w_in, "b_forget": b_forget, "conf_dw_w": conf_dw_w, "conf_dw_b": conf_dw_b,
            "conf_ln_g": conf_ln_g, "conf_ln_b": conf_ln_b, "sc_dw_w": sc_dw_w, "w_out": w_out,
            "ffn_norm_g": ffn_norm_g, "w_up": w_up, "ffn_dw_w": ffn_dw_w, "ffn_dw_b": ffn_dw_b,
            "w_down": w_down, "final_norm_g": final_norm_g}


def _fwd_reference(x, c, ada_w, ada_b, mix_norm_g, w_in, b_forget, conf_dw_w, conf_dw_b,
              conf_ln_g, conf_ln_b, sc_dw_w, w_out, ffn_norm_g, w_up, ffn_dw_w, ffn_dw_b,
              w_down, final_norm_g):
    B, S, _ = x.shape
    split_at = list(np.cumsum([D_ATTN, D_ATTN, D_ATTN, ATTN_HEADS,
                               D_CONF, D_CONF, D_SCONV, D_SCONV]))
    c_act = jax.nn.silu(c)
    for l in range(DEPTH):
        ada = c_act @ ada_w[l] + ada_b[l]
        sh_m, sc_m, g_m, sh_f, sc_f, g_f = jnp.split(ada, N_ADA, axis=-1)

        h = modulate(rms_norm(x, mix_norm_g[l]), sh_m, sc_m)
        proj = h @ w_in[l]
        q, k, v, f_logit, cv, cg, s_x, s_b, s_c = jnp.split(proj, split_at, axis=-1)

        log_f = jax.nn.log_sigmoid((f_logit + b_forget[l]).astype(jnp.float32))
        attn = forgetting_attention(q.reshape(B, S, ATTN_HEADS, ATTN_HEAD_DIM),
                                    k.reshape(B, S, ATTN_HEADS, ATTN_HEAD_DIM),
                                    v.reshape(B, S, ATTN_HEADS, ATTN_HEAD_DIM), log_f)

        conf = cv * jax.nn.sigmoid(cg)
        conf = causal_dwconv(conf, conf_dw_w[l], conf_dw_b[l])
        conf = jax.nn.silu(layer_norm(conf, conf_ln_g[l], conf_ln_b[l]))

        sconv = s_b * causal_dwconv(s_c * s_x, sc_dw_w[l])

        mixed = jnp.concatenate([attn, conf, sconv], axis=-1) @ w_out[l]
        x = x + g_m[:, None, :] * mixed

        h = modulate(rms_norm(x, ffn_norm_g[l]), sh_f, sc_f)
        u = causal_dwconv(h @ w_up[l], ffn_dw_w[l], ffn_dw_b[l])
        gate, val = jnp.split(u, 2, axis=-1)
        x = x + g_f[:, None, :] * ((jax.nn.silu(gate) * val) @ w_down[l])

    return rms_norm(x, final_norm_g)


import jax as _jax
import jax.numpy as _jnp

TWIN_FORMAT = 'train_step'
FWD_PARAMS = ['x', 'c', 'ada_w', 'ada_b', 'mix_norm_g', 'w_in', 'b_forget', 'conf_dw_w', 'conf_dw_b', 'conf_ln_g', 'conf_ln_b', 'sc_dw_w', 'w_out', 'ffn_norm_g', 'w_up', 'ffn_dw_w', 'ffn_dw_b', 'w_down', 'final_norm_g']
TWIN_WEIGHTS = ['ada_w', 'ada_b', 'mix_norm_g', 'w_in', 'b_forget', 'conf_dw_w', 'conf_dw_b', 'conf_ln_g', 'conf_ln_b', 'sc_dw_w', 'w_out', 'ffn_norm_g', 'w_up', 'ffn_dw_w', 'ffn_dw_b', 'w_down', 'final_norm_g']
TWIN_DIFF_INPUT = 'x'
TWIN_INPUTS = ['x', 'c', 'ada_w', 'ada_b', 'mix_norm_g', 'w_in', 'b_forget', 'conf_dw_w', 'conf_dw_b', 'conf_ln_g', 'conf_ln_b', 'sc_dw_w', 'w_out', 'ffn_norm_g', 'w_up', 'ffn_dw_w', 'ffn_dw_b', 'w_down', 'final_norm_g', 'loss_target', 'm_ada_w', 'm_ada_b', 'm_mix_norm_g', 'm_w_in', 'm_b_forget', 'm_conf_dw_w', 'm_conf_dw_b', 'm_conf_ln_g', 'm_conf_ln_b', 'm_sc_dw_w', 'm_w_out', 'm_ffn_norm_g', 'm_w_up', 'm_ffn_dw_w', 'm_ffn_dw_b', 'm_w_down', 'm_final_norm_g', 'v_ada_w', 'v_ada_b', 'v_mix_norm_g', 'v_w_in', 'v_b_forget', 'v_conf_dw_w', 'v_conf_dw_b', 'v_conf_ln_g', 'v_conf_ln_b', 'v_sc_dw_w', 'v_w_out', 'v_ffn_norm_g', 'v_w_up', 'v_ffn_dw_w', 'v_ffn_dw_b', 'v_w_down', 'v_final_norm_g']
TWIN_OUTPUTS = ['loss', 'grad_x', 'grad_ada_w', 'grad_ada_b', 'grad_mix_norm_g', 'grad_w_in', 'grad_b_forget', 'grad_conf_dw_w', 'grad_conf_dw_b', 'grad_conf_ln_g', 'grad_conf_ln_b', 'grad_sc_dw_w', 'grad_w_out', 'grad_ffn_norm_g', 'grad_w_up', 'grad_ffn_dw_w', 'grad_ffn_dw_b', 'grad_w_down', 'grad_final_norm_g', 'delta_ada_w', 'delta_ada_b', 'delta_mix_norm_g', 'delta_w_in', 'delta_b_forget', 'delta_conf_dw_w', 'delta_conf_dw_b', 'delta_conf_ln_g', 'delta_conf_ln_b', 'delta_sc_dw_w', 'delta_w_out', 'delta_ffn_norm_g', 'delta_w_up', 'delta_ffn_dw_w', 'delta_ffn_dw_b', 'delta_w_down', 'delta_final_norm_g', 'new_m_ada_w', 'new_m_ada_b', 'new_m_mix_norm_g', 'new_m_w_in', 'new_m_b_forget', 'new_m_conf_dw_w', 'new_m_conf_dw_b', 'new_m_conf_ln_g', 'new_m_conf_ln_b', 'new_m_sc_dw_w', 'new_m_w_out', 'new_m_ffn_norm_g', 'new_m_w_up', 'new_m_ffn_dw_w', 'new_m_ffn_dw_b', 'new_m_w_down', 'new_m_final_norm_g', 'new_v_ada_w', 'new_v_ada_b', 'new_v_mix_norm_g', 'new_v_w_in', 'new_v_b_forget', 'new_v_conf_dw_w', 'new_v_conf_dw_b', 'new_v_conf_ln_g', 'new_v_conf_ln_b', 'new_v_sc_dw_w', 'new_v_w_out', 'new_v_ffn_norm_g', 'new_v_w_up', 'new_v_ffn_dw_w', 'new_v_ffn_dw_b', 'new_v_w_down', 'new_v_final_norm_g']
TWIN_LEAF_KINDS = {'loss': 'loss', 'grad_x': 'grad_x', 'grad_ada_w': 'grad_w', 'grad_ada_b': 'grad_w', 'grad_mix_norm_g': 'grad_w', 'grad_w_in': 'grad_w', 'grad_b_forget': 'grad_w', 'grad_conf_dw_w': 'grad_w', 'grad_conf_dw_b': 'grad_w', 'grad_conf_ln_g': 'grad_w', 'grad_conf_ln_b': 'grad_w', 'grad_sc_dw_w': 'grad_w', 'grad_w_out': 'grad_w', 'grad_ffn_norm_g': 'grad_w', 'grad_w_up': 'grad_w', 'grad_ffn_dw_w': 'grad_w', 'grad_ffn_dw_b': 'grad_w', 'grad_w_down': 'grad_w', 'grad_final_norm_g': 'grad_w', 'delta_ada_w': 'delta_w', 'delta_ada_b': 'delta_w', 'delta_mix_norm_g': 'delta_w', 'delta_w_in': 'delta_w', 'delta_b_forget': 'delta_w', 'delta_conf_dw_w': 'delta_w', 'delta_conf_dw_b': 'delta_w', 'delta_conf_ln_g': 'delta_w', 'delta_conf_ln_b': 'delta_w', 'delta_sc_dw_w': 'delta_w', 'delta_w_out': 'delta_w', 'delta_ffn_norm_g': 'delta_w', 'delta_w_up': 'delta_w', 'delta_ffn_dw_w': 'delta_w', 'delta_ffn_dw_b': 'delta_w', 'delta_w_down': 'delta_w', 'delta_final_norm_g': 'delta_w', 'new_m_ada_w': 'new_m', 'new_m_ada_b': 'new_m', 'new_m_mix_norm_g': 'new_m', 'new_m_w_in': 'new_m', 'new_m_b_forget': 'new_m', 'new_m_conf_dw_w': 'new_m', 'new_m_conf_dw_b': 'new_m', 'new_m_conf_ln_g': 'new_m', 'new_m_conf_ln_b': 'new_m', 'new_m_sc_dw_w': 'new_m', 'new_m_w_out': 'new_m', 'new_m_ffn_norm_g': 'new_m', 'new_m_w_up': 'new_m', 'new_m_ffn_dw_w': 'new_m', 'new_m_ffn_dw_b': 'new_m', 'new_m_w_down': 'new_m', 'new_m_final_norm_g': 'new_m', 'new_v_ada_w': 'new_v', 'new_v_ada_b': 'new_v', 'new_v_mix_norm_g': 'new_v', 'new_v_w_in': 'new_v', 'new_v_b_forget': 'new_v', 'new_v_conf_dw_w': 'new_v', 'new_v_conf_dw_b': 'new_v', 'new_v_conf_ln_g': 'new_v', 'new_v_conf_ln_b': 'new_v', 'new_v_sc_dw_w': 'new_v', 'new_v_w_out': 'new_v', 'new_v_ffn_norm_g': 'new_v', 'new_v_w_up': 'new_v', 'new_v_ffn_dw_w': 'new_v', 'new_v_ffn_dw_b': 'new_v', 'new_v_w_down': 'new_v', 'new_v_final_norm_g': 'new_v'}


def _forward(args):
    return _fwd_reference(*[args[k] for k in FWD_PARAMS])


def _output_shape():
    def fwd():
        inp = _fwd_setup_inputs(0)
        return _fwd_reference(*[inp[k] for k in FWD_PARAMS])
    out = _jax.eval_shape(fwd)
    return out.shape, out.dtype

N_MICROBATCH = 1
ADAM_LR = 0.001
ADAM_B1 = 0.9
ADAM_B2 = 0.999
ADAM_EPS = 1e-08
ADAM_WD = 0.01
ADAM_STEP = 10
PER_EXAMPLE_BATCH_AXIS = {'x': 0, 'c': 0, 'loss_target': 0}
SHARED_INPUTS = []
_WEIGHT_DTYPES = {'ada_w': _jnp.float32, 'ada_b': _jnp.float32, 'mix_norm_g': _jnp.float32, 'w_in': _jnp.float32, 'b_forget': _jnp.float32, 'conf_dw_w': _jnp.float32, 'conf_dw_b': _jnp.float32, 'conf_ln_g': _jnp.float32, 'conf_ln_b': _jnp.float32, 'sc_dw_w': _jnp.float32, 'w_out': _jnp.float32, 'ffn_norm_g': _jnp.float32, 'w_up': _jnp.float32, 'ffn_dw_w': _jnp.float32, 'ffn_dw_b': _jnp.float32, 'w_down': _jnp.float32, 'final_norm_g': _jnp.float32}
MOMENT_SCALE = {'ada_w': 4.823745e-02, 'ada_b': 8.954616e-02, 'mix_norm_g': 4.502088e-02, 'w_in': 2.885756e-02, 'b_forget': 9.739342e-02, 'conf_dw_w': 2.431674e-02, 'conf_dw_b': 4.778212e-02, 'conf_ln_g': 3.024812e-02, 'conf_ln_b': 2.755593e-02, 'sc_dw_w': 4.983751e-02, 'w_out': 3.080044e-02, 'ffn_norm_g': 3.706653e-02, 'w_up': 1.648512e-02, 'ffn_dw_w': 1.651688e-02, 'ffn_dw_b': 1.536523e-02, 'w_down': 2.706594e-02, 'final_norm_g': 3.209598e+01}


def _to_microbatches(a, axis):
    t = _jnp.moveaxis(a, axis, 0)
    t = t.reshape((N_MICROBATCH, t.shape[0] // N_MICROBATCH) + t.shape[1:])
    return _jnp.moveaxis(t, 1, axis + 1)


def setup_inputs(seed: int = 0) -> dict:
    inp = _fwd_setup_inputs(seed)
    key = _jax.random.fold_in(_jax.random.key(seed), 7919)
    shape, _ = _output_shape()
    out = dict(inp)
    out["loss_target"] = _jax.random.normal(_jax.random.fold_in(key, 0), shape, _jnp.float32)
    for i, name in enumerate(TWIN_WEIGHTS):
        w = inp[name].astype(_jnp.float32)
        if MOMENT_SCALE is None:
            s = _jnp.sqrt(_jnp.mean(_jnp.square(w)) + 1e-30)
        else:
            s = MOMENT_SCALE[name]
        km, kv = _jax.random.split(_jax.random.fold_in(key, i + 1))
        out[name] = w
        out["m_" + name] = s * _jax.random.normal(km, w.shape, _jnp.float32)
        out["v_" + name] = (s * s) * _jax.random.uniform(kv, w.shape, _jnp.float32, 0.5, 1.5)
    if N_MICROBATCH > 1:
        for name, axis in PER_EXAMPLE_BATCH_AXIS.items():
            out[name] = _to_microbatches(out[name], axis)
    return {'x': out['x'], 'c': out['c'], 'ada_w': out['ada_w'], 'ada_b': out['ada_b'], 'mix_norm_g': out['mix_norm_g'], 'w_in': out['w_in'], 'b_forget': out['b_forget'], 'conf_dw_w': out['conf_dw_w'], 'conf_dw_b': out['conf_dw_b'], 'conf_ln_g': out['conf_ln_g'], 'conf_ln_b': out['conf_ln_b'], 'sc_dw_w': out['sc_dw_w'], 'w_out': out['w_out'], 'ffn_norm_g': out['ffn_norm_g'], 'w_up': out['w_up'], 'ffn_dw_w': out['ffn_dw_w'], 'ffn_dw_b': out['ffn_dw_b'], 'w_down': out['w_down'], 'final_norm_g': out['final_norm_g'], 'loss_target': out['loss_target'], 'm_ada_w': out['m_ada_w'], 'm_ada_b': out['m_ada_b'], 'm_mix_norm_g': out['m_mix_norm_g'], 'm_w_in': out['m_w_in'], 'm_b_forget': out['m_b_forget'], 'm_conf_dw_w': out['m_conf_dw_w'], 'm_conf_dw_b': out['m_conf_dw_b'], 'm_conf_ln_g': out['m_conf_ln_g'], 'm_conf_ln_b': out['m_conf_ln_b'], 'm_sc_dw_w': out['m_sc_dw_w'], 'm_w_out': out['m_w_out'], 'm_ffn_norm_g': out['m_ffn_norm_g'], 'm_w_up': out['m_w_up'], 'm_ffn_dw_w': out['m_ffn_dw_w'], 'm_ffn_dw_b': out['m_ffn_dw_b'], 'm_w_down': out['m_w_down'], 'm_final_norm_g': out['m_final_norm_g'], 'v_ada_w': out['v_ada_w'], 'v_ada_b': out['v_ada_b'], 'v_mix_norm_g': out['v_mix_norm_g'], 'v_w_in': out['v_w_in'], 'v_b_forget': out['v_b_forget'], 'v_conf_dw_w': out['v_conf_dw_w'], 'v_conf_dw_b': out['v_conf_dw_b'], 'v_conf_ln_g': out['v_conf_ln_g'], 'v_conf_ln_b': out['v_conf_ln_b'], 'v_sc_dw_w': out['v_sc_dw_w'], 'v_w_out': out['v_w_out'], 'v_ffn_norm_g': out['v_ffn_norm_g'], 'v_w_up': out['v_w_up'], 'v_ffn_dw_w': out['v_ffn_dw_w'], 'v_ffn_dw_b': out['v_ffn_dw_b'], 'v_w_down': out['v_w_down'], 'v_final_norm_g': out['v_final_norm_g']}


def _loss(weights, diff, rest, loss_target):
    with _jax.named_scope("forward"):
        args = {**rest, TWIN_DIFF_INPUT: diff, **{k: w.astype(_WEIGHT_DTYPES[k]) for k, w in weights.items()}}
        y = _forward(args)
    with _jax.named_scope("loss_head"):
        err = _jnp.square(y.astype(_jnp.float32) - loss_target)
        return 0.5 * _jnp.sum(_jnp.mean(err, axis=-1)) if err.ndim else 0.5 * err


def _adamw(w, g, m, v):
    m = ADAM_B1 * m + (1.0 - ADAM_B1) * g
    v = ADAM_B2 * v + (1.0 - ADAM_B2) * _jnp.square(g)
    m_hat = m / (1.0 - ADAM_B1 ** ADAM_STEP)
    v_hat = v / (1.0 - ADAM_B2 ** ADAM_STEP)
    delta = -ADAM_LR * (m_hat / (_jnp.sqrt(v_hat) + ADAM_EPS) + ADAM_WD * w)
    return delta, m, v


def reference(x, c, ada_w, ada_b, mix_norm_g, w_in, b_forget, conf_dw_w, conf_dw_b, conf_ln_g, conf_ln_b, sc_dw_w, w_out, ffn_norm_g, w_up, ffn_dw_w, ffn_dw_b, w_down, final_norm_g, loss_target, m_ada_w, m_ada_b, m_mix_norm_g, m_w_in, m_b_forget, m_conf_dw_w, m_conf_dw_b, m_conf_ln_g, m_conf_ln_b, m_sc_dw_w, m_w_out, m_ffn_norm_g, m_w_up, m_ffn_dw_w, m_ffn_dw_b, m_w_down, m_final_norm_g, v_ada_w, v_ada_b, v_mix_norm_g, v_w_in, v_b_forget, v_conf_dw_w, v_conf_dw_b, v_conf_ln_g, v_conf_ln_b, v_sc_dw_w, v_w_out, v_ffn_norm_g, v_w_up, v_ffn_dw_w, v_ffn_dw_b, v_w_down, v_final_norm_g):
    given = dict(x=x, c=c, ada_w=ada_w, ada_b=ada_b, mix_norm_g=mix_norm_g, w_in=w_in, b_forget=b_forget, conf_dw_w=conf_dw_w, conf_dw_b=conf_dw_b, conf_ln_g=conf_ln_g, conf_ln_b=conf_ln_b, sc_dw_w=sc_dw_w, w_out=w_out, ffn_norm_g=ffn_norm_g, w_up=w_up, ffn_dw_w=ffn_dw_w, ffn_dw_b=ffn_dw_b, w_down=w_down, final_norm_g=final_norm_g, loss_target=loss_target, m_ada_w=m_ada_w, m_ada_b=m_ada_b, m_mix_norm_g=m_mix_norm_g, m_w_in=m_w_in, m_b_forget=m_b_forget, m_conf_dw_w=m_conf_dw_w, m_conf_dw_b=m_conf_dw_b, m_conf_ln_g=m_conf_ln_g, m_conf_ln_b=m_conf_ln_b, m_sc_dw_w=m_sc_dw_w, m_w_out=m_w_out, m_ffn_norm_g=m_ffn_norm_g, m_w_up=m_w_up, m_ffn_dw_w=m_ffn_dw_w, m_ffn_dw_b=m_ffn_dw_b, m_w_down=m_w_down, m_final_norm_g=m_final_norm_g, v_ada_w=v_ada_w, v_ada_b=v_ada_b, v_mix_norm_g=v_mix_norm_g, v_w_in=v_w_in, v_b_forget=v_b_forget, v_conf_dw_w=v_conf_dw_w, v_conf_dw_b=v_conf_dw_b, v_conf_ln_g=v_conf_ln_g, v_conf_ln_b=v_conf_ln_b, v_sc_dw_w=v_sc_dw_w, v_w_out=v_w_out, v_ffn_norm_g=v_ffn_norm_g, v_w_up=v_w_up, v_ffn_dw_w=v_ffn_dw_w, v_ffn_dw_b=v_ffn_dw_b, v_w_down=v_w_down, v_final_norm_g=v_final_norm_g)
    weights = {n: given[n] for n in TWIN_WEIGHTS}
    shared = {n: given[n] for n in SHARED_INPUTS}
    per_example = {n: given[n] for n in ['x', 'c']}
    grad_fn = _jax.value_and_grad(_loss, argnums=(0, 1))

    def one_microbatch(ex, loss_target):
        ex = dict(ex)
        diff = ex.pop(TWIN_DIFF_INPUT)
        return grad_fn(weights, diff, {**shared, **ex}, loss_target)

    if N_MICROBATCH == 1:
        loss, (grad_w, grad_x) = one_microbatch(per_example, given["loss_target"])
    else:
        def body(carry, xs):
            loss_sum, grad_sum = carry
            l_k, (gw_k, gx_k) = one_microbatch(xs[0], xs[1])
            with _jax.named_scope("update"):
                return (loss_sum + l_k, _jax.tree.map(_jnp.add, grad_sum, gw_k)), gx_k

        init = (_jnp.zeros((), _jnp.float32), _jax.tree.map(_jnp.zeros_like, weights))
        (loss, grad_w), grad_x = _jax.lax.scan(body, init, (per_example, given["loss_target"]))
    with _jax.named_scope("update"):
        delta_w, new_m, new_v = {}, {}, {}
        for n in TWIN_WEIGHTS:
            delta_w[n], new_m[n], new_v[n] = _adamw(weights[n], grad_w[n], given["m_" + n], given["v_" + n])
    return (loss, grad_x, *[grad_w[n] for n in TWIN_WEIGHTS], *[delta_w[n] for n in TWIN_WEIGHTS],
            *[new_m[n] for n in TWIN_WEIGHTS], *[new_v[n] for n in TWIN_WEIGHTS])
```

```python
import functools

import numpy as np
import jax
import jax.numpy as jnp
from jax import lax
from jax.experimental import pallas as pl
from jax.experimental.pallas import tpu as pltpu

F32 = jnp.float32
BF16 = jnp.bfloat16
RMS_EPS = 1e-6
LN_EPS = 1e-5
HEAD_DIM = 128
N_ADA = 6
ADAM_LR = 0.001
ADAM_B1 = 0.9
ADAM_B2 = 0.999
ADAM_EPS = 1e-08
ADAM_WD = 0.01
ADAM_STEP = 10
N_DEV = 8
LANES = 128
VMEM_LIMIT_BYTES = 56 * 1024 * 1024
MESH = pl.DeviceIdType.MESH
PEER_FLIPS = ((0, 0, 1), (1, 0, 0), (0, 1, 0), (1, 1, 0), (1, 0, 1), (0, 1, 1), (1, 1, 1))


def _params(*sem):
    return pltpu.CompilerParams(dimension_semantics=sem, vmem_limit_bytes=VMEM_LIMIT_BYTES)


def _tile(n, cap):
    if n <= cap:
        return n
    for t in range(cap - cap % LANES, 0, -LANES):
        if n % t == 0:
            return t
    raise ValueError(f"no tile for {n}")


def _sigmoid(v):
    return jax.nn.sigmoid(v)


def _matmul(a, b, out_dtype, name):
    M, K = a.shape
    _, N = b.shape
    tm, tn = _tile(M, 1024), _tile(N, 1024)
    tk = K if K <= 2048 else _tile(K, 1024)
    nk = K // tk

    if nk == 1:
        def body(a_ref, b_ref, o_ref):
            o_ref[...] = jnp.dot(a_ref[...], b_ref[...], preferred_element_type=F32).astype(o_ref.dtype)
        scratch = []
    else:
        def body(a_ref, b_ref, o_ref, acc_ref):
            k = pl.program_id(2)

            @pl.when(k == 0)
            def _():
                acc_ref[...] = jnp.zeros_like(acc_ref)

            acc_ref[...] += jnp.dot(a_ref[...], b_ref[...], preferred_element_type=F32)

            @pl.when(k == nk - 1)
            def _():
                o_ref[...] = acc_ref[...].astype(o_ref.dtype)
        scratch = [pltpu.VMEM((tm, tn), F32)]

    return pl.pallas_call(
        body, name=name,
        out_shape=jax.ShapeDtypeStruct((M, N), out_dtype),
        grid=(M // tm, N // tn, nk),
        in_specs=[pl.BlockSpec((tm, tk), lambda i, j, k: (i, k)),
                  pl.BlockSpec((tk, tn), lambda i, j, k: (k, j))],
        out_specs=pl.BlockSpec((tm, tn), lambda i, j, k: (i, j)),
        scratch_shapes=scratch,
        compiler_params=_params("parallel", "parallel", "arbitrary"),
    )(a, b)


def _site_fwd(x, delta, gate, g, sc, sh, name):
    S, D = x.shape
    T = min(256, S)
    res = delta is not None

    def body(*refs):
        if res:
            x_ref, d_ref, gate_ref, g_ref, sc_ref, sh_ref, xo_ref, h_ref = refs
            xv = x_ref[...] + gate_ref[...] * d_ref[...]
            xo_ref[...] = xv
        else:
            x_ref, g_ref, sc_ref, sh_ref, h_ref = refs
            xv = x_ref[...]
        r = lax.rsqrt(jnp.mean(xv * xv, axis=-1, keepdims=True) + RMS_EPS)
        a = g_ref[...] * (1.0 + sc_ref[...])
        h_ref[...] = (xv * r * a + sh_ref[...]).astype(BF16)

    row = pl.BlockSpec((T, D), lambda i: (i, 0))
    vec = pl.BlockSpec((1, D), lambda i: (0, 0))
    if res:
        ins, in_specs = (x, delta, gate, g, sc, sh), [row, row, vec, vec, vec, vec]
        out_shape = (jax.ShapeDtypeStruct((S, D), F32), jax.ShapeDtypeStruct((S, D), BF16))
        out_specs = (row, row)
    else:
        ins, in_specs = (x, g, sc, sh), [row, vec, vec, vec]
        out_shape = jax.ShapeDtypeStruct((S, D), BF16)
        out_specs = row
    out = pl.pallas_call(body, name=name, out_shape=out_shape, grid=(S // T,), in_specs=in_specs,
                         out_specs=out_specs, compiler_params=_params("parallel"))(*ins)
    return out if res else (x, out)


def _site_bwd(x, dh, dres, g, sc, sh, delta, gate, name):
    S, D = x.shape
    T = min(256, S)
    res = delta is not None

    def body(*refs):
        if res:
            (x_ref, dh_ref, dres_ref, g_ref, sc_ref, delta_ref, gate_ref,
             dx_ref, dsh_ref, da_ref, dd_ref, dgate_ref) = refs
        else:
            x_ref, dh_ref, dres_ref, g_ref, sc_ref, dx_ref, dsh_ref, da_ref = refs
        i = pl.program_id(0)
        xv = x_ref[...]
        dhv = dh_ref[...]
        r = lax.rsqrt(jnp.mean(xv * xv, axis=-1, keepdims=True) + RMS_EPS)
        xh = xv * r
        dxh = dhv * (g_ref[...] * (1.0 + sc_ref[...]))
        dx = r * (dxh - xh * jnp.mean(dxh * xh, axis=-1, keepdims=True)) + dres_ref[...]
        dx_ref[...] = dx

        @pl.when(i == 0)
        def _():
            dsh_ref[...] = jnp.zeros_like(dsh_ref)
            da_ref[...] = jnp.zeros_like(da_ref)
            if res:
                dgate_ref[...] = jnp.zeros_like(dgate_ref)

        dsh_ref[...] += jnp.sum(dhv, axis=0, keepdims=True)
        da_ref[...] += jnp.sum(dhv * xh, axis=0, keepdims=True)
        if res:
            dd_ref[...] = (gate_ref[...] * dx).astype(BF16)
            dgate_ref[...] += jnp.sum(dx * delta_ref[...], axis=0, keepdims=True)

    row = pl.BlockSpec((T, D), lambda i: (i, 0))
    vec = pl.BlockSpec((1, D), lambda i: (0, 0))
    vshape = jax.ShapeDtypeStruct((1, D), F32)
    if res:
        ins, in_specs = (x, dh, dres, g, sc, delta, gate), [row, row, row, vec, vec, row, vec]
        out_shape = (jax.ShapeDtypeStruct((S, D), F32), vshape, vshape, jax.ShapeDtypeStruct((S, D), BF16), vshape)
        out_specs = (row, vec, vec, row, vec)
    else:
        ins, in_specs = (x, dh, dres, g, sc), [row, row, row, vec, vec]
        out_shape = (jax.ShapeDtypeStruct((S, D), F32), vshape, vshape)
        out_specs = (row, vec, vec)
    return pl.pallas_call(body, name=name, out_shape=out_shape, grid=(S // T,), in_specs=in_specs,
                          out_specs=out_specs, compiler_params=_params("arbitrary"))(*ins)


def _final_fwd_bwd(x, delta, gate, gfin, target, name):
    S, D = x.shape
    T = min(256, S)

    def body(x_ref, delta_ref, gate_ref, g_ref, t_ref, loss_ref, dx_ref, dd_ref, dgate_ref, dg_ref):
        i = pl.program_id(0)
        dl = delta_ref[...]
        xv = x_ref[...] + gate_ref[...] * dl
        r = lax.rsqrt(jnp.mean(xv * xv, axis=-1, keepdims=True) + RMS_EPS)
        xh = xv * r
        gv = g_ref[...]
        e = xh * gv - t_ref[...]
        dy = e * (1.0 / D)
        dxh = dy * gv
        dx = r * (dxh - xh * jnp.mean(dxh * xh, axis=-1, keepdims=True))
        dx_ref[...] = dx
        dd_ref[...] = (gate_ref[...] * dx).astype(BF16)

        @pl.when(i == 0)
        def _():
            loss_ref[...] = jnp.zeros_like(loss_ref)
            dgate_ref[...] = jnp.zeros_like(dgate_ref)
            dg_ref[...] = jnp.zeros_like(dg_ref)

        loss_ref[...] += jnp.sum(e * e, axis=0, keepdims=True)
        dgate_ref[...] += jnp.sum(dx * dl, axis=0, keepdims=True)
        dg_ref[...] += jnp.sum(dy * xh, axis=0, keepdims=True)

    row = pl.BlockSpec((T, D), lambda i: (i, 0))
    vec = pl.BlockSpec((1, D), lambda i: (0, 0))
    vshape = jax.ShapeDtypeStruct((1, D), F32)
    return pl.pallas_call(
        body, name=name,
        out_shape=(vshape, jax.ShapeDtypeStruct((S, D), F32), jax.ShapeDtypeStruct((S, D), BF16), vshape, vshape),
        grid=(S // T,), in_specs=[row, row, vec, vec, row], out_specs=(vec, row, row, vec, vec),
        compiler_params=_params("arbitrary"))(x, delta, gate, gfin, target)


def _split3(v):
    hi = v.astype(BF16)
    r1 = v - hi.astype(F32)
    mid = r1.astype(BF16)
    lo = (r1 - mid.astype(F32)).astype(BF16)
    return hi, mid, lo


def _tri_dot(tri, v):
    hi, mid, lo = _split3(v)
    d = functools.partial(jnp.dot, preferred_element_type=F32)
    return d(tri, hi) + d(tri, mid) + d(tri, lo)


def _fgate_fwd(rest, bpad, fblk, name):
    S = rest.shape[0]
    CH = min(256, S)
    nch = S // CH

    def body(f_ref, b_ref, o_ref):
        row = lax.broadcasted_iota(jnp.int32, (CH, CH), 0)
        col = lax.broadcasted_iota(jnp.int32, (CH, CH), 1)
        tri = (row >= col).astype(BF16)

        def step(ci, carry):
            rows = pl.ds(pl.multiple_of(ci * CH, CH), CH)
            z = f_ref[rows, :] + b_ref[...]
            lf = jnp.minimum(z, 0.0) - jnp.log(1.0 + jnp.exp(-jnp.abs(z)))
            o_ref[rows, :] = _tri_dot(tri, lf) + carry
            return carry + jnp.sum(lf, axis=0, keepdims=True)

        lax.fori_loop(0, nch, step, jnp.zeros((1, LANES), F32))

    return pl.pallas_call(
        body, name=name, out_shape=jax.ShapeDtypeStruct((S, LANES), F32), grid=(1,),
        in_specs=[pl.BlockSpec((S, LANES), lambda i: (0, fblk)), pl.BlockSpec((1, LANES), lambda i: (0, 0))],
        out_specs=pl.BlockSpec((S, LANES), lambda i: (0, 0)),
        compiler_params=_params("arbitrary"))(rest, bpad)


def _fgate_bwd(rest, bpad, dF, fblk, name):
    S = rest.shape[0]
    CH = min(256, S)
    nch = S // CH

    def body(f_ref, b_ref, df_ref, o_ref, db_ref):
        row = lax.broadcasted_iota(jnp.int32, (CH, CH), 0)
        col = lax.broadcasted_iota(jnp.int32, (CH, CH), 1)
        tri = (col >= row).astype(BF16)

        def step(n, carry):
            sfx_carry, db = carry
            ci = nch - 1 - n
            rows = pl.ds(pl.multiple_of(ci * CH, CH), CH)
            z = f_ref[rows, :] + b_ref[...]
            dfv = df_ref[rows, :]
            dz = (_tri_dot(tri, dfv) + sfx_carry) * _sigmoid(-z)
            o_ref[rows, :] = dz.astype(BF16)
            return sfx_carry + jnp.sum(dfv, axis=0, keepdims=True), db + jnp.sum(dz, axis=0, keepdims=True)

        zero = jnp.zeros((1, LANES), F32)
        _, db = lax.fori_loop(0, nch, step, (zero, zero))
        db_ref[...] = db

    blk = pl.BlockSpec((S, LANES), lambda i: (0, 0))
    return pl.pallas_call(
        body, name=name,
        out_shape=(jax.ShapeDtypeStruct((S, LANES), BF16), jax.ShapeDtypeStruct((1, LANES), F32)), grid=(1,),
        in_specs=[pl.BlockSpec((S, LANES), lambda i: (0, fblk)), pl.BlockSpec((1, LANES), lambda i: (0, 0)), blk],
        out_specs=(blk, pl.BlockSpec((1, LANES), lambda i: (0, 0))),
        compiler_params=_params("arbitrary"))(rest, bpad, dF)


def _causal_pairs(nb, by_key):
    if by_key:
        pairs = [(i, j) for j in range(nb) for i in range(j, nb)]
    else:
        pairs = [(i, j) for i in range(nb) for j in range(i + 1)]
    qi = np.asarray([p[0] for p in pairs], np.int32)
    kj = np.asarray([p[1] for p in pairs], np.int32)
    return jnp.asarray(qi), jnp.asarray(kj)


_NT = (((1,), (1,)), ((), ()))


def _attn_fwd(qkv, nf_row, H, name):
    S = qkv.shape[0]
    TA = min(512, S)
    nb = S // TA
    qi, kj = _causal_pairs(nb, by_key=False)
    scale = HEAD_DIM ** -0.5

    def body(qi_ref, kj_ref, q_ref, k_ref, v_ref, nf_ref, o_ref, o32_ref, lse_ref, m_ref, l_ref, acc_ref):
        p_id = pl.program_id(1)
        i, j = qi_ref[p_id], kj_ref[p_id]

        @pl.when(j == 0)
        def _():
            m_ref[...] = jnp.full_like(m_ref, -jnp.inf)
            l_ref[...] = jnp.zeros_like(l_ref)
            acc_ref[...] = jnp.zeros_like(acc_ref)

        def step(masked):
            s = lax.dot_general(q_ref[...], k_ref[...], _NT, preferred_element_type=F32) * scale + nf_ref[0]
            if masked:
                row = lax.broadcasted_iota(jnp.int32, (TA, TA), 0)
                col = lax.broadcasted_iota(jnp.int32, (TA, TA), 1)
                s = jnp.where(col <= row, s, -jnp.inf)
            m_old = m_ref[...]
            m_new = jnp.maximum(m_old, jnp.max(s, axis=-1, keepdims=True))
            alpha = jnp.exp(m_old - m_new)
            p = jnp.exp(s - m_new)
            l_ref[...] = alpha * l_ref[...] + jnp.sum(p, axis=-1, keepdims=True)
            acc_ref[...] = alpha * acc_ref[...] + jnp.dot(p.astype(BF16), v_ref[...], preferred_element_type=F32)
            m_ref[...] = m_new

        @pl.when(j < i)
        def _():
            step(False)

        @pl.when(j == i)
        def _():
            step(True)
            o = acc_ref[...] / l_ref[...]
            o32_ref[...] = o
            o_ref[...] = o.astype(BF16)
            lse_ref[0] = m_ref[...] + jnp.log(l_ref[...])

    grid_spec = pltpu.PrefetchScalarGridSpec(
        num_scalar_prefetch=2, grid=(H, qi.shape[0]),
        in_specs=[pl.BlockSpec((TA, HEAD_DIM), lambda h, p, qi, kj: (qi[p], h)),
                  pl.BlockSpec((TA, HEAD_DIM), lambda h, p, qi, kj: (kj[p], H + h)),
                  pl.BlockSpec((TA, HEAD_DIM), lambda h, p, qi, kj: (kj[p], 2 * H + h)),
                  pl.BlockSpec((1, 1, TA), lambda h, p, qi, kj: (h, 0, kj[p]))],
        out_specs=(pl.BlockSpec((TA, HEAD_DIM), lambda h, p, qi, kj: (qi[p], h)),
                   pl.BlockSpec((TA, HEAD_DIM), lambda h, p, qi, kj: (qi[p], h)),
                   pl.BlockSpec((1, TA, 1), lambda h, p, qi, kj: (h, qi[p], 0))),
        scratch_shapes=[pltpu.VMEM((TA, 1), F32), pltpu.VMEM((TA, 1), F32), pltpu.VMEM((TA, HEAD_DIM), F32)])
    return pl.pallas_call(
        body, name=name, grid_spec=grid_spec,
        out_shape=(jax.ShapeDtypeStruct((S, H * HEAD_DIM), BF16), jax.ShapeDtypeStruct((S, H * HEAD_DIM), F32),
                   jax.ShapeDtypeStruct((H, S, 1), F32)),
        compiler_params=_params("parallel", "arbitrary"))(qi, kj, qkv, qkv, qkv, nf_row)


def _attn_delta(o, do, H, name):
    S = o.shape[0]
    T = min(512, S)

    def body(o_ref, do_ref, d_ref):
        d_ref[0] = jnp.sum(o_ref[...].astype(F32) * do_ref[...].astype(F32), axis=-1, keepdims=True)

    blk = pl.BlockSpec((T, HEAD_DIM), lambda h, i: (i, h))
    return pl.pallas_call(
        body, name=name, out_shape=jax.ShapeDtypeStruct((H, S, 1), F32), grid=(H, S // T),
        in_specs=[blk, blk], out_specs=pl.BlockSpec((1, T, 1), lambda h, i: (h, i, 0)),
        compiler_params=_params("parallel", "parallel"))(o, do)


def _attn_bwd_dq(qkv, nf_row, do, lse, delta, H, name):
    S = qkv.shape[0]
    TA = min(512, S)
    nb = S // TA
    qi, kj = _causal_pairs(nb, by_key=False)
    scale = HEAD_DIM ** -0.5

    def body(qi_ref, kj_ref, q_ref, k_ref, v_ref, nf_ref, do_ref, lse_ref, dl_ref, dq_ref, drow_ref,
             acc_ref, row_acc):
        p_id = pl.program_id(1)
        i, j = qi_ref[p_id], kj_ref[p_id]

        @pl.when(j == 0)
        def _():
            acc_ref[...] = jnp.zeros_like(acc_ref)
            row_acc[...] = jnp.zeros_like(row_acc)

        def step(masked):
            s = lax.dot_general(q_ref[...], k_ref[...], _NT, preferred_element_type=F32) * scale + nf_ref[0]
            p = jnp.exp(s - lse_ref[0])
            if masked:
                row = lax.broadcasted_iota(jnp.int32, (TA, TA), 0)
                col = lax.broadcasted_iota(jnp.int32, (TA, TA), 1)
                p = jnp.where(col <= row, p, 0.0)
            dp = lax.dot_general(do_ref[...], v_ref[...], _NT, preferred_element_type=F32)
            ds = p * (dp - dl_ref[0])
            acc_ref[...] += jnp.dot(ds.astype(BF16), k_ref[...], preferred_element_type=F32)
            row_acc[...] += jnp.sum(ds, axis=-1, keepdims=True)

        @pl.when(j < i)
        def _():
            step(False)

        @pl.when(j == i)
        def _():
            step(True)
            dq_ref[...] = (acc_ref[...] * scale).astype(BF16)
            drow_ref[0] = row_acc[...]

    col_stat = pl.BlockSpec((1, TA, 1), lambda h, p, qi, kj: (h, qi[p], 0))
    grid_spec = pltpu.PrefetchScalarGridSpec(
        num_scalar_prefetch=2, grid=(H, qi.shape[0]),
        in_specs=[pl.BlockSpec((TA, HEAD_DIM), lambda h, p, qi, kj: (qi[p], h)),
                  pl.BlockSpec((TA, HEAD_DIM), lambda h, p, qi, kj: (kj[p], H + h)),
                  pl.BlockSpec((TA, HEAD_DIM), lambda h, p, qi, kj: (kj[p], 2 * H + h)),
                  pl.BlockSpec((1, 1, TA), lambda h, p, qi, kj: (h, 0, kj[p])),
                  pl.BlockSpec((TA, HEAD_DIM), lambda h, p, qi, kj: (qi[p], h)),
                  col_stat, col_stat],
        out_specs=(pl.BlockSpec((TA, HEAD_DIM), lambda h, p, qi, kj: (qi[p], h)), col_stat),
        scratch_shapes=[pltpu.VMEM((TA, HEAD_DIM), F32), pltpu.VMEM((TA, 1), F32)])
    return pl.pallas_call(
        body, name=name, grid_spec=grid_spec,
        out_shape=(jax.ShapeDtypeStruct((S, H * HEAD_DIM), BF16), jax.ShapeDtypeStruct((H, S, 1), F32)),
        compiler_params=_params("parallel", "arbitrary"))(qi, kj, qkv, qkv, qkv, nf_row, do, lse, delta)


def _attn_bwd_dkv(qkv, nf_col, do, lse_row, delta_row, H, name):
    S = qkv.shape[0]
    TA = min(512, S)
    nb = S // TA
    qi, kj = _causal_pairs(nb, by_key=True)
    scale = HEAD_DIM ** -0.5

    def body(qi_ref, kj_ref, q_ref, k_ref, v_ref, nf_ref, do_ref, lse_ref, dl_ref,
             dk_ref, dv_ref, dnf_ref, dk_acc, dv_acc, dnf_acc):
        p_id = pl.program_id(1)
        i, j = qi_ref[p_id], kj_ref[p_id]

        def step(masked):
            st = lax.dot_general(k_ref[...], q_ref[...], _NT, preferred_element_type=F32) * scale + nf_ref[0]
            pt = jnp.exp(st - lse_ref[0])
            if masked:
                row = lax.broadcasted_iota(jnp.int32, (TA, TA), 0)
                col = lax.broadcasted_iota(jnp.int32, (TA, TA), 1)
                pt = jnp.where(row <= col, pt, 0.0)
            dpt = lax.dot_general(v_ref[...], do_ref[...], _NT, preferred_element_type=F32)
            dst = pt * (dpt - dl_ref[0])
            dv_acc[...] += jnp.dot(pt.astype(BF16), do_ref[...], preferred_element_type=F32)
            dk_acc[...] += jnp.dot(dst.astype(BF16), q_ref[...], preferred_element_type=F32)
            dnf_acc[...] += jnp.sum(dst, axis=-1, keepdims=True)

        @pl.when(i == j)
        def _():
            dk_acc[...] = jnp.zeros_like(dk_acc)
            dv_acc[...] = jnp.zeros_like(dv_acc)
            dnf_acc[...] = jnp.zeros_like(dnf_acc)
            step(True)

        @pl.when(i > j)
        def _():
            step(False)

        @pl.when(i == nb - 1)
        def _():
            dk_ref[...] = (dk_acc[...] * scale).astype(BF16)
            dv_ref[...] = dv_acc[...].astype(BF16)
            dnf_ref[0] = dnf_acc[...]

    row_stat = pl.BlockSpec((1, 1, TA), lambda h, p, qi, kj: (h, 0, qi[p]))
    kblk = lambda c0: pl.BlockSpec((TA, HEAD_DIM), lambda h, p, qi, kj: (kj[p], c0 + h))
    grid_spec = pltpu.PrefetchScalarGridSpec(
        num_scalar_prefetch=2, grid=(H, qi.shape[0]),
        in_specs=[pl.BlockSpec((TA, HEAD_DIM), lambda h, p, qi, kj: (qi[p], h)),
                  kblk(H), kblk(2 * H),
                  pl.BlockSpec((1, TA, 1), lambda h, p, qi, kj: (h, kj[p], 0)),
                  pl.BlockSpec((TA, HEAD_DIM), lambda h, p, qi, kj: (qi[p], h)),
                  row_stat, row_stat],
        out_specs=(kblk(0), kblk(0), pl.BlockSpec((1, TA, 1), lambda h, p, qi, kj: (h, kj[p], 0))),
        scratch_shapes=[pltpu.VMEM((TA, HEAD_DIM), F32), pltpu.VMEM((TA, HEAD_DIM), F32), pltpu.VMEM((TA, 1), F32)])
    shp = jax.ShapeDtypeStruct((S, H * HEAD_DIM), BF16)
    return pl.pallas_call(
        body, name=name, grid_spec=grid_spec,
        out_shape=(shp, shp, jax.ShapeDtypeStruct((H, S, 1), F32)),
        compiler_params=_params("parallel", "arbitrary"))(qi, kj, qkv, qkv, qkv, nf_col, do, lse_row, delta_row)


def _taps(ext_ref, w_ref, K, base, r0, rows, cols, reverse=False, init=None):
    acc = init
    for k in range(K):
        wk = w_ref[(K - 1 - k) if reverse else k:((K - 1 - k) if reverse else k) + 1, cols]
        term = wk * ext_ref[base + k + r0:base + k + r0 + rows, cols]
        acc = term if acc is None else acc + term
    return acc


def _prev_blk(T, H):
    return lambda i: jnp.maximum(i * (T // H) - 1, 0)


def _next_blk(T, H, S):
    return lambda i: jnp.minimum((i + 1) * (T // H), S // H - 1)


def _conf_fwd(rest, w, b, lng, lnb, name):
    S = rest.shape[0]
    K, C = w.shape
    H, T = 32, min(256, S)
    RS = min(64, T)
    base = H - (K - 1)

    def body(cv_ref, cg_ref, cvp_ref, cgp_ref, w_ref, b_ref, g_ref, bb_ref, o_ref, ext_ref):
        i = pl.program_id(0)
        ext_ref[0:H, :] = jnp.where(i > 0, cvp_ref[...] * _sigmoid(cgp_ref[...]), 0.0)
        ext_ref[H:H + T, :] = cv_ref[...] * _sigmoid(cg_ref[...])
        for r0 in range(0, T, RS):
            cc = _taps(ext_ref, w_ref, K, base, r0, RS, slice(None), init=jnp.broadcast_to(b_ref[...], (RS, C)))
            xc = cc - jnp.mean(cc, axis=-1, keepdims=True)
            y = xc * lax.rsqrt(jnp.mean(xc * xc, axis=-1, keepdims=True) + LN_EPS) * g_ref[...] + bb_ref[...]
            o_ref[r0:r0 + RS, :] = (y * _sigmoid(y)).astype(BF16)

    pb = _prev_blk(T, H)
    cur = lambda cb: pl.BlockSpec((T, C), lambda i: (i, cb))
    prev = lambda cb: pl.BlockSpec((H, C), lambda i: (pb(i), cb))
    full = lambda a: pl.BlockSpec(a.shape, lambda i: (0, 0))
    return pl.pallas_call(
        body, name=name, out_shape=jax.ShapeDtypeStruct((S, C), BF16), grid=(S // T,),
        in_specs=[cur(0), cur(1), prev(0), prev(1), full(w), full(b), full(lng), full(lnb)],
        out_specs=pl.BlockSpec((T, C), lambda i: (i, 0)),
        scratch_shapes=[pltpu.VMEM((H + T, C), F32)],
        compiler_params=_params("parallel"))(rest, rest, rest, rest, w, b, lng, lnb)


def _conf_bwd(rest, dcs, w, b, lng, lnb, name):
    S = rest.shape[0]
    K, C = w.shape
    H, T = 32, min(256, S)
    RS = 32
    nI = S // T
    base = H - (K - 1)

    def body(cv_ref, cg_ref, cvp_ref, cgp_ref, cvn_ref, cgn_ref, do_ref, don_ref, w_ref, b_ref, g_ref, bb_ref,
             dcvg_ref, dw_ref, dvec_ref, ext_ref, dcc_ref):
        i = pl.program_id(0)
        ext_ref[0:H, :] = jnp.where(i > 0, cvp_ref[...] * _sigmoid(cgp_ref[...]), 0.0)
        ext_ref[H:H + T, :] = cv_ref[...] * _sigmoid(cg_ref[...])
        ext_ref[H + T:H + T + H, :] = cvn_ref[...] * _sigmoid(cgn_ref[...])

        @pl.when(i == 0)
        def _():
            dw_ref[...] = jnp.zeros_like(dw_ref)
            dvec_ref[...] = jnp.zeros_like(dvec_ref)

        db = jnp.zeros((1, C), F32)
        dg = jnp.zeros((1, C), F32)
        dbb = jnp.zeros((1, C), F32)
        for r0 in range(0, T + H, RS):
            cc = _taps(ext_ref, w_ref, K, base, r0, RS, slice(None), init=jnp.broadcast_to(b_ref[...], (RS, C)))
            xc = cc - jnp.mean(cc, axis=-1, keepdims=True)
            r = lax.rsqrt(jnp.mean(xc * xc, axis=-1, keepdims=True) + LN_EPS)
            xh = xc * r
            y = xh * g_ref[...] + bb_ref[...]
            sy = _sigmoid(y)
            if r0 < T:
                d_o = do_ref[r0:r0 + RS, :]
            else:
                d_o = jnp.where(i < nI - 1, don_ref[r0 - T:r0 - T + RS, :], 0.0)
            dy = d_o * (sy * (1.0 + y * (1.0 - sy)))
            dxh = dy * g_ref[...]
            dcc = r * (dxh - jnp.mean(dxh, axis=-1, keepdims=True)
                       - xh * jnp.mean(dxh * xh, axis=-1, keepdims=True))
            dcc_ref[r0:r0 + RS, :] = dcc
            if r0 < T:
                dbb = dbb + jnp.sum(dy, axis=0, keepdims=True)
                dg = dg + jnp.sum(dy * xh, axis=0, keepdims=True)
                db = db + jnp.sum(dcc, axis=0, keepdims=True)
        dvec_ref[0:1, :] += db
        dvec_ref[1:2, :] += dg
        dvec_ref[2:3, :] += dbb
        R2 = min(64, T)
        for k in range(K):
            s = jnp.zeros((1, C), F32)
            for r0 in range(0, T, R2):
                s = s + jnp.sum(dcc_ref[r0:r0 + R2, :] * ext_ref[base + k + r0:base + k + r0 + R2, :],
                                axis=0, keepdims=True)
            dw_ref[k:k + 1, :] += s
        for r0 in range(0, T, R2):
            dci = _taps(dcc_ref, w_ref, K, 0, r0, R2, slice(None), reverse=True)
            cvv = cv_ref[r0:r0 + R2, :]
            sg = _sigmoid(cg_ref[r0:r0 + R2, :])
            dcvg_ref[r0:r0 + R2, 0:C] = (dci * sg).astype(BF16)
            dcvg_ref[r0:r0 + R2, C:2 * C] = (dci * cvv * sg * (1.0 - sg)).astype(BF16)

    pb, nb_ = _prev_blk(T, H), _next_blk(T, H, S)
    cur = lambda cb: pl.BlockSpec((T, C), lambda i: (i, cb))
    prev = lambda cb: pl.BlockSpec((H, C), lambda i: (pb(i), cb))
    nxt = lambda cb: pl.BlockSpec((H, C), lambda i: (nb_(i), cb))
    full = lambda a: pl.BlockSpec(a.shape, lambda i: (0, 0))
    return pl.pallas_call(
        body, name=name,
        out_shape=(jax.ShapeDtypeStruct((S, 2 * C), BF16), jax.ShapeDtypeStruct((32, C), F32),
                   jax.ShapeDtypeStruct((8, C), F32)),
        grid=(nI,),
        in_specs=[cur(0), cur(1), prev(0), prev(1), nxt(0), nxt(1), cur(0), nxt(0),
                  full(w), full(b), full(lng), full(lnb)],
        out_specs=(pl.BlockSpec((T, 2 * C), lambda i: (i, 0)), pl.BlockSpec((32, C), lambda i: (0, 0)),
                   pl.BlockSpec((8, C), lambda i: (0, 0))),
        scratch_shapes=[pltpu.VMEM((H + T + H, C), F32), pltpu.VMEM((T + H, C), F32)],
        compiler_params=_params("arbitrary"))(rest, rest, rest, rest, rest, rest, dcs, dcs, w, b, lng, lnb)


def _sconv_fwd(rest, w, name):
    S = rest.shape[0]
    K, C = w.shape
    H, T = 8, min(256, S)
    RS = min(64, T)
    base = H - (K - 1)

    def body(sx_ref, sb_ref, sc_ref, sxp_ref, scp_ref, w_ref, o_ref, ext_ref):
        i = pl.program_id(0)
        ext_ref[0:H, :] = jnp.where(i > 0, sxp_ref[...] * scp_ref[...], 0.0)
        ext_ref[H:H + T, :] = sx_ref[...] * sc_ref[...]
        for r0 in range(0, T, RS):
            cz = _taps(ext_ref, w_ref, K, base, r0, RS, slice(None))
            o_ref[r0:r0 + RS, :] = (sb_ref[r0:r0 + RS, :] * cz).astype(BF16)

    pb = _prev_blk(T, H)
    cur = lambda cb: pl.BlockSpec((T, C), lambda i: (i, cb))
    prev = lambda cb: pl.BlockSpec((H, C), lambda i: (pb(i), cb))
    return pl.pallas_call(
        body, name=name, out_shape=jax.ShapeDtypeStruct((S, C), BF16), grid=(S // T,),
        in_specs=[cur(2), cur(3), cur(4), prev(2), prev(4), pl.BlockSpec(w.shape, lambda i: (0, 0))],
        out_specs=pl.BlockSpec((T, C), lambda i: (i, 0)),
        scratch_shapes=[pltpu.VMEM((H + T, C), F32)],
        compiler_params=_params("parallel"))(rest, rest, rest, rest, rest, w)


def _sconv_bwd(rest, dcs, w, name):
    S = rest.shape[0]
    K, C = w.shape
    H, T = 8, min(256, S)
    RS = min(64, T)
    nI = S // T
    base = H - (K - 1)

    def body(sx_ref, sb_ref, sc_ref, sxp_ref, scp_ref, sbn_ref, do_ref, don_ref, w_ref,
             dout_ref, dw_ref, ext_ref, dcv_ref):
        i = pl.program_id(0)
        ext_ref[0:H, :] = jnp.where(i > 0, sxp_ref[...] * scp_ref[...], 0.0)
        ext_ref[H:H + T, :] = sx_ref[...] * sc_ref[...]
        dcv_ref[0:T, :] = do_ref[...] * sb_ref[...]
        dcv_ref[T:T + H, :] = jnp.where(i < nI - 1, don_ref[...] * sbn_ref[...], 0.0)

        @pl.when(i == 0)
        def _():
            dw_ref[...] = jnp.zeros_like(dw_ref)

        for k in range(K):
            s = jnp.zeros((1, C), F32)
            for r0 in range(0, T, RS):
                s = s + jnp.sum(dcv_ref[r0:r0 + RS, :] * ext_ref[base + k + r0:base + k + r0 + RS, :],
                                axis=0, keepdims=True)
            dw_ref[k:k + 1, :] += s
        for r0 in range(0, T, RS):
            cz = _taps(ext_ref, w_ref, K, base, r0, RS, slice(None))
            dz = _taps(dcv_ref, w_ref, K, 0, r0, RS, slice(None), reverse=True)
            dout_ref[r0:r0 + RS, 0:C] = (dz * sc_ref[r0:r0 + RS, :]).astype(BF16)
            dout_ref[r0:r0 + RS, C:2 * C] = (do_ref[r0:r0 + RS, :] * cz).astype(BF16)
            dout_ref[r0:r0 + RS, 2 * C:3 * C] = (dz * sx_ref[r0:r0 + RS, :]).astype(BF16)

    pb, nb_ = _prev_blk(T, H), _next_blk(T, H, S)
    cur = lambda cb: pl.BlockSpec((T, C), lambda i: (i, cb))
    prev = lambda cb: pl.BlockSpec((H, C), lambda i: (pb(i), cb))
    nxt = lambda cb: pl.BlockSpec((H, C), lambda i: (nb_(i), cb))
    return pl.pallas_call(
        body, name=name,
        out_shape=(jax.ShapeDtypeStruct((S, 3 * C), BF16), jax.ShapeDtypeStruct((8, C), F32)),
        grid=(nI,),
        in_specs=[cur(2), cur(3), cur(4), prev(2), prev(4), nxt(3), cur(1), nxt(1),
                  pl.BlockSpec(w.shape, lambda i: (0, 0))],
        out_specs=(pl.BlockSpec((T, 3 * C), lambda i: (i, 0)), pl.BlockSpec((8, C), lambda i: (0, 0))),
        scratch_shapes=[pltpu.VMEM((H + T, C), F32), pltpu.VMEM((T + H, C), F32)],
        compiler_params=_params("arbitrary"))(rest, rest, rest, rest, rest, rest, dcs, dcs, w)


def _ffn_fwd(hu, w, b, name):
    S, F2 = hu.shape
    Fd = F2 // 2
    K = w.shape[0]
    H, T = 8, min(256, S)
    tc = _tile(Fd, 512)
    nJ = Fd // tc
    RS = min(64, T)
    base = H - (K - 1)

    def body(g_ref, v_ref, gp_ref, vp_ref, wg_ref, wv_ref, bg_ref, bv_ref, o_ref, eg_ref, ev_ref):
        i = pl.program_id(1)
        eg_ref[0:H, :] = jnp.where(i > 0, gp_ref[...], 0.0)
        ev_ref[0:H, :] = jnp.where(i > 0, vp_ref[...], 0.0)
        eg_ref[H:H + T, :] = g_ref[...]
        ev_ref[H:H + T, :] = v_ref[...]
        for r0 in range(0, T, RS):
            ug = _taps(eg_ref, wg_ref, K, base, r0, RS, slice(None), init=jnp.broadcast_to(bg_ref[...], (RS, tc)))
            uv = _taps(ev_ref, wv_ref, K, base, r0, RS, slice(None), init=jnp.broadcast_to(bv_ref[...], (RS, tc)))
            o_ref[r0:r0 + RS, :] = (ug * _sigmoid(ug) * uv).astype(BF16)

    pb = _prev_blk(T, H)
    cur = lambda off: pl.BlockSpec((T, tc), lambda j, i: (i, j + off))
    prev = lambda off: pl.BlockSpec((H, tc), lambda j, i: (pb(i), j + off))
    wsp = lambda off: pl.BlockSpec((K, tc), lambda j, i: (0, j + off))
    bsp = lambda off: pl.BlockSpec((1, tc), lambda j, i: (0, j + off))
    return pl.pallas_call(
        body, name=name, out_shape=jax.ShapeDtypeStruct((S, Fd), BF16), grid=(nJ, S // T),
        in_specs=[cur(0), cur(nJ), prev(0), prev(nJ), wsp(0), wsp(nJ), bsp(0), bsp(nJ)],
        out_specs=pl.BlockSpec((T, tc), lambda j, i: (i, j)),
        scratch_shapes=[pltpu.VMEM((H + T, tc), F32), pltpu.VMEM((H + T, tc), F32)],
        compiler_params=_params("parallel", "parallel"))(hu, hu, hu, hu, w, w, b, b)


def _ffn_bwd(hu, dact, w, b, name):
    S, F2 = hu.shape
    Fd = F2 // 2
    K = w.shape[0]
    H, T = 8, min(256, S)
    HB = 16
    tc = _tile(Fd, 512)
    nJ = Fd // tc
    nI = S // T
    RS = min(64, T)
    base = H - (K - 1)

    def body(g_ref, v_ref, gp_ref, vp_ref, gn_ref, vn_ref, da_ref, dan_ref, wg_ref, wv_ref, bg_ref, bv_ref,
             dg_ref, dv_ref, dwg_ref, dwv_ref, eg_ref, ev_ref, dug_ref, duv_ref):
        i = pl.program_id(1)
        eg_ref[0:H, :] = jnp.where(i > 0, gp_ref[...], 0.0)
        ev_ref[0:H, :] = jnp.where(i > 0, vp_ref[...], 0.0)
        eg_ref[H:H + T, :] = g_ref[...]
        ev_ref[H:H + T, :] = v_ref[...]
        eg_ref[H + T:H + T + H, :] = gn_ref[...]
        ev_ref[H + T:H + T + H, :] = vn_ref[...]

        @pl.when(i == 0)
        def _():
            dwg_ref[...] = jnp.zeros_like(dwg_ref)
            dwv_ref[...] = jnp.zeros_like(dwv_ref)

        def du_rows(r0, rows, d_a):
            ug = _taps(eg_ref, wg_ref, K, base, r0, rows, slice(None),
                       init=jnp.broadcast_to(bg_ref[...], (rows, tc)))
            uv = _taps(ev_ref, wv_ref, K, base, r0, rows, slice(None),
                       init=jnp.broadcast_to(bv_ref[...], (rows, tc)))
            sg = _sigmoid(ug)
            dug_ref[r0:r0 + rows, :] = d_a * uv * (sg * (1.0 + ug * (1.0 - sg)))
            duv_ref[r0:r0 + rows, :] = d_a * (ug * sg)

        for r0 in range(0, T, RS):
            du_rows(r0, RS, da_ref[r0:r0 + RS, :].astype(F32))
        du_rows(T, H, jnp.where(i < nI - 1, dan_ref[...].astype(F32)[0:H, :], 0.0))

        for ext, du, dw in ((eg_ref, dug_ref, dwg_ref), (ev_ref, duv_ref, dwv_ref)):
            for k in range(K):
                s = jnp.zeros((1, tc), F32)
                for r0 in range(0, T, RS):
                    s = s + jnp.sum(du[r0:r0 + RS, :] * ext[base + k + r0:base + k + r0 + RS, :],
                                    axis=0, keepdims=True)
                dw[k:k + 1, :] += s
            s = jnp.zeros((1, tc), F32)
            for r0 in range(0, T, RS):
                s = s + jnp.sum(du[r0:r0 + RS, :], axis=0, keepdims=True)
            dw[K:K + 1, :] += s
        for r0 in range(0, T, RS):
            dg_ref[r0:r0 + RS, :] = _taps(dug_ref, wg_ref, K, 0, r0, RS, slice(None), reverse=True).astype(BF16)
            dv_ref[r0:r0 + RS, :] = _taps(duv_ref, wv_ref, K, 0, r0, RS, slice(None), reverse=True).astype(BF16)

    pb, nb_, nbb = _prev_blk(T, H), _next_blk(T, H, S), _next_blk(T, HB, S)
    cur = lambda off: pl.BlockSpec((T, tc), lambda j, i: (i, j + off))
    prev = lambda off: pl.BlockSpec((H, tc), lambda j, i: (pb(i), j + off))
    nxt = lambda off: pl.BlockSpec((H, tc), lambda j, i: (nb_(i), j + off))
    wsp = lambda off: pl.BlockSpec((K, tc), lambda j, i: (0, j + off))
    bsp = lambda off: pl.BlockSpec((1, tc), lambda j, i: (0, j + off))
    half = jax.ShapeDtypeStruct((S, Fd), BF16)
    dws = jax.ShapeDtypeStruct((8, Fd), F32)
    return pl.pallas_call(
        body, name=name, out_shape=(half, half, dws, dws), grid=(nJ, nI),
        in_specs=[cur(0), cur(nJ), prev(0), prev(nJ), nxt(0), nxt(nJ),
                  pl.BlockSpec((T, tc), lambda j, i: (i, j)), pl.BlockSpec((HB, tc), lambda j, i: (nbb(i), j)),
                  wsp(0), wsp(nJ), bsp(0), bsp(nJ)],
        out_specs=(pl.BlockSpec((T, tc), lambda j, i: (i, j)), pl.BlockSpec((T, tc), lambda j, i: (i, j)),
                   pl.BlockSpec((8, tc), lambda j, i: (0, j)), pl.BlockSpec((8, tc), lambda j, i: (0, j))),
        scratch_shapes=[pltpu.VMEM((H + T + H, tc), F32), pltpu.VMEM((H + T + H, tc), F32),
                        pltpu.VMEM((T + H, tc), F32), pltpu.VMEM((T + H, tc), F32)],
        compiler_params=_params("parallel", "arbitrary"))(hu, hu, hu, hu, hu, hu, dact, dact, w, w, b, b)


def _position():
    return lax.axis_index("x"), lax.axis_index("y"), lax.axis_index("c")


def _slot(px, py, pc):
    return 4 * px + 2 * py + pc


def _all_gather(x, in_vmem, name):
    space = pltpu.VMEM if in_vmem else pl.ANY

    def body(x_ref, out_ref, send_sems, recv_sems, local_sem):
        px, py, pc = _position()
        me, sibling = (px, py, pc), (px, py, 1 - pc)
        chips = [(1 - px, py), (px, 1 - py), (1 - px, 1 - py)]

        def copy(k, block, to, src=None):
            dst = out_ref.at[_slot(*block)]
            return pltpu.make_async_remote_copy(
                src_ref=dst if src is None else src, dst_ref=dst,
                send_sem=send_sems.at[k], recv_sem=recv_sems.at[k], device_id=to, device_id_type=MESH)

        mine = pltpu.make_async_copy(x_ref, out_ref.at[_slot(*me)], local_sem)
        mine.start()
        first = [copy(0, me, sibling, src=x_ref)]
        first += [copy(1 + n, me, (*chip, pc), src=x_ref) for n, chip in enumerate(chips)]
        for cp in first:
            cp.start()
        passed = [copy(4 + n, (*chip, pc), sibling) for n, chip in enumerate(chips)]
        for n, chip in enumerate(chips):
            copy(1 + n, (*chip, pc), me).wait_recv()
            passed[n].start()
        copy(0, sibling, me).wait_recv()
        for n, chip in enumerate(chips):
            copy(4 + n, (*chip, 1 - pc), me).wait_recv()
        for cp in first + passed:
            cp.wait_send()
        mine.wait()

    return pl.pallas_call(
        body, name=name, out_shape=jax.ShapeDtypeStruct((N_DEV,) + x.shape, x.dtype),
        in_specs=[pl.BlockSpec(memory_space=space)], out_specs=pl.BlockSpec(memory_space=space),
        scratch_shapes=[pltpu.SemaphoreType.DMA((7,)), pltpu.SemaphoreType.DMA((7,)), pltpu.SemaphoreType.DMA],
        compiler_params=pltpu.CompilerParams(vmem_limit_bytes=VMEM_LIMIT_BYTES),
    )(x)


def _scatter_slabs(g, name):
    def body(g_ref, out_ref, send_sems, recv_sems, local_sem):
        px, py, pc = _position()
        me = _slot(px, py, pc)
        mine = pltpu.make_async_copy(g_ref.at[me], out_ref.at[me], local_sem)
        mine.start()
        peers = [(px ^ fx, py ^ fy, pc ^ fc) for fx, fy, fc in PEER_FLIPS]
        sends = []
        for k, peer in enumerate(peers):
            cp = pltpu.make_async_remote_copy(
                src_ref=g_ref.at[_slot(*peer)], dst_ref=out_ref.at[me],
                send_sem=send_sems.at[k], recv_sem=recv_sems.at[k], device_id=peer, device_id_type=MESH)
            cp.start()
            sends.append(cp)
        for k, peer in enumerate(peers):
            pltpu.make_async_remote_copy(
                src_ref=g_ref.at[me], dst_ref=out_ref.at[_slot(*peer)],
                send_sem=send_sems.at[k], recv_sem=recv_sems.at[k], device_id=peer, device_id_type=MESH).wait_recv()
        for cp in sends:
            cp.wait_send()
        mine.wait()

    return pl.pallas_call(
        body, name=name, out_shape=jax.ShapeDtypeStruct(g.shape, g.dtype),
        in_specs=[pl.BlockSpec(memory_space=pl.ANY)], out_specs=pl.BlockSpec(memory_space=pl.ANY),
        scratch_shapes=[pltpu.SemaphoreType.DMA((7,)), pltpu.SemaphoreType.DMA((7,)), pltpu.SemaphoreType.DMA],
        compiler_params=pltpu.CompilerParams(vmem_limit_bytes=VMEM_LIMIT_BYTES),
    )(g)


def _adam_sum(stage, w, m, v, layer, prev, name):
    n = stage.shape[0]
    L, R, C = w.shape
    tr = R if R * C <= 256 * 1024 else _row_tile(R, C)
    c1 = 1.0 / (1.0 - ADAM_B1 ** ADAM_STEP)
    c2 = 1.0 / (1.0 - ADAM_B2 ** ADAM_STEP)

    def body(*refs):
        st_ref, w_ref, m_ref, v_ref = refs[:4]
        g_ref, d_ref, nm_ref, nv_ref = refs[-4:]
        g = st_ref[0]
        for s in range(1, n):
            g = g + st_ref[s]
        wv = w_ref[0]
        mn = ADAM_B1 * m_ref[0] + (1.0 - ADAM_B1) * g
        vn = ADAM_B2 * v_ref[0] + (1.0 - ADAM_B2) * (g * g)
        g_ref[0] = g
        nm_ref[0] = mn
        nv_ref[0] = vn
        d_ref[0] = -ADAM_LR * ((mn * c1) / (jnp.sqrt(vn * c2) + ADAM_EPS) + ADAM_WD * wv)

    lay = pl.BlockSpec((1, tr, C), lambda i: (layer, i, 0))
    in_specs = [pl.BlockSpec((n, tr, C), lambda i: (0, i, 0)), lay, lay, lay]
    ins = [stage, w, m, v]
    aliases = {}
    if prev is not None:
        in_specs += [pl.BlockSpec(memory_space=pl.ANY)] * 4
        ins += list(prev)
        aliases = {4: 0, 5: 1, 6: 2, 7: 3}
    shp = jax.ShapeDtypeStruct((L, R, C), F32)
    return pl.pallas_call(
        body, name=name, out_shape=(shp, shp, shp, shp), grid=(R // tr,),
        in_specs=in_specs, out_specs=(lay, lay, lay, lay), input_output_aliases=aliases,
        compiler_params=_params("parallel"))(*ins)


def _row_tile(R, C):
    cpad = -(-C // LANES) * LANES
    want = max(8, (256 * 1024) // cpad)
    best = 8
    for t in range(8, R + 1, 8):
        if R % t == 0 and t <= want:
            best = t
    return best


def _sum_slabs(st, name):
    n, R, C = st.shape
    tr = R if n * R * C * 4 <= (12 << 20) else _row_tile(R, C)

    def body(st_ref, o_ref):
        g = st_ref[0]
        for s in range(1, n):
            g = g + st_ref[s]
        o_ref[...] = g

    return pl.pallas_call(
        body, name=name, out_shape=jax.ShapeDtypeStruct((R, C), F32), grid=(R // tr,),
        in_specs=[pl.BlockSpec((n, tr, C), lambda i: (0, i, 0))], out_specs=pl.BlockSpec((tr, C), lambda i: (i, 0)),
        compiler_params=_params("parallel"))(st)


def _pack(arrs):
    flat = [a.reshape(-1).astype(F32) for a in arrs]
    sizes = [f.shape[0] for f in flat]
    total = sum(sizes)
    padded = -(-total // (8 * LANES)) * (8 * LANES)
    if padded > total:
        flat.append(jnp.zeros((padded - total,), F32))
    return jnp.concatenate(flat).reshape(padded // LANES, LANES), (sizes, [a.shape for a in arrs])


def _unpack(packed, layout, lead=()):
    sizes, shapes = layout
    flat = packed.reshape(lead + (-1,))
    out, off = [], 0
    for sz, shp in zip(sizes, shapes):
        out.append(flat[..., off:off + sz].reshape(lead + tuple(shp)))
        off += sz
    return out


def _local_step(x, tgt, ada, mix_norm_g, wts, b_forget, conf_dw_w, conf_dw_b, conf_ln_g, conf_ln_b, sc_dw_w,
                ffn_norm_g, ffn_dw_w, ffn_dw_b, final_norm_g):
    S, D = x.shape
    L = ada.shape[0]
    H = b_forget.shape[1]
    DA = H * HEAD_DIM
    C = conf_dw_b.shape[1]
    NQ = 3 * DA
    NR = 5 * C + LANES
    fblk = (5 * C) // LANES
    row = lambda a: a.reshape(1, -1)
    adav = ada.reshape(L, N_ADA, 1, D)

    saved = []
    xcur, delta, gate = x, None, None
    for l in range(L):
        sh_m, sc_m, g_m, sh_f, sc_f, g_f = [adav[l, n] for n in range(N_ADA)]
        w = wts[l]
        x1, h1 = _site_fwd(xcur, delta, gate, row(mix_norm_g[l]), sc_m, sh_m, name=f"site_fwd_mix")
        qkv = _matmul(h1, w["w_in_perm"][:, :NQ], BF16, name="mm_qkv")
        rest = _matmul(h1, w["w_in_perm"][:, NQ:], F32, name="mm_rest")
        bpad = jnp.zeros((1, LANES), F32).at[0, :H].set(b_forget[l])
        Fc = _fgate_fwd(rest, bpad, fblk, name="fgate_fwd")
        nf = -jnp.transpose(Fc[:, :H])
        attn, attn32, lse = _attn_fwd(qkv, nf.reshape(H, 1, S), H, name="attn_fwd")
        conf = _conf_fwd(rest, conf_dw_w[l], row(conf_dw_b[l]), row(conf_ln_g[l]), row(conf_ln_b[l]), name="conf_fwd")
        sconv = _sconv_fwd(rest, sc_dw_w[l], name="sconv_fwd")
        mixcat = jnp.concatenate([attn, conf, sconv], axis=1)
        mixed = _matmul(mixcat, w["w_out"], F32, name="mm_out")
        x2, h2 = _site_fwd(x1, mixed, g_m, row(ffn_norm_g[l]), sc_f, sh_f, name="site_fwd_ffn")
        hu = _matmul(h2, w["w_up"], F32, name="mm_up")
        act = _ffn_fwd(hu, ffn_dw_w[l], row(ffn_dw_b[l]), name="ffn_fwd")
        ffn_out = _matmul(act, w["w_down"], F32, name="mm_down")
        saved.append(dict(x1=x1, h1=h1, qkv=qkv, rest=rest, bpad=bpad, nf=nf, attn32=attn32, lse=lse, mixcat=mixcat,
                          mixed=mixed, x2=x2, h2=h2, hu=hu, act=act, ffn_out=ffn_out))
        xcur, delta, gate = x2, ffn_out, g_f

    loss_lanes, dx, d_delta, d_gate, d_gfin = _final_fwd_bwd(xcur, delta, gate, row(final_norm_g), tgt, name="final")
    loss = (0.5 / D) * jnp.sum(loss_lanes)

    grads = dict(final_norm_g=d_gfin[0], ada=[None] * L, mix_norm_g=[None] * L, ffn_norm_g=[None] * L,
                 b_forget=[None] * L, conf_dw_w=[None] * L, conf_dw_b=[None] * L, conf_ln_g=[None] * L,
                 conf_ln_b=[None] * L, sc_dw_w=[None] * L, ffn_dw_w=[None] * L, ffn_dw_b=[None] * L,
                 w_in_perm=[None] * L, w_out=[None] * L, w_up=[None] * L, w_down=[None] * L)
    Fd = wts[0]["w_down"].shape[0]
    K3 = ffn_dw_w.shape[1]
    for l in reversed(range(L)):
        sv, w = saved[l], wts[l]
        sh_m, sc_m, g_m, sh_f, sc_f, g_f = [adav[l, n] for n in range(N_ADA)]
        d_gf = d_gate
        grads["w_down"][l] = _matmul(jnp.transpose(sv["act"]), d_delta, F32, name="mm_dw_down")
        dact = _matmul(d_delta, jnp.transpose(w["w_down"]), BF16, name="mm_dact")
        dhu_g, dhu_v, dwg, dwv = _ffn_bwd(sv["hu"], dact, ffn_dw_w[l], row(ffn_dw_b[l]), name="ffn_bwd")
        grads["ffn_dw_w"][l] = jnp.concatenate([dwg[:K3], dwv[:K3]], axis=1)
        grads["ffn_dw_b"][l] = jnp.concatenate([dwg[K3], dwv[K3]])
        dhu = jnp.concatenate([dhu_g, dhu_v], axis=1)
        grads["w_up"][l] = _matmul(jnp.transpose(sv["h2"]), dhu, F32, name="mm_dw_up")
        dh2 = _matmul(dhu, jnp.transpose(w["w_up"]), F32, name="mm_dh2")
        dx, d_sh_f, d_a_f, d_mixed, d_gm = _site_bwd(sv["x2"], dh2, dx, row(ffn_norm_g[l]), sc_f,
                                                      sh_f, sv["mixed"], g_m, name="site_bwd_ffn")
        grads["ffn_norm_g"][l] = (d_a_f * (1.0 + sc_f))[0]
        d_sc_f = d_a_f * row(ffn_norm_g[l])
        grads["w_out"][l] = _matmul(jnp.transpose(sv["mixcat"]), d_mixed, F32, name="mm_dw_out")
        w_out_t = jnp.transpose(w["w_out"])
        dattn = _matmul(d_mixed, w_out_t[:, :DA], BF16, name="mm_dattn")
        dcs = _matmul(d_mixed, w_out_t[:, DA:], F32, name="mm_dcs")
        delta_a = _attn_delta(sv["attn32"], dattn, H, name="attn_delta")
        nf = sv["nf"]
        dq, drow = _attn_bwd_dq(sv["qkv"], nf.reshape(H, 1, S), dattn, sv["lse"], delta_a, H, name="attn_bwd_dq")
        dk, dv, dnf = _attn_bwd_dkv(sv["qkv"], nf.reshape(H, S, 1), dattn, sv["lse"].reshape(H, 1, S),
                                    delta_a.reshape(H, 1, S), H, name="attn_bwd_dkv")
        dF = jnp.zeros((S, LANES), F32).at[:, :H].set(jnp.transpose((drow - dnf)[:, :, 0]))
        dfl, dbf = _fgate_bwd(sv["rest"], sv["bpad"], dF, fblk, name="fgate_bwd")
        grads["b_forget"][l] = dbf[0, :H]
        dcvg, dcw, dcvec = _conf_bwd(sv["rest"], dcs, conf_dw_w[l], row(conf_dw_b[l]), row(conf_ln_g[l]),
                                     row(conf_ln_b[l]), name="conf_bwd")
        grads["conf_dw_w"][l] = dcw[:conf_dw_w.shape[1]]
        grads["conf_dw_b"][l], grads["conf_ln_g"][l], grads["conf_ln_b"][l] = dcvec[0], dcvec[1], dcvec[2]
        dsc3, dsw = _sconv_bwd(sv["rest"], dcs, sc_dw_w[l], name="sconv_bwd")
        grads["sc_dw_w"][l] = dsw[:sc_dw_w.shape[1]]
        dproj = jnp.concatenate([dq, dk, dv, dcvg, dsc3, dfl], axis=1)
        grads["w_in_perm"][l] = _matmul(jnp.transpose(sv["h1"]), dproj, F32, name="mm_dw_in")
        dh1 = _matmul(dproj, jnp.transpose(w["w_in_perm"]), F32, name="mm_dh1")
        if l > 0:
            pv = saved[l - 1]
            g_f_prev = adav[l - 1, 5]
            dx, d_sh_m, d_a_m, d_delta, d_gate = _site_bwd(sv["x1"], dh1, dx, row(mix_norm_g[l]), sc_m, sh_m,
                                                           pv["ffn_out"], g_f_prev, name="site_bwd_mix")
        else:
            dx, d_sh_m, d_a_m = _site_bwd(sv["x1"], dh1, dx, row(mix_norm_g[l]), sc_m, sh_m, None, None,
                                          name="site_bwd_first")
        grads["mix_norm_g"][l] = (d_a_m * (1.0 + sc_m))[0]
        d_sc_m = d_a_m * row(mix_norm_g[l])
        grads["ada"][l] = jnp.concatenate([d_sh_m, d_sc_m, d_gm, d_sh_f, d_sc_f, d_gf], axis=1)[0]
    return loss, dx, grads


def kernel(x, c, ada_w, ada_b, mix_norm_g, w_in, b_forget, conf_dw_w, conf_dw_b, conf_ln_g, conf_ln_b, sc_dw_w, w_out, ffn_norm_g, w_up, ffn_dw_w, ffn_dw_b, w_down, final_norm_g, loss_target, m_ada_w, m_ada_b, m_mix_norm_g, m_w_in, m_b_forget, m_conf_dw_w, m_conf_dw_b, m_conf_ln_g, m_conf_ln_b, m_sc_dw_w, m_w_out, m_ffn_norm_g, m_w_up, m_ffn_dw_w, m_ffn_dw_b, m_w_down, m_final_norm_g, v_ada_w, v_ada_b, v_mix_norm_g, v_w_in, v_b_forget, v_conf_dw_w, v_conf_dw_b, v_conf_ln_g, v_conf_ln_b, v_sc_dw_w, v_w_out, v_ffn_norm_g, v_w_up, v_ffn_dw_w, v_ffn_dw_b, v_w_down, v_final_norm_g):
    L, D, ada_loc = ada_w.shape
    S = x.shape[1]
    H = b_forget.shape[1]
    DA = H * HEAD_DIM
    C = conf_dw_b.shape[1]
    in_loc = w_in.shape[2]
    IN = in_loc * N_DEV
    px, py, pc = _position()
    me = _slot(px, py, pc)

    pk, lay = _pack([c, conf_dw_w, sc_dw_w, ffn_dw_w])
    gathered = _all_gather(pk, True, name="ag_small_fwd")
    c_all, cw_all, sw_all, fw_all = _unpack(gathered, lay, lead=(N_DEV,))
    c_all = c_all[:, 0]
    unshard = lambda a: jnp.moveaxis(a, 0, 2).reshape(a.shape[1], a.shape[2], -1)
    conf_w_full, sc_w_full, ffn_w_full = unshard(cw_all), unshard(sw_all), unshard(fw_all)
    c_act = c_all * jax.nn.sigmoid(c_all)
    c_act16 = jnp.zeros((16, D), F32).at[:N_DEV].set(c_act).astype(BF16)
    ada_cols = jnp.stack([_matmul(c_act16, ada_w[l].astype(BF16), F32, name="mm_ada")[:N_DEV] for l in range(L)])
    ada_g = _all_gather(ada_cols.reshape(L * N_DEV, ada_loc), True, name="ag_ada")
    ada_mine = lax.dynamic_index_in_dim(ada_g.reshape(N_DEV, L, N_DEV, ada_loc), me, axis=2, keepdims=False)
    ada = jnp.moveaxis(ada_mine, 0, 1).reshape(L, N_DEV * ada_loc) + ada_b

    g_in = _all_gather(w_in.astype(BF16), False, name="ag_w_in")
    g_out = _all_gather(w_out.astype(BF16), False, name="ag_w_out")
    g_up = _all_gather(w_up.astype(BF16), False, name="ag_w_up")
    g_down = _all_gather(w_down.astype(BF16), False, name="ag_w_down")
    NQ = 3 * DA
    wts = []
    for l in range(L):
        wi = jnp.moveaxis(g_in[:, l], 0, 1).reshape(D, IN)
        w_in_perm = jnp.concatenate([wi[:, :NQ], wi[:, NQ + H:], wi[:, NQ:NQ + H],
                                     jnp.zeros((D, LANES - H), BF16)], axis=1)
        wts.append(dict(w_in_perm=w_in_perm,
                        w_out=g_out[:, l].reshape(-1, D),
                        w_up=jnp.moveaxis(g_up[:, l], 0, 1).reshape(D, -1),
                        w_down=g_down[:, l].reshape(-1, D)))

    loss_loc, dx, gr = _local_step(x[0], loss_target[0], ada, mix_norm_g, wts, b_forget, conf_w_full, conf_dw_b,
                                   conf_ln_g, conf_ln_b, sc_w_full, ffn_norm_g, ffn_w_full, ffn_dw_b, final_norm_g)
    loss = lax.psum(loss_loc, ("x", "y", "c"))

    small_names = ["ada", "mix_norm_g", "ffn_norm_g", "b_forget", "conf_dw_b", "conf_ln_g", "conf_ln_b",
                   "ffn_dw_b", "conf_dw_w", "sc_dw_w", "ffn_dw_w"]
    pk, lay = _pack([jnp.stack(gr[n]) for n in small_names] + [gr["final_norm_g"]])
    parts = _all_gather(pk, True, name="ag_small_bwd")
    tot = _unpack(_sum_slabs(parts, name="sum_small"), lay)
    g_small = dict(zip(small_names + ["final_norm_g"], tot))
    d_ada_all = _unpack(parts, lay, lead=(N_DEV,))[0]
    my_cols = lambda a, n: lax.dynamic_slice_in_dim(a, me * n, n, axis=a.ndim - 1)

    c_act_t = jnp.zeros((D, LANES), F32).at[:, :N_DEV].set(jnp.transpose(c_act)).astype(BF16)
    res = None
    for l in range(L):
        d_loc = jnp.zeros((LANES, ada_loc), F32).at[:N_DEV].set(my_cols(d_ada_all[:, l], ada_loc)).astype(BF16)
        g_l = _matmul(c_act_t, d_loc, F32, name="mm_dada")
        res = _adam_sum(g_l[None], ada_w, m_ada_w, v_ada_w, l, res, name="adam_ada_w")
    out_ada_w = res

    def shard_cols(g):
        return jnp.moveaxis(g.reshape(g.shape[0], N_DEV, -1), 1, 0)

    def shard_rows(g):
        return g.reshape(N_DEV, -1, g.shape[1])

    big = {}
    for nm, wq, mq, vq in (("w_down", w_down, m_w_down, v_w_down), ("w_up", w_up, m_w_up, v_w_up),
                           ("w_out", w_out, m_w_out, v_w_out), ("w_in", w_in, m_w_in, v_w_in)):
        res = None
        for l in reversed(range(L)):
            if nm == "w_in":
                gp = gr["w_in_perm"][l]
                PR = NQ + 5 * C
                g_full = shard_cols(jnp.concatenate([gp[:, :NQ], gp[:, PR:PR + H], gp[:, NQ:PR]], axis=1))
            elif nm == "w_up":
                g_full = shard_cols(gr["w_up"][l])
            else:
                g_full = shard_rows(gr[nm][l])
            stage = _scatter_slabs(g_full, name="rs_" + nm)
            res = _adam_sum(stage, wq, mq, vq, l, res, name="adam_" + nm)
        big[nm] = res

    K31, K3 = conf_dw_w.shape[1], sc_dw_w.shape[1]
    sm = [("ada_b", ada_b, m_ada_b, v_ada_b, g_small["ada"]),
          ("mix_norm_g", mix_norm_g, m_mix_norm_g, v_mix_norm_g, g_small["mix_norm_g"]),
          ("b_forget", b_forget, m_b_forget, v_b_forget, g_small["b_forget"]),
          ("conf_dw_w", conf_dw_w, m_conf_dw_w, v_conf_dw_w, my_cols(g_small["conf_dw_w"], conf_dw_w.shape[2])),
          ("conf_dw_b", conf_dw_b, m_conf_dw_b, v_conf_dw_b, g_small["conf_dw_b"]),
          ("conf_ln_g", conf_ln_g, m_conf_ln_g, v_conf_ln_g, g_small["conf_ln_g"]),
          ("conf_ln_b", conf_ln_b, m_conf_ln_b, v_conf_ln_b, g_small["conf_ln_b"]),
          ("sc_dw_w", sc_dw_w, m_sc_dw_w, v_sc_dw_w, my_cols(g_small["sc_dw_w"], sc_dw_w.shape[2])),
          ("ffn_norm_g", ffn_norm_g, m_ffn_norm_g, v_ffn_norm_g, g_small["ffn_norm_g"]),
          ("ffn_dw_w", ffn_dw_w, m_ffn_dw_w, v_ffn_dw_w, my_cols(g_small["ffn_dw_w"], ffn_dw_w.shape[2])),
          ("ffn_dw_b", ffn_dw_b, m_ffn_dw_b, v_ffn_dw_b, g_small["ffn_dw_b"]),
          ("final_norm_g", final_norm_g, m_final_norm_g, v_final_norm_g, g_small["final_norm_g"])]
    pw, lay = _pack([t[1] for t in sm])
    pm, _ = _pack([t[2] for t in sm])
    pv, _ = _pack([t[3] for t in sm])
    pg, _ = _pack([t[4] for t in sm])
    sres = _adam_sum(pg[None], pw[None], pm[None], pv[None], 0, None, name="adam_small")
    s_g, s_d, s_m, s_v = [dict(zip([t[0] for t in sm], _unpack(r[0], lay))) for r in sres]

    def pick(idx, name):
        if name == "ada_w":
            return out_ada_w[idx]
        if name in big:
            return big[name][idx]
        return (s_g, s_d, s_m, s_v)[idx][name]

    order = ["ada_w", "ada_b", "mix_norm_g", "w_in", "b_forget", "conf_dw_w", "conf_dw_b", "conf_ln_g", "conf_ln_b",
             "sc_dw_w", "w_out", "ffn_norm_g", "w_up", "ffn_dw_w", "ffn_dw_b", "w_down", "final_norm_g"]
    outs = [loss, dx[None]]
    for idx in range(4):
        outs += [pick(idx, n) for n in order]
    return tuple(outs)
```

```python
import functools

import jax
import jax.numpy as jnp
from jax import lax
from jax.experimental import pallas as pl
from jax.experimental.pallas import tpu as pltpu

F32 = jnp.float32
BF16 = jnp.bfloat16
RMS_EPS = 1e-6
LN_EPS = 1e-5
HEAD_DIM = 128
N_ADA = 6
ADAM_LR = 0.001
ADAM_B1 = 0.9
ADAM_B2 = 0.999
ADAM_EPS = 1e-08
ADAM_WD = 0.01
ADAM_STEP = 10
N_DEV = 8
LANES = 128
VMEM_LIMIT_BYTES = 56 * 1024 * 1024
MESH = pl.DeviceIdType.MESH
PEER_FLIPS = ((0, 0, 1), (1, 0, 0), (0, 1, 0), (1, 1, 0), (1, 0, 1), (0, 1, 1), (1, 1, 1))


def _params(*sem):
    return pltpu.CompilerParams(dimension_semantics=sem, vmem_limit_bytes=VMEM_LIMIT_BYTES)


def _tile(n, cap):
    if n <= cap:
        return n
    for t in range(cap - cap % LANES, 0, -LANES):
        if n % t == 0:
            return t
    raise ValueError(f"no tile for {n}")


def _sigmoid(v):
    return jax.nn.sigmoid(v)


def _matmul(a, b, out_dtype, name):
    M, K = a.shape
    _, N = b.shape
    tm, tn = _tile(M, 1024), _tile(N, 1024)
    tk = K if K <= 2048 else _tile(K, 1024)
    nk = K // tk

    if nk == 1:
        def body(a_ref, b_ref, o_ref):
            o_ref[...] = jnp.dot(a_ref[...], b_ref[...], preferred_element_type=F32).astype(o_ref.dtype)
        scratch = []
    else:
        def body(a_ref, b_ref, o_ref, acc_ref):
            k = pl.program_id(2)

            @pl.when(k == 0)
            def _():
                acc_ref[...] = jnp.zeros_like(acc_ref)

            acc_ref[...] += jnp.dot(a_ref[...], b_ref[...], preferred_element_type=F32)

            @pl.when(k == nk - 1)
            def _():
                o_ref[...] = acc_ref[...].astype(o_ref.dtype)
        scratch = [pltpu.VMEM((tm, tn), F32)]

    return pl.pallas_call(
        body, name=name,
        out_shape=jax.ShapeDtypeStruct((M, N), out_dtype),
        grid=(M // tm, N // tn, nk),
        in_specs=[pl.BlockSpec((tm, tk), lambda i, j, k: (i, k)),
                  pl.BlockSpec((tk, tn), lambda i, j, k: (k, j))],
        out_specs=pl.BlockSpec((tm, tn), lambda i, j, k: (i, j)),
        scratch_shapes=scratch,
        compiler_params=_params("parallel", "parallel", "arbitrary"),
    )(a, b)


_TN = (((0,), (0,)), ((), ()))


def _matmul_tn(a, b, out_dtype, name):
    S, M = a.shape
    _, N = b.shape
    tm, tn, ts = _tile(M, 1024), _tile(N, 1024), _tile(S, 1024)
    ns = S // ts

    def body(a_ref, b_ref, o_ref, acc_ref):
        k = pl.program_id(2)

        @pl.when(k == 0)
        def _():
            acc_ref[...] = jnp.zeros_like(acc_ref)

        acc_ref[...] += lax.dot_general(a_ref[...], b_ref[...], _TN, preferred_element_type=F32)

        @pl.when(k == ns - 1)
        def _():
            o_ref[...] = acc_ref[...].astype(o_ref.dtype)

    return pl.pallas_call(
        body, name=name,
        out_shape=jax.ShapeDtypeStruct((M, N), out_dtype),
        grid=(M // tm, N // tn, ns),
        in_specs=[pl.BlockSpec((ts, tm), lambda i, j, k: (k, i)),
                  pl.BlockSpec((ts, tn), lambda i, j, k: (k, j))],
        out_specs=pl.BlockSpec((tm, tn), lambda i, j, k: (i, j)),
        scratch_shapes=[pltpu.VMEM((tm, tn), F32)],
        compiler_params=_params("parallel", "parallel", "arbitrary"),
    )(a, b)


def _site_fwd(x, delta, gate, g, sc, sh, name):
    S, D = x.shape
    T = min(256, S)
    res = delta is not None

    def body(*refs):
        if res:
            x_ref, d_ref, gate_ref, g_ref, sc_ref, sh_ref, xo_ref, h_ref = refs
            xv = x_ref[...] + gate_ref[...] * d_ref[...]
            xo_ref[...] = xv
        else:
            x_ref, g_ref, sc_ref, sh_ref, h_ref = refs
            xv = x_ref[...]
        r = lax.rsqrt(jnp.mean(xv * xv, axis=-1, keepdims=True) + RMS_EPS)
        a = g_ref[...] * (1.0 + sc_ref[...])
        h_ref[...] = (xv * r * a + sh_ref[...]).astype(BF16)

    row = pl.BlockSpec((T, D), lambda i: (i, 0))
    vec = pl.BlockSpec((1, D), lambda i: (0, 0))
    if res:
        ins, in_specs = (x, delta, gate, g, sc, sh), [row, row, vec, vec, vec, vec]
        out_shape = (jax.ShapeDtypeStruct((S, D), F32), jax.ShapeDtypeStruct((S, D), BF16))
        out_specs = (row, row)
    else:
        ins, in_specs = (x, g, sc, sh), [row, vec, vec, vec]
        out_shape = jax.ShapeDtypeStruct((S, D), BF16)
        out_specs = row
    out = pl.pallas_call(body, name=name, out_shape=out_shape, grid=(S // T,), in_specs=in_specs,
                         out_specs=out_specs, compiler_params=_params("parallel"))(*ins)
    return out if res else (x, out)


def _site_bwd(x, dh, dres, g, sc, sh, delta, gate, name):
    S, D = x.shape
    T = min(256, S)
    res = delta is not None

    def body(*refs):
        if res:
            (x_ref, dh_ref, dres_ref, g_ref, sc_ref, delta_ref, gate_ref,
             dx_ref, dsh_ref, da_ref, dd_ref, dgate_ref) = refs
        else:
            x_ref, dh_ref, dres_ref, g_ref, sc_ref, dx_ref, dsh_ref, da_ref = refs
        i = pl.program_id(0)
        xv = x_ref[...]
        dhv = dh_ref[...]
        r = lax.rsqrt(jnp.mean(xv * xv, axis=-1, keepdims=True) + RMS_EPS)
        xh = xv * r
        dxh = dhv * (g_ref[...] * (1.0 + sc_ref[...]))
        dx = r * (dxh - xh * jnp.mean(dxh * xh, axis=-1, keepdims=True)) + dres_ref[...]
        dx_ref[...] = dx

        @pl.when(i == 0)
        def _():
            dsh_ref[...] = jnp.zeros_like(dsh_ref)
            da_ref[...] = jnp.zeros_like(da_ref)
            if res:
                dgate_ref[...] = jnp.zeros_like(dgate_ref)

        dsh_ref[...] += jnp.sum(dhv, axis=0, keepdims=True)
        da_ref[...] += jnp.sum(dhv * xh, axis=0, keepdims=True)
        if res:
            dd_ref[...] = (gate_ref[...] * dx).astype(BF16)
            dgate_ref[...] += jnp.sum(dx * delta_ref[...], axis=0, keepdims=True)

    row = pl.BlockSpec((T, D), lambda i: (i, 0))
    vec = pl.BlockSpec((1, D), lambda i: (0, 0))
    vshape = jax.ShapeDtypeStruct((1, D), F32)
    if res:
        ins, in_specs = (x, dh, dres, g, sc, delta, gate), [row, row, row, vec, vec, row, vec]
        out_shape = (jax.ShapeDtypeStruct((S, D), F32), vshape, vshape, jax.ShapeDtypeStruct((S, D), BF16), vshape)
        out_specs = (row, vec, vec, row, vec)
    else:
        ins, in_specs = (x, dh, dres, g, sc), [row, row, row, vec, vec]
        out_shape = (jax.ShapeDtypeStruct((S, D), F32), vshape, vshape)
        out_specs = (row, vec, vec)
    return pl.pallas_call(body, name=name, out_shape=out_shape, grid=(S // T,), in_specs=in_specs,
                          out_specs=out_specs, compiler_params=_params("arbitrary"))(*ins)


def _final_fwd_bwd(x, delta, gate, gfin, target, name):
    S, D = x.shape
    T = min(256, S)

    def body(x_ref, delta_ref, gate_ref, g_ref, t_ref, loss_ref, dx_ref, dd_ref, dgate_ref, dg_ref):
        i = pl.program_id(0)
        dl = delta_ref[...]
        xv = x_ref[...] + gate_ref[...] * dl
        r = lax.rsqrt(jnp.mean(xv * xv, axis=-1, keepdims=True) + RMS_EPS)
        xh = xv * r
        gv = g_ref[...]
        e = xh * gv - t_ref[...]
        dy = e * (1.0 / D)
        dxh = dy * gv
        dx = r * (dxh - xh * jnp.mean(dxh * xh, axis=-1, keepdims=True))
        dx_ref[...] = dx
        dd_ref[...] = (gate_ref[...] * dx).astype(BF16)

        @pl.when(i == 0)
        def _():
            loss_ref[...] = jnp.zeros_like(loss_ref)
            dgate_ref[...] = jnp.zeros_like(dgate_ref)
            dg_ref[...] = jnp.zeros_like(dg_ref)

        loss_ref[...] += jnp.sum(e * e, axis=0, keepdims=True)
        dgate_ref[...] += jnp.sum(dx * dl, axis=0, keepdims=True)
        dg_ref[...] += jnp.sum(dy * xh, axis=0, keepdims=True)

    row = pl.BlockSpec((T, D), lambda i: (i, 0))
    vec = pl.BlockSpec((1, D), lambda i: (0, 0))
    vshape = jax.ShapeDtypeStruct((1, D), F32)
    return pl.pallas_call(
        body, name=name,
        out_shape=(vshape, jax.ShapeDtypeStruct((S, D), F32), jax.ShapeDtypeStruct((S, D), BF16), vshape, vshape),
        grid=(S // T,), in_specs=[row, row, vec, vec, row], out_specs=(vec, row, row, vec, vec),
        compiler_params=_params("arbitrary"))(x, delta, gate, gfin, target)


def _split3(v):
    hi = v.astype(BF16)
    r1 = v - hi.astype(F32)
    mid = r1.astype(BF16)
    lo = (r1 - mid.astype(F32)).astype(BF16)
    return hi, mid, lo


def _tri_dot(tri, v):
    hi, mid, lo = _split3(v)
    d = functools.partial(jnp.dot, preferred_element_type=F32)
    return d(tri, hi) + d(tri, mid) + d(tri, lo)


def _fgate_fwd(rest, bpad, fblk, name):
    S = rest.shape[0]
    CH = min(256, S)
    nch = S // CH

    def body(f_ref, b_ref, o_ref):
        row = lax.broadcasted_iota(jnp.int32, (CH, CH), 0)
        col = lax.broadcasted_iota(jnp.int32, (CH, CH), 1)
        tri = (row >= col).astype(BF16)

        def step(ci, carry):
            rows = pl.ds(pl.multiple_of(ci * CH, CH), CH)
            z = f_ref[rows, :] + b_ref[...]
            lf = jnp.minimum(z, 0.0) - jnp.log(1.0 + jnp.exp(-jnp.abs(z)))
            o_ref[rows, :] = _tri_dot(tri, lf) + carry
            return carry + jnp.sum(lf, axis=0, keepdims=True)

        lax.fori_loop(0, nch, step, jnp.zeros((1, LANES), F32))

    return pl.pallas_call(
        body, name=name, out_shape=jax.ShapeDtypeStruct((S, LANES), F32), grid=(1,),
        in_specs=[pl.BlockSpec((S, LANES), lambda i: (0, fblk)), pl.BlockSpec((1, LANES), lambda i: (0, 0))],
        out_specs=pl.BlockSpec((S, LANES), lambda i: (0, 0)),
        compiler_params=_params("arbitrary"))(rest, bpad)


def _fgate_bwd(rest, bpad, dF, fblk, name):
    S = rest.shape[0]
    CH = min(256, S)
    nch = S // CH

    def body(f_ref, b_ref, df_ref, o_ref, db_ref):
        row = lax.broadcasted_iota(jnp.int32, (CH, CH), 0)
        col = lax.broadcasted_iota(jnp.int32, (CH, CH), 1)
        tri = (col >= row).astype(BF16)

        def step(n, carry):
            sfx_carry, db = carry
            ci = nch - 1 - n
            rows = pl.ds(pl.multiple_of(ci * CH, CH), CH)
            z = f_ref[rows, :] + b_ref[...]
            dfv = df_ref[rows, :]
            dz = (_tri_dot(tri, dfv) + sfx_carry) * _sigmoid(-z)
            o_ref[rows, :] = dz.astype(BF16)
            return sfx_carry + jnp.sum(dfv, axis=0, keepdims=True), db + jnp.sum(dz, axis=0, keepdims=True)

        zero = jnp.zeros((1, LANES), F32)
        _, db = lax.fori_loop(0, nch, step, (zero, zero))
        db_ref[...] = db

    blk = pl.BlockSpec((S, LANES), lambda i: (0, 0))
    return pl.pallas_call(
        body, name=name,
        out_shape=(jax.ShapeDtypeStruct((S, LANES), BF16), jax.ShapeDtypeStruct((1, LANES), F32)), grid=(1,),
        in_specs=[pl.BlockSpec((S, LANES), lambda i: (0, fblk)), pl.BlockSpec((1, LANES), lambda i: (0, 0)), blk],
        out_specs=(blk, pl.BlockSpec((1, LANES), lambda i: (0, 0))),
        compiler_params=_params("arbitrary"))(rest, bpad, dF)


_NT = (((1,), (1,)), ((), ()))
LOG2E = 1.4426950408889634
ATTN_TILE = 512


def _blocked_rows(a, TA):
    H, S = a.shape
    return a.reshape(H, S // TA, 1, TA)


def _attn_fwd(qkv, nfb, H, name):
    S = qkv.shape[0]
    TA = min(ATTN_TILE, S)
    nb = S // TA
    c = HEAD_DIM ** -0.5 * LOG2E

    def body(q_ref, k_ref, v_ref, nf_ref, o_ref, o32_ref, lse_ref, m_ref, l_ref, acc_ref):
        i = pl.program_id(1)
        m_ref[...] = jnp.full_like(m_ref, -jnp.inf)
        l_ref[...] = jnp.zeros_like(l_ref)
        acc_ref[...] = jnp.zeros_like(acc_ref)

        def block(j, masked):
            rows = pl.ds(pl.multiple_of(j * TA, TA), TA)
            st = (lax.dot_general(k_ref[rows, :], q_ref[...], _NT, preferred_element_type=F32) * c
                  + jnp.tile(nf_ref[0, rows, :], (1, TA // LANES)))
            if masked:
                key = lax.broadcasted_iota(jnp.int32, (TA, TA), 0)
                qry = lax.broadcasted_iota(jnp.int32, (TA, TA), 1)
                st = jnp.where(key <= qry, st, -jnp.inf)
            m_old = m_ref[...]
            m_new = jnp.maximum(m_old, jnp.max(st, axis=0, keepdims=True))
            alpha = jnp.exp2(m_old - m_new)
            pt = jnp.exp2(st - m_new)
            l_ref[...] = alpha * l_ref[...] + jnp.sum(pt, axis=0, keepdims=True)
            acc_ref[...] = alpha * acc_ref[...] + lax.dot_general(v_ref[rows, :], pt.astype(BF16), _TN,
                                                                  preferred_element_type=F32)
            m_ref[...] = m_new

        def loop(j, carry):
            block(j, False)
            return carry

        lax.fori_loop(0, i, loop, 0)
        block(i, True)
        o = jnp.transpose(acc_ref[...] / l_ref[...])
        o32_ref[...] = o
        o_ref[...] = o.astype(BF16)
        lse_ref[0, 0] = m_ref[...] + jnp.log(l_ref[...]) * LOG2E

    qblk = pl.BlockSpec((TA, HEAD_DIM), lambda h, i: (i, h))
    return pl.pallas_call(
        body, name=name, grid=(H, nb),
        in_specs=[qblk,
                  pl.BlockSpec((S, HEAD_DIM), lambda h, i: (0, H + h)),
                  pl.BlockSpec((S, HEAD_DIM), lambda h, i: (0, 2 * H + h)),
                  pl.BlockSpec((1, S, LANES), lambda h, i: (h, 0, 0))],
        out_specs=(qblk, qblk, pl.BlockSpec((1, 1, 1, TA), lambda h, i: (h, i, 0, 0))),
        scratch_shapes=[pltpu.VMEM((1, TA), F32), pltpu.VMEM((1, TA), F32), pltpu.VMEM((HEAD_DIM, TA), F32)],
        out_shape=(jax.ShapeDtypeStruct((S, H * HEAD_DIM), BF16), jax.ShapeDtypeStruct((S, H * HEAD_DIM), F32),
                   jax.ShapeDtypeStruct((H, nb, 1, TA), F32)),
        compiler_params=_params("parallel", "parallel"))(qkv, qkv, qkv, nfb)


def _attn_delta(o, do, H, name):
    S = o.shape[0]
    T = min(512, S)

    def body(o_ref, do_ref, d_ref):
        d_ref[0] = jnp.sum(o_ref[...].astype(F32) * do_ref[...].astype(F32), axis=-1, keepdims=True)

    blk = pl.BlockSpec((T, HEAD_DIM), lambda h, i: (i, h))
    return pl.pallas_call(
        body, name=name, out_shape=jax.ShapeDtypeStruct((H, S, 1), F32), grid=(H, S // T),
        in_specs=[blk, blk], out_specs=pl.BlockSpec((1, T, 1), lambda h, i: (h, i, 0)),
        compiler_params=_params("parallel", "parallel"))(o, do)


def _ds_tile(k, q, v, do, nfb, lse_row, delta_row, c, masked):
    TK, TQ = k.shape[0], q.shape[0]
    st = lax.dot_general(k, q, _NT, preferred_element_type=F32) * c + jnp.tile(nfb, (1, TQ // LANES))
    pt = jnp.exp2(st - lse_row)
    if masked:
        key = lax.broadcasted_iota(jnp.int32, (TK, TQ), 0)
        qry = lax.broadcasted_iota(jnp.int32, (TK, TQ), 1)
        pt = jnp.where(key <= qry, pt, 0.0)
    dpt = lax.dot_general(v, do, _NT, preferred_element_type=F32)
    return pt, pt * (dpt - delta_row)


def _attn_bwd_dq(qkv, nfb, do, lse, delta, H, name):
    S = qkv.shape[0]
    TA = min(ATTN_TILE, S)
    nb = S // TA
    scale = HEAD_DIM ** -0.5
    c = scale * LOG2E

    def body(q_ref, k_ref, v_ref, nf_ref, do_ref, lse_ref, dl_ref, dq_ref, drow_ref, acc_ref, row_acc):
        i = pl.program_id(1)
        acc_ref[...] = jnp.zeros_like(acc_ref)
        row_acc[...] = jnp.zeros_like(row_acc)

        def block(j, masked):
            rows = pl.ds(pl.multiple_of(j * TA, TA), TA)
            kb = k_ref[rows, :]
            _, dst = _ds_tile(kb, q_ref[...], v_ref[rows, :], do_ref[...], nf_ref[0, rows, :],
                              lse_ref[0, 0], dl_ref[0, 0], c, masked)
            acc_ref[...] += lax.dot_general(kb, dst.astype(BF16), _TN, preferred_element_type=F32)
            row_acc[...] += jnp.sum(dst, axis=0, keepdims=True)

        def loop(j, carry):
            block(j, False)
            return carry

        lax.fori_loop(0, i, loop, 0)
        block(i, True)
        dq_ref[...] = (jnp.transpose(acc_ref[...]) * scale).astype(BF16)
        drow_ref[0, 0] = row_acc[...]

    qblk = pl.BlockSpec((TA, HEAD_DIM), lambda h, i: (i, h))
    row_stat = pl.BlockSpec((1, 1, 1, TA), lambda h, i: (h, i, 0, 0))
    return pl.pallas_call(
        body, name=name, grid=(H, nb),
        in_specs=[qblk,
                  pl.BlockSpec((S, HEAD_DIM), lambda h, i: (0, H + h)),
                  pl.BlockSpec((S, HEAD_DIM), lambda h, i: (0, 2 * H + h)),
                  pl.BlockSpec((1, S, LANES), lambda h, i: (h, 0, 0)),
                  qblk, row_stat, row_stat],
        out_specs=(qblk, row_stat),
        scratch_shapes=[pltpu.VMEM((HEAD_DIM, TA), F32), pltpu.VMEM((1, TA), F32)],
        out_shape=(jax.ShapeDtypeStruct((S, H * HEAD_DIM), BF16), jax.ShapeDtypeStruct((H, nb, 1, TA), F32)),
        compiler_params=_params("parallel", "parallel"))(qkv, qkv, qkv, nfb, do, lse, delta)


def _attn_bwd_dkv(qkv, nfb, do, lse, delta, H, name):
    S = qkv.shape[0]
    TA = min(ATTN_TILE, S)
    nb = S // TA
    scale = HEAD_DIM ** -0.5
    c = scale * LOG2E

    def body(q_ref, k_ref, v_ref, nf_ref, do_ref, lse_ref, dl_ref, dk_ref, dv_ref, dnf_ref, dk_acc, dv_acc, dnf_acc):
        j = pl.program_id(1)
        dk_acc[...] = jnp.zeros_like(dk_acc)
        dv_acc[...] = jnp.zeros_like(dv_acc)
        dnf_acc[...] = jnp.zeros_like(dnf_acc)

        def block(i, masked):
            rows = pl.ds(pl.multiple_of(i * TA, TA), TA)
            qb = q_ref[rows, :]
            dob = do_ref[rows, :]
            pt, dst = _ds_tile(k_ref[...], qb, v_ref[...], dob, nf_ref[0], lse_ref[0, i], dl_ref[0, i], c, masked)
            dv_acc[...] += jnp.dot(pt.astype(BF16), dob, preferred_element_type=F32)
            dk_acc[...] += jnp.dot(dst.astype(BF16), qb, preferred_element_type=F32)
            part = dst[:, 0:LANES]
            for t in range(1, TA // LANES):
                part = part + dst[:, t * LANES:(t + 1) * LANES]
            dnf_acc[...] += part

        def loop(i, carry):
            block(i, False)
            return carry

        block(j, True)
        lax.fori_loop(j + 1, nb, loop, 0)
        dk_ref[...] = (dk_acc[...] * scale).astype(BF16)
        dv_ref[...] = dv_acc[...].astype(BF16)
        dnf_ref[0] = jnp.sum(dnf_acc[...], axis=-1, keepdims=True)

    full = pl.BlockSpec((S, HEAD_DIM), lambda h, j: (0, h))
    row_stat = pl.BlockSpec((1, nb, 1, TA), lambda h, j: (h, 0, 0, 0))
    kblk = lambda c0: pl.BlockSpec((TA, HEAD_DIM), lambda h, j: (j, c0 + h))
    shp = jax.ShapeDtypeStruct((S, H * HEAD_DIM), BF16)
    return pl.pallas_call(
        body, name=name, grid=(H, nb),
        in_specs=[full, kblk(H), kblk(2 * H), pl.BlockSpec((1, TA, LANES), lambda h, j: (h, j, 0)), full,
                  row_stat, row_stat],
        out_specs=(kblk(0), kblk(0), pl.BlockSpec((1, TA, 1), lambda h, j: (h, j, 0))),
        scratch_shapes=[pltpu.VMEM((TA, HEAD_DIM), F32), pltpu.VMEM((TA, HEAD_DIM), F32), pltpu.VMEM((TA, LANES), F32)],
        out_shape=(shp, shp, jax.ShapeDtypeStruct((H, S, 1), F32)),
        compiler_params=_params("parallel", "parallel"))(qkv, qkv, qkv, nfb, do, lse, delta)


def _taps(ext_ref, w_ref, K, base, r0, rows, cols, reverse=False, init=None):
    acc = init
    for k in range(K):
        wk = w_ref[(K - 1 - k) if reverse else k:((K - 1 - k) if reverse else k) + 1, cols]
        term = wk * ext_ref[base + k + r0:base + k + r0 + rows, cols]
        acc = term if acc is None else acc + term
    return acc


def _prev_blk(T, H):
    return lambda i: jnp.maximum(i * (T // H) - 1, 0)


def _next_blk(T, H, S):
    return lambda i: jnp.minimum((i + 1) * (T // H), S // H - 1)


def _conf_fwd(rest, w, b, lng, lnb, name):
    S = rest.shape[0]
    K, C = w.shape
    H, T = 32, min(256, S)
    RS = min(64, T)
    base = H - (K - 1)

    def body(cv_ref, cg_ref, cvp_ref, cgp_ref, w_ref, b_ref, g_ref, bb_ref, o_ref, ext_ref):
        i = pl.program_id(0)
        ext_ref[0:H, :] = jnp.where(i > 0, cvp_ref[...] * _sigmoid(cgp_ref[...]), 0.0)
        ext_ref[H:H + T, :] = cv_ref[...] * _sigmoid(cg_ref[...])
        for r0 in range(0, T, RS):
            cc = _taps(ext_ref, w_ref, K, base, r0, RS, slice(None), init=jnp.broadcast_to(b_ref[...], (RS, C)))
            xc = cc - jnp.mean(cc, axis=-1, keepdims=True)
            y = xc * lax.rsqrt(jnp.mean(xc * xc, axis=-1, keepdims=True) + LN_EPS) * g_ref[...] + bb_ref[...]
            o_ref[r0:r0 + RS, :] = (y * _sigmoid(y)).astype(BF16)

    pb = _prev_blk(T, H)
    cur = lambda cb: pl.BlockSpec((T, C), lambda i: (i, cb))
    prev = lambda cb: pl.BlockSpec((H, C), lambda i: (pb(i), cb))
    full = lambda a: pl.BlockSpec(a.shape, lambda i: (0, 0))
    return pl.pallas_call(
        body, name=name, out_shape=jax.ShapeDtypeStruct((S, C), BF16), grid=(S // T,),
        in_specs=[cur(0), cur(1), prev(0), prev(1), full(w), full(b), full(lng), full(lnb)],
        out_specs=pl.BlockSpec((T, C), lambda i: (i, 0)),
        scratch_shapes=[pltpu.VMEM((H + T, C), F32)],
        compiler_params=_params("parallel"))(rest, rest, rest, rest, w, b, lng, lnb)


def _conf_bwd(rest, dcs, w, b, lng, lnb, name):
    S = rest.shape[0]
    K, C = w.shape
    H, T = 32, min(256, S)
    RS = 32
    nI = S // T
    base = H - (K - 1)

    def body(cv_ref, cg_ref, cvp_ref, cgp_ref, cvn_ref, cgn_ref, do_ref, don_ref, w_ref, b_ref, g_ref, bb_ref,
             dcvg_ref, dw_ref, dvec_ref, ext_ref, dcc_ref):
        i = pl.program_id(0)
        ext_ref[0:H, :] = jnp.where(i > 0, cvp_ref[...] * _sigmoid(cgp_ref[...]), 0.0)
        ext_ref[H:H + T, :] = cv_ref[...] * _sigmoid(cg_ref[...])
        ext_ref[H + T:H + T + H, :] = cvn_ref[...] * _sigmoid(cgn_ref[...])

        @pl.when(i == 0)
        def _():
            dw_ref[...] = jnp.zeros_like(dw_ref)
            dvec_ref[...] = jnp.zeros_like(dvec_ref)

        db = jnp.zeros((1, C), F32)
        dg = jnp.zeros((1, C), F32)
        dbb = jnp.zeros((1, C), F32)
        for r0 in range(0, T + H, RS):
            cc = _taps(ext_ref, w_ref, K, base, r0, RS, slice(None), init=jnp.broadcast_to(b_ref[...], (RS, C)))
            xc = cc - jnp.mean(cc, axis=-1, keepdims=True)
            r = lax.rsqrt(jnp.mean(xc * xc, axis=-1, keepdims=True) + LN_EPS)
            xh = xc * r
            y = xh * g_ref[...] + bb_ref[...]
            sy = _sigmoid(y)
            if r0 < T:
                d_o = do_ref[r0:r0 + RS, :]
            else:
                d_o = jnp.where(i < nI - 1, don_ref[r0 - T:r0 - T + RS, :], 0.0)
            dy = d_o * (sy * (1.0 + y * (1.0 - sy)))
            dxh = dy * g_ref[...]
            dcc = r * (dxh - jnp.mean(dxh, axis=-1, keepdims=True)
                       - xh * jnp.mean(dxh * xh, axis=-1, keepdims=True))
            dcc_ref[r0:r0 + RS, :] = dcc
            if r0 < T:
                dbb = dbb + jnp.sum(dy, axis=0, keepdims=True)
                dg = dg + jnp.sum(dy * xh, axis=0, keepdims=True)
                db = db + jnp.sum(dcc, axis=0, keepdims=True)
        dvec_ref[0:1, :] += db
        dvec_ref[1:2, :] += dg
        dvec_ref[2:3, :] += dbb
        R2 = min(64, T)
        for k in range(K):
            s = jnp.zeros((1, C), F32)
            for r0 in range(0, T, R2):
                s = s + jnp.sum(dcc_ref[r0:r0 + R2, :] * ext_ref[base + k + r0:base + k + r0 + R2, :],
                                axis=0, keepdims=True)
            dw_ref[k:k + 1, :] += s
        for r0 in range(0, T, R2):
            dci = _taps(dcc_ref, w_ref, K, 0, r0, R2, slice(None), reverse=True)
            cvv = cv_ref[r0:r0 + R2, :]
            sg = _sigmoid(cg_ref[r0:r0 + R2, :])
            dcvg_ref[r0:r0 + R2, 0:C] = (dci * sg).astype(BF16)
            dcvg_ref[r0:r0 + R2, C:2 * C] = (dci * cvv * sg * (1.0 - sg)).astype(BF16)

    pb, nb_ = _prev_blk(T, H), _next_blk(T, H, S)
    cur = lambda cb: pl.BlockSpec((T, C), lambda i: (i, cb))
    prev = lambda cb: pl.BlockSpec((H, C), lambda i: (pb(i), cb))
    nxt = lambda cb: pl.BlockSpec((H, C), lambda i: (nb_(i), cb))
    full = lambda a: pl.BlockSpec(a.shape, lambda i: (0, 0))
    return pl.pallas_call(
        body, name=name,
        out_shape=(jax.ShapeDtypeStruct((S, 2 * C), BF16), jax.ShapeDtypeStruct((32, C), F32),
                   jax.ShapeDtypeStruct((8, C), F32)),
        grid=(nI,),
        in_specs=[cur(0), cur(1), prev(0), prev(1), nxt(0), nxt(1), cur(0), nxt(0),
                  full(w), full(b), full(lng), full(lnb)],
        out_specs=(pl.BlockSpec((T, 2 * C), lambda i: (i, 0)), pl.BlockSpec((32, C), lambda i: (0, 0)),
                   pl.BlockSpec((8, C), lambda i: (0, 0))),
        scratch_shapes=[pltpu.VMEM((H + T + H, C), F32), pltpu.VMEM((T + H, C), F32)],
        compiler_params=_params("arbitrary"))(rest, rest, rest, rest, rest, rest, dcs, dcs, w, b, lng, lnb)


def _sconv_fwd(rest, w, name):
    S = rest.shape[0]
    K, C = w.shape
    H, T = 8, min(256, S)
    RS = min(64, T)
    base = H - (K - 1)

    def body(sx_ref, sb_ref, sc_ref, sxp_ref, scp_ref, w_ref, o_ref, ext_ref):
        i = pl.program_id(0)
        ext_ref[0:H, :] = jnp.where(i > 0, sxp_ref[...] * scp_ref[...], 0.0)
        ext_ref[H:H + T, :] = sx_ref[...] * sc_ref[...]
        for r0 in range(0, T, RS):
            cz = _taps(ext_ref, w_ref, K, base, r0, RS, slice(None))
            o_ref[r0:r0 + RS, :] = (sb_ref[r0:r0 + RS, :] * cz).astype(BF16)

    pb = _prev_blk(T, H)
    cur = lambda cb: pl.BlockSpec((T, C), lambda i: (i, cb))
    prev = lambda cb: pl.BlockSpec((H, C), lambda i: (pb(i), cb))
    return pl.pallas_call(
        body, name=name, out_shape=jax.ShapeDtypeStruct((S, C), BF16), grid=(S // T,),
        in_specs=[cur(2), cur(3), cur(4), prev(2), prev(4), pl.BlockSpec(w.shape, lambda i: (0, 0))],
        out_specs=pl.BlockSpec((T, C), lambda i: (i, 0)),
        scratch_shapes=[pltpu.VMEM((H + T, C), F32)],
        compiler_params=_params("parallel"))(rest, rest, rest, rest, rest, w)


def _sconv_bwd(rest, dcs, w, name):
    S = rest.shape[0]
    K, C = w.shape
    H, T = 8, min(256, S)
    RS = min(64, T)
    nI = S // T
    base = H - (K - 1)

    def body(sx_ref, sb_ref, sc_ref, sxp_ref, scp_ref, sbn_ref, do_ref, don_ref, w_ref,
             dout_ref, dw_ref, ext_ref, dcv_ref):
        i = pl.program_id(0)
        ext_ref[0:H, :] = jnp.where(i > 0, sxp_ref[...] * scp_ref[...], 0.0)
        ext_ref[H:H + T, :] = sx_ref[...] * sc_ref[...]
        dcv_ref[0:T, :] = do_ref[...] * sb_ref[...]
        dcv_ref[T:T + H, :] = jnp.where(i < nI - 1, don_ref[...] * sbn_ref[...], 0.0)

        @pl.when(i == 0)
        def _():
            dw_ref[...] = jnp.zeros_like(dw_ref)

        for k in range(K):
            s = jnp.zeros((1, C), F32)
            for r0 in range(0, T, RS):
                s = s + jnp.sum(dcv_ref[r0:r0 + RS, :] * ext_ref[base + k + r0:base + k + r0 + RS, :],
                                axis=0, keepdims=True)
            dw_ref[k:k + 1, :] += s
        for r0 in range(0, T, RS):
            cz = _taps(ext_ref, w_ref, K, base, r0, RS, slice(None))
            dz = _taps(dcv_ref, w_ref, K, 0, r0, RS, slice(None), reverse=True)
            dout_ref[r0:r0 + RS, 0:C] = (dz * sc_ref[r0:r0 + RS, :]).astype(BF16)
            dout_ref[r0:r0 + RS, C:2 * C] = (do_ref[r0:r0 + RS, :] * cz).astype(BF16)
            dout_ref[r0:r0 + RS, 2 * C:3 * C] = (dz * sx_ref[r0:r0 + RS, :]).astype(BF16)

    pb, nb_ = _prev_blk(T, H), _next_blk(T, H, S)
    cur = lambda cb: pl.BlockSpec((T, C), lambda i: (i, cb))
    prev = lambda cb: pl.BlockSpec((H, C), lambda i: (pb(i), cb))
    nxt = lambda cb: pl.BlockSpec((H, C), lambda i: (nb_(i), cb))
    return pl.pallas_call(
        body, name=name,
        out_shape=(jax.ShapeDtypeStruct((S, 3 * C), BF16), jax.ShapeDtypeStruct((8, C), F32)),
        grid=(nI,),
        in_specs=[cur(2), cur(3), cur(4), prev(2), prev(4), nxt(3), cur(1), nxt(1),
                  pl.BlockSpec(w.shape, lambda i: (0, 0))],
        out_specs=(pl.BlockSpec((T, 3 * C), lambda i: (i, 0)), pl.BlockSpec((8, C), lambda i: (0, 0))),
        scratch_shapes=[pltpu.VMEM((H + T, C), F32), pltpu.VMEM((T + H, C), F32)],
        compiler_params=_params("arbitrary"))(rest, rest, rest, rest, rest, rest, dcs, dcs, w)


def _ffn_fwd(hu, w, b, name):
    S, F2 = hu.shape
    Fd = F2 // 2
    K = w.shape[0]
    H, T = 8, min(256, S)
    tc = _tile(Fd, 512)
    nJ = Fd // tc
    RS = min(64, T)
    base = H - (K - 1)

    def body(g_ref, v_ref, gp_ref, vp_ref, wg_ref, wv_ref, bg_ref, bv_ref, o_ref, eg_ref, ev_ref):
        i = pl.program_id(1)
        eg_ref[0:H, :] = jnp.where(i > 0, gp_ref[...], 0.0)
        ev_ref[0:H, :] = jnp.where(i > 0, vp_ref[...], 0.0)
        eg_ref[H:H + T, :] = g_ref[...]
        ev_ref[H:H + T, :] = v_ref[...]
        for r0 in range(0, T, RS):
            ug = _taps(eg_ref, wg_ref, K, base, r0, RS, slice(None), init=jnp.broadcast_to(bg_ref[...], (RS, tc)))
            uv = _taps(ev_ref, wv_ref, K, base, r0, RS, slice(None), init=jnp.broadcast_to(bv_ref[...], (RS, tc)))
            o_ref[r0:r0 + RS, :] = (ug * _sigmoid(ug) * uv).astype(BF16)

    pb = _prev_blk(T, H)
    cur = lambda off: pl.BlockSpec((T, tc), lambda j, i: (i, j + off))
    prev = lambda off: pl.BlockSpec((H, tc), lambda j, i: (pb(i), j + off))
    wsp = lambda off: pl.BlockSpec((K, tc), lambda j, i: (0, j + off))
    bsp = lambda off: pl.BlockSpec((1, tc), lambda j, i: (0, j + off))
    return pl.pallas_call(
        body, name=name, out_shape=jax.ShapeDtypeStruct((S, Fd), BF16), grid=(nJ, S // T),
        in_specs=[cur(0), cur(nJ), prev(0), prev(nJ), wsp(0), wsp(nJ), bsp(0), bsp(nJ)],
        out_specs=pl.BlockSpec((T, tc), lambda j, i: (i, j)),
        scratch_shapes=[pltpu.VMEM((H + T, tc), F32), pltpu.VMEM((H + T, tc), F32)],
        compiler_params=_params("parallel", "parallel"))(hu, hu, hu, hu, w, w, b, b)


def _ffn_bwd(hu, dact, w, b, name):
    S, F2 = hu.shape
    Fd = F2 // 2
    K = w.shape[0]
    H, T = 8, min(256, S)
    HB = 16
    tc = _tile(Fd, 512)
    nJ = Fd // tc
    nI = S // T
    RS = min(64, T)
    base = H - (K - 1)

    def body(g_ref, v_ref, gp_ref, vp_ref, gn_ref, vn_ref, da_ref, dan_ref, wg_ref, wv_ref, bg_ref, bv_ref,
             dg_ref, dv_ref, dwg_ref, dwv_ref, eg_ref, ev_ref, dug_ref, duv_ref):
        i = pl.program_id(1)
        eg_ref[0:H, :] = jnp.where(i > 0, gp_ref[...], 0.0)
        ev_ref[0:H, :] = jnp.where(i > 0, vp_ref[...], 0.0)
        eg_ref[H:H + T, :] = g_ref[...]
        ev_ref[H:H + T, :] = v_ref[...]
        eg_ref[H + T:H + T + H, :] = gn_ref[...]
        ev_ref[H + T:H + T + H, :] = vn_ref[...]

        @pl.when(i == 0)
        def _():
            dwg_ref[...] = jnp.zeros_like(dwg_ref)
            dwv_ref[...] = jnp.zeros_like(dwv_ref)

        def du_rows(r0, rows, d_a):
            ug = _taps(eg_ref, wg_ref, K, base, r0, rows, slice(None),
                       init=jnp.broadcast_to(bg_ref[...], (rows, tc)))
            uv = _taps(ev_ref, wv_ref, K, base, r0, rows, slice(None),
                       init=jnp.broadcast_to(bv_ref[...], (rows, tc)))
            sg = _sigmoid(ug)
            dug_ref[r0:r0 + rows, :] = d_a * uv * (sg * (1.0 + ug * (1.0 - sg)))
            duv_ref[r0:r0 + rows, :] = d_a * (ug * sg)

        for r0 in range(0, T, RS):
            du_rows(r0, RS, da_ref[r0:r0 + RS, :].astype(F32))
        du_rows(T, H, jnp.where(i < nI - 1, dan_ref[...].astype(F32)[0:H, :], 0.0))

        for ext, du, dw in ((eg_ref, dug_ref, dwg_ref), (ev_ref, duv_ref, dwv_ref)):
            for k in range(K):
                s = jnp.zeros((1, tc), F32)
                for r0 in range(0, T, RS):
                    s = s + jnp.sum(du[r0:r0 + RS, :] * ext[base + k + r0:base + k + r0 + RS, :],
                                    axis=0, keepdims=True)
                dw[k:k + 1, :] += s
            s = jnp.zeros((1, tc), F32)
            for r0 in range(0, T, RS):
                s = s + jnp.sum(du[r0:r0 + RS, :], axis=0, keepdims=True)
            dw[K:K + 1, :] += s
        for r0 in range(0, T, RS):
            dg_ref[r0:r0 + RS, :] = _taps(dug_ref, wg_ref, K, 0, r0, RS, slice(None), reverse=True).astype(BF16)
            dv_ref[r0:r0 + RS, :] = _taps(duv_ref, wv_ref, K, 0, r0, RS, slice(None), reverse=True).astype(BF16)

    pb, nb_, nbb = _prev_blk(T, H), _next_blk(T, H, S), _next_blk(T, HB, S)
    cur = lambda off: pl.BlockSpec((T, tc), lambda j, i: (i, j + off))
    prev = lambda off: pl.BlockSpec((H, tc), lambda j, i: (pb(i), j + off))
    nxt = lambda off: pl.BlockSpec((H, tc), lambda j, i: (nb_(i), j + off))
    wsp = lambda off: pl.BlockSpec((K, tc), lambda j, i: (0, j + off))
    bsp = lambda off: pl.BlockSpec((1, tc), lambda j, i: (0, j + off))
    half = jax.ShapeDtypeStruct((S, Fd), BF16)
    dws = jax.ShapeDtypeStruct((8, Fd), F32)
    return pl.pallas_call(
        body, name=name, out_shape=(half, half, dws, dws), grid=(nJ, nI),
        in_specs=[cur(0), cur(nJ), prev(0), prev(nJ), nxt(0), nxt(nJ),
                  pl.BlockSpec((T, tc), lambda j, i: (i, j)), pl.BlockSpec((HB, tc), lambda j, i: (nbb(i), j)),
                  wsp(0), wsp(nJ), bsp(0), bsp(nJ)],
        out_specs=(pl.BlockSpec((T, tc), lambda j, i: (i, j)), pl.BlockSpec((T, tc), lambda j, i: (i, j)),
                   pl.BlockSpec((8, tc), lambda j, i: (0, j)), pl.BlockSpec((8, tc), lambda j, i: (0, j))),
        scratch_shapes=[pltpu.VMEM((H + T + H, tc), F32), pltpu.VMEM((H + T + H, tc), F32),
                        pltpu.VMEM((T + H, tc), F32), pltpu.VMEM((T + H, tc), F32)],
        compiler_params=_params("parallel", "arbitrary"))(hu, hu, hu, hu, hu, hu, dact, dact, w, w, b, b)


def _position():
    return lax.axis_index("x"), lax.axis_index("y"), lax.axis_index("c")


def _slot(px, py, pc):
    return 4 * px + 2 * py + pc


def _all_gather(x, in_vmem, name):
    space = pltpu.VMEM if in_vmem else pl.ANY

    def body(x_ref, out_ref, send_sems, recv_sems, local_sem):
        px, py, pc = _position()
        me, sibling = (px, py, pc), (px, py, 1 - pc)
        chips = [(1 - px, py), (px, 1 - py), (1 - px, 1 - py)]

        def copy(k, block, to, src=None):
            dst = out_ref.at[_slot(*block)]
            return pltpu.make_async_remote_copy(
                src_ref=dst if src is None else src, dst_ref=dst,
                send_sem=send_sems.at[k], recv_sem=recv_sems.at[k], device_id=to, device_id_type=MESH)

        mine = pltpu.make_async_copy(x_ref, out_ref.at[_slot(*me)], local_sem)
        mine.start()
        first = [copy(0, me, sibling, src=x_ref)]
        first += [copy(1 + n, me, (*chip, pc), src=x_ref) for n, chip in enumerate(chips)]
        for cp in first:
            cp.start()
        passed = [copy(4 + n, (*chip, pc), sibling) for n, chip in enumerate(chips)]
        for n, chip in enumerate(chips):
            copy(1 + n, (*chip, pc), me).wait_recv()
            passed[n].start()
        copy(0, sibling, me).wait_recv()
        for n, chip in enumerate(chips):
            copy(4 + n, (*chip, 1 - pc), me).wait_recv()
        for cp in first + passed:
            cp.wait_send()
        mine.wait()

    return pl.pallas_call(
        body, name=name, out_shape=jax.ShapeDtypeStruct((N_DEV,) + x.shape, x.dtype),
        in_specs=[pl.BlockSpec(memory_space=space)], out_specs=pl.BlockSpec(memory_space=space),
        scratch_shapes=[pltpu.SemaphoreType.DMA((7,)), pltpu.SemaphoreType.DMA((7,)), pltpu.SemaphoreType.DMA],
        compiler_params=pltpu.CompilerParams(vmem_limit_bytes=VMEM_LIMIT_BYTES),
    )(x)


def _scatter_slabs(g, name):
    def body(g_ref, out_ref, send_sems, recv_sems, local_sem):
        px, py, pc = _position()
        me = _slot(px, py, pc)
        mine = pltpu.make_async_copy(g_ref.at[me], out_ref.at[me], local_sem)
        mine.start()
        peers = [(px ^ fx, py ^ fy, pc ^ fc) for fx, fy, fc in PEER_FLIPS]
        sends = []
        for k, peer in enumerate(peers):
            cp = pltpu.make_async_remote_copy(
                src_ref=g_ref.at[_slot(*peer)], dst_ref=out_ref.at[me],
                send_sem=send_sems.at[k], recv_sem=recv_sems.at[k], device_id=peer, device_id_type=MESH)
            cp.start()
            sends.append(cp)
        for k, peer in enumerate(peers):
            pltpu.make_async_remote_copy(
                src_ref=g_ref.at[me], dst_ref=out_ref.at[_slot(*peer)],
                send_sem=send_sems.at[k], recv_sem=recv_sems.at[k], device_id=peer, device_id_type=MESH).wait_recv()
        for cp in sends:
            cp.wait_send()
        mine.wait()

    return pl.pallas_call(
        body, name=name, out_shape=jax.ShapeDtypeStruct(g.shape, g.dtype),
        in_specs=[pl.BlockSpec(memory_space=pl.ANY)], out_specs=pl.BlockSpec(memory_space=pl.ANY),
        scratch_shapes=[pltpu.SemaphoreType.DMA((7,)), pltpu.SemaphoreType.DMA((7,)), pltpu.SemaphoreType.DMA],
        compiler_params=pltpu.CompilerParams(vmem_limit_bytes=VMEM_LIMIT_BYTES),
    )(g)


def _adam_sum(stage, w, m, v, layer, prev, name):
    n = stage.shape[0]
    L, R, C = w.shape
    tr = R if R * C <= 256 * 1024 else _row_tile(R, C)
    c1 = 1.0 / (1.0 - ADAM_B1 ** ADAM_STEP)
    c2 = 1.0 / (1.0 - ADAM_B2 ** ADAM_STEP)

    def body(*refs):
        st_ref, w_ref, m_ref, v_ref = refs[:4]
        g_ref, d_ref, nm_ref, nv_ref = refs[-4:]
        g = st_ref[0].astype(F32)
        for s in range(1, n):
            g = g + st_ref[s].astype(F32)
        wv = w_ref[0]
        mn = ADAM_B1 * m_ref[0] + (1.0 - ADAM_B1) * g
        vn = ADAM_B2 * v_ref[0] + (1.0 - ADAM_B2) * (g * g)
        g_ref[0] = g
        nm_ref[0] = mn
        nv_ref[0] = vn
        d_ref[0] = -ADAM_LR * ((mn * c1) / (jnp.sqrt(vn * c2) + ADAM_EPS) + ADAM_WD * wv)

    lay = pl.BlockSpec((1, tr, C), lambda i: (layer, i, 0))
    in_specs = [pl.BlockSpec((n, tr, C), lambda i: (0, i, 0)), lay, lay, lay]
    ins = [stage, w, m, v]
    aliases = {}
    if prev is not None:
        in_specs += [pl.BlockSpec(memory_space=pl.ANY)] * 4
        ins += list(prev)
        aliases = {4: 0, 5: 1, 6: 2, 7: 3}
    shp = jax.ShapeDtypeStruct((L, R, C), F32)
    return pl.pallas_call(
        body, name=name, out_shape=(shp, shp, shp, shp), grid=(R // tr,),
        in_specs=in_specs, out_specs=(lay, lay, lay, lay), input_output_aliases=aliases,
        compiler_params=_params("parallel"))(*ins)


def _row_tile(R, C):
    cpad = -(-C // LANES) * LANES
    want = max(16, (256 * 1024) // cpad)
    best = 16
    for t in range(16, R + 1, 16):
        if R % t == 0 and t <= want:
            best = t
    return best


def _sum_slabs(st, name):
    n, R, C = st.shape
    tr = R if n * R * C * 4 <= (12 << 20) else _row_tile(R, C)

    def body(st_ref, o_ref):
        g = st_ref[0]
        for s in range(1, n):
            g = g + st_ref[s]
        o_ref[...] = g

    return pl.pallas_call(
        body, name=name, out_shape=jax.ShapeDtypeStruct((R, C), F32), grid=(R // tr,),
        in_specs=[pl.BlockSpec((n, tr, C), lambda i: (0, i, 0))], out_specs=pl.BlockSpec((tr, C), lambda i: (i, 0)),
        compiler_params=_params("parallel"))(st)


def _pack(arrs):
    flat = [a.reshape(-1).astype(F32) for a in arrs]
    sizes = [f.shape[0] for f in flat]
    total = sum(sizes)
    padded = -(-total // (16 * LANES)) * (16 * LANES)
    if padded > total:
        flat.append(jnp.zeros((padded - total,), F32))
    return jnp.concatenate(flat).reshape(padded // LANES, LANES), (sizes, [a.shape for a in arrs])


def _unpack(packed, layout, lead=()):
    sizes, shapes = layout
    flat = packed.reshape(lead + (-1,))
    out, off = [], 0
    for sz, shp in zip(sizes, shapes):
        out.append(flat[..., off:off + sz].reshape(lead + tuple(shp)))
        off += sz
    return out


def _local_step(x, tgt, ada, mix_norm_g, wts, b_forget, conf_dw_w, conf_dw_b, conf_ln_g, conf_ln_b, sc_dw_w,
                ffn_norm_g, ffn_dw_w, ffn_dw_b, final_norm_g):
    S, D = x.shape
    L = ada.shape[0]
    H = b_forget.shape[1]
    DA = H * HEAD_DIM
    C = conf_dw_b.shape[1]
    NQ = 3 * DA
    NR = 5 * C + LANES
    fblk = (5 * C) // LANES
    row = lambda a: a.reshape(1, -1)
    adav = ada.reshape(L, N_ADA, 1, D)

    saved = []
    xcur, delta, gate = x, None, None
    for l in range(L):
        sh_m, sc_m, g_m, sh_f, sc_f, g_f = [adav[l, n] for n in range(N_ADA)]
        w = wts[l]
        x1, h1 = _site_fwd(xcur, delta, gate, row(mix_norm_g[l]), sc_m, sh_m, name=f"site_fwd_mix")
        qkv = _matmul(h1, w["w_in_perm"][:, :NQ], BF16, name="mm_qkv")
        rest = _matmul(h1, w["w_in_perm"][:, NQ:], F32, name="mm_rest")
        bpad = jnp.zeros((1, LANES), F32).at[0, :H].set(b_forget[l])
        Fc = _fgate_fwd(rest, bpad, fblk, name="fgate_fwd")
        nf = -LOG2E * jnp.transpose(Fc[:, :H])
        attn, attn32, lse = _attn_fwd(qkv, jnp.broadcast_to(nf[:, :, None], (H, S, LANES)), H, name="attn_fwd")
        conf = _conf_fwd(rest, conf_dw_w[l], row(conf_dw_b[l]), row(conf_ln_g[l]), row(conf_ln_b[l]), name="conf_fwd")
        sconv = _sconv_fwd(rest, sc_dw_w[l], name="sconv_fwd")
        mixcat = jnp.concatenate([attn, conf, sconv], axis=1)
        mixed = _matmul(mixcat, w["w_out"], F32, name="mm_out")
        x2, h2 = _site_fwd(x1, mixed, g_m, row(ffn_norm_g[l]), sc_f, sh_f, name="site_fwd_ffn")
        hu = _matmul(h2, w["w_up"], F32, name="mm_up")
        act = _ffn_fwd(hu, ffn_dw_w[l], row(ffn_dw_b[l]), name="ffn_fwd")
        ffn_out = _matmul(act, w["w_down"], F32, name="mm_down")
        saved.append(dict(x1=x1, h1=h1, qkv=qkv, rest=rest, bpad=bpad, nf=nf, attn32=attn32, lse=lse, mixcat=mixcat,
                          mixed=mixed, x2=x2, h2=h2, hu=hu, act=act, ffn_out=ffn_out))
        xcur, delta, gate = x2, ffn_out, g_f

    loss_lanes, dx, d_delta, d_gate, d_gfin = _final_fwd_bwd(xcur, delta, gate, row(final_norm_g), tgt, name="final")
    loss = (0.5 / D) * jnp.sum(loss_lanes)

    grads = dict(final_norm_g=d_gfin[0], ada=[None] * L, mix_norm_g=[None] * L, ffn_norm_g=[None] * L,
                 b_forget=[None] * L, conf_dw_w=[None] * L, conf_dw_b=[None] * L, conf_ln_g=[None] * L,
                 conf_ln_b=[None] * L, sc_dw_w=[None] * L, ffn_dw_w=[None] * L, ffn_dw_b=[None] * L,
                 w_in_perm=[None] * L, w_out=[None] * L, w_up=[None] * L, w_down=[None] * L)
    Fd = wts[0]["w_down"].shape[0]
    K3 = ffn_dw_w.shape[1]
    for l in reversed(range(L)):
        sv, w = saved[l], wts[l]
        sh_m, sc_m, g_m, sh_f, sc_f, g_f = [adav[l, n] for n in range(N_ADA)]
        d_gf = d_gate
        grads["w_down"][l] = _matmul_tn(sv["act"], d_delta, BF16, name="mm_dw_down")
        dact = _matmul(d_delta, jnp.transpose(w["w_down"]), BF16, name="mm_dact")
        dhu_g, dhu_v, dwg, dwv = _ffn_bwd(sv["hu"], dact, ffn_dw_w[l], row(ffn_dw_b[l]), name="ffn_bwd")
        grads["ffn_dw_w"][l] = jnp.concatenate([dwg[:K3], dwv[:K3]], axis=1)
        grads["ffn_dw_b"][l] = jnp.concatenate([dwg[K3], dwv[K3]])
        dhu = jnp.concatenate([dhu_g, dhu_v], axis=1)
        grads["w_up"][l] = _matmul_tn(sv["h2"], dhu, BF16, name="mm_dw_up")
        dh2 = _matmul(dhu, jnp.transpose(w["w_up"]), F32, name="mm_dh2")
        dx, d_sh_f, d_a_f, d_mixed, d_gm = _site_bwd(sv["x2"], dh2, dx, row(ffn_norm_g[l]), sc_f,
                                                      sh_f, sv["mixed"], g_m, name="site_bwd_ffn")
        grads["ffn_norm_g"][l] = (d_a_f * (1.0 + sc_f))[0]
        d_sc_f = d_a_f * row(ffn_norm_g[l])
        grads["w_out"][l] = _matmul_tn(sv["mixcat"], d_mixed, BF16, name="mm_dw_out")
        w_out_t = jnp.transpose(w["w_out"])
        dattn = _matmul(d_mixed, w_out_t[:, :DA], BF16, name="mm_dattn")
        dcs = _matmul(d_mixed, w_out_t[:, DA:], F32, name="mm_dcs")
        delta_a = _blocked_rows(_attn_delta(sv["attn32"], dattn, H, name="attn_delta")[:, :, 0], min(ATTN_TILE, S))
        nfb = jnp.broadcast_to(sv["nf"][:, :, None], (H, S, LANES))
        dq, drow = _attn_bwd_dq(sv["qkv"], nfb, dattn, sv["lse"], delta_a, H, name="attn_bwd_dq")
        dk, dv, dnf = _attn_bwd_dkv(sv["qkv"], nfb, dattn, sv["lse"], delta_a, H, name="attn_bwd_dkv")
        dF = jnp.zeros((S, LANES), F32).at[:, :H].set(jnp.transpose(drow.reshape(H, S) - dnf[:, :, 0]))
        dfl, dbf = _fgate_bwd(sv["rest"], sv["bpad"], dF, fblk, name="fgate_bwd")
        grads["b_forget"][l] = dbf[0, :H]
        dcvg, dcw, dcvec = _conf_bwd(sv["rest"], dcs, conf_dw_w[l], row(conf_dw_b[l]), row(conf_ln_g[l]),
                                     row(conf_ln_b[l]), name="conf_bwd")
        grads["conf_dw_w"][l] = dcw[:conf_dw_w.shape[1]]
        grads["conf_dw_b"][l], grads["conf_ln_g"][l], grads["conf_ln_b"][l] = dcvec[0], dcvec[1], dcvec[2]
        dsc3, dsw = _sconv_bwd(sv["rest"], dcs, sc_dw_w[l], name="sconv_bwd")
        grads["sc_dw_w"][l] = dsw[:sc_dw_w.shape[1]]
        dproj = jnp.concatenate([dq, dk, dv, dcvg, dsc3, dfl], axis=1)
        grads["w_in_perm"][l] = _matmul_tn(sv["h1"], dproj, BF16, name="mm_dw_in")
        dh1 = _matmul(dproj, jnp.transpose(w["w_in_perm"]), F32, name="mm_dh1")
        if l > 0:
            pv = saved[l - 1]
            g_f_prev = adav[l - 1, 5]
            dx, d_sh_m, d_a_m, d_delta, d_gate = _site_bwd(sv["x1"], dh1, dx, row(mix_norm_g[l]), sc_m, sh_m,
                                                           pv["ffn_out"], g_f_prev, name="site_bwd_mix")
        else:
            dx, d_sh_m, d_a_m = _site_bwd(sv["x1"], dh1, dx, row(mix_norm_g[l]), sc_m, sh_m, None, None,
                                          name="site_bwd_first")
        grads["mix_norm_g"][l] = (d_a_m * (1.0 + sc_m))[0]
        d_sc_m = d_a_m * row(mix_norm_g[l])
        grads["ada"][l] = jnp.concatenate([d_sh_m, d_sc_m, d_gm, d_sh_f, d_sc_f, d_gf], axis=1)[0]
    return loss, dx, grads


def kernel(x, c, ada_w, ada_b, mix_norm_g, w_in, b_forget, conf_dw_w, conf_dw_b, conf_ln_g, conf_ln_b, sc_dw_w, w_out, ffn_norm_g, w_up, ffn_dw_w, ffn_dw_b, w_down, final_norm_g, loss_target, m_ada_w, m_ada_b, m_mix_norm_g, m_w_in, m_b_forget, m_conf_dw_w, m_conf_dw_b, m_conf_ln_g, m_conf_ln_b, m_sc_dw_w, m_w_out, m_ffn_norm_g, m_w_up, m_ffn_dw_w, m_ffn_dw_b, m_w_down, m_final_norm_g, v_ada_w, v_ada_b, v_mix_norm_g, v_w_in, v_b_forget, v_conf_dw_w, v_conf_dw_b, v_conf_ln_g, v_conf_ln_b, v_sc_dw_w, v_w_out, v_ffn_norm_g, v_w_up, v_ffn_dw_w, v_ffn_dw_b, v_w_down, v_final_norm_g):
    L, D, ada_loc = ada_w.shape
    S = x.shape[1]
    H = b_forget.shape[1]
    DA = H * HEAD_DIM
    C = conf_dw_b.shape[1]
    in_loc = w_in.shape[2]
    IN = in_loc * N_DEV
    px, py, pc = _position()
    me = _slot(px, py, pc)

    pk, lay = _pack([c, conf_dw_w, sc_dw_w, ffn_dw_w])
    gathered = _all_gather(pk, True, name="ag_small_fwd")
    c_all, cw_all, sw_all, fw_all = _unpack(gathered, lay, lead=(N_DEV,))
    c_all = c_all[:, 0]
    unshard = lambda a: jnp.moveaxis(a, 0, 2).reshape(a.shape[1], a.shape[2], -1)
    conf_w_full, sc_w_full, ffn_w_full = unshard(cw_all), unshard(sw_all), unshard(fw_all)
    c_act = c_all * jax.nn.sigmoid(c_all)
    c_act16 = jnp.zeros((16, D), F32).at[:N_DEV].set(c_act).astype(BF16)
    ada_cols = jnp.stack([_matmul(c_act16, ada_w[l].astype(BF16), F32, name="mm_ada")[:N_DEV] for l in range(L)])
    ada_g = _all_gather(ada_cols.reshape(L * N_DEV, ada_loc), True, name="ag_ada")
    ada_mine = lax.dynamic_index_in_dim(ada_g.reshape(N_DEV, L, N_DEV, ada_loc), me, axis=2, keepdims=False)
    ada = jnp.moveaxis(ada_mine, 0, 1).reshape(L, N_DEV * ada_loc) + ada_b

    g_in = _all_gather(w_in.astype(BF16), False, name="ag_w_in")
    g_out = _all_gather(w_out.astype(BF16), False, name="ag_w_out")
    g_up = _all_gather(w_up.astype(BF16), False, name="ag_w_up")
    g_down = _all_gather(w_down.astype(BF16), False, name="ag_w_down")
    NQ = 3 * DA
    wts = []
    for l in range(L):
        wi = jnp.moveaxis(g_in[:, l], 0, 1).reshape(D, IN)
        w_in_perm = jnp.concatenate([wi[:, :NQ], wi[:, NQ + H:], wi[:, NQ:NQ + H],
                                     jnp.zeros((D, LANES - H), BF16)], axis=1)
        wts.append(dict(w_in_perm=w_in_perm,
                        w_out=g_out[:, l].reshape(-1, D),
                        w_up=jnp.moveaxis(g_up[:, l], 0, 1).reshape(D, -1),
                        w_down=g_down[:, l].reshape(-1, D)))

    loss_loc, dx, gr = _local_step(x[0], loss_target[0], ada, mix_norm_g, wts, b_forget, conf_w_full, conf_dw_b,
                                   conf_ln_g, conf_ln_b, sc_w_full, ffn_norm_g, ffn_w_full, ffn_dw_b, final_norm_g)
    loss = lax.psum(loss_loc, ("x", "y", "c"))

    small_names = ["ada", "mix_norm_g", "ffn_norm_g", "b_forget", "conf_dw_b", "conf_ln_g", "conf_ln_b",
                   "ffn_dw_b", "conf_dw_w", "sc_dw_w", "ffn_dw_w"]
    pk, lay = _pack([jnp.stack(gr[n]) for n in small_names] + [gr["final_norm_g"]])
    parts = _all_gather(pk, True, name="ag_small_bwd")
    tot = _unpack(_sum_slabs(parts, name="sum_small"), lay)
    g_small = dict(zip(small_names + ["final_norm_g"], tot))
    d_ada_all = _unpack(parts, lay, lead=(N_DEV,))[0]
    my_cols = lambda a, n: lax.dynamic_slice_in_dim(a, me * n, n, axis=a.ndim - 1)

    c_act_t = jnp.zeros((D, LANES), F32).at[:, :N_DEV].set(jnp.transpose(c_act)).astype(BF16)
    res = None
    for l in range(L):
        d_loc = jnp.zeros((LANES, ada_loc), F32).at[:N_DEV].set(my_cols(d_ada_all[:, l], ada_loc)).astype(BF16)
        g_l = _matmul(c_act_t, d_loc, F32, name="mm_dada")
        res = _adam_sum(g_l[None], ada_w, m_ada_w, v_ada_w, l, res, name="adam_ada_w")
    out_ada_w = res

    def shard_cols(g):
        return jnp.moveaxis(g.reshape(g.shape[0], N_DEV, -1), 1, 0)

    def shard_rows(g):
        return g.reshape(N_DEV, -1, g.shape[1])

    big = {}
    for nm, wq, mq, vq in (("w_down", w_down, m_w_down, v_w_down), ("w_up", w_up, m_w_up, v_w_up),
                           ("w_out", w_out, m_w_out, v_w_out), ("w_in", w_in, m_w_in, v_w_in)):
        res = None
        for l in reversed(range(L)):
            if nm == "w_in":
                gp = gr["w_in_perm"][l]
                PR = NQ + 5 * C
                g_full = shard_cols(jnp.concatenate([gp[:, :NQ], gp[:, PR:PR + H], gp[:, NQ:PR]], axis=1))
            elif nm == "w_up":
                g_full = shard_cols(gr["w_up"][l])
            else:
                g_full = shard_rows(gr[nm][l])
            stage = _scatter_slabs(g_full, name="rs_" + nm)
            res = _adam_sum(stage, wq, mq, vq, l, res, name="adam_" + nm)
        big[nm] = res

    K31, K3 = conf_dw_w.shape[1], sc_dw_w.shape[1]
    sm = [("ada_b", ada_b, m_ada_b, v_ada_b, g_small["ada"]),
          ("mix_norm_g", mix_norm_g, m_mix_norm_g, v_mix_norm_g, g_small["mix_norm_g"]),
          ("b_forget", b_forget, m_b_forget, v_b_forget, g_small["b_forget"]),
          ("conf_dw_w", conf_dw_w, m_conf_dw_w, v_conf_dw_w, my_cols(g_small["conf_dw_w"], conf_dw_w.shape[2])),
          ("conf_dw_b", conf_dw_b, m_conf_dw_b, v_conf_dw_b, g_small["conf_dw_b"]),
          ("conf_ln_g", conf_ln_g, m_conf_ln_g, v_conf_ln_g, g_small["conf_ln_g"]),
          ("conf_ln_b", conf_ln_b, m_conf_ln_b, v_conf_ln_b, g_small["conf_ln_b"]),
          ("sc_dw_w", sc_dw_w, m_sc_dw_w, v_sc_dw_w, my_cols(g_small["sc_dw_w"], sc_dw_w.shape[2])),
          ("ffn_norm_g", ffn_norm_g, m_ffn_norm_g, v_ffn_norm_g, g_small["ffn_norm_g"]),
          ("ffn_dw_w", ffn_dw_w, m_ffn_dw_w, v_ffn_dw_w, my_cols(g_small["ffn_dw_w"], ffn_dw_w.shape[2])),
          ("ffn_dw_b", ffn_dw_b, m_ffn_dw_b, v_ffn_dw_b, g_small["ffn_dw_b"]),
          ("final_norm_g", final_norm_g, m_final_norm_g, v_final_norm_g, g_small["final_norm_g"])]
    pw, lay = _pack([t[1] for t in sm])
    pm, _ = _pack([t[2] for t in sm])
    pv, _ = _pack([t[3] for t in sm])
    pg, _ = _pack([t[4] for t in sm])
    sres = _adam_sum(pg[None], pw[None], pm[None], pv[None], 0, None, name="adam_small")
    s_g, s_d, s_m, s_v = [dict(zip([t[0] for t in sm], _unpack(r[0], lay))) for r in sres]

    def pick(idx, name):
        if name == "ada_w":
            return out_ada_w[idx]
        if name in big:
            return big[name][idx]
        return (s_g, s_d, s_m, s_v)[idx][name]

    order = ["ada_w", "ada_b", "mix_norm_g", "w_in", "b_forget", "conf_dw_w", "conf_dw_b", "conf_ln_g", "conf_ln_b",
             "sc_dw_w", "w_out", "ffn_norm_g", "w_up", "ffn_dw_w", "ffn_dw_b", "w_down", "final_norm_g"]
    outs = [loss, dx[None]]
    for idx in range(4):
        outs += [pick(idx, n) for n in order]
    return tuple(outs)
```

```python
import functools

import jax
import jax.numpy as jnp
from jax import lax
from jax.experimental import pallas as pl
from jax.experimental.pallas import tpu as pltpu

F32 = jnp.float32
BF16 = jnp.bfloat16
RMS_EPS = 1e-6
LN_EPS = 1e-5
HEAD_DIM = 128
N_ADA = 6
ADAM_LR = 0.001
ADAM_B1 = 0.9
ADAM_B2 = 0.999
ADAM_EPS = 1e-08
ADAM_WD = 0.01
ADAM_STEP = 10
N_DEV = 8
LANES = 128
VMEM_LIMIT_BYTES = 56 * 1024 * 1024
MESH = pl.DeviceIdType.MESH
PEER_FLIPS = ((0, 0, 1), (1, 0, 0), (0, 1, 0), (1, 1, 0), (1, 0, 1), (0, 1, 1), (1, 1, 1))


def _params(*sem):
    return pltpu.CompilerParams(dimension_semantics=sem, vmem_limit_bytes=VMEM_LIMIT_BYTES)


def _tile(n, cap):
    if n <= cap:
        return n
    for t in range(cap - cap % LANES, 0, -LANES):
        if n % t == 0:
            return t
    raise ValueError(f"no tile for {n}")


def _sigmoid(v):
    return jax.nn.sigmoid(v)


def _matmul(a, b, out_dtype, name, rider=None):
    M, K = a.shape
    _, N = b.shape
    tm, tn = _tile(M, 1024), _tile(N, 1024)
    tk = K if K <= 2048 else _tile(K, 1024)
    nk = K // tk
    grid = (M // tm, N // tn, nk)
    kind, arrs = rider if rider is not None else (None, [])
    nr = len(arrs)

    def body(*refs):
        a_ref, b_ref = refs[:2]
        r_in = refs[2:2 + nr]
        o_ref = refs[2 + nr]
        r_out = refs[3 + nr:3 + 2 * nr]
        rest = refs[3 + 2 * nr:]
        i, j, k = pl.program_id(0), pl.program_id(1), pl.program_id(2)
        if nr:
            sems = rest[-3:]

            @pl.when((i == 0) & (j == 0) & (k == 0))
            def _():
                for r in range(nr):
                    _rider_start(kind, r_in[r], r_out[r], sems, r)

        if nk == 1:
            o_ref[...] = jnp.dot(a_ref[...], b_ref[...], preferred_element_type=F32).astype(o_ref.dtype)
        else:
            acc_ref = rest[0]

            @pl.when(k == 0)
            def _():
                acc_ref[...] = jnp.zeros_like(acc_ref)

            acc_ref[...] += jnp.dot(a_ref[...], b_ref[...], preferred_element_type=F32)

            @pl.when(k == nk - 1)
            def _():
                o_ref[...] = acc_ref[...].astype(o_ref.dtype)

        if nr:
            @pl.when((i == grid[0] - 1) & (j == grid[1] - 1) & (k == nk - 1))
            def _():
                for r in range(nr):
                    _rider_finish(kind, r_in[r], r_out[r], sems, r)

    scratch = [] if nk == 1 else [pltpu.VMEM((tm, tn), F32)]
    hbm = pl.BlockSpec(memory_space=pl.ANY)
    out_shape = jax.ShapeDtypeStruct((M, N), out_dtype)
    out_specs = pl.BlockSpec((tm, tn), lambda i, j, k: (i, j))
    if nr:
        scratch += [pltpu.SemaphoreType.DMA((7 * nr,)), pltpu.SemaphoreType.DMA((7 * nr,)),
                    pltpu.SemaphoreType.DMA((nr,))]
        out_shape = (out_shape,) + tuple(
            jax.ShapeDtypeStruct(x.shape if kind == "scatter" else (N_DEV,) + x.shape, x.dtype) for x in arrs)
        out_specs = (out_specs,) + (hbm,) * nr
    out = pl.pallas_call(
        body, name=name,
        out_shape=out_shape,
        grid=grid,
        in_specs=[pl.BlockSpec((tm, tk), lambda i, j, k: (i, k)),
                  pl.BlockSpec((tk, tn), lambda i, j, k: (k, j))] + [hbm] * nr,
        out_specs=out_specs,
        scratch_shapes=scratch,
        compiler_params=_params(*(("arbitrary",) * 3 if nr else ("parallel", "parallel", "arbitrary"))),
    )(a, b, *arrs)
    return (out[0], list(out[1:])) if nr else out


_TN = (((0,), (0,)), ((), ()))


def _matmul_tn(a, b, out_dtype, name):
    S, M = a.shape
    _, N = b.shape
    tm, tn, ts = _tile(M, 1024), _tile(N, 1024), _tile(S, 1024)
    ns = S // ts

    def body(a_ref, b_ref, o_ref, acc_ref):
        k = pl.program_id(2)

        @pl.when(k == 0)
        def _():
            acc_ref[...] = jnp.zeros_like(acc_ref)

        acc_ref[...] += lax.dot_general(a_ref[...], b_ref[...], _TN, preferred_element_type=F32)

        @pl.when(k == ns - 1)
        def _():
            o_ref[...] = acc_ref[...].astype(o_ref.dtype)

    return pl.pallas_call(
        body, name=name,
        out_shape=jax.ShapeDtypeStruct((M, N), out_dtype),
        grid=(M // tm, N // tn, ns),
        in_specs=[pl.BlockSpec((ts, tm), lambda i, j, k: (k, i)),
                  pl.BlockSpec((ts, tn), lambda i, j, k: (k, j))],
        out_specs=pl.BlockSpec((tm, tn), lambda i, j, k: (i, j)),
        scratch_shapes=[pltpu.VMEM((tm, tn), F32)],
        compiler_params=_params("parallel", "parallel", "arbitrary"),
    )(a, b)


def _site_fwd(x, delta, gate, g, sc, sh, name):
    S, D = x.shape
    T = min(256, S)
    res = delta is not None

    def body(*refs):
        if res:
            x_ref, d_ref, gate_ref, g_ref, sc_ref, sh_ref, xo_ref, h_ref = refs
            xv = x_ref[...] + gate_ref[...] * d_ref[...]
            xo_ref[...] = xv
        else:
            x_ref, g_ref, sc_ref, sh_ref, h_ref = refs
            xv = x_ref[...]
        r = lax.rsqrt(jnp.mean(xv * xv, axis=-1, keepdims=True) + RMS_EPS)
        a = g_ref[...] * (1.0 + sc_ref[...])
        h_ref[...] = (xv * r * a + sh_ref[...]).astype(BF16)

    row = pl.BlockSpec((T, D), lambda i: (i, 0))
    vec = pl.BlockSpec((1, D), lambda i: (0, 0))
    if res:
        ins, in_specs = (x, delta, gate, g, sc, sh), [row, row, vec, vec, vec, vec]
        out_shape = (jax.ShapeDtypeStruct((S, D), F32), jax.ShapeDtypeStruct((S, D), BF16))
        out_specs = (row, row)
    else:
        ins, in_specs = (x, g, sc, sh), [row, vec, vec, vec]
        out_shape = jax.ShapeDtypeStruct((S, D), BF16)
        out_specs = row
    out = pl.pallas_call(body, name=name, out_shape=out_shape, grid=(S // T,), in_specs=in_specs,
                         out_specs=out_specs, compiler_params=_params("parallel"))(*ins)
    return out if res else (x, out)


def _site_bwd(x, dh, dres, g, sc, sh, delta, gate, name):
    S, D = x.shape
    T = min(256, S)
    res = delta is not None

    def body(*refs):
        if res:
            (x_ref, dh_ref, dres_ref, g_ref, sc_ref, delta_ref, gate_ref,
             dx_ref, dsh_ref, da_ref, dd_ref, dgate_ref) = refs
        else:
            x_ref, dh_ref, dres_ref, g_ref, sc_ref, dx_ref, dsh_ref, da_ref = refs
        i = pl.program_id(0)
        xv = x_ref[...]
        dhv = dh_ref[...]
        r = lax.rsqrt(jnp.mean(xv * xv, axis=-1, keepdims=True) + RMS_EPS)
        xh = xv * r
        dxh = dhv * (g_ref[...] * (1.0 + sc_ref[...]))
        dx = r * (dxh - xh * jnp.mean(dxh * xh, axis=-1, keepdims=True)) + dres_ref[...]
        dx_ref[...] = dx

        @pl.when(i == 0)
        def _():
            dsh_ref[...] = jnp.zeros_like(dsh_ref)
            da_ref[...] = jnp.zeros_like(da_ref)
            if res:
                dgate_ref[...] = jnp.zeros_like(dgate_ref)

        dsh_ref[...] += jnp.sum(dhv, axis=0, keepdims=True)
        da_ref[...] += jnp.sum(dhv * xh, axis=0, keepdims=True)
        if res:
            dd_ref[...] = (gate_ref[...] * dx).astype(BF16)
            dgate_ref[...] += jnp.sum(dx * delta_ref[...], axis=0, keepdims=True)

    row = pl.BlockSpec((T, D), lambda i: (i, 0))
    vec = pl.BlockSpec((1, D), lambda i: (0, 0))
    vshape = jax.ShapeDtypeStruct((1, D), F32)
    if res:
        ins, in_specs = (x, dh, dres, g, sc, delta, gate), [row, row, row, vec, vec, row, vec]
        out_shape = (jax.ShapeDtypeStruct((S, D), F32), vshape, vshape, jax.ShapeDtypeStruct((S, D), BF16), vshape)
        out_specs = (row, vec, vec, row, vec)
    else:
        ins, in_specs = (x, dh, dres, g, sc), [row, row, row, vec, vec]
        out_shape = (jax.ShapeDtypeStruct((S, D), F32), vshape, vshape)
        out_specs = (row, vec, vec)
    return pl.pallas_call(body, name=name, out_shape=out_shape, grid=(S // T,), in_specs=in_specs,
                          out_specs=out_specs, compiler_params=_params("arbitrary"))(*ins)


def _final_fwd_bwd(x, delta, gate, gfin, target, name):
    S, D = x.shape
    T = min(256, S)

    def body(x_ref, delta_ref, gate_ref, g_ref, t_ref, loss_ref, dx_ref, dd_ref, dgate_ref, dg_ref):
        i = pl.program_id(0)
        dl = delta_ref[...]
        xv = x_ref[...] + gate_ref[...] * dl
        r = lax.rsqrt(jnp.mean(xv * xv, axis=-1, keepdims=True) + RMS_EPS)
        xh = xv * r
        gv = g_ref[...]
        e = xh * gv - t_ref[...]
        dy = e * (1.0 / D)
        dxh = dy * gv
        dx = r * (dxh - xh * jnp.mean(dxh * xh, axis=-1, keepdims=True))
        dx_ref[...] = dx
        dd_ref[...] = (gate_ref[...] * dx).astype(BF16)

        @pl.when(i == 0)
        def _():
            loss_ref[...] = jnp.zeros_like(loss_ref)
            dgate_ref[...] = jnp.zeros_like(dgate_ref)
            dg_ref[...] = jnp.zeros_like(dg_ref)

        loss_ref[...] += jnp.sum(e * e, axis=0, keepdims=True)
        dgate_ref[...] += jnp.sum(dx * dl, axis=0, keepdims=True)
        dg_ref[...] += jnp.sum(dy * xh, axis=0, keepdims=True)

    row = pl.BlockSpec((T, D), lambda i: (i, 0))
    vec = pl.BlockSpec((1, D), lambda i: (0, 0))
    vshape = jax.ShapeDtypeStruct((1, D), F32)
    return pl.pallas_call(
        body, name=name,
        out_shape=(vshape, jax.ShapeDtypeStruct((S, D), F32), jax.ShapeDtypeStruct((S, D), BF16), vshape, vshape),
        grid=(S // T,), in_specs=[row, row, vec, vec, row], out_specs=(vec, row, row, vec, vec),
        compiler_params=_params("arbitrary"))(x, delta, gate, gfin, target)


def _split3(v):
    hi = v.astype(BF16)
    r1 = v - hi.astype(F32)
    mid = r1.astype(BF16)
    lo = (r1 - mid.astype(F32)).astype(BF16)
    return hi, mid, lo


def _tri_dot(tri, v):
    hi, mid, lo = _split3(v)
    d = functools.partial(jnp.dot, preferred_element_type=F32)
    return d(tri, hi) + d(tri, mid) + d(tri, lo)


def _fgate_fwd(rest, bpad, fblk, name):
    S = rest.shape[0]
    CH = min(256, S)
    nch = S // CH

    def body(f_ref, b_ref, o_ref):
        row = lax.broadcasted_iota(jnp.int32, (CH, CH), 0)
        col = lax.broadcasted_iota(jnp.int32, (CH, CH), 1)
        tri = (row >= col).astype(BF16)

        def step(ci, carry):
            rows = pl.ds(pl.multiple_of(ci * CH, CH), CH)
            z = f_ref[rows, :] + b_ref[...]
            lf = jnp.minimum(z, 0.0) - jnp.log(1.0 + jnp.exp(-jnp.abs(z)))
            o_ref[rows, :] = _tri_dot(tri, lf) + carry
            return carry + jnp.sum(lf, axis=0, keepdims=True)

        lax.fori_loop(0, nch, step, jnp.zeros((1, LANES), F32))

    return pl.pallas_call(
        body, name=name, out_shape=jax.ShapeDtypeStruct((S, LANES), F32), grid=(1,),
        in_specs=[pl.BlockSpec((S, LANES), lambda i: (0, fblk)), pl.BlockSpec((1, LANES), lambda i: (0, 0))],
        out_specs=pl.BlockSpec((S, LANES), lambda i: (0, 0)),
        compiler_params=_params("arbitrary"))(rest, bpad)


def _fgate_bwd(rest, bpad, dF, fblk, name):
    S = rest.shape[0]
    CH = min(256, S)
    nch = S // CH

    def body(f_ref, b_ref, df_ref, o_ref, db_ref):
        row = lax.broadcasted_iota(jnp.int32, (CH, CH), 0)
        col = lax.broadcasted_iota(jnp.int32, (CH, CH), 1)
        tri = (col >= row).astype(BF16)

        def step(n, carry):
            sfx_carry, db = carry
            ci = nch - 1 - n
            rows = pl.ds(pl.multiple_of(ci * CH, CH), CH)
            z = f_ref[rows, :] + b_ref[...]
            dfv = df_ref[rows, :]
            dz = (_tri_dot(tri, dfv) + sfx_carry) * _sigmoid(-z)
            o_ref[rows, :] = dz.astype(BF16)
            return sfx_carry + jnp.sum(dfv, axis=0, keepdims=True), db + jnp.sum(dz, axis=0, keepdims=True)

        zero = jnp.zeros((1, LANES), F32)
        _, db = lax.fori_loop(0, nch, step, (zero, zero))
        db_ref[...] = db

    blk = pl.BlockSpec((S, LANES), lambda i: (0, 0))
    return pl.pallas_call(
        body, name=name,
        out_shape=(jax.ShapeDtypeStruct((S, LANES), BF16), jax.ShapeDtypeStruct((1, LANES), F32)), grid=(1,),
        in_specs=[pl.BlockSpec((S, LANES), lambda i: (0, fblk)), pl.BlockSpec((1, LANES), lambda i: (0, 0)), blk],
        out_specs=(blk, pl.BlockSpec((1, LANES), lambda i: (0, 0))),
        compiler_params=_params("arbitrary"))(rest, bpad, dF)


_NT = (((1,), (1,)), ((), ()))
LOG2E = 1.4426950408889634
ATTN_TILE = 512


def _blocked_rows(a, TA):
    H, S = a.shape
    return a.reshape(H, S // TA, 1, TA)


def _attn_fwd(qkv, nfb, H, name):
    S = qkv.shape[0]
    TA = min(ATTN_TILE, S)
    nb = S // TA
    c = HEAD_DIM ** -0.5 * LOG2E

    def body(q_ref, k_ref, v_ref, nf_ref, o_ref, o32_ref, lse_ref, m_ref, l_ref, acc_ref):
        i = pl.program_id(1)
        m_ref[...] = jnp.full_like(m_ref, -jnp.inf)
        l_ref[...] = jnp.zeros_like(l_ref)
        acc_ref[...] = jnp.zeros_like(acc_ref)

        def block(j, masked):
            rows = pl.ds(pl.multiple_of(j * TA, TA), TA)
            st = (lax.dot_general(k_ref[rows, :], q_ref[...], _NT, preferred_element_type=F32) * c
                  + jnp.tile(nf_ref[0, rows, :], (1, TA // LANES)))
            if masked:
                key = lax.broadcasted_iota(jnp.int32, (TA, TA), 0)
                qry = lax.broadcasted_iota(jnp.int32, (TA, TA), 1)
                st = jnp.where(key <= qry, st, -jnp.inf)
            m_old = m_ref[...]
            m_new = jnp.maximum(m_old, jnp.max(st, axis=0, keepdims=True))
            alpha = jnp.exp2(m_old - m_new)
            pt = jnp.exp2(st - m_new)
            l_ref[...] = alpha * l_ref[...] + jnp.sum(pt, axis=0, keepdims=True)
            acc_ref[...] = alpha * acc_ref[...] + lax.dot_general(v_ref[rows, :], pt.astype(BF16), _TN,
                                                                  preferred_element_type=F32)
            m_ref[...] = m_new

        def loop(j, carry):
            block(j, False)
            return carry

        lax.fori_loop(0, i, loop, 0)
        block(i, True)
        o = jnp.transpose(acc_ref[...] / l_ref[...])
        o32_ref[...] = o
        o_ref[...] = o.astype(BF16)
        lse_ref[0, 0] = m_ref[...] + jnp.log(l_ref[...]) * LOG2E

    qblk = pl.BlockSpec((TA, HEAD_DIM), lambda h, i: (i, h))
    return pl.pallas_call(
        body, name=name, grid=(H, nb),
        in_specs=[qblk,
                  pl.BlockSpec((S, HEAD_DIM), lambda h, i: (0, H + h)),
                  pl.BlockSpec((S, HEAD_DIM), lambda h, i: (0, 2 * H + h)),
                  pl.BlockSpec((1, S, LANES), lambda h, i: (h, 0, 0))],
        out_specs=(qblk, qblk, pl.BlockSpec((1, 1, 1, TA), lambda h, i: (h, i, 0, 0))),
        scratch_shapes=[pltpu.VMEM((1, TA), F32), pltpu.VMEM((1, TA), F32), pltpu.VMEM((HEAD_DIM, TA), F32)],
        out_shape=(jax.ShapeDtypeStruct((S, H * HEAD_DIM), BF16), jax.ShapeDtypeStruct((S, H * HEAD_DIM), F32),
                   jax.ShapeDtypeStruct((H, nb, 1, TA), F32)),
        compiler_params=_params("parallel", "parallel"))(qkv, qkv, qkv, nfb)


def _attn_delta(o, do, H, name):
    S = o.shape[0]
    T = min(512, S)

    def body(o_ref, do_ref, d_ref):
        d_ref[0] = jnp.sum(o_ref[...].astype(F32) * do_ref[...].astype(F32), axis=-1, keepdims=True)

    blk = pl.BlockSpec((T, HEAD_DIM), lambda h, i: (i, h))
    return pl.pallas_call(
        body, name=name, out_shape=jax.ShapeDtypeStruct((H, S, 1), F32), grid=(H, S // T),
        in_specs=[blk, blk], out_specs=pl.BlockSpec((1, T, 1), lambda h, i: (h, i, 0)),
        compiler_params=_params("parallel", "parallel"))(o, do)


def _ds_tile(k, q, v, do, nfb, lse_row, delta_row, c, masked):
    TK, TQ = k.shape[0], q.shape[0]
    st = lax.dot_general(k, q, _NT, preferred_element_type=F32) * c + jnp.tile(nfb, (1, TQ // LANES))
    pt = jnp.exp2(st - lse_row)
    if masked:
        key = lax.broadcasted_iota(jnp.int32, (TK, TQ), 0)
        qry = lax.broadcasted_iota(jnp.int32, (TK, TQ), 1)
        pt = jnp.where(key <= qry, pt, 0.0)
    dpt = lax.dot_general(v, do, _NT, preferred_element_type=F32)
    return pt, pt * (dpt - delta_row)


def _attn_bwd_dq(qkv, nfb, do, lse, delta, H, name):
    S = qkv.shape[0]
    TA = min(ATTN_TILE, S)
    nb = S // TA
    scale = HEAD_DIM ** -0.5
    c = scale * LOG2E

    def body(q_ref, k_ref, v_ref, nf_ref, do_ref, lse_ref, dl_ref, dq_ref, drow_ref, acc_ref, row_acc):
        i = pl.program_id(1)
        acc_ref[...] = jnp.zeros_like(acc_ref)
        row_acc[...] = jnp.zeros_like(row_acc)

        def block(j, masked):
            rows = pl.ds(pl.multiple_of(j * TA, TA), TA)
            kb = k_ref[rows, :]
            _, dst = _ds_tile(kb, q_ref[...], v_ref[rows, :], do_ref[...], nf_ref[0, rows, :],
                              lse_ref[0, 0], dl_ref[0, 0], c, masked)
            acc_ref[...] += lax.dot_general(kb, dst.astype(BF16), _TN, preferred_element_type=F32)
            row_acc[...] += jnp.sum(dst, axis=0, keepdims=True)

        def loop(j, carry):
            block(j, False)
            return carry

        lax.fori_loop(0, i, loop, 0)
        block(i, True)
        dq_ref[...] = (jnp.transpose(acc_ref[...]) * scale).astype(BF16)
        drow_ref[0, 0] = row_acc[...]

    qblk = pl.BlockSpec((TA, HEAD_DIM), lambda h, i: (i, h))
    row_stat = pl.BlockSpec((1, 1, 1, TA), lambda h, i: (h, i, 0, 0))
    return pl.pallas_call(
        body, name=name, grid=(H, nb),
        in_specs=[qblk,
                  pl.BlockSpec((S, HEAD_DIM), lambda h, i: (0, H + h)),
                  pl.BlockSpec((S, HEAD_DIM), lambda h, i: (0, 2 * H + h)),
                  pl.BlockSpec((1, S, LANES), lambda h, i: (h, 0, 0)),
                  qblk, row_stat, row_stat],
        out_specs=(qblk, row_stat),
        scratch_shapes=[pltpu.VMEM((HEAD_DIM, TA), F32), pltpu.VMEM((1, TA), F32)],
        out_shape=(jax.ShapeDtypeStruct((S, H * HEAD_DIM), BF16), jax.ShapeDtypeStruct((H, nb, 1, TA), F32)),
        compiler_params=_params("parallel", "parallel"))(qkv, qkv, qkv, nfb, do, lse, delta)


def _attn_bwd_dkv(qkv, nfb, do, lse, delta, H, name):
    S = qkv.shape[0]
    TA = min(ATTN_TILE, S)
    nb = S // TA
    scale = HEAD_DIM ** -0.5
    c = scale * LOG2E

    def body(q_ref, k_ref, v_ref, nf_ref, do_ref, lse_ref, dl_ref, dk_ref, dv_ref, dnf_ref, dk_acc, dv_acc, dnf_acc):
        j = pl.program_id(1)
        dk_acc[...] = jnp.zeros_like(dk_acc)
        dv_acc[...] = jnp.zeros_like(dv_acc)
        dnf_acc[...] = jnp.zeros_like(dnf_acc)

        def block(i, masked):
            rows = pl.ds(pl.multiple_of(i * TA, TA), TA)
            qb = q_ref[rows, :]
            dob = do_ref[rows, :]
            pt, dst = _ds_tile(k_ref[...], qb, v_ref[...], dob, nf_ref[0], lse_ref[0, i], dl_ref[0, i], c, masked)
            dv_acc[...] += jnp.dot(pt.astype(BF16), dob, preferred_element_type=F32)
            dk_acc[...] += jnp.dot(dst.astype(BF16), qb, preferred_element_type=F32)
            part = dst[:, 0:LANES]
            for t in range(1, TA // LANES):
                part = part + dst[:, t * LANES:(t + 1) * LANES]
            dnf_acc[...] += part

        def loop(i, carry):
            block(i, False)
            return carry

        block(j, True)
        lax.fori_loop(j + 1, nb, loop, 0)
        dk_ref[...] = (dk_acc[...] * scale).astype(BF16)
        dv_ref[...] = dv_acc[...].astype(BF16)
        dnf_ref[0] = jnp.sum(dnf_acc[...], axis=-1, keepdims=True)

    full = pl.BlockSpec((S, HEAD_DIM), lambda h, j: (0, h))
    row_stat = pl.BlockSpec((1, nb, 1, TA), lambda h, j: (h, 0, 0, 0))
    kblk = lambda c0: pl.BlockSpec((TA, HEAD_DIM), lambda h, j: (j, c0 + h))
    shp = jax.ShapeDtypeStruct((S, H * HEAD_DIM), BF16)
    return pl.pallas_call(
        body, name=name, grid=(H, nb),
        in_specs=[full, kblk(H), kblk(2 * H), pl.BlockSpec((1, TA, LANES), lambda h, j: (h, j, 0)), full,
                  row_stat, row_stat],
        out_specs=(kblk(0), kblk(0), pl.BlockSpec((1, TA, 1), lambda h, j: (h, j, 0))),
        scratch_shapes=[pltpu.VMEM((TA, HEAD_DIM), F32), pltpu.VMEM((TA, HEAD_DIM), F32), pltpu.VMEM((TA, LANES), F32)],
        out_shape=(shp, shp, jax.ShapeDtypeStruct((H, S, 1), F32)),
        compiler_params=_params("parallel", "parallel"))(qkv, qkv, qkv, nfb, do, lse, delta)


def _taps(ext_ref, w_ref, K, base, r0, rows, cols, reverse=False, init=None):
    acc = init
    for k in range(K):
        wk = w_ref[(K - 1 - k) if reverse else k:((K - 1 - k) if reverse else k) + 1, cols]
        term = wk * ext_ref[base + k + r0:base + k + r0 + rows, cols]
        acc = term if acc is None else acc + term
    return acc


def _prev_blk(T, H):
    return lambda i: jnp.maximum(i * (T // H) - 1, 0)


def _next_blk(T, H, S):
    return lambda i: jnp.minimum((i + 1) * (T // H), S // H - 1)


def _conf_fwd(rest, w, b, lng, lnb, name):
    S = rest.shape[0]
    K, C = w.shape
    H, T = 32, min(256, S)
    RS = min(64, T)
    base = H - (K - 1)

    def body(cv_ref, cg_ref, cvp_ref, cgp_ref, w_ref, b_ref, g_ref, bb_ref, o_ref, ext_ref):
        i = pl.program_id(0)
        ext_ref[0:H, :] = jnp.where(i > 0, cvp_ref[...] * _sigmoid(cgp_ref[...]), 0.0)
        ext_ref[H:H + T, :] = cv_ref[...] * _sigmoid(cg_ref[...])
        for r0 in range(0, T, RS):
            cc = _taps(ext_ref, w_ref, K, base, r0, RS, slice(None), init=jnp.broadcast_to(b_ref[...], (RS, C)))
            xc = cc - jnp.mean(cc, axis=-1, keepdims=True)
            y = xc * lax.rsqrt(jnp.mean(xc * xc, axis=-1, keepdims=True) + LN_EPS) * g_ref[...] + bb_ref[...]
            o_ref[r0:r0 + RS, :] = (y * _sigmoid(y)).astype(BF16)

    pb = _prev_blk(T, H)
    cur = lambda cb: pl.BlockSpec((T, C), lambda i: (i, cb))
    prev = lambda cb: pl.BlockSpec((H, C), lambda i: (pb(i), cb))
    full = lambda a: pl.BlockSpec(a.shape, lambda i: (0, 0))
    return pl.pallas_call(
        body, name=name, out_shape=jax.ShapeDtypeStruct((S, C), BF16), grid=(S // T,),
        in_specs=[cur(0), cur(1), prev(0), prev(1), full(w), full(b), full(lng), full(lnb)],
        out_specs=pl.BlockSpec((T, C), lambda i: (i, 0)),
        scratch_shapes=[pltpu.VMEM((H + T, C), F32)],
        compiler_params=_params("parallel"))(rest, rest, rest, rest, w, b, lng, lnb)


def _conf_bwd(rest, dcs, w, b, lng, lnb, name):
    S = rest.shape[0]
    K, C = w.shape
    H, T = 32, min(256, S)
    RS = 32
    nI = S // T
    base = H - (K - 1)

    def body(cv_ref, cg_ref, cvp_ref, cgp_ref, cvn_ref, cgn_ref, do_ref, don_ref, w_ref, b_ref, g_ref, bb_ref,
             dcvg_ref, dw_ref, dvec_ref, ext_ref, dcc_ref):
        i = pl.program_id(0)
        ext_ref[0:H, :] = jnp.where(i > 0, cvp_ref[...] * _sigmoid(cgp_ref[...]), 0.0)
        ext_ref[H:H + T, :] = cv_ref[...] * _sigmoid(cg_ref[...])
        ext_ref[H + T:H + T + H, :] = cvn_ref[...] * _sigmoid(cgn_ref[...])

        @pl.when(i == 0)
        def _():
            dw_ref[...] = jnp.zeros_like(dw_ref)
            dvec_ref[...] = jnp.zeros_like(dvec_ref)

        db = jnp.zeros((1, C), F32)
        dg = jnp.zeros((1, C), F32)
        dbb = jnp.zeros((1, C), F32)
        for r0 in range(0, T + H, RS):
            cc = _taps(ext_ref, w_ref, K, base, r0, RS, slice(None), init=jnp.broadcast_to(b_ref[...], (RS, C)))
            xc = cc - jnp.mean(cc, axis=-1, keepdims=True)
            r = lax.rsqrt(jnp.mean(xc * xc, axis=-1, keepdims=True) + LN_EPS)
            xh = xc * r
            y = xh * g_ref[...] + bb_ref[...]
            sy = _sigmoid(y)
            if r0 < T:
                d_o = do_ref[r0:r0 + RS, :]
            else:
                d_o = jnp.where(i < nI - 1, don_ref[r0 - T:r0 - T + RS, :], 0.0)
            dy = d_o * (sy * (1.0 + y * (1.0 - sy)))
            dxh = dy * g_ref[...]
            dcc = r * (dxh - jnp.mean(dxh, axis=-1, keepdims=True)
                       - xh * jnp.mean(dxh * xh, axis=-1, keepdims=True))
            dcc_ref[r0:r0 + RS, :] = dcc
            if r0 < T:
                dbb = dbb + jnp.sum(dy, axis=0, keepdims=True)
                dg = dg + jnp.sum(dy * xh, axis=0, keepdims=True)
                db = db + jnp.sum(dcc, axis=0, keepdims=True)
        dvec_ref[0:1, :] += db
        dvec_ref[1:2, :] += dg
        dvec_ref[2:3, :] += dbb
        R2 = min(64, T)
        for k in range(K):
            s = jnp.zeros((1, C), F32)
            for r0 in range(0, T, R2):
                s = s + jnp.sum(dcc_ref[r0:r0 + R2, :] * ext_ref[base + k + r0:base + k + r0 + R2, :],
                                axis=0, keepdims=True)
            dw_ref[k:k + 1, :] += s
        for r0 in range(0, T, R2):
            dci = _taps(dcc_ref, w_ref, K, 0, r0, R2, slice(None), reverse=True)
            cvv = cv_ref[r0:r0 + R2, :]
            sg = _sigmoid(cg_ref[r0:r0 + R2, :])
            dcvg_ref[r0:r0 + R2, 0:C] = (dci * sg).astype(BF16)
            dcvg_ref[r0:r0 + R2, C:2 * C] = (dci * cvv * sg * (1.0 - sg)).astype(BF16)

    pb, nb_ = _prev_blk(T, H), _next_blk(T, H, S)
    cur = lambda cb: pl.BlockSpec((T, C), lambda i: (i, cb))
    prev = lambda cb: pl.BlockSpec((H, C), lambda i: (pb(i), cb))
    nxt = lambda cb: pl.BlockSpec((H, C), lambda i: (nb_(i), cb))
    full = lambda a: pl.BlockSpec(a.shape, lambda i: (0, 0))
    return pl.pallas_call(
        body, name=name,
        out_shape=(jax.ShapeDtypeStruct((S, 2 * C), BF16), jax.ShapeDtypeStruct((32, C), F32),
                   jax.ShapeDtypeStruct((8, C), F32)),
        grid=(nI,),
        in_specs=[cur(0), cur(1), prev(0), prev(1), nxt(0), nxt(1), cur(0), nxt(0),
                  full(w), full(b), full(lng), full(lnb)],
        out_specs=(pl.BlockSpec((T, 2 * C), lambda i: (i, 0)), pl.BlockSpec((32, C), lambda i: (0, 0)),
                   pl.BlockSpec((8, C), lambda i: (0, 0))),
        scratch_shapes=[pltpu.VMEM((H + T + H, C), F32), pltpu.VMEM((T + H, C), F32)],
        compiler_params=_params("arbitrary"))(rest, rest, rest, rest, rest, rest, dcs, dcs, w, b, lng, lnb)


def _sconv_fwd(rest, w, name):
    S = rest.shape[0]
    K, C = w.shape
    H, T = 8, min(256, S)
    RS = min(64, T)
    base = H - (K - 1)

    def body(sx_ref, sb_ref, sc_ref, sxp_ref, scp_ref, w_ref, o_ref, ext_ref):
        i = pl.program_id(0)
        ext_ref[0:H, :] = jnp.where(i > 0, sxp_ref[...] * scp_ref[...], 0.0)
        ext_ref[H:H + T, :] = sx_ref[...] * sc_ref[...]
        for r0 in range(0, T, RS):
            cz = _taps(ext_ref, w_ref, K, base, r0, RS, slice(None))
            o_ref[r0:r0 + RS, :] = (sb_ref[r0:r0 + RS, :] * cz).astype(BF16)

    pb = _prev_blk(T, H)
    cur = lambda cb: pl.BlockSpec((T, C), lambda i: (i, cb))
    prev = lambda cb: pl.BlockSpec((H, C), lambda i: (pb(i), cb))
    return pl.pallas_call(
        body, name=name, out_shape=jax.ShapeDtypeStruct((S, C), BF16), grid=(S // T,),
        in_specs=[cur(2), cur(3), cur(4), prev(2), prev(4), pl.BlockSpec(w.shape, lambda i: (0, 0))],
        out_specs=pl.BlockSpec((T, C), lambda i: (i, 0)),
        scratch_shapes=[pltpu.VMEM((H + T, C), F32)],
        compiler_params=_params("parallel"))(rest, rest, rest, rest, rest, w)


def _sconv_bwd(rest, dcs, w, name):
    S = rest.shape[0]
    K, C = w.shape
    H, T = 8, min(256, S)
    RS = min(64, T)
    nI = S // T
    base = H - (K - 1)

    def body(sx_ref, sb_ref, sc_ref, sxp_ref, scp_ref, sbn_ref, do_ref, don_ref, w_ref,
             dout_ref, dw_ref, ext_ref, dcv_ref):
        i = pl.program_id(0)
        ext_ref[0:H, :] = jnp.where(i > 0, sxp_ref[...] * scp_ref[...], 0.0)
        ext_ref[H:H + T, :] = sx_ref[...] * sc_ref[...]
        dcv_ref[0:T, :] = do_ref[...] * sb_ref[...]
        dcv_ref[T:T + H, :] = jnp.where(i < nI - 1, don_ref[...] * sbn_ref[...], 0.0)

        @pl.when(i == 0)
        def _():
            dw_ref[...] = jnp.zeros_like(dw_ref)

        for k in range(K):
            s = jnp.zeros((1, C), F32)
            for r0 in range(0, T, RS):
                s = s + jnp.sum(dcv_ref[r0:r0 + RS, :] * ext_ref[base + k + r0:base + k + r0 + RS, :],
                                axis=0, keepdims=True)
            dw_ref[k:k + 1, :] += s
        for r0 in range(0, T, RS):
            cz = _taps(ext_ref, w_ref, K, base, r0, RS, slice(None))
            dz = _taps(dcv_ref, w_ref, K, 0, r0, RS, slice(None), reverse=True)
            dout_ref[r0:r0 + RS, 0:C] = (dz * sc_ref[r0:r0 + RS, :]).astype(BF16)
            dout_ref[r0:r0 + RS, C:2 * C] = (do_ref[r0:r0 + RS, :] * cz).astype(BF16)
            dout_ref[r0:r0 + RS, 2 * C:3 * C] = (dz * sx_ref[r0:r0 + RS, :]).astype(BF16)

    pb, nb_ = _prev_blk(T, H), _next_blk(T, H, S)
    cur = lambda cb: pl.BlockSpec((T, C), lambda i: (i, cb))
    prev = lambda cb: pl.BlockSpec((H, C), lambda i: (pb(i), cb))
    nxt = lambda cb: pl.BlockSpec((H, C), lambda i: (nb_(i), cb))
    return pl.pallas_call(
        body, name=name,
        out_shape=(jax.ShapeDtypeStruct((S, 3 * C), BF16), jax.ShapeDtypeStruct((8, C), F32)),
        grid=(nI,),
        in_specs=[cur(2), cur(3), cur(4), prev(2), prev(4), nxt(3), cur(1), nxt(1),
                  pl.BlockSpec(w.shape, lambda i: (0, 0))],
        out_specs=(pl.BlockSpec((T, 3 * C), lambda i: (i, 0)), pl.BlockSpec((8, C), lambda i: (0, 0))),
        scratch_shapes=[pltpu.VMEM((H + T, C), F32), pltpu.VMEM((T + H, C), F32)],
        compiler_params=_params("arbitrary"))(rest, rest, rest, rest, rest, rest, dcs, dcs, w)


def _ffn_fwd(hu, w, b, name):
    S, F2 = hu.shape
    Fd = F2 // 2
    K = w.shape[0]
    H, T = 8, min(256, S)
    tc = _tile(Fd, 512)
    nJ = Fd // tc
    RS = min(64, T)
    base = H - (K - 1)

    def body(g_ref, v_ref, gp_ref, vp_ref, wg_ref, wv_ref, bg_ref, bv_ref, o_ref, eg_ref, ev_ref):
        i = pl.program_id(1)
        eg_ref[0:H, :] = jnp.where(i > 0, gp_ref[...], 0.0)
        ev_ref[0:H, :] = jnp.where(i > 0, vp_ref[...], 0.0)
        eg_ref[H:H + T, :] = g_ref[...]
        ev_ref[H:H + T, :] = v_ref[...]
        for r0 in range(0, T, RS):
            ug = _taps(eg_ref, wg_ref, K, base, r0, RS, slice(None), init=jnp.broadcast_to(bg_ref[...], (RS, tc)))
            uv = _taps(ev_ref, wv_ref, K, base, r0, RS, slice(None), init=jnp.broadcast_to(bv_ref[...], (RS, tc)))
            o_ref[r0:r0 + RS, :] = (ug * _sigmoid(ug) * uv).astype(BF16)

    pb = _prev_blk(T, H)
    cur = lambda off: pl.BlockSpec((T, tc), lambda j, i: (i, j + off))
    prev = lambda off: pl.BlockSpec((H, tc), lambda j, i: (pb(i), j + off))
    wsp = lambda off: pl.BlockSpec((K, tc), lambda j, i: (0, j + off))
    bsp = lambda off: pl.BlockSpec((1, tc), lambda j, i: (0, j + off))
    return pl.pallas_call(
        body, name=name, out_shape=jax.ShapeDtypeStruct((S, Fd), BF16), grid=(nJ, S // T),
        in_specs=[cur(0), cur(nJ), prev(0), prev(nJ), wsp(0), wsp(nJ), bsp(0), bsp(nJ)],
        out_specs=pl.BlockSpec((T, tc), lambda j, i: (i, j)),
        scratch_shapes=[pltpu.VMEM((H + T, tc), F32), pltpu.VMEM((H + T, tc), F32)],
        compiler_params=_params("parallel", "parallel"))(hu, hu, hu, hu, w, w, b, b)


def _ffn_bwd(hu, dact, w, b, name):
    S, F2 = hu.shape
    Fd = F2 // 2
    K = w.shape[0]
    H, T = 8, min(256, S)
    HB = 16
    tc = _tile(Fd, 512)
    nJ = Fd // tc
    nI = S // T
    RS = min(64, T)
    base = H - (K - 1)

    def body(g_ref, v_ref, gp_ref, vp_ref, gn_ref, vn_ref, da_ref, dan_ref, wg_ref, wv_ref, bg_ref, bv_ref,
             dg_ref, dv_ref, dwg_ref, dwv_ref, eg_ref, ev_ref, dug_ref, duv_ref):
        i = pl.program_id(1)
        eg_ref[0:H, :] = jnp.where(i > 0, gp_ref[...], 0.0)
        ev_ref[0:H, :] = jnp.where(i > 0, vp_ref[...], 0.0)
        eg_ref[H:H + T, :] = g_ref[...]
        ev_ref[H:H + T, :] = v_ref[...]
        eg_ref[H + T:H + T + H, :] = gn_ref[...]
        ev_ref[H + T:H + T + H, :] = vn_ref[...]

        @pl.when(i == 0)
        def _():
            dwg_ref[...] = jnp.zeros_like(dwg_ref)
            dwv_ref[...] = jnp.zeros_like(dwv_ref)

        def du_rows(r0, rows, d_a):
            ug = _taps(eg_ref, wg_ref, K, base, r0, rows, slice(None),
                       init=jnp.broadcast_to(bg_ref[...], (rows, tc)))
            uv = _taps(ev_ref, wv_ref, K, base, r0, rows, slice(None),
                       init=jnp.broadcast_to(bv_ref[...], (rows, tc)))
            sg = _sigmoid(ug)
            dug_ref[r0:r0 + rows, :] = d_a * uv * (sg * (1.0 + ug * (1.0 - sg)))
            duv_ref[r0:r0 + rows, :] = d_a * (ug * sg)

        for r0 in range(0, T, RS):
            du_rows(r0, RS, da_ref[r0:r0 + RS, :].astype(F32))
        du_rows(T, H, jnp.where(i < nI - 1, dan_ref[...].astype(F32)[0:H, :], 0.0))

        for ext, du, dw in ((eg_ref, dug_ref, dwg_ref), (ev_ref, duv_ref, dwv_ref)):
            for k in range(K):
                s = jnp.zeros((1, tc), F32)
                for r0 in range(0, T, RS):
                    s = s + jnp.sum(du[r0:r0 + RS, :] * ext[base + k + r0:base + k + r0 + RS, :],
                                    axis=0, keepdims=True)
                dw[k:k + 1, :] += s
            s = jnp.zeros((1, tc), F32)
            for r0 in range(0, T, RS):
                s = s + jnp.sum(du[r0:r0 + RS, :], axis=0, keepdims=True)
            dw[K:K + 1, :] += s
        for r0 in range(0, T, RS):
            dg_ref[r0:r0 + RS, :] = _taps(dug_ref, wg_ref, K, 0, r0, RS, slice(None), reverse=True).astype(BF16)
            dv_ref[r0:r0 + RS, :] = _taps(duv_ref, wv_ref, K, 0, r0, RS, slice(None), reverse=True).astype(BF16)

    pb, nb_, nbb = _prev_blk(T, H), _next_blk(T, H, S), _next_blk(T, HB, S)
    cur = lambda off: pl.BlockSpec((T, tc), lambda j, i: (i, j + off))
    prev = lambda off: pl.BlockSpec((H, tc), lambda j, i: (pb(i), j + off))
    nxt = lambda off: pl.BlockSpec((H, tc), lambda j, i: (nb_(i), j + off))
    wsp = lambda off: pl.BlockSpec((K, tc), lambda j, i: (0, j + off))
    bsp = lambda off: pl.BlockSpec((1, tc), lambda j, i: (0, j + off))
    half = jax.ShapeDtypeStruct((S, Fd), BF16)
    dws = jax.ShapeDtypeStruct((8, Fd), F32)
    return pl.pallas_call(
        body, name=name, out_shape=(half, half, dws, dws), grid=(nJ, nI),
        in_specs=[cur(0), cur(nJ), prev(0), prev(nJ), nxt(0), nxt(nJ),
                  pl.BlockSpec((T, tc), lambda j, i: (i, j)), pl.BlockSpec((HB, tc), lambda j, i: (nbb(i), j)),
                  wsp(0), wsp(nJ), bsp(0), bsp(nJ)],
        out_specs=(pl.BlockSpec((T, tc), lambda j, i: (i, j)), pl.BlockSpec((T, tc), lambda j, i: (i, j)),
                   pl.BlockSpec((8, tc), lambda j, i: (0, j)), pl.BlockSpec((8, tc), lambda j, i: (0, j))),
        scratch_shapes=[pltpu.VMEM((H + T + H, tc), F32), pltpu.VMEM((H + T + H, tc), F32),
                        pltpu.VMEM((T + H, tc), F32), pltpu.VMEM((T + H, tc), F32)],
        compiler_params=_params("parallel", "arbitrary"))(hu, hu, hu, hu, hu, hu, dact, dact, w, w, b, b)


def _position():
    return lax.axis_index("x"), lax.axis_index("y"), lax.axis_index("c")


def _slot(px, py, pc):
    return 4 * px + 2 * py + pc


def _gather_copies(x_ref, out_ref, sems, r, starting=False):
    send_sems, recv_sems, local_sems = sems
    px, py, pc = _position()
    me, sibling = (px, py, pc), (px, py, 1 - pc)
    chips = [(1 - px, py), (px, 1 - py), (1 - px, 1 - py)]

    def copy(k, block, to, src=None):
        dst = out_ref.at[_slot(*block)]
        return pltpu.make_async_remote_copy(
            src_ref=dst if src is None else src, dst_ref=dst,
            send_sem=send_sems.at[7 * r + k], recv_sem=recv_sems.at[7 * r + k], device_id=to, device_id_type=MESH)

    mine = pltpu.make_async_copy(x_ref, out_ref.at[_slot(*me)], local_sems.at[r])
    first = [copy(0, me, sibling, src=x_ref)] + [copy(1 + n, me, (*chip, pc), src=x_ref)
                                                  for n, chip in enumerate(chips)]
    if starting:
        return mine, first
    passed = [copy(4 + n, (*chip, pc), sibling) for n, chip in enumerate(chips)]
    landed = [copy(1 + n, (*chip, pc), me) for n, chip in enumerate(chips)]
    from_sibling = [copy(0, sibling, me)] + [copy(4 + n, (*chip, 1 - pc), me) for n, chip in enumerate(chips)]
    return mine, first, passed, landed, from_sibling


def _scatter_copies(g_ref, out_ref, sems, r, starting=False):
    send_sems, recv_sems, local_sems = sems
    px, py, pc = _position()
    me = _slot(px, py, pc)
    mine = pltpu.make_async_copy(g_ref.at[me], out_ref.at[me], local_sems.at[r])
    peers = [(px ^ fx, py ^ fy, pc ^ fc) for fx, fy, fc in PEER_FLIPS]

    def copy(k, peer, src_slot, dst_slot):
        return pltpu.make_async_remote_copy(
            src_ref=g_ref.at[src_slot], dst_ref=out_ref.at[dst_slot],
            send_sem=send_sems.at[7 * r + k], recv_sem=recv_sems.at[7 * r + k], device_id=peer, device_id_type=MESH)

    sends = [copy(k, peer, _slot(*peer), me) for k, peer in enumerate(peers)]
    if starting:
        return mine, sends
    arrivals = [copy(k, peer, me, _slot(*peer)) for k, peer in enumerate(peers)]
    return mine, sends, arrivals


def _rider_start(kind, in_ref, out_ref, sems, r):
    if kind == "gather":
        mine, first = _gather_copies(in_ref, out_ref, sems, r, starting=True)
    else:
        mine, first = _scatter_copies(in_ref, out_ref, sems, r, starting=True)
    mine.start()
    for cp in first:
        cp.start()


def _rider_finish(kind, in_ref, out_ref, sems, r):
    if kind == "gather":
        mine, first, passed, landed, from_sibling = _gather_copies(in_ref, out_ref, sems, r)
        for cp, fwd in zip(landed, passed):
            cp.wait_recv()
            fwd.start()
        for cp in from_sibling:
            cp.wait_recv()
        for cp in first + passed:
            cp.wait_send()
    else:
        mine, sends, arrivals = _scatter_copies(in_ref, out_ref, sems, r)
        for cp in arrivals:
            cp.wait_recv()
        for cp in sends:
            cp.wait_send()
    mine.wait()


def _all_gather(x, in_vmem, name):
    space = pltpu.VMEM if in_vmem else pl.ANY

    def body(x_ref, out_ref, send_sems, recv_sems, local_sems):
        sems = (send_sems, recv_sems, local_sems)
        _rider_start("gather", x_ref, out_ref, sems, 0)
        _rider_finish("gather", x_ref, out_ref, sems, 0)

    return pl.pallas_call(
        body, name=name, out_shape=jax.ShapeDtypeStruct((N_DEV,) + x.shape, x.dtype),
        in_specs=[pl.BlockSpec(memory_space=space)], out_specs=pl.BlockSpec(memory_space=space),
        scratch_shapes=[pltpu.SemaphoreType.DMA((7,)), pltpu.SemaphoreType.DMA((7,)), pltpu.SemaphoreType.DMA((1,))],
        compiler_params=pltpu.CompilerParams(vmem_limit_bytes=VMEM_LIMIT_BYTES),
    )(x)


def _adam_sum(stage, w, m, v, layer, prev, name):
    n = stage.shape[0]
    L, R, C = w.shape
    tr = R if R * C <= 256 * 1024 else _row_tile(R, C)
    c1 = 1.0 / (1.0 - ADAM_B1 ** ADAM_STEP)
    c2 = 1.0 / (1.0 - ADAM_B2 ** ADAM_STEP)

    def body(*refs):
        st_ref, w_ref, m_ref, v_ref = refs[:4]
        g_ref, d_ref, nm_ref, nv_ref = refs[-4:]
        g = st_ref[0].astype(F32)
        for s in range(1, n):
            g = g + st_ref[s].astype(F32)
        wv = w_ref[0]
        mn = ADAM_B1 * m_ref[0] + (1.0 - ADAM_B1) * g
        vn = ADAM_B2 * v_ref[0] + (1.0 - ADAM_B2) * (g * g)
        g_ref[0] = g
        nm_ref[0] = mn
        nv_ref[0] = vn
        d_ref[0] = -ADAM_LR * ((mn * c1) / (jnp.sqrt(vn * c2) + ADAM_EPS) + ADAM_WD * wv)

    lay = pl.BlockSpec((1, tr, C), lambda i: (layer, i, 0))
    in_specs = [pl.BlockSpec((n, tr, C), lambda i: (0, i, 0)), lay, lay, lay]
    ins = [stage, w, m, v]
    aliases = {}
    if prev is not None:
        in_specs += [pl.BlockSpec(memory_space=pl.ANY)] * 4
        ins += list(prev)
        aliases = {4: 0, 5: 1, 6: 2, 7: 3}
    shp = jax.ShapeDtypeStruct((L, R, C), F32)
    return pl.pallas_call(
        body, name=name, out_shape=(shp, shp, shp, shp), grid=(R // tr,),
        in_specs=in_specs, out_specs=(lay, lay, lay, lay), input_output_aliases=aliases,
        compiler_params=_params("parallel"))(*ins)


def _row_tile(R, C):
    cpad = -(-C // LANES) * LANES
    want = max(16, (256 * 1024) // cpad)
    best = 16
    for t in range(16, R + 1, 16):
        if R % t == 0 and t <= want:
            best = t
    return best


def _sum_slabs(st, name):
    n, R, C = st.shape
    tr = R if n * R * C * 4 <= (12 << 20) else _row_tile(R, C)

    def body(st_ref, o_ref):
        g = st_ref[0]
        for s in range(1, n):
            g = g + st_ref[s]
        o_ref[...] = g

    return pl.pallas_call(
        body, name=name, out_shape=jax.ShapeDtypeStruct((R, C), F32), grid=(R // tr,),
        in_specs=[pl.BlockSpec((n, tr, C), lambda i: (0, i, 0))], out_specs=pl.BlockSpec((tr, C), lambda i: (i, 0)),
        compiler_params=_params("parallel"))(st)


def _pack(arrs):
    flat = [a.reshape(-1).astype(F32) for a in arrs]
    sizes = [f.shape[0] for f in flat]
    total = sum(sizes)
    padded = -(-total // (16 * LANES)) * (16 * LANES)
    if padded > total:
        flat.append(jnp.zeros((padded - total,), F32))
    return jnp.concatenate(flat).reshape(padded // LANES, LANES), (sizes, [a.shape for a in arrs])


def _unpack(packed, layout, lead=()):
    sizes, shapes = layout
    flat = packed.reshape(lead + (-1,))
    out, off = [], 0
    for sz, shp in zip(sizes, shapes):
        out.append(flat[..., off:off + sz].reshape(lead + tuple(shp)))
        off += sz
    return out


class _NoComm:
    def __init__(self, wts):
        self.wts, self.grads = wts, {}

    def weights(self, l):
        return self.wts[l]

    def gather_rider(self, l, names):
        return None

    def scatter_rider(self, name, l, g):
        self.grads[(name, l)] = g
        return None


def _local_step(x, tgt, ada, mix_norm_g, comm, b_forget, conf_dw_w, conf_dw_b, conf_ln_g, conf_ln_b, sc_dw_w,
                ffn_norm_g, ffn_dw_w, ffn_dw_b, final_norm_g):
    S, D = x.shape
    L = ada.shape[0]

    def mm_gather(a, b, dtype, name, l_next, names):
        rider = comm.gather_rider(l_next, names) if l_next < L else None
        if rider is None:
            return _matmul(a, b, dtype, name=name)
        out, got = _matmul(a, b, dtype, name="cm_" + name, rider=rider)
        comm.gathered(l_next, names, got)
        return out

    def mm_scatter(a, b, dtype, name, wname, l, g):
        rider = comm.scatter_rider(wname, l, g)
        if rider is None:
            return _matmul(a, b, dtype, name=name)
        out, got = _matmul(a, b, dtype, name="cm_" + name, rider=rider)
        comm.scattered(wname, l, got[0])
        return out

    H = b_forget.shape[1]
    DA = H * HEAD_DIM
    C = conf_dw_b.shape[1]
    NQ = 3 * DA
    NR = 5 * C + LANES
    fblk = (5 * C) // LANES
    row = lambda a: a.reshape(1, -1)
    adav = ada.reshape(L, N_ADA, 1, D)

    saved = []
    xcur, delta, gate = x, None, None
    for l in range(L):
        sh_m, sc_m, g_m, sh_f, sc_f, g_f = [adav[l, n] for n in range(N_ADA)]
        w = comm.weights(l)
        x1, h1 = _site_fwd(xcur, delta, gate, row(mix_norm_g[l]), sc_m, sh_m, name="site_fwd_mix")
        qkv = _matmul(h1, w["w_in_perm"][:, :NQ], BF16, name="mm_qkv")
        rest = _matmul(h1, w["w_in_perm"][:, NQ:], F32, name="mm_rest")
        bpad = jnp.zeros((1, LANES), F32).at[0, :H].set(b_forget[l])
        Fc = _fgate_fwd(rest, bpad, fblk, name="fgate_fwd")
        nf = -LOG2E * jnp.transpose(Fc[:, :H])
        attn, attn32, lse = _attn_fwd(qkv, jnp.broadcast_to(nf[:, :, None], (H, S, LANES)), H, name="attn_fwd")
        conf = _conf_fwd(rest, conf_dw_w[l], row(conf_dw_b[l]), row(conf_ln_g[l]), row(conf_ln_b[l]), name="conf_fwd")
        sconv = _sconv_fwd(rest, sc_dw_w[l], name="sconv_fwd")
        mixcat = jnp.concatenate([attn, conf, sconv], axis=1)
        mixed = _matmul(mixcat, w["w_out"], F32, name="mm_out")
        x2, h2 = _site_fwd(x1, mixed, g_m, row(ffn_norm_g[l]), sc_f, sh_f, name="site_fwd_ffn")
        hu = mm_gather(h2, w["w_up"], F32, "mm_up", l + 1, ("w_up", "w_down"))
        act = _ffn_fwd(hu, ffn_dw_w[l], row(ffn_dw_b[l]), name="ffn_fwd")
        ffn_out = mm_gather(act, w["w_down"], F32, "mm_down", l + 1, ("w_in", "w_out"))
        saved.append(dict(x1=x1, h1=h1, qkv=qkv, rest=rest, bpad=bpad, nf=nf, attn32=attn32, lse=lse, mixcat=mixcat,
                          mixed=mixed, x2=x2, h2=h2, hu=hu, act=act, ffn_out=ffn_out))
        xcur, delta, gate = x2, ffn_out, g_f

    loss_lanes, dx, d_delta, d_gate, d_gfin = _final_fwd_bwd(xcur, delta, gate, row(final_norm_g), tgt, name="final")
    loss = (0.5 / D) * jnp.sum(loss_lanes)

    grads = dict(final_norm_g=d_gfin[0], ada=[None] * L, mix_norm_g=[None] * L, ffn_norm_g=[None] * L,
                 b_forget=[None] * L, conf_dw_w=[None] * L, conf_dw_b=[None] * L, conf_ln_g=[None] * L,
                 conf_ln_b=[None] * L, sc_dw_w=[None] * L, ffn_dw_w=[None] * L, ffn_dw_b=[None] * L)
    K3 = ffn_dw_w.shape[1]
    for l in reversed(range(L)):
        sv, w = saved[l], comm.weights(l)
        sh_m, sc_m, g_m, sh_f, sc_f, g_f = [adav[l, n] for n in range(N_ADA)]
        d_gf = d_gate
        g_down = _matmul_tn(sv["act"], d_delta, BF16, name="mm_dw_down")
        dact = mm_scatter(d_delta, jnp.transpose(w["w_down"]), BF16, "mm_dact", "w_down", l, g_down)
        dhu_g, dhu_v, dwg, dwv = _ffn_bwd(sv["hu"], dact, ffn_dw_w[l], row(ffn_dw_b[l]), name="ffn_bwd")
        grads["ffn_dw_w"][l] = jnp.concatenate([dwg[:K3], dwv[:K3]], axis=1)
        grads["ffn_dw_b"][l] = jnp.concatenate([dwg[K3], dwv[K3]])
        dhu = jnp.concatenate([dhu_g, dhu_v], axis=1)
        g_up = _matmul_tn(sv["h2"], dhu, BF16, name="mm_dw_up")
        dh2 = mm_scatter(dhu, jnp.transpose(w["w_up"]), F32, "mm_dh2", "w_up", l, g_up)
        dx, d_sh_f, d_a_f, d_mixed, d_gm = _site_bwd(sv["x2"], dh2, dx, row(ffn_norm_g[l]), sc_f,
                                                      sh_f, sv["mixed"], g_m, name="site_bwd_ffn")
        grads["ffn_norm_g"][l] = (d_a_f * (1.0 + sc_f))[0]
        d_sc_f = d_a_f * row(ffn_norm_g[l])
        g_out = _matmul_tn(sv["mixcat"], d_mixed, BF16, name="mm_dw_out")
        w_out_t = jnp.transpose(w["w_out"])
        dattn = mm_scatter(d_mixed, w_out_t[:, :DA], BF16, "mm_dattn", "w_out", l, g_out)
        dcs = _matmul(d_mixed, w_out_t[:, DA:], F32, name="mm_dcs")
        delta_a = _blocked_rows(_attn_delta(sv["attn32"], dattn, H, name="attn_delta")[:, :, 0], min(ATTN_TILE, S))
        nfb = jnp.broadcast_to(sv["nf"][:, :, None], (H, S, LANES))
        dq, drow = _attn_bwd_dq(sv["qkv"], nfb, dattn, sv["lse"], delta_a, H, name="attn_bwd_dq")
        dk, dv, dnf = _attn_bwd_dkv(sv["qkv"], nfb, dattn, sv["lse"], delta_a, H, name="attn_bwd_dkv")
        dF = jnp.zeros((S, LANES), F32).at[:, :H].set(jnp.transpose(drow.reshape(H, S) - dnf[:, :, 0]))
        dfl, dbf = _fgate_bwd(sv["rest"], sv["bpad"], dF, fblk, name="fgate_bwd")
        grads["b_forget"][l] = dbf[0, :H]
        dcvg, dcw, dcvec = _conf_bwd(sv["rest"], dcs, conf_dw_w[l], row(conf_dw_b[l]), row(conf_ln_g[l]),
                                     row(conf_ln_b[l]), name="conf_bwd")
        grads["conf_dw_w"][l] = dcw[:conf_dw_w.shape[1]]
        grads["conf_dw_b"][l], grads["conf_ln_g"][l], grads["conf_ln_b"][l] = dcvec[0], dcvec[1], dcvec[2]
        dsc3, dsw = _sconv_bwd(sv["rest"], dcs, sc_dw_w[l], name="sconv_bwd")
        grads["sc_dw_w"][l] = dsw[:sc_dw_w.shape[1]]
        dproj = jnp.concatenate([dq, dk, dv, dcvg, dsc3, dfl], axis=1)
        g_in = _matmul_tn(sv["h1"], dproj, BF16, name="mm_dw_in")
        dh1 = mm_scatter(dproj, jnp.transpose(w["w_in_perm"]), F32, "mm_dh1", "w_in_perm", l, g_in)
        if l > 0:
            pv = saved[l - 1]
            g_f_prev = adav[l - 1, 5]
            dx, d_sh_m, d_a_m, d_delta, d_gate = _site_bwd(sv["x1"], dh1, dx, row(mix_norm_g[l]), sc_m, sh_m,
                                                           pv["ffn_out"], g_f_prev, name="site_bwd_mix")
        else:
            dx, d_sh_m, d_a_m = _site_bwd(sv["x1"], dh1, dx, row(mix_norm_g[l]), sc_m, sh_m, None, None,
                                          name="site_bwd_first")
        grads["mix_norm_g"][l] = (d_a_m * (1.0 + sc_m))[0]
        d_sc_m = d_a_m * row(mix_norm_g[l])
        grads["ada"][l] = jnp.concatenate([d_sh_m, d_sc_m, d_gm, d_sh_f, d_sc_f, d_gf], axis=1)[0]
    return loss, dx, grads


def kernel(x, c, ada_w, ada_b, mix_norm_g, w_in, b_forget, conf_dw_w, conf_dw_b, conf_ln_g, conf_ln_b, sc_dw_w, w_out, ffn_norm_g, w_up, ffn_dw_w, ffn_dw_b, w_down, final_norm_g, loss_target, m_ada_w, m_ada_b, m_mix_norm_g, m_w_in, m_b_forget, m_conf_dw_w, m_conf_dw_b, m_conf_ln_g, m_conf_ln_b, m_sc_dw_w, m_w_out, m_ffn_norm_g, m_w_up, m_ffn_dw_w, m_ffn_dw_b, m_w_down, m_final_norm_g, v_ada_w, v_ada_b, v_mix_norm_g, v_w_in, v_b_forget, v_conf_dw_w, v_conf_dw_b, v_conf_ln_g, v_conf_ln_b, v_sc_dw_w, v_w_out, v_ffn_norm_g, v_w_up, v_ffn_dw_w, v_ffn_dw_b, v_w_down, v_final_norm_g):
    L, D, ada_loc = ada_w.shape
    S = x.shape[1]
    H = b_forget.shape[1]
    DA = H * HEAD_DIM
    C = conf_dw_b.shape[1]
    in_loc = w_in.shape[2]
    IN = in_loc * N_DEV
    px, py, pc = _position()
    me = _slot(px, py, pc)

    pk, lay = _pack([c, conf_dw_w, sc_dw_w, ffn_dw_w])
    gathered = _all_gather(pk, True, name="ag_small_fwd")
    c_all, cw_all, sw_all, fw_all = _unpack(gathered, lay, lead=(N_DEV,))
    c_all = c_all[:, 0]
    unshard = lambda a: jnp.moveaxis(a, 0, 2).reshape(a.shape[1], a.shape[2], -1)
    conf_w_full, sc_w_full, ffn_w_full = unshard(cw_all), unshard(sw_all), unshard(fw_all)
    c_act = c_all * jax.nn.sigmoid(c_all)
    c_act16 = jnp.zeros((16, D), F32).at[:N_DEV].set(c_act).astype(BF16)
    ada_cols = jnp.stack([_matmul(c_act16, ada_w[l].astype(BF16), F32, name="mm_ada")[:N_DEV] for l in range(L)])
    ada_g = _all_gather(ada_cols.reshape(L * N_DEV, ada_loc), True, name="ag_ada")
    ada_mine = lax.dynamic_index_in_dim(ada_g.reshape(N_DEV, L, N_DEV, ada_loc), me, axis=2, keepdims=False)
    ada = jnp.moveaxis(ada_mine, 0, 1).reshape(L, N_DEV * ada_loc) + ada_b

    NQ = 3 * DA
    PR = NQ + 5 * C
    shards = dict(w_in=w_in.astype(BF16), w_out=w_out.astype(BF16), w_up=w_up.astype(BF16),
                  w_down=w_down.astype(BF16))

    def shard_cols(g):
        return jnp.moveaxis(g.reshape(g.shape[0], N_DEV, -1), 1, 0)

    def shard_rows(g):
        return g.reshape(N_DEV, -1, g.shape[1])

    class MeshComm:
        def __init__(self):
            self.got = {0: {n: _all_gather(shards[n][0], False, name="ag_" + n) for n in shards}}
            self.full, self.stage = {}, {}

        def weights(self, l):
            if l not in self.full:
                g = self.got[l]
                wi = jnp.moveaxis(g["w_in"], 0, 1).reshape(D, IN)
                w_in_perm = jnp.concatenate([wi[:, :NQ], wi[:, NQ + H:], wi[:, NQ:NQ + H],
                                             jnp.zeros((D, LANES - H), BF16)], axis=1)
                self.full[l] = dict(w_in_perm=w_in_perm, w_out=g["w_out"].reshape(-1, D),
                                    w_up=jnp.moveaxis(g["w_up"], 0, 1).reshape(D, -1),
                                    w_down=g["w_down"].reshape(-1, D))
            return self.full[l]

        def gather_rider(self, l, names):
            return "gather", [shards[n][l] for n in names]

        def gathered(self, l, names, outs):
            self.got.setdefault(l, {}).update(zip(names, outs))

        def scatter_rider(self, name, l, g):
            if name == "w_in_perm":
                slabs = shard_cols(jnp.concatenate([g[:, :NQ], g[:, PR:PR + H], g[:, NQ:PR]], axis=1))
            elif name == "w_up":
                slabs = shard_cols(g)
            else:
                slabs = shard_rows(g)
            return "scatter", [slabs]

        def scattered(self, name, l, out):
            self.stage[(name, l)] = out

    comm = MeshComm()
    loss_loc, dx, gr = _local_step(x[0], loss_target[0], ada, mix_norm_g, comm, b_forget, conf_w_full, conf_dw_b,
                                   conf_ln_g, conf_ln_b, sc_w_full, ffn_norm_g, ffn_w_full, ffn_dw_b, final_norm_g)
    loss = lax.psum(loss_loc, ("x", "y", "c"))

    small_names = ["ada", "mix_norm_g", "ffn_norm_g", "b_forget", "conf_dw_b", "conf_ln_g", "conf_ln_b",
                   "ffn_dw_b", "conf_dw_w", "sc_dw_w", "ffn_dw_w"]
    pk, lay = _pack([jnp.stack(gr[n]) for n in small_names] + [gr["final_norm_g"]])
    parts = _all_gather(pk, True, name="ag_small_bwd")
    tot = _unpack(_sum_slabs(parts, name="sum_small"), lay)
    g_small = dict(zip(small_names + ["final_norm_g"], tot))
    d_ada_all = _unpack(parts, lay, lead=(N_DEV,))[0]
    my_cols = lambda a, n: lax.dynamic_slice_in_dim(a, me * n, n, axis=a.ndim - 1)

    c_act_t = jnp.zeros((D, LANES), F32).at[:, :N_DEV].set(jnp.transpose(c_act)).astype(BF16)
    res = None
    for l in range(L):
        d_loc = jnp.zeros((LANES, ada_loc), F32).at[:N_DEV].set(my_cols(d_ada_all[:, l], ada_loc)).astype(BF16)
        g_l = _matmul(c_act_t, d_loc, F32, name="mm_dada")
        res = _adam_sum(g_l[None], ada_w, m_ada_w, v_ada_w, l, res, name="adam_ada_w")
    out_ada_w = res

    big = {}
    for nm, key, wq, mq, vq in (("w_down", "w_down", w_down, m_w_down, v_w_down), ("w_up", "w_up", w_up, m_w_up, v_w_up),
                                ("w_out", "w_out", w_out, m_w_out, v_w_out), ("w_in", "w_in_perm", w_in, m_w_in, v_w_in)):
        res = None
        for l in reversed(range(L)):
            res = _adam_sum(comm.stage[(key, l)], wq, mq, vq, l, res, name="adam_" + nm)
        big[nm] = res

    K31, K3 = conf_dw_w.shape[1], sc_dw_w.shape[1]
    sm = [("ada_b", ada_b, m_ada_b, v_ada_b, g_small["ada"]),
          ("mix_norm_g", mix_norm_g, m_mix_norm_g, v_mix_norm_g, g_small["mix_norm_g"]),
          ("b_forget", b_forget, m_b_forget, v_b_forget, g_small["b_forget"]),
          ("conf_dw_w", conf_dw_w, m_conf_dw_w, v_conf_dw_w, my_cols(g_small["conf_dw_w"], conf_dw_w.shape[2])),
          ("conf_dw_b", conf_dw_b, m_conf_dw_b, v_conf_dw_b, g_small["conf_dw_b"]),
          ("conf_ln_g", conf_ln_g, m_conf_ln_g, v_conf_ln_g, g_small["conf_ln_g"]),
          ("conf_ln_b", conf_ln_b, m_conf_ln_b, v_conf_ln_b, g_small["conf_ln_b"]),
          ("sc_dw_w", sc_dw_w, m_sc_dw_w, v_sc_dw_w, my_cols(g_small["sc_dw_w"], sc_dw_w.shape[2])),
          ("ffn_norm_g", ffn_norm_g, m_ffn_norm_g, v_ffn_norm_g, g_small["ffn_norm_g"]),
          ("ffn_dw_w", ffn_dw_w, m_ffn_dw_w, v_ffn_dw_w, my_cols(g_small["ffn_dw_w"], ffn_dw_w.shape[2])),
          ("ffn_dw_b", ffn_dw_b, m_ffn_dw_b, v_ffn_dw_b, g_small["ffn_dw_b"]),
          ("final_norm_g", final_norm_g, m_final_norm_g, v_final_norm_g, g_small["final_norm_g"])]
    pw, lay = _pack([t[1] for t in sm])
    pm, _ = _pack([t[2] for t in sm])
    pv, _ = _pack([t[3] for t in sm])
    pg, _ = _pack([t[4] for t in sm])
    sres = _adam_sum(pg[None], pw[None], pm[None], pv[None], 0, None, name="adam_small")
    s_g, s_d, s_m, s_v = [dict(zip([t[0] for t in sm], _unpack(r[0], lay))) for r in sres]

    def pick(idx, name):
        if name == "ada_w":
            return out_ada_w[idx]
        if name in big:
            return big[name][idx]
        return (s_g, s_d, s_m, s_v)[idx][name]

    order = ["ada_w", "ada_b", "mix_norm_g", "w_in", "b_forget", "conf_dw_w", "conf_dw_b", "conf_ln_g", "conf_ln_b",
             "sc_dw_w", "w_out", "ffn_norm_g", "w_up", "ffn_dw_w", "ffn_dw_b", "w_down", "final_norm_g"]
    outs = [loss, dx[None]]
    for idx in range(4):
        outs += [pick(idx, n) for n in order]
    return tuple(outs)
```

```python
import functools

import jax
import jax.numpy as jnp
from jax import lax
from jax.experimental import pallas as pl
from jax.experimental.pallas import tpu as pltpu

F32 = jnp.float32
BF16 = jnp.bfloat16
RMS_EPS = 1e-6
LN_EPS = 1e-5
HEAD_DIM = 128
N_ADA = 6
ADAM_LR = 0.001
ADAM_B1 = 0.9
ADAM_B2 = 0.999
ADAM_EPS = 1e-08
ADAM_WD = 0.01
ADAM_STEP = 10
N_DEV = 8
LANES = 128
VMEM_LIMIT_BYTES = 56 * 1024 * 1024
MM_TILE = 1024
MM_TILE_WIDE = 1536
MESH = pl.DeviceIdType.MESH
PEER_FLIPS = ((0, 0, 1), (1, 0, 0), (0, 1, 0), (1, 1, 0), (1, 0, 1), (0, 1, 1), (1, 1, 1))


def _params(*sem):
    return pltpu.CompilerParams(dimension_semantics=sem, vmem_limit_bytes=VMEM_LIMIT_BYTES)


def _tile(n, cap):
    if n <= cap:
        return n
    for t in range(cap - cap % LANES, 0, -LANES):
        if n % t == 0:
            return t
    raise ValueError(f"no tile for {n}")


def _sigmoid(v):
    return jax.nn.sigmoid(v)


def _matmul(a, b, out_dtype, name, rider=None):
    M, K = a.shape
    _, N = b.shape
    tm, tn = _tile(M, MM_TILE), _tile(N, MM_TILE_WIDE)
    tk = K if K <= 2048 else _tile(K, MM_TILE_WIDE)
    nk = K // tk
    grid = (M // tm, N // tn, nk)
    kind, arrs = rider if rider is not None else (None, [])
    nr = len(arrs)

    def body(*refs):
        a_ref, b_ref = refs[:2]
        r_in = refs[2:2 + nr]
        o_ref = refs[2 + nr]
        r_out = refs[3 + nr:3 + 2 * nr]
        rest = refs[3 + 2 * nr:]
        i, j, k = pl.program_id(0), pl.program_id(1), pl.program_id(2)
        if nr:
            sems = rest[-3:]

            @pl.when((i == 0) & (j == 0) & (k == 0))
            def _():
                for r in range(nr):
                    _rider_start(kind, r_in[r], r_out[r], sems, r)

        if nk == 1:
            o_ref[...] = jnp.dot(a_ref[...], b_ref[...], preferred_element_type=F32).astype(o_ref.dtype)
        else:
            acc_ref = rest[0]

            @pl.when(k == 0)
            def _():
                acc_ref[...] = jnp.zeros_like(acc_ref)

            acc_ref[...] += jnp.dot(a_ref[...], b_ref[...], preferred_element_type=F32)

            @pl.when(k == nk - 1)
            def _():
                o_ref[...] = acc_ref[...].astype(o_ref.dtype)

        if nr:
            @pl.when((i == grid[0] - 1) & (j == grid[1] - 1) & (k == nk - 1))
            def _():
                for r in range(nr):
                    _rider_finish(kind, r_in[r], r_out[r], sems, r)

    scratch = [] if nk == 1 else [pltpu.VMEM((tm, tn), F32)]
    hbm = pl.BlockSpec(memory_space=pl.ANY)
    out_shape = jax.ShapeDtypeStruct((M, N), out_dtype)
    out_specs = pl.BlockSpec((tm, tn), lambda i, j, k: (i, j))
    if nr:
        scratch += [pltpu.SemaphoreType.DMA((7 * nr,)), pltpu.SemaphoreType.DMA((7 * nr,)),
                    pltpu.SemaphoreType.DMA((nr,))]
        out_shape = (out_shape,) + tuple(
            jax.ShapeDtypeStruct(x.shape if kind == "scatter" else (N_DEV,) + x.shape, x.dtype) for x in arrs)
        out_specs = (out_specs,) + (hbm,) * nr
    out = pl.pallas_call(
        body, name=name,
        out_shape=out_shape,
        grid=grid,
        in_specs=[pl.BlockSpec((tm, tk), lambda i, j, k: (i, k)),
                  pl.BlockSpec((tk, tn), lambda i, j, k: (k, j))] + [hbm] * nr,
        out_specs=out_specs,
        scratch_shapes=scratch,
        compiler_params=_params(*(("arbitrary",) * 3 if nr else ("parallel", "parallel", "arbitrary"))),
    )(a, b, *arrs)
    return (out[0], list(out[1:])) if nr else out


_TN = (((0,), (0,)), ((), ()))


def _matmul_tn(a, b, out_dtype, name):
    S, M = a.shape
    _, N = b.shape
    tm, tn, ts = _tile(M, MM_TILE_WIDE), _tile(N, MM_TILE_WIDE), _tile(S, MM_TILE)
    ns = S // ts

    def body(a_ref, b_ref, o_ref, acc_ref):
        k = pl.program_id(2)

        @pl.when(k == 0)
        def _():
            acc_ref[...] = jnp.zeros_like(acc_ref)

        acc_ref[...] += lax.dot_general(a_ref[...], b_ref[...], _TN, preferred_element_type=F32)

        @pl.when(k == ns - 1)
        def _():
            o_ref[...] = acc_ref[...].astype(o_ref.dtype)

    return pl.pallas_call(
        body, name=name,
        out_shape=jax.ShapeDtypeStruct((M, N), out_dtype),
        grid=(M // tm, N // tn, ns),
        in_specs=[pl.BlockSpec((ts, tm), lambda i, j, k: (k, i)),
                  pl.BlockSpec((ts, tn), lambda i, j, k: (k, j))],
        out_specs=pl.BlockSpec((tm, tn), lambda i, j, k: (i, j)),
        scratch_shapes=[pltpu.VMEM((tm, tn), F32)],
        compiler_params=_params("parallel", "parallel", "arbitrary"),
    )(a, b)


def _site_fwd(x, delta, gate, g, sc, sh, name):
    S, D = x.shape
    T = min(256, S)
    res = delta is not None

    def body(*refs):
        if res:
            x_ref, d_ref, gate_ref, g_ref, sc_ref, sh_ref, xo_ref, h_ref = refs
            xv = x_ref[...] + gate_ref[...] * d_ref[...]
            xo_ref[...] = xv
        else:
            x_ref, g_ref, sc_ref, sh_ref, h_ref = refs
            xv = x_ref[...]
        r = lax.rsqrt(jnp.mean(xv * xv, axis=-1, keepdims=True) + RMS_EPS)
        a = g_ref[...] * (1.0 + sc_ref[...])
        h_ref[...] = (xv * r * a + sh_ref[...]).astype(BF16)

    row = pl.BlockSpec((T, D), lambda i: (i, 0))
    vec = pl.BlockSpec((1, D), lambda i: (0, 0))
    if res:
        ins, in_specs = (x, delta, gate, g, sc, sh), [row, row, vec, vec, vec, vec]
        out_shape = (jax.ShapeDtypeStruct((S, D), F32), jax.ShapeDtypeStruct((S, D), BF16))
        out_specs = (row, row)
    else:
        ins, in_specs = (x, g, sc, sh), [row, vec, vec, vec]
        out_shape = jax.ShapeDtypeStruct((S, D), BF16)
        out_specs = row
    out = pl.pallas_call(body, name=name, out_shape=out_shape, grid=(S // T,), in_specs=in_specs,
                         out_specs=out_specs, compiler_params=_params("parallel"))(*ins)
    return out if res else (x, out)


def _site_bwd(x, dh, dres, g, sc, sh, delta, gate, name):
    S, D = x.shape
    T = min(256, S)
    res = delta is not None

    def body(*refs):
        if res:
            (x_ref, dh_ref, dres_ref, g_ref, sc_ref, delta_ref, gate_ref,
             dx_ref, dsh_ref, da_ref, dd_ref, dgate_ref) = refs
        else:
            x_ref, dh_ref, dres_ref, g_ref, sc_ref, dx_ref, dsh_ref, da_ref = refs
        i = pl.program_id(0)
        xv = x_ref[...]
        dhv = dh_ref[...]
        r = lax.rsqrt(jnp.mean(xv * xv, axis=-1, keepdims=True) + RMS_EPS)
        xh = xv * r
        dxh = dhv * (g_ref[...] * (1.0 + sc_ref[...]))
        dx = r * (dxh - xh * jnp.mean(dxh * xh, axis=-1, keepdims=True)) + dres_ref[...]
        dx_ref[...] = dx

        @pl.when(i == 0)
        def _():
            dsh_ref[...] = jnp.zeros_like(dsh_ref)
            da_ref[...] = jnp.zeros_like(da_ref)
            if res:
                dgate_ref[...] = jnp.zeros_like(dgate_ref)

        dsh_ref[...] += jnp.sum(dhv, axis=0, keepdims=True)
        da_ref[...] += jnp.sum(dhv * xh, axis=0, keepdims=True)
        if res:
            dd_ref[...] = (gate_ref[...] * dx).astype(BF16)
            dgate_ref[...] += jnp.sum(dx * delta_ref[...], axis=0, keepdims=True)

    row = pl.BlockSpec((T, D), lambda i: (i, 0))
    vec = pl.BlockSpec((1, D), lambda i: (0, 0))
    vshape = jax.ShapeDtypeStruct((1, D), F32)
    if res:
        ins, in_specs = (x, dh, dres, g, sc, delta, gate), [row, row, row, vec, vec, row, vec]
        out_shape = (jax.ShapeDtypeStruct((S, D), F32), vshape, vshape, jax.ShapeDtypeStruct((S, D), BF16), vshape)
        out_specs = (row, vec, vec, row, vec)
    else:
        ins, in_specs = (x, dh, dres, g, sc), [row, row, row, vec, vec]
        out_shape = (jax.ShapeDtypeStruct((S, D), F32), vshape, vshape)
        out_specs = (row, vec, vec)
    return pl.pallas_call(body, name=name, out_shape=out_shape, grid=(S // T,), in_specs=in_specs,
                          out_specs=out_specs, compiler_params=_params("arbitrary"))(*ins)


def _final_fwd_bwd(x, delta, gate, gfin, target, name):
    S, D = x.shape
    T = min(256, S)

    def body(x_ref, delta_ref, gate_ref, g_ref, t_ref, loss_ref, dx_ref, dd_ref, dgate_ref, dg_ref):
        i = pl.program_id(0)
        dl = delta_ref[...]
        xv = x_ref[...] + gate_ref[...] * dl
        r = lax.rsqrt(jnp.mean(xv * xv, axis=-1, keepdims=True) + RMS_EPS)
        xh = xv * r
        gv = g_ref[...]
        e = xh * gv - t_ref[...]
        dy = e * (1.0 / D)
        dxh = dy * gv
        dx = r * (dxh - xh * jnp.mean(dxh * xh, axis=-1, keepdims=True))
        dx_ref[...] = dx
        dd_ref[...] = (gate_ref[...] * dx).astype(BF16)

        @pl.when(i == 0)
        def _():
            loss_ref[...] = jnp.zeros_like(loss_ref)
            dgate_ref[...] = jnp.zeros_like(dgate_ref)
            dg_ref[...] = jnp.zeros_like(dg_ref)

        loss_ref[...] += jnp.sum(e * e, axis=0, keepdims=True)
        dgate_ref[...] += jnp.sum(dx * dl, axis=0, keepdims=True)
        dg_ref[...] += jnp.sum(dy * xh, axis=0, keepdims=True)

    row = pl.BlockSpec((T, D), lambda i: (i, 0))
    vec = pl.BlockSpec((1, D), lambda i: (0, 0))
    vshape = jax.ShapeDtypeStruct((1, D), F32)
    return pl.pallas_call(
        body, name=name,
        out_shape=(vshape, jax.ShapeDtypeStruct((S, D), F32), jax.ShapeDtypeStruct((S, D), BF16), vshape, vshape),
        grid=(S // T,), in_specs=[row, row, vec, vec, row], out_specs=(vec, row, row, vec, vec),
        compiler_params=_params("arbitrary"))(x, delta, gate, gfin, target)


def _split3(v):
    hi = v.astype(BF16)
    r1 = v - hi.astype(F32)
    mid = r1.astype(BF16)
    lo = (r1 - mid.astype(F32)).astype(BF16)
    return hi, mid, lo


def _tri_dot(tri, v):
    hi, mid, lo = _split3(v)
    d = functools.partial(jnp.dot, preferred_element_type=F32)
    return d(tri, hi) + d(tri, mid) + d(tri, lo)


def _fgate_fwd(rest, bpad, fblk, name):
    S = rest.shape[0]
    CH = min(256, S)
    nch = S // CH

    def body(f_ref, b_ref, o_ref):
        row = lax.broadcasted_iota(jnp.int32, (CH, CH), 0)
        col = lax.broadcasted_iota(jnp.int32, (CH, CH), 1)
        tri = (row >= col).astype(BF16)

        def step(ci, carry):
            rows = pl.ds(pl.multiple_of(ci * CH, CH), CH)
            z = f_ref[rows, :] + b_ref[...]
            lf = jnp.minimum(z, 0.0) - jnp.log(1.0 + jnp.exp(-jnp.abs(z)))
            o_ref[rows, :] = _tri_dot(tri, lf) + carry
            return carry + jnp.sum(lf, axis=0, keepdims=True)

        lax.fori_loop(0, nch, step, jnp.zeros((1, LANES), F32))

    return pl.pallas_call(
        body, name=name, out_shape=jax.ShapeDtypeStruct((S, LANES), F32), grid=(1,),
        in_specs=[pl.BlockSpec((S, LANES), lambda i: (0, fblk)), pl.BlockSpec((1, LANES), lambda i: (0, 0))],
        out_specs=pl.BlockSpec((S, LANES), lambda i: (0, 0)),
        compiler_params=_params("arbitrary"))(rest, bpad)


def _fgate_bwd(rest, bpad, dF, fblk, name):
    S = rest.shape[0]
    CH = min(256, S)
    nch = S // CH

    def body(f_ref, b_ref, df_ref, o_ref, db_ref):
        row = lax.broadcasted_iota(jnp.int32, (CH, CH), 0)
        col = lax.broadcasted_iota(jnp.int32, (CH, CH), 1)
        tri = (col >= row).astype(BF16)

        def step(n, carry):
            sfx_carry, db = carry
            ci = nch - 1 - n
            rows = pl.ds(pl.multiple_of(ci * CH, CH), CH)
            z = f_ref[rows, :] + b_ref[...]
            dfv = df_ref[rows, :]
            dz = (_tri_dot(tri, dfv) + sfx_carry) * _sigmoid(-z)
            o_ref[rows, :] = dz.astype(BF16)
            return sfx_carry + jnp.sum(dfv, axis=0, keepdims=True), db + jnp.sum(dz, axis=0, keepdims=True)

        zero = jnp.zeros((1, LANES), F32)
        _, db = lax.fori_loop(0, nch, step, (zero, zero))
        db_ref[...] = db

    blk = pl.BlockSpec((S, LANES), lambda i: (0, 0))
    return pl.pallas_call(
        body, name=name,
        out_shape=(jax.ShapeDtypeStruct((S, LANES), BF16), jax.ShapeDtypeStruct((1, LANES), F32)), grid=(1,),
        in_specs=[pl.BlockSpec((S, LANES), lambda i: (0, fblk)), pl.BlockSpec((1, LANES), lambda i: (0, 0)), blk],
        out_specs=(blk, pl.BlockSpec((1, LANES), lambda i: (0, 0))),
        compiler_params=_params("arbitrary"))(rest, bpad, dF)


_NT = (((1,), (1,)), ((), ()))
LOG2E = 1.4426950408889634
ATTN_TILE = 512


def _blocked_rows(a, TA):
    H, S = a.shape
    return a.reshape(H, S // TA, 1, TA)


def _attn_fwd(qkv, nfb, H, name):
    S = qkv.shape[0]
    TA = min(ATTN_TILE, S)
    nb = S // TA
    c = HEAD_DIM ** -0.5 * LOG2E

    def body(q_ref, k_ref, v_ref, nf_ref, o_ref, o32_ref, lse_ref, m_ref, l_ref, acc_ref):
        i = pl.program_id(1)
        m_ref[...] = jnp.full_like(m_ref, -jnp.inf)
        l_ref[...] = jnp.zeros_like(l_ref)
        acc_ref[...] = jnp.zeros_like(acc_ref)

        def block(j, masked):
            rows = pl.ds(pl.multiple_of(j * TA, TA), TA)
            st = (lax.dot_general(k_ref[rows, :], q_ref[...], _NT, preferred_element_type=F32) * c
                  + jnp.tile(nf_ref[0, rows, :], (1, TA // LANES)))
            if masked:
                key = lax.broadcasted_iota(jnp.int32, (TA, TA), 0)
                qry = lax.broadcasted_iota(jnp.int32, (TA, TA), 1)
                st = jnp.where(key <= qry, st, -jnp.inf)
            m_old = m_ref[...]
            m_new = jnp.maximum(m_old, jnp.max(st, axis=0, keepdims=True))
            alpha = jnp.exp2(m_old - m_new)
            pt = jnp.exp2(st - m_new)
            l_ref[...] = alpha * l_ref[...] + jnp.sum(pt, axis=0, keepdims=True)
            acc_ref[...] = alpha * acc_ref[...] + lax.dot_general(v_ref[rows, :], pt.astype(BF16), _TN,
                                                                  preferred_element_type=F32)
            m_ref[...] = m_new

        def loop(j, carry):
            block(j, False)
            return carry

        lax.fori_loop(0, i, loop, 0)
        block(i, True)
        o = jnp.transpose(acc_ref[...] / l_ref[...])
        o32_ref[...] = o
        o_ref[...] = o.astype(BF16)
        lse_ref[0, 0] = m_ref[...] + jnp.log(l_ref[...]) * LOG2E

    qblk = pl.BlockSpec((TA, HEAD_DIM), lambda h, i: (i, h))
    return pl.pallas_call(
        body, name=name, grid=(H, nb),
        in_specs=[qblk,
                  pl.BlockSpec((S, HEAD_DIM), lambda h, i: (0, H + h)),
                  pl.BlockSpec((S, HEAD_DIM), lambda h, i: (0, 2 * H + h)),
                  pl.BlockSpec((1, S, LANES), lambda h, i: (h, 0, 0))],
        out_specs=(qblk, qblk, pl.BlockSpec((1, 1, 1, TA), lambda h, i: (h, i, 0, 0))),
        scratch_shapes=[pltpu.VMEM((1, TA), F32), pltpu.VMEM((1, TA), F32), pltpu.VMEM((HEAD_DIM, TA), F32)],
        out_shape=(jax.ShapeDtypeStruct((S, H * HEAD_DIM), BF16), jax.ShapeDtypeStruct((S, H * HEAD_DIM), F32),
                   jax.ShapeDtypeStruct((H, nb, 1, TA), F32)),
        compiler_params=_params("parallel", "parallel"))(qkv, qkv, qkv, nfb)


def _attn_delta(o, do, H, name):
    S = o.shape[0]
    T = min(512, S)

    def body(o_ref, do_ref, d_ref):
        d_ref[0] = jnp.sum(o_ref[...].astype(F32) * do_ref[...].astype(F32), axis=-1, keepdims=True)

    blk = pl.BlockSpec((T, HEAD_DIM), lambda h, i: (i, h))
    return pl.pallas_call(
        body, name=name, out_shape=jax.ShapeDtypeStruct((H, S, 1), F32), grid=(H, S // T),
        in_specs=[blk, blk], out_specs=pl.BlockSpec((1, T, 1), lambda h, i: (h, i, 0)),
        compiler_params=_params("parallel", "parallel"))(o, do)


def _ds_tile(k, q, v, do, nfb, lse_row, delta_row, c, masked):
    TK, TQ = k.shape[0], q.shape[0]
    st = lax.dot_general(k, q, _NT, preferred_element_type=F32) * c + jnp.tile(nfb, (1, TQ // LANES))
    pt = jnp.exp2(st - lse_row)
    if masked:
        key = lax.broadcasted_iota(jnp.int32, (TK, TQ), 0)
        qry = lax.broadcasted_iota(jnp.int32, (TK, TQ), 1)
        pt = jnp.where(key <= qry, pt, 0.0)
    dpt = lax.dot_general(v, do, _NT, preferred_element_type=F32)
    return pt, pt * (dpt - delta_row)


def _attn_bwd(qkv, nfb, do, lse, delta, H, name):
    S = qkv.shape[0]
    TA = min(ATTN_TILE, S)
    nb = S // TA
    scale = HEAD_DIM ** -0.5
    c = scale * LOG2E

    def body(q_ref, k_ref, v_ref, nf_ref, do_ref, lse_ref, dl_ref, dq_ref, dk_ref, dv_ref, drow_ref, dnf_ref,
             dq_acc, dk_acc, dv_acc, dnf_acc):
        j = pl.program_id(1)

        @pl.when(j == 0)
        def _():
            dq_acc[...] = jnp.zeros_like(dq_acc)
            drow_ref[...] = jnp.zeros_like(drow_ref)

        dk_acc[...] = jnp.zeros_like(dk_acc)
        dv_acc[...] = jnp.zeros_like(dv_acc)
        dnf_acc[...] = jnp.zeros_like(dnf_acc)
        kb = k_ref[...]

        def block(i, masked):
            rows = pl.ds(pl.multiple_of(i * TA, TA), TA)
            qb = q_ref[rows, :]
            dob = do_ref[rows, :]
            pt, dst = _ds_tile(kb, qb, v_ref[...], dob, nf_ref[0], lse_ref[0, i], dl_ref[0, i], c, masked)
            dsb = dst.astype(BF16)
            dv_acc[...] += jnp.dot(pt.astype(BF16), dob, preferred_element_type=F32)
            dk_acc[...] += jnp.dot(dsb, qb, preferred_element_type=F32)
            dq_acc[rows, :] += lax.dot_general(dsb, kb, _TN, preferred_element_type=F32)
            drow_ref[0, i] += jnp.sum(dst, axis=0, keepdims=True)
            part = dst[:, 0:LANES]
            for t in range(1, TA // LANES):
                part = part + dst[:, t * LANES:(t + 1) * LANES]
            dnf_acc[...] += part

        def loop(i, carry):
            block(i, False)
            return carry

        block(j, True)
        lax.fori_loop(j + 1, nb, loop, 0)
        dk_ref[...] = (dk_acc[...] * scale).astype(BF16)
        dv_ref[...] = dv_acc[...].astype(BF16)
        dnf_ref[0] = jnp.sum(dnf_acc[...], axis=-1, keepdims=True)

        @pl.when(j == nb - 1)
        def _():
            dq_ref[...] = (dq_acc[...] * scale).astype(BF16)

    full = pl.BlockSpec((S, HEAD_DIM), lambda h, j: (0, h))
    row_stat = pl.BlockSpec((1, nb, 1, TA), lambda h, j: (h, 0, 0, 0))
    kblk = lambda c0: pl.BlockSpec((TA, HEAD_DIM), lambda h, j: (j, c0 + h))
    shp = jax.ShapeDtypeStruct((S, H * HEAD_DIM), BF16)
    return pl.pallas_call(
        body, name=name, grid=(H, nb),
        in_specs=[full, kblk(H), kblk(2 * H), pl.BlockSpec((1, TA, LANES), lambda h, j: (h, j, 0)), full,
                  row_stat, row_stat],
        out_specs=(full, kblk(0), kblk(0), row_stat, pl.BlockSpec((1, TA, 1), lambda h, j: (h, j, 0))),
        scratch_shapes=[pltpu.VMEM((S, HEAD_DIM), F32), pltpu.VMEM((TA, HEAD_DIM), F32),
                        pltpu.VMEM((TA, HEAD_DIM), F32), pltpu.VMEM((TA, LANES), F32)],
        out_shape=(shp, shp, shp, jax.ShapeDtypeStruct((H, nb, 1, TA), F32), jax.ShapeDtypeStruct((H, S, 1), F32)),
        compiler_params=_params("parallel", "arbitrary"))(qkv, qkv, qkv, nfb, do, lse, delta)


def _taps(ext_ref, w_ref, K, base, r0, rows, cols, reverse=False, init=None):
    acc = init
    for k in range(K):
        wk = w_ref[(K - 1 - k) if reverse else k:((K - 1 - k) if reverse else k) + 1, cols]
        term = wk * ext_ref[base + k + r0:base + k + r0 + rows, cols]
        acc = term if acc is None else acc + term
    return acc


def _prev_blk(T, H):
    return lambda i: jnp.maximum(i * (T // H) - 1, 0)


def _next_blk(T, H, S):
    return lambda i: jnp.minimum((i + 1) * (T // H), S // H - 1)


def _conf_fwd(rest, w, b, lng, lnb, name):
    S = rest.shape[0]
    K, C = w.shape
    H, T = 32, min(256, S)
    RS = min(64, T)
    base = H - (K - 1)

    def body(cv_ref, cg_ref, cvp_ref, cgp_ref, w_ref, b_ref, g_ref, bb_ref, o_ref, ext_ref):
        i = pl.program_id(0)
        ext_ref[0:H, :] = jnp.where(i > 0, cvp_ref[...] * _sigmoid(cgp_ref[...]), 0.0)
        ext_ref[H:H + T, :] = cv_ref[...] * _sigmoid(cg_ref[...])
        for r0 in range(0, T, RS):
            cc = _taps(ext_ref, w_ref, K, base, r0, RS, slice(None), init=jnp.broadcast_to(b_ref[...], (RS, C)))
            xc = cc - jnp.mean(cc, axis=-1, keepdims=True)
            y = xc * lax.rsqrt(jnp.mean(xc * xc, axis=-1, keepdims=True) + LN_EPS) * g_ref[...] + bb_ref[...]
            o_ref[r0:r0 + RS, :] = (y * _sigmoid(y)).astype(BF16)

    pb = _prev_blk(T, H)
    cur = lambda cb: pl.BlockSpec((T, C), lambda i: (i, cb))
    prev = lambda cb: pl.BlockSpec((H, C), lambda i: (pb(i), cb))
    full = lambda a: pl.BlockSpec(a.shape, lambda i: (0, 0))
    return pl.pallas_call(
        body, name=name, out_shape=jax.ShapeDtypeStruct((S, C), BF16), grid=(S // T,),
        in_specs=[cur(0), cur(1), prev(0), prev(1), full(w), full(b), full(lng), full(lnb)],
        out_specs=pl.BlockSpec((T, C), lambda i: (i, 0)),
        scratch_shapes=[pltpu.VMEM((H + T, C), F32)],
        compiler_params=_params("parallel"))(rest, rest, rest, rest, w, b, lng, lnb)


def _conf_bwd(rest, dcs, w, b, lng, lnb, name):
    S = rest.shape[0]
    K, C = w.shape
    H, T = 32, min(256, S)
    RS = 32
    nI = S // T
    base = H - (K - 1)

    def body(cv_ref, cg_ref, cvp_ref, cgp_ref, cvn_ref, cgn_ref, do_ref, don_ref, w_ref, b_ref, g_ref, bb_ref,
             dcvg_ref, dw_ref, dvec_ref, ext_ref, dcc_ref):
        i = pl.program_id(0)
        ext_ref[0:H, :] = jnp.where(i > 0, cvp_ref[...] * _sigmoid(cgp_ref[...]), 0.0)
        ext_ref[H:H + T, :] = cv_ref[...] * _sigmoid(cg_ref[...])
        ext_ref[H + T:H + T + H, :] = cvn_ref[...] * _sigmoid(cgn_ref[...])

        @pl.when(i == 0)
        def _():
            dw_ref[...] = jnp.zeros_like(dw_ref)
            dvec_ref[...] = jnp.zeros_like(dvec_ref)

        db = jnp.zeros((1, C), F32)
        dg = jnp.zeros((1, C), F32)
        dbb = jnp.zeros((1, C), F32)
        for r0 in range(0, T + H, RS):
            cc = _taps(ext_ref, w_ref, K, base, r0, RS, slice(None), init=jnp.broadcast_to(b_ref[...], (RS, C)))
            xc = cc - jnp.mean(cc, axis=-1, keepdims=True)
            r = lax.rsqrt(jnp.mean(xc * xc, axis=-1, keepdims=True) + LN_EPS)
            xh = xc * r
            y = xh * g_ref[...] + bb_ref[...]
            sy = _sigmoid(y)
            if r0 < T:
                d_o = do_ref[r0:r0 + RS, :]
            else:
                d_o = jnp.where(i < nI - 1, don_ref[r0 - T:r0 - T + RS, :], 0.0)
            dy = d_o * (sy * (1.0 + y * (1.0 - sy)))
            dxh = dy * g_ref[...]
            dcc = r * (dxh - jnp.mean(dxh, axis=-1, keepdims=True)
                       - xh * jnp.mean(dxh * xh, axis=-1, keepdims=True))
            dcc_ref[r0:r0 + RS, :] = dcc
            if r0 < T:
                dbb = dbb + jnp.sum(dy, axis=0, keepdims=True)
                dg = dg + jnp.sum(dy * xh, axis=0, keepdims=True)
                db = db + jnp.sum(dcc, axis=0, keepdims=True)
        dvec_ref[0:1, :] += db
        dvec_ref[1:2, :] += dg
        dvec_ref[2:3, :] += dbb
        R2 = min(64, T)
        for k in range(K):
            s = jnp.zeros((1, C), F32)
            for r0 in range(0, T, R2):
                s = s + jnp.sum(dcc_ref[r0:r0 + R2, :] * ext_ref[base + k + r0:base + k + r0 + R2, :],
                                axis=0, keepdims=True)
            dw_ref[k:k + 1, :] += s
        for r0 in range(0, T, R2):
            dci = _taps(dcc_ref, w_ref, K, 0, r0, R2, slice(None), reverse=True)
            cvv = cv_ref[r0:r0 + R2, :]
            sg = _sigmoid(cg_ref[r0:r0 + R2, :])
            dcvg_ref[r0:r0 + R2, 0:C] = (dci * sg).astype(BF16)
            dcvg_ref[r0:r0 + R2, C:2 * C] = (dci * cvv * sg * (1.0 - sg)).astype(BF16)

    pb, nb_ = _prev_blk(T, H), _next_blk(T, H, S)
    cur = lambda cb: pl.BlockSpec((T, C), lambda i: (i, cb))
    prev = lambda cb: pl.BlockSpec((H, C), lambda i: (pb(i), cb))
    nxt = lambda cb: pl.BlockSpec((H, C), lambda i: (nb_(i), cb))
    full = lambda a: pl.BlockSpec(a.shape, lambda i: (0, 0))
    return pl.pallas_call(
        body, name=name,
        out_shape=(jax.ShapeDtypeStruct((S, 2 * C), BF16), jax.ShapeDtypeStruct((32, C), F32),
                   jax.ShapeDtypeStruct((8, C), F32)),
        grid=(nI,),
        in_specs=[cur(0), cur(1), prev(0), prev(1), nxt(0), nxt(1), cur(0), nxt(0),
                  full(w), full(b), full(lng), full(lnb)],
        out_specs=(pl.BlockSpec((T, 2 * C), lambda i: (i, 0)), pl.BlockSpec((32, C), lambda i: (0, 0)),
                   pl.BlockSpec((8, C), lambda i: (0, 0))),
        scratch_shapes=[pltpu.VMEM((H + T + H, C), F32), pltpu.VMEM((T + H, C), F32)],
        compiler_params=_params("arbitrary"))(rest, rest, rest, rest, rest, rest, dcs, dcs, w, b, lng, lnb)


def _sconv_fwd(rest, w, name):
    S = rest.shape[0]
    K, C = w.shape
    H, T = 8, min(256, S)
    RS = min(64, T)
    base = H - (K - 1)

    def body(sx_ref, sb_ref, sc_ref, sxp_ref, scp_ref, w_ref, o_ref, ext_ref):
        i = pl.program_id(0)
        ext_ref[0:H, :] = jnp.where(i > 0, sxp_ref[...] * scp_ref[...], 0.0)
        ext_ref[H:H + T, :] = sx_ref[...] * sc_ref[...]
        for r0 in range(0, T, RS):
            cz = _taps(ext_ref, w_ref, K, base, r0, RS, slice(None))
            o_ref[r0:r0 + RS, :] = (sb_ref[r0:r0 + RS, :] * cz).astype(BF16)

    pb = _prev_blk(T, H)
    cur = lambda cb: pl.BlockSpec((T, C), lambda i: (i, cb))
    prev = lambda cb: pl.BlockSpec((H, C), lambda i: (pb(i), cb))
    return pl.pallas_call(
        body, name=name, out_shape=jax.ShapeDtypeStruct((S, C), BF16), grid=(S // T,),
        in_specs=[cur(2), cur(3), cur(4), prev(2), prev(4), pl.BlockSpec(w.shape, lambda i: (0, 0))],
        out_specs=pl.BlockSpec((T, C), lambda i: (i, 0)),
        scratch_shapes=[pltpu.VMEM((H + T, C), F32)],
        compiler_params=_params("parallel"))(rest, rest, rest, rest, rest, w)


def _sconv_bwd(rest, dcs, w, name):
    S = rest.shape[0]
    K, C = w.shape
    H, T = 8, min(256, S)
    RS = min(64, T)
    nI = S // T
    base = H - (K - 1)

    def body(sx_ref, sb_ref, sc_ref, sxp_ref, scp_ref, sbn_ref, do_ref, don_ref, w_ref,
             dout_ref, dw_ref, ext_ref, dcv_ref):
        i = pl.program_id(0)
        ext_ref[0:H, :] = jnp.where(i > 0, sxp_ref[...] * scp_ref[...], 0.0)
        ext_ref[H:H + T, :] = sx_ref[...] * sc_ref[...]
        dcv_ref[0:T, :] = do_ref[...] * sb_ref[...]
        dcv_ref[T:T + H, :] = jnp.where(i < nI - 1, don_ref[...] * sbn_ref[...], 0.0)

        @pl.when(i == 0)
        def _():
            dw_ref[...] = jnp.zeros_like(dw_ref)

        for k in range(K):
            s = jnp.zeros((1, C), F32)
            for r0 in range(0, T, RS):
                s = s + jnp.sum(dcv_ref[r0:r0 + RS, :] * ext_ref[base + k + r0:base + k + r0 + RS, :],
                                axis=0, keepdims=True)
            dw_ref[k:k + 1, :] += s
        for r0 in range(0, T, RS):
            cz = _taps(ext_ref, w_ref, K, base, r0, RS, slice(None))
            dz = _taps(dcv_ref, w_ref, K, 0, r0, RS, slice(None), reverse=True)
            dout_ref[r0:r0 + RS, 0:C] = (dz * sc_ref[r0:r0 + RS, :]).astype(BF16)
            dout_ref[r0:r0 + RS, C:2 * C] = (do_ref[r0:r0 + RS, :] * cz).astype(BF16)
            dout_ref[r0:r0 + RS, 2 * C:3 * C] = (dz * sx_ref[r0:r0 + RS, :]).astype(BF16)

    pb, nb_ = _prev_blk(T, H), _next_blk(T, H, S)
    cur = lambda cb: pl.BlockSpec((T, C), lambda i: (i, cb))
    prev = lambda cb: pl.BlockSpec((H, C), lambda i: (pb(i), cb))
    nxt = lambda cb: pl.BlockSpec((H, C), lambda i: (nb_(i), cb))
    return pl.pallas_call(
        body, name=name,
        out_shape=(jax.ShapeDtypeStruct((S, 3 * C), BF16), jax.ShapeDtypeStruct((8, C), F32)),
        grid=(nI,),
        in_specs=[cur(2), cur(3), cur(4), prev(2), prev(4), nxt(3), cur(1), nxt(1),
                  pl.BlockSpec(w.shape, lambda i: (0, 0))],
        out_specs=(pl.BlockSpec((T, 3 * C), lambda i: (i, 0)), pl.BlockSpec((8, C), lambda i: (0, 0))),
        scratch_shapes=[pltpu.VMEM((H + T, C), F32), pltpu.VMEM((T + H, C), F32)],
        compiler_params=_params("arbitrary"))(rest, rest, rest, rest, rest, rest, dcs, dcs, w)


def _ffn_fwd(hu, w, b, name):
    S, F2 = hu.shape
    Fd = F2 // 2
    K = w.shape[0]
    H, T = 8, min(256, S)
    tc = _tile(Fd, 512)
    nJ = Fd // tc
    RS = min(64, T)
    base = H - (K - 1)

    def body(g_ref, v_ref, gp_ref, vp_ref, wg_ref, wv_ref, bg_ref, bv_ref, o_ref, eg_ref, ev_ref):
        i = pl.program_id(1)
        eg_ref[0:H, :] = jnp.where(i > 0, gp_ref[...], 0.0)
        ev_ref[0:H, :] = jnp.where(i > 0, vp_ref[...], 0.0)
        eg_ref[H:H + T, :] = g_ref[...]
        ev_ref[H:H + T, :] = v_ref[...]
        for r0 in range(0, T, RS):
            ug = _taps(eg_ref, wg_ref, K, base, r0, RS, slice(None), init=jnp.broadcast_to(bg_ref[...], (RS, tc)))
            uv = _taps(ev_ref, wv_ref, K, base, r0, RS, slice(None), init=jnp.broadcast_to(bv_ref[...], (RS, tc)))
            o_ref[r0:r0 + RS, :] = (ug * _sigmoid(ug) * uv).astype(BF16)

    pb = _prev_blk(T, H)
    cur = lambda off: pl.BlockSpec((T, tc), lambda j, i: (i, j + off))
    prev = lambda off: pl.BlockSpec((H, tc), lambda j, i: (pb(i), j + off))
    wsp = lambda off: pl.BlockSpec((K, tc), lambda j, i: (0, j + off))
    bsp = lambda off: pl.BlockSpec((1, tc), lambda j, i: (0, j + off))
    return pl.pallas_call(
        body, name=name, out_shape=jax.ShapeDtypeStruct((S, Fd), BF16), grid=(nJ, S // T),
        in_specs=[cur(0), cur(nJ), prev(0), prev(nJ), wsp(0), wsp(nJ), bsp(0), bsp(nJ)],
        out_specs=pl.BlockSpec((T, tc), lambda j, i: (i, j)),
        scratch_shapes=[pltpu.VMEM((H + T, tc), F32), pltpu.VMEM((H + T, tc), F32)],
        compiler_params=_params("parallel", "parallel"))(hu, hu, hu, hu, w, w, b, b)


def _ffn_bwd(hu, dact, w, b, name):
    S, F2 = hu.shape
    Fd = F2 // 2
    K = w.shape[0]
    H, T = 8, min(256, S)
    HB = 16
    tc = _tile(Fd, 512)
    nJ = Fd // tc
    nI = S // T
    RS = min(64, T)
    base = H - (K - 1)

    def body(g_ref, v_ref, gp_ref, vp_ref, gn_ref, vn_ref, da_ref, dan_ref, wg_ref, wv_ref, bg_ref, bv_ref,
             dg_ref, dv_ref, dwg_ref, dwv_ref, eg_ref, ev_ref, dug_ref, duv_ref):
        i = pl.program_id(1)
        eg_ref[0:H, :] = jnp.where(i > 0, gp_ref[...], 0.0)
        ev_ref[0:H, :] = jnp.where(i > 0, vp_ref[...], 0.0)
        eg_ref[H:H + T, :] = g_ref[...]
        ev_ref[H:H + T, :] = v_ref[...]
        eg_ref[H + T:H + T + H, :] = gn_ref[...]
        ev_ref[H + T:H + T + H, :] = vn_ref[...]

        @pl.when(i == 0)
        def _():
            dwg_ref[...] = jnp.zeros_like(dwg_ref)
            dwv_ref[...] = jnp.zeros_like(dwv_ref)

        def du_rows(r0, rows, d_a):
            ug = _taps(eg_ref, wg_ref, K, base, r0, rows, slice(None),
                       init=jnp.broadcast_to(bg_ref[...], (rows, tc)))
            uv = _taps(ev_ref, wv_ref, K, base, r0, rows, slice(None),
                       init=jnp.broadcast_to(bv_ref[...], (rows, tc)))
            sg = _sigmoid(ug)
            dug_ref[r0:r0 + rows, :] = d_a * uv * (sg * (1.0 + ug * (1.0 - sg)))
            duv_ref[r0:r0 + rows, :] = d_a * (ug * sg)

        for r0 in range(0, T, RS):
            du_rows(r0, RS, da_ref[r0:r0 + RS, :].astype(F32))
        du_rows(T, H, jnp.where(i < nI - 1, dan_ref[...].astype(F32)[0:H, :], 0.0))

        for ext, du, dw in ((eg_ref, dug_ref, dwg_ref), (ev_ref, duv_ref, dwv_ref)):
            for k in range(K):
                s = jnp.zeros((1, tc), F32)
                for r0 in range(0, T, RS):
                    s = s + jnp.sum(du[r0:r0 + RS, :] * ext[base + k + r0:base + k + r0 + RS, :],
                                    axis=0, keepdims=True)
                dw[k:k + 1, :] += s
            s = jnp.zeros((1, tc), F32)
            for r0 in range(0, T, RS):
                s = s + jnp.sum(du[r0:r0 + RS, :], axis=0, keepdims=True)
            dw[K:K + 1, :] += s
        for r0 in range(0, T, RS):
            dg_ref[r0:r0 + RS, :] = _taps(dug_ref, wg_ref, K, 0, r0, RS, slice(None), reverse=True).astype(BF16)
            dv_ref[r0:r0 + RS, :] = _taps(duv_ref, wv_ref, K, 0, r0, RS, slice(None), reverse=True).astype(BF16)

    pb, nb_, nbb = _prev_blk(T, H), _next_blk(T, H, S), _next_blk(T, HB, S)
    cur = lambda off: pl.BlockSpec((T, tc), lambda j, i: (i, j + off))
    prev = lambda off: pl.BlockSpec((H, tc), lambda j, i: (pb(i), j + off))
    nxt = lambda off: pl.BlockSpec((H, tc), lambda j, i: (nb_(i), j + off))
    wsp = lambda off: pl.BlockSpec((K, tc), lambda j, i: (0, j + off))
    bsp = lambda off: pl.BlockSpec((1, tc), lambda j, i: (0, j + off))
    half = jax.ShapeDtypeStruct((S, Fd), BF16)
    dws = jax.ShapeDtypeStruct((8, Fd), F32)
    return pl.pallas_call(
        body, name=name, out_shape=(half, half, dws, dws), grid=(nJ, nI),
        in_specs=[cur(0), cur(nJ), prev(0), prev(nJ), nxt(0), nxt(nJ),
                  pl.BlockSpec((T, tc), lambda j, i: (i, j)), pl.BlockSpec((HB, tc), lambda j, i: (nbb(i), j)),
                  wsp(0), wsp(nJ), bsp(0), bsp(nJ)],
        out_specs=(pl.BlockSpec((T, tc), lambda j, i: (i, j)), pl.BlockSpec((T, tc), lambda j, i: (i, j)),
                   pl.BlockSpec((8, tc), lambda j, i: (0, j)), pl.BlockSpec((8, tc), lambda j, i: (0, j))),
        scratch_shapes=[pltpu.VMEM((H + T + H, tc), F32), pltpu.VMEM((H + T + H, tc), F32),
                        pltpu.VMEM((T + H, tc), F32), pltpu.VMEM((T + H, tc), F32)],
        compiler_params=_params("parallel", "arbitrary"))(hu, hu, hu, hu, hu, hu, dact, dact, w, w, b, b)


def _position():
    return lax.axis_index("x"), lax.axis_index("y"), lax.axis_index("c")


def _slot(px, py, pc):
    return 4 * px + 2 * py + pc


def _gather_copies(x_ref, out_ref, sems, r, starting=False):
    send_sems, recv_sems, local_sems = sems
    px, py, pc = _position()
    me, sibling = (px, py, pc), (px, py, 1 - pc)
    chips = [(1 - px, py), (px, 1 - py), (1 - px, 1 - py)]

    def copy(k, block, to, src=None):
        dst = out_ref.at[_slot(*block)]
        return pltpu.make_async_remote_copy(
            src_ref=dst if src is None else src, dst_ref=dst,
            send_sem=send_sems.at[7 * r + k], recv_sem=recv_sems.at[7 * r + k], device_id=to, device_id_type=MESH)

    mine = pltpu.make_async_copy(x_ref, out_ref.at[_slot(*me)], local_sems.at[r])
    first = [copy(0, me, sibling, src=x_ref)] + [copy(1 + n, me, (*chip, pc), src=x_ref)
                                                  for n, chip in enumerate(chips)]
    if starting:
        return mine, first
    passed = [copy(4 + n, (*chip, pc), sibling) for n, chip in enumerate(chips)]
    landed = [copy(1 + n, (*chip, pc), me) for n, chip in enumerate(chips)]
    from_sibling = [copy(0, sibling, me)] + [copy(4 + n, (*chip, 1 - pc), me) for n, chip in enumerate(chips)]
    return mine, first, passed, landed, from_sibling


def _scatter_copies(g_ref, out_ref, sems, r, starting=False):
    send_sems, recv_sems, local_sems = sems
    px, py, pc = _position()
    me = _slot(px, py, pc)
    mine = pltpu.make_async_copy(g_ref.at[me], out_ref.at[me], local_sems.at[r])
    peers = [(px ^ fx, py ^ fy, pc ^ fc) for fx, fy, fc in PEER_FLIPS]

    def copy(k, peer, src_slot, dst_slot):
        return pltpu.make_async_remote_copy(
            src_ref=g_ref.at[src_slot], dst_ref=out_ref.at[dst_slot],
            send_sem=send_sems.at[7 * r + k], recv_sem=recv_sems.at[7 * r + k], device_id=peer, device_id_type=MESH)

    sends = [copy(k, peer, _slot(*peer), me) for k, peer in enumerate(peers)]
    if starting:
        return mine, sends
    arrivals = [copy(k, peer, me, _slot(*peer)) for k, peer in enumerate(peers)]
    return mine, sends, arrivals


def _rider_start(kind, in_ref, out_ref, sems, r):
    if kind == "gather":
        mine, first = _gather_copies(in_ref, out_ref, sems, r, starting=True)
    else:
        mine, first = _scatter_copies(in_ref, out_ref, sems, r, starting=True)
    mine.start()
    for cp in first:
        cp.start()


def _rider_finish(kind, in_ref, out_ref, sems, r):
    if kind == "gather":
        mine, first, passed, landed, from_sibling = _gather_copies(in_ref, out_ref, sems, r)
        for cp, fwd in zip(landed, passed):
            cp.wait_recv()
            fwd.start()
        for cp in from_sibling:
            cp.wait_recv()
        for cp in first + passed:
            cp.wait_send()
    else:
        mine, sends, arrivals = _scatter_copies(in_ref, out_ref, sems, r)
        for cp in arrivals:
            cp.wait_recv()
        for cp in sends:
            cp.wait_send()
    mine.wait()


def _all_gather(x, in_vmem, name):
    space = pltpu.VMEM if in_vmem else pl.ANY

    def body(x_ref, out_ref, send_sems, recv_sems, local_sems):
        sems = (send_sems, recv_sems, local_sems)
        _rider_start("gather", x_ref, out_ref, sems, 0)
        _rider_finish("gather", x_ref, out_ref, sems, 0)

    return pl.pallas_call(
        body, name=name, out_shape=jax.ShapeDtypeStruct((N_DEV,) + x.shape, x.dtype),
        in_specs=[pl.BlockSpec(memory_space=space)], out_specs=pl.BlockSpec(memory_space=space),
        scratch_shapes=[pltpu.SemaphoreType.DMA((7,)), pltpu.SemaphoreType.DMA((7,)), pltpu.SemaphoreType.DMA((1,))],
        compiler_params=pltpu.CompilerParams(vmem_limit_bytes=VMEM_LIMIT_BYTES),
    )(x)


def _adam_sum(stage, w, m, v, layer, prev, name):
    n = stage.shape[0]
    L, R, C = w.shape
    tr = R if R * C <= 256 * 1024 else _row_tile(R, C)
    c1 = 1.0 / (1.0 - ADAM_B1 ** ADAM_STEP)
    c2 = 1.0 / (1.0 - ADAM_B2 ** ADAM_STEP)

    def body(*refs):
        st_ref, w_ref, m_ref, v_ref = refs[:4]
        g_ref, d_ref, nm_ref, nv_ref = refs[-4:]
        g = st_ref[0].astype(F32)
        for s in range(1, n):
            g = g + st_ref[s].astype(F32)
        wv = w_ref[0]
        mn = ADAM_B1 * m_ref[0] + (1.0 - ADAM_B1) * g
        vn = ADAM_B2 * v_ref[0] + (1.0 - ADAM_B2) * (g * g)
        g_ref[0] = g
        nm_ref[0] = mn
        nv_ref[0] = vn
        d_ref[0] = -ADAM_LR * ((mn * c1) / (jnp.sqrt(vn * c2) + ADAM_EPS) + ADAM_WD * wv)

    lay = pl.BlockSpec((1, tr, C), lambda i: (layer, i, 0))
    in_specs = [pl.BlockSpec((n, tr, C), lambda i: (0, i, 0)), lay, lay, lay]
    ins = [stage, w, m, v]
    aliases = {}
    if prev is not None:
        in_specs += [pl.BlockSpec(memory_space=pl.ANY)] * 4
        ins += list(prev)
        aliases = {4: 0, 5: 1, 6: 2, 7: 3}
    shp = jax.ShapeDtypeStruct((L, R, C), F32)
    return pl.pallas_call(
        body, name=name, out_shape=(shp, shp, shp, shp), grid=(R // tr,),
        in_specs=in_specs, out_specs=(lay, lay, lay, lay), input_output_aliases=aliases,
        compiler_params=_params("parallel"))(*ins)


def _row_tile(R, C):
    cpad = -(-C // LANES) * LANES
    want = max(16, (256 * 1024) // cpad)
    best = 16
    for t in range(16, R + 1, 16):
        if R % t == 0 and t <= want:
            best = t
    return best


def _sum_slabs(st, name):
    n, R, C = st.shape
    tr = R if n * R * C * 4 <= (12 << 20) else _row_tile(R, C)

    def body(st_ref, o_ref):
        g = st_ref[0]
        for s in range(1, n):
            g = g + st_ref[s]
        o_ref[...] = g

    return pl.pallas_call(
        body, name=name, out_shape=jax.ShapeDtypeStruct((R, C), F32), grid=(R // tr,),
        in_specs=[pl.BlockSpec((n, tr, C), lambda i: (0, i, 0))], out_specs=pl.BlockSpec((tr, C), lambda i: (i, 0)),
        compiler_params=_params("parallel"))(st)


def _pack(arrs):
    flat = [a.reshape(-1).astype(F32) for a in arrs]
    sizes = [f.shape[0] for f in flat]
    total = sum(sizes)
    padded = -(-total // (16 * LANES)) * (16 * LANES)
    if padded > total:
        flat.append(jnp.zeros((padded - total,), F32))
    return jnp.concatenate(flat).reshape(padded // LANES, LANES), (sizes, [a.shape for a in arrs])


def _unpack(packed, layout, lead=()):
    sizes, shapes = layout
    flat = packed.reshape(lead + (-1,))
    out, off = [], 0
    for sz, shp in zip(sizes, shapes):
        out.append(flat[..., off:off + sz].reshape(lead + tuple(shp)))
        off += sz
    return out


class _NoComm:
    def __init__(self, wts):
        self.wts, self.grads = wts, {}

    def weights(self, l):
        return self.wts[l]

    def gather_rider(self, l, names):
        return None

    def scatter_rider(self, name, l, g):
        self.grads[(name, l)] = g
        return None


def _local_step(x, tgt, ada, mix_norm_g, comm, b_forget, conf_dw_w, conf_dw_b, conf_ln_g, conf_ln_b, sc_dw_w,
                ffn_norm_g, ffn_dw_w, ffn_dw_b, final_norm_g):
    S, D = x.shape
    L = ada.shape[0]

    def mm_gather(a, b, dtype, name, l_next, names):
        rider = comm.gather_rider(l_next, names) if l_next < L else None
        if rider is None:
            return _matmul(a, b, dtype, name=name)
        out, got = _matmul(a, b, dtype, name="cm_" + name, rider=rider)
        comm.gathered(l_next, names, got)
        return out

    def mm_scatter(a, b, dtype, name, wname, l, g):
        rider = comm.scatter_rider(wname, l, g)
        if rider is None:
            return _matmul(a, b, dtype, name=name)
        out, got = _matmul(a, b, dtype, name="cm_" + name, rider=rider)
        comm.scattered(wname, l, got[0])
        return out

    H = b_forget.shape[1]
    DA = H * HEAD_DIM
    C = conf_dw_b.shape[1]
    NQ = 3 * DA
    NR = 5 * C + LANES
    fblk = (5 * C) // LANES
    row = lambda a: a.reshape(1, -1)
    adav = ada.reshape(L, N_ADA, 1, D)

    saved = []
    xcur, delta, gate = x, None, None
    for l in range(L):
        sh_m, sc_m, g_m, sh_f, sc_f, g_f = [adav[l, n] for n in range(N_ADA)]
        w = comm.weights(l)
        x1, h1 = _site_fwd(xcur, delta, gate, row(mix_norm_g[l]), sc_m, sh_m, name="site_fwd_mix")
        qkv = _matmul(h1, w["w_in_perm"][:, :NQ], BF16, name="mm_qkv")
        rest = _matmul(h1, w["w_in_perm"][:, NQ:], F32, name="mm_rest")
        bpad = jnp.zeros((1, LANES), F32).at[0, :H].set(b_forget[l])
        Fc = _fgate_fwd(rest, bpad, fblk, name="fgate_fwd")
        nf = -LOG2E * jnp.transpose(Fc[:, :H])
        attn, attn32, lse = _attn_fwd(qkv, jnp.broadcast_to(nf[:, :, None], (H, S, LANES)), H, name="attn_fwd")
        conf = _conf_fwd(rest, conf_dw_w[l], row(conf_dw_b[l]), row(conf_ln_g[l]), row(conf_ln_b[l]), name="conf_fwd")
        sconv = _sconv_fwd(rest, sc_dw_w[l], name="sconv_fwd")
        mixcat = jnp.concatenate([attn, conf, sconv], axis=1)
        mixed = _matmul(mixcat, w["w_out"], F32, name="mm_out")
        x2, h2 = _site_fwd(x1, mixed, g_m, row(ffn_norm_g[l]), sc_f, sh_f, name="site_fwd_ffn")
        hu = mm_gather(h2, w["w_up"], F32, "mm_up", l + 1, ("w_up", "w_down"))
        act = _ffn_fwd(hu, ffn_dw_w[l], row(ffn_dw_b[l]), name="ffn_fwd")
        ffn_out = mm_gather(act, w["w_down"], F32, "mm_down", l + 1, ("w_in", "w_out"))
        saved.append(dict(x1=x1, h1=h1, qkv=qkv, rest=rest, bpad=bpad, nf=nf, attn32=attn32, lse=lse, mixcat=mixcat,
                          mixed=mixed, x2=x2, h2=h2, hu=hu, act=act, ffn_out=ffn_out))
        xcur, delta, gate = x2, ffn_out, g_f

    loss_lanes, dx, d_delta, d_gate, d_gfin = _final_fwd_bwd(xcur, delta, gate, row(final_norm_g), tgt, name="final")
    loss = (0.5 / D) * jnp.sum(loss_lanes)

    grads = dict(final_norm_g=d_gfin[0], ada=[None] * L, mix_norm_g=[None] * L, ffn_norm_g=[None] * L,
                 b_forget=[None] * L, conf_dw_w=[None] * L, conf_dw_b=[None] * L, conf_ln_g=[None] * L,
                 conf_ln_b=[None] * L, sc_dw_w=[None] * L, ffn_dw_w=[None] * L, ffn_dw_b=[None] * L)
    K3 = ffn_dw_w.shape[1]
    for l in reversed(range(L)):
        sv, w = saved[l], comm.weights(l)
        sh_m, sc_m, g_m, sh_f, sc_f, g_f = [adav[l, n] for n in range(N_ADA)]
        d_gf = d_gate
        g_down = _matmul_tn(sv["act"], d_delta, BF16, name="mm_dw_down")
        dact = mm_scatter(d_delta, jnp.transpose(w["w_down"]), BF16, "mm_dact", "w_down", l, g_down)
        dhu_g, dhu_v, dwg, dwv = _ffn_bwd(sv["hu"], dact, ffn_dw_w[l], row(ffn_dw_b[l]), name="ffn_bwd")
        grads["ffn_dw_w"][l] = jnp.concatenate([dwg[:K3], dwv[:K3]], axis=1)
        grads["ffn_dw_b"][l] = jnp.concatenate([dwg[K3], dwv[K3]])
        dhu = jnp.concatenate([dhu_g, dhu_v], axis=1)
        g_up = _matmul_tn(sv["h2"], dhu, BF16, name="mm_dw_up")
        dh2 = mm_scatter(dhu, jnp.transpose(w["w_up"]), F32, "mm_dh2", "w_up", l, g_up)
        dx, d_sh_f, d_a_f, d_mixed, d_gm = _site_bwd(sv["x2"], dh2, dx, row(ffn_norm_g[l]), sc_f,
                                                      sh_f, sv["mixed"], g_m, name="site_bwd_ffn")
        grads["ffn_norm_g"][l] = (d_a_f * (1.0 + sc_f))[0]
        d_sc_f = d_a_f * row(ffn_norm_g[l])
        g_out = _matmul_tn(sv["mixcat"], d_mixed, BF16, name="mm_dw_out")
        w_out_t = jnp.transpose(w["w_out"])
        dattn = mm_scatter(d_mixed, w_out_t[:, :DA], BF16, "mm_dattn", "w_out", l, g_out)
        dcs = _matmul(d_mixed, w_out_t[:, DA:], F32, name="mm_dcs")
        delta_a = _blocked_rows(_attn_delta(sv["attn32"], dattn, H, name="attn_delta")[:, :, 0], min(ATTN_TILE, S))
        nfb = jnp.broadcast_to(sv["nf"][:, :, None], (H, S, LANES))
        dq, dk, dv, drow, dnf = _attn_bwd(sv["qkv"], nfb, dattn, sv["lse"], delta_a, H, name="attn_bwd")
        dF = jnp.zeros((S, LANES), F32).at[:, :H].set(jnp.transpose(drow.reshape(H, S) - dnf[:, :, 0]))
        dfl, dbf = _fgate_bwd(sv["rest"], sv["bpad"], dF, fblk, name="fgate_bwd")
        grads["b_forget"][l] = dbf[0, :H]
        dcvg, dcw, dcvec = _conf_bwd(sv["rest"], dcs, conf_dw_w[l], row(conf_dw_b[l]), row(conf_ln_g[l]),
                                     row(conf_ln_b[l]), name="conf_bwd")
        grads["conf_dw_w"][l] = dcw[:conf_dw_w.shape[1]]
        grads["conf_dw_b"][l], grads["conf_ln_g"][l], grads["conf_ln_b"][l] = dcvec[0], dcvec[1], dcvec[2]
        dsc3, dsw = _sconv_bwd(sv["rest"], dcs, sc_dw_w[l], name="sconv_bwd")
        grads["sc_dw_w"][l] = dsw[:sc_dw_w.shape[1]]
        dproj = jnp.concatenate([dq, dk, dv, dcvg, dsc3, dfl], axis=1)
        g_in = _matmul_tn(sv["h1"], dproj, BF16, name="mm_dw_in")
        dh1 = mm_scatter(dproj, jnp.transpose(w["w_in_perm"]), F32, "mm_dh1", "w_in_perm", l, g_in)
        if l > 0:
            pv = saved[l - 1]
            g_f_prev = adav[l - 1, 5]
            dx, d_sh_m, d_a_m, d_delta, d_gate = _site_bwd(sv["x1"], dh1, dx, row(mix_norm_g[l]), sc_m, sh_m,
                                                           pv["ffn_out"], g_f_prev, name="site_bwd_mix")
        else:
            dx, d_sh_m, d_a_m = _site_bwd(sv["x1"], dh1, dx, row(mix_norm_g[l]), sc_m, sh_m, None, None,
                                          name="site_bwd_first")
        grads["mix_norm_g"][l] = (d_a_m * (1.0 + sc_m))[0]
        d_sc_m = d_a_m * row(mix_norm_g[l])
        grads["ada"][l] = jnp.concatenate([d_sh_m, d_sc_m, d_gm, d_sh_f, d_sc_f, d_gf], axis=1)[0]
    return loss, dx, grads


def kernel(x, c, ada_w, ada_b, mix_norm_g, w_in, b_forget, conf_dw_w, conf_dw_b, conf_ln_g, conf_ln_b, sc_dw_w, w_out, ffn_norm_g, w_up, ffn_dw_w, ffn_dw_b, w_down, final_norm_g, loss_target, m_ada_w, m_ada_b, m_mix_norm_g, m_w_in, m_b_forget, m_conf_dw_w, m_conf_dw_b, m_conf_ln_g, m_conf_ln_b, m_sc_dw_w, m_w_out, m_ffn_norm_g, m_w_up, m_ffn_dw_w, m_ffn_dw_b, m_w_down, m_final_norm_g, v_ada_w, v_ada_b, v_mix_norm_g, v_w_in, v_b_forget, v_conf_dw_w, v_conf_dw_b, v_conf_ln_g, v_conf_ln_b, v_sc_dw_w, v_w_out, v_ffn_norm_g, v_w_up, v_ffn_dw_w, v_ffn_dw_b, v_w_down, v_final_norm_g):
    L, D, ada_loc = ada_w.shape
    S = x.shape[1]
    H = b_forget.shape[1]
    DA = H * HEAD_DIM
    C = conf_dw_b.shape[1]
    in_loc = w_in.shape[2]
    IN = in_loc * N_DEV
    px, py, pc = _position()
    me = _slot(px, py, pc)

    pk, lay = _pack([c, conf_dw_w, sc_dw_w, ffn_dw_w])
    gathered = _all_gather(pk, True, name="ag_small_fwd")
    c_all, cw_all, sw_all, fw_all = _unpack(gathered, lay, lead=(N_DEV,))
    c_all = c_all[:, 0]
    unshard = lambda a: jnp.moveaxis(a, 0, 2).reshape(a.shape[1], a.shape[2], -1)
    conf_w_full, sc_w_full, ffn_w_full = unshard(cw_all), unshard(sw_all), unshard(fw_all)
    c_act = c_all * jax.nn.sigmoid(c_all)
    c_act16 = jnp.zeros((16, D), F32).at[:N_DEV].set(c_act).astype(BF16)
    ada_cols = jnp.stack([_matmul(c_act16, ada_w[l].astype(BF16), F32, name="mm_ada")[:N_DEV] for l in range(L)])
    ada_g = _all_gather(ada_cols.reshape(L * N_DEV, ada_loc), True, name="ag_ada")
    ada_mine = lax.dynamic_index_in_dim(ada_g.reshape(N_DEV, L, N_DEV, ada_loc), me, axis=2, keepdims=False)
    ada = jnp.moveaxis(ada_mine, 0, 1).reshape(L, N_DEV * ada_loc) + ada_b

    NQ = 3 * DA
    PR = NQ + 5 * C
    shards = dict(w_in=w_in.astype(BF16), w_out=w_out.astype(BF16), w_up=w_up.astype(BF16),
                  w_down=w_down.astype(BF16))

    def shard_cols(g):
        return jnp.moveaxis(g.reshape(g.shape[0], N_DEV, -1), 1, 0)

    def shard_rows(g):
        return g.reshape(N_DEV, -1, g.shape[1])

    class MeshComm:
        def __init__(self):
            self.got = {0: {n: _all_gather(shards[n][0], False, name="ag_" + n) for n in shards}}
            self.full, self.stage = {}, {}

        def weights(self, l):
            if l not in self.full:
                g = self.got[l]
                wi = jnp.moveaxis(g["w_in"], 0, 1).reshape(D, IN)
                w_in_perm = jnp.concatenate([wi[:, :NQ], wi[:, NQ + H:], wi[:, NQ:NQ + H],
                                             jnp.zeros((D, LANES - H), BF16)], axis=1)
                self.full[l] = dict(w_in_perm=w_in_perm, w_out=g["w_out"].reshape(-1, D),
                                    w_up=jnp.moveaxis(g["w_up"], 0, 1).reshape(D, -1),
                                    w_down=g["w_down"].reshape(-1, D))
            return self.full[l]

        def gather_rider(self, l, names):
            return "gather", [shards[n][l] for n in names]

        def gathered(self, l, names, outs):
            self.got.setdefault(l, {}).update(zip(names, outs))

        def scatter_rider(self, name, l, g):
            if name == "w_in_perm":
                slabs = shard_cols(jnp.concatenate([g[:, :NQ], g[:, PR:PR + H], g[:, NQ:PR]], axis=1))
            elif name == "w_up":
                slabs = shard_cols(g)
            else:
                slabs = shard_rows(g)
            return "scatter", [slabs]

        def scattered(self, name, l, out):
            self.stage[(name, l)] = out

    comm = MeshComm()
    loss_loc, dx, gr = _local_step(x[0], loss_target[0], ada, mix_norm_g, comm, b_forget, conf_w_full, conf_dw_b,
                                   conf_ln_g, conf_ln_b, sc_w_full, ffn_norm_g, ffn_w_full, ffn_dw_b, final_norm_g)
    loss = lax.psum(loss_loc, ("x", "y", "c"))

    small_names = ["ada", "mix_norm_g", "ffn_norm_g", "b_forget", "conf_dw_b", "conf_ln_g", "conf_ln_b",
                   "ffn_dw_b", "conf_dw_w", "sc_dw_w", "ffn_dw_w"]
    pk, lay = _pack([jnp.stack(gr[n]) for n in small_names] + [gr["final_norm_g"]])
    parts = _all_gather(pk, True, name="ag_small_bwd")
    tot = _unpack(_sum_slabs(parts, name="sum_small"), lay)
    g_small = dict(zip(small_names + ["final_norm_g"], tot))
    d_ada_all = _unpack(parts, lay, lead=(N_DEV,))[0]
    my_cols = lambda a, n: lax.dynamic_slice_in_dim(a, me * n, n, axis=a.ndim - 1)

    c_act_t = jnp.zeros((D, LANES), F32).at[:, :N_DEV].set(jnp.transpose(c_act)).astype(BF16)
    res = None
    for l in range(L):
        d_loc = jnp.zeros((LANES, ada_loc), F32).at[:N_DEV].set(my_cols(d_ada_all[:, l], ada_loc)).astype(BF16)
        g_l = _matmul(c_act_t, d_loc, F32, name="mm_dada")
        res = _adam_sum(g_l[None], ada_w, m_ada_w, v_ada_w, l, res, name="adam_ada_w")
    out_ada_w = res

    big = {}
    for nm, key, wq, mq, vq in (("w_down", "w_down", w_down, m_w_down, v_w_down), ("w_up", "w_up", w_up, m_w_up, v_w_up),
                                ("w_out", "w_out", w_out, m_w_out, v_w_out), ("w_in", "w_in_perm", w_in, m_w_in, v_w_in)):
        res = None
        for l in reversed(range(L)):
            res = _adam_sum(comm.stage[(key, l)], wq, mq, vq, l, res, name="adam_" + nm)
        big[nm] = res

    K31, K3 = conf_dw_w.shape[1], sc_dw_w.shape[1]
    sm = [("ada_b", ada_b, m_ada_b, v_ada_b, g_small["ada"]),
          ("mix_norm_g", mix_norm_g, m_mix_norm_g, v_mix_norm_g, g_small["mix_norm_g"]),
          ("b_forget", b_forget, m_b_forget, v_b_forget, g_small["b_forget"]),
          ("conf_dw_w", conf_dw_w, m_conf_dw_w, v_conf_dw_w, my_cols(g_small["conf_dw_w"], conf_dw_w.shape[2])),
          ("conf_dw_b", conf_dw_b, m_conf_dw_b, v_conf_dw_b, g_small["conf_dw_b"]),
          ("conf_ln_g", conf_ln_g, m_conf_ln_g, v_conf_ln_g, g_small["conf_ln_g"]),
          ("conf_ln_b", conf_ln_b, m_conf_ln_b, v_conf_ln_b, g_small["conf_ln_b"]),
          ("sc_dw_w", sc_dw_w, m_sc_dw_w, v_sc_dw_w, my_cols(g_small["sc_dw_w"], sc_dw_w.shape[2])),
          ("ffn_norm_g", ffn_norm_g, m_ffn_norm_g, v_ffn_norm_g, g_small["ffn_norm_g"]),
          ("ffn_dw_w", ffn_dw_w, m_ffn_dw_w, v_ffn_dw_w, my_cols(g_small["ffn_dw_w"], ffn_dw_w.shape[2])),
          ("ffn_dw_b", ffn_dw_b, m_ffn_dw_b, v_ffn_dw_b, g_small["ffn_dw_b"]),
          ("final_norm_g", final_norm_g, m_final_norm_g, v_final_norm_g, g_small["final_norm_g"])]
    pw, lay = _pack([t[1] for t in sm])
    pm, _ = _pack([t[2] for t in sm])
    pv, _ = _pack([t[3] for t in sm])
    pg, _ = _pack([t[4] for t in sm])
    sres = _adam_sum(pg[None], pw[None], pm[None], pv[None], 0, None, name="adam_small")
    s_g, s_d, s_m, s_v = [dict(zip([t[0] for t in sm], _unpack(r[0], lay))) for r in sres]

    def pick(idx, name):
        if name == "ada_w":
            return out_ada_w[idx]
        if name in big:
            return big[name][idx]
        return (s_g, s_d, s_m, s_v)[idx][name]

    order = ["ada_w", "ada_b", "mix_norm_g", "w_in", "b_forget", "conf_dw_w", "conf_dw_b", "conf_ln_g", "conf_ln_b",
             "sc_dw_w", "w_out", "ffn_norm_g", "w_up", "ffn_dw_w", "ffn_dw_b", "w_down", "final_norm_g"]
    outs = [loss, dx[None]]
    for idx in range(4):
        outs += [pick(idx, n) for n in order]
    return tuple(outs)
```

```python
import functools

import jax
import jax.numpy as jnp
from jax import lax
from jax.experimental import pallas as pl
from jax.experimental.pallas import tpu as pltpu

F32 = jnp.float32
BF16 = jnp.bfloat16
RMS_EPS = 1e-6
LN_EPS = 1e-5
HEAD_DIM = 128
N_ADA = 6
ADAM_LR = 0.001
ADAM_B1 = 0.9
ADAM_B2 = 0.999
ADAM_EPS = 1e-08
ADAM_WD = 0.01
ADAM_STEP = 10
N_DEV = 8
LANES = 128
VMEM_LIMIT_BYTES = 56 * 1024 * 1024
MM_TILE = 1024
MM_TILE_WIDE = 1536
MESH = pl.DeviceIdType.MESH
PEER_FLIPS = ((0, 0, 1), (1, 0, 0), (0, 1, 0), (1, 1, 0), (1, 0, 1), (0, 1, 1), (1, 1, 1))


def _params(*sem):
    return pltpu.CompilerParams(dimension_semantics=sem, vmem_limit_bytes=VMEM_LIMIT_BYTES)


def _tile(n, cap):
    if n <= cap:
        return n
    for t in range(cap - cap % LANES, 0, -LANES):
        if n % t == 0:
            return t
    raise ValueError(f"no tile for {n}")


def _mm_tile(n):
    t = _tile(n, MM_TILE)
    return t if t == min(n, MM_TILE) else _tile(n, MM_TILE_WIDE)


def _sigmoid(v):
    return jax.nn.sigmoid(v)


def _matmul(a, b, out_dtype, name, rider=None, b_transposed=False):
    M, K = a.shape
    N = b.shape[0] if b_transposed else b.shape[1]
    tm, tn = _mm_tile(M), _mm_tile(N)
    tk = K if K <= 2048 else _mm_tile(K)
    nk = K // tk
    grid = (M // tm, N // tn, nk)
    dims = _NT if b_transposed else (((1,), (0,)), ((), ()))
    kind, arrs = rider if rider is not None else (None, [])
    nr = len(arrs)

    def body(*refs):
        a_ref, b_ref = refs[:2]
        r_in = refs[2:2 + nr]
        o_ref = refs[2 + nr]
        r_out = refs[3 + nr:3 + 2 * nr]
        rest = refs[3 + 2 * nr:]
        i, j, k = pl.program_id(0), pl.program_id(1), pl.program_id(2)
        if nr:
            sems = rest[-3:]

            @pl.when((i == 0) & (j == 0) & (k == 0))
            def _():
                for r in range(nr):
                    _rider_start(kind, r_in[r], r_out[r], sems, r)

        if nk == 1:
            o_ref[...] = lax.dot_general(a_ref[...], b_ref[...], dims, preferred_element_type=F32).astype(o_ref.dtype)
        else:
            acc_ref = rest[0]

            @pl.when(k == 0)
            def _():
                acc_ref[...] = jnp.zeros_like(acc_ref)

            acc_ref[...] += lax.dot_general(a_ref[...], b_ref[...], dims, preferred_element_type=F32)

            @pl.when(k == nk - 1)
            def _():
                o_ref[...] = acc_ref[...].astype(o_ref.dtype)

        if nr:
            @pl.when((i == grid[0] - 1) & (j == grid[1] - 1) & (k == nk - 1))
            def _():
                for r in range(nr):
                    _rider_finish(kind, r_in[r], r_out[r], sems, r)

    scratch = [] if nk == 1 else [pltpu.VMEM((tm, tn), F32)]
    hbm = pl.BlockSpec(memory_space=pl.ANY)
    out_shape = jax.ShapeDtypeStruct((M, N), out_dtype)
    out_specs = pl.BlockSpec((tm, tn), lambda i, j, k: (i, j))
    if nr:
        scratch += [pltpu.SemaphoreType.DMA((7 * nr,)), pltpu.SemaphoreType.DMA((7 * nr,)),
                    pltpu.SemaphoreType.DMA((nr,))]
        out_shape = (out_shape,) + tuple(
            jax.ShapeDtypeStruct(x.shape if kind == "scatter" else (N_DEV,) + x.shape, x.dtype) for x in arrs)
        out_specs = (out_specs,) + (hbm,) * nr
    out = pl.pallas_call(
        body, name=name,
        out_shape=out_shape,
        grid=grid,
        in_specs=[pl.BlockSpec((tm, tk), lambda i, j, k: (i, k)),
                  pl.BlockSpec((tn, tk), lambda i, j, k: (j, k)) if b_transposed
                  else pl.BlockSpec((tk, tn), lambda i, j, k: (k, j))] + [hbm] * nr,
        out_specs=out_specs,
        scratch_shapes=scratch,
        compiler_params=_params(*(("arbitrary",) * 3 if nr else ("parallel", "parallel", "arbitrary"))),
    )(a, b, *arrs)
    return (out[0], list(out[1:])) if nr else out


_TN = (((0,), (0,)), ((), ()))


def _matmul_tn(a, b, out_dtype, name, col_slabs=None):
    S, M = a.shape
    _, N = b.shape
    tm, ts = _mm_tile(M), _tile(S, MM_TILE)
    tn = _mm_tile(N) if col_slabs is None else N // col_slabs
    ns = S // ts

    def body(a_ref, b_ref, o_ref, acc_ref):
        k = pl.program_id(2)

        @pl.when(k == 0)
        def _():
            acc_ref[...] = jnp.zeros_like(acc_ref)

        acc_ref[...] += lax.dot_general(a_ref[...], b_ref[...], _TN, preferred_element_type=F32)

        @pl.when(k == ns - 1)
        def _():
            if col_slabs is None:
                o_ref[...] = acc_ref[...].astype(o_ref.dtype)
            else:
                o_ref[0] = acc_ref[...].astype(o_ref.dtype)

    if col_slabs is None:
        out_shape = jax.ShapeDtypeStruct((M, N), out_dtype)
        out_spec = pl.BlockSpec((tm, tn), lambda i, j, k: (i, j))
    else:
        out_shape = jax.ShapeDtypeStruct((col_slabs, M, tn), out_dtype)
        out_spec = pl.BlockSpec((1, tm, tn), lambda i, j, k: (j, i, 0))
    return pl.pallas_call(
        body, name=name,
        out_shape=out_shape,
        grid=(M // tm, N // tn, ns),
        in_specs=[pl.BlockSpec((ts, tm), lambda i, j, k: (k, i)),
                  pl.BlockSpec((ts, tn), lambda i, j, k: (k, j))],
        out_specs=out_spec,
        scratch_shapes=[pltpu.VMEM((tm, tn), F32)],
        compiler_params=_params("parallel", "parallel", "arbitrary"),
    )(a, b)


def _site_fwd(x, delta, gate, g, sc, sh, name):
    S, D = x.shape
    T = min(256, S)
    res = delta is not None

    def body(*refs):
        if res:
            x_ref, d_ref, gate_ref, g_ref, sc_ref, sh_ref, xo_ref, h_ref = refs
            xv = x_ref[...] + gate_ref[...] * d_ref[...]
            xo_ref[...] = xv
        else:
            x_ref, g_ref, sc_ref, sh_ref, h_ref = refs
            xv = x_ref[...]
        r = lax.rsqrt(jnp.mean(xv * xv, axis=-1, keepdims=True) + RMS_EPS)
        a = g_ref[...] * (1.0 + sc_ref[...])
        h_ref[...] = (xv * r * a + sh_ref[...]).astype(BF16)

    row = pl.BlockSpec((T, D), lambda i: (i, 0))
    vec = pl.BlockSpec((1, D), lambda i: (0, 0))
    if res:
        ins, in_specs = (x, delta, gate, g, sc, sh), [row, row, vec, vec, vec, vec]
        out_shape = (jax.ShapeDtypeStruct((S, D), F32), jax.ShapeDtypeStruct((S, D), BF16))
        out_specs = (row, row)
    else:
        ins, in_specs = (x, g, sc, sh), [row, vec, vec, vec]
        out_shape = jax.ShapeDtypeStruct((S, D), BF16)
        out_specs = row
    out = pl.pallas_call(body, name=name, out_shape=out_shape, grid=(S // T,), in_specs=in_specs,
                         out_specs=out_specs, compiler_params=_params("parallel"))(*ins)
    return out if res else (x, out)


def _site_bwd(x, dh, dres, g, sc, sh, delta, gate, name):
    S, D = x.shape
    T = min(256, S)
    res = delta is not None

    def body(*refs):
        if res:
            (x_ref, dh_ref, dres_ref, g_ref, sc_ref, delta_ref, gate_ref,
             dx_ref, dsh_ref, da_ref, dd_ref, dgate_ref) = refs
        else:
            x_ref, dh_ref, dres_ref, g_ref, sc_ref, dx_ref, dsh_ref, da_ref = refs
        i = pl.program_id(0)
        xv = x_ref[...]
        dhv = dh_ref[...]
        r = lax.rsqrt(jnp.mean(xv * xv, axis=-1, keepdims=True) + RMS_EPS)
        xh = xv * r
        dxh = dhv * (g_ref[...] * (1.0 + sc_ref[...]))
        dx = r * (dxh - xh * jnp.mean(dxh * xh, axis=-1, keepdims=True)) + dres_ref[...]
        dx_ref[...] = dx

        @pl.when(i == 0)
        def _():
            dsh_ref[...] = jnp.zeros_like(dsh_ref)
            da_ref[...] = jnp.zeros_like(da_ref)
            if res:
                dgate_ref[...] = jnp.zeros_like(dgate_ref)

        dsh_ref[...] += jnp.sum(dhv, axis=0, keepdims=True)
        da_ref[...] += jnp.sum(dhv * xh, axis=0, keepdims=True)
        if res:
            dd_ref[...] = (gate_ref[...] * dx).astype(BF16)
            dgate_ref[...] += jnp.sum(dx * delta_ref[...], axis=0, keepdims=True)

    row = pl.BlockSpec((T, D), lambda i: (i, 0))
    vec = pl.BlockSpec((1, D), lambda i: (0, 0))
    vshape = jax.ShapeDtypeStruct((1, D), F32)
    if res:
        ins, in_specs = (x, dh, dres, g, sc, delta, gate), [row, row, row, vec, vec, row, vec]
        out_shape = (jax.ShapeDtypeStruct((S, D), F32), vshape, vshape, jax.ShapeDtypeStruct((S, D), BF16), vshape)
        out_specs = (row, vec, vec, row, vec)
    else:
        ins, in_specs = (x, dh, dres, g, sc), [row, row, row, vec, vec]
        out_shape = (jax.ShapeDtypeStruct((S, D), F32), vshape, vshape)
        out_specs = (row, vec, vec)
    return pl.pallas_call(body, name=name, out_shape=out_shape, grid=(S // T,), in_specs=in_specs,
                          out_specs=out_specs, compiler_params=_params("arbitrary"))(*ins)


def _final_fwd_bwd(x, delta, gate, gfin, target, name):
    S, D = x.shape
    T = min(256, S)

    def body(x_ref, delta_ref, gate_ref, g_ref, t_ref, loss_ref, dx_ref, dd_ref, dgate_ref, dg_ref):
        i = pl.program_id(0)
        dl = delta_ref[...]
        xv = x_ref[...] + gate_ref[...] * dl
        r = lax.rsqrt(jnp.mean(xv * xv, axis=-1, keepdims=True) + RMS_EPS)
        xh = xv * r
        gv = g_ref[...]
        e = xh * gv - t_ref[...]
        dy = e * (1.0 / D)
        dxh = dy * gv
        dx = r * (dxh - xh * jnp.mean(dxh * xh, axis=-1, keepdims=True))
        dx_ref[...] = dx
        dd_ref[...] = (gate_ref[...] * dx).astype(BF16)

        @pl.when(i == 0)
        def _():
            loss_ref[...] = jnp.zeros_like(loss_ref)
            dgate_ref[...] = jnp.zeros_like(dgate_ref)
            dg_ref[...] = jnp.zeros_like(dg_ref)

        loss_ref[...] += jnp.sum(e * e, axis=0, keepdims=True)
        dgate_ref[...] += jnp.sum(dx * dl, axis=0, keepdims=True)
        dg_ref[...] += jnp.sum(dy * xh, axis=0, keepdims=True)

    row = pl.BlockSpec((T, D), lambda i: (i, 0))
    vec = pl.BlockSpec((1, D), lambda i: (0, 0))
    vshape = jax.ShapeDtypeStruct((1, D), F32)
    return pl.pallas_call(
        body, name=name,
        out_shape=(vshape, jax.ShapeDtypeStruct((S, D), F32), jax.ShapeDtypeStruct((S, D), BF16), vshape, vshape),
        grid=(S // T,), in_specs=[row, row, vec, vec, row], out_specs=(vec, row, row, vec, vec),
        compiler_params=_params("arbitrary"))(x, delta, gate, gfin, target)


def _split3(v):
    hi = v.astype(BF16)
    r1 = v - hi.astype(F32)
    mid = r1.astype(BF16)
    lo = (r1 - mid.astype(F32)).astype(BF16)
    return hi, mid, lo


def _tri_dot(tri, v):
    hi, mid, lo = _split3(v)
    d = functools.partial(jnp.dot, preferred_element_type=F32)
    return d(tri, hi) + d(tri, mid) + d(tri, lo)


def _fgate_fwd(rest, bpad, fblk, name):
    S = rest.shape[0]
    CH = min(256, S)
    nch = S // CH

    def body(f_ref, b_ref, o_ref):
        row = lax.broadcasted_iota(jnp.int32, (CH, CH), 0)
        col = lax.broadcasted_iota(jnp.int32, (CH, CH), 1)
        tri = (row >= col).astype(BF16)

        def step(ci, carry):
            rows = pl.ds(pl.multiple_of(ci * CH, CH), CH)
            z = f_ref[rows, :] + b_ref[...]
            lf = jnp.minimum(z, 0.0) - jnp.log(1.0 + jnp.exp(-jnp.abs(z)))
            o_ref[rows, :] = _tri_dot(tri, lf) + carry
            return carry + jnp.sum(lf, axis=0, keepdims=True)

        lax.fori_loop(0, nch, step, jnp.zeros((1, LANES), F32))

    return pl.pallas_call(
        body, name=name, out_shape=jax.ShapeDtypeStruct((S, LANES), F32), grid=(1,),
        in_specs=[pl.BlockSpec((S, LANES), lambda i: (0, fblk)), pl.BlockSpec((1, LANES), lambda i: (0, 0))],
        out_specs=pl.BlockSpec((S, LANES), lambda i: (0, 0)),
        compiler_params=_params("arbitrary"))(rest, bpad)


def _fgate_bwd(rest, bpad, dF, fblk, name):
    S = rest.shape[0]
    CH = min(256, S)
    nch = S // CH

    def body(f_ref, b_ref, df_ref, o_ref, db_ref):
        row = lax.broadcasted_iota(jnp.int32, (CH, CH), 0)
        col = lax.broadcasted_iota(jnp.int32, (CH, CH), 1)
        tri = (col >= row).astype(BF16)

        def step(n, carry):
            sfx_carry, db = carry
            ci = nch - 1 - n
            rows = pl.ds(pl.multiple_of(ci * CH, CH), CH)
            z = f_ref[rows, :] + b_ref[...]
            dfv = df_ref[rows, :]
            dz = (_tri_dot(tri, dfv) + sfx_carry) * _sigmoid(-z)
            o_ref[rows, :] = dz.astype(BF16)
            return sfx_carry + jnp.sum(dfv, axis=0, keepdims=True), db + jnp.sum(dz, axis=0, keepdims=True)

        zero = jnp.zeros((1, LANES), F32)
        _, db = lax.fori_loop(0, nch, step, (zero, zero))
        db_ref[...] = db

    blk = pl.BlockSpec((S, LANES), lambda i: (0, 0))
    return pl.pallas_call(
        body, name=name,
        out_shape=(jax.ShapeDtypeStruct((S, LANES), BF16), jax.ShapeDtypeStruct((1, LANES), F32)), grid=(1,),
        in_specs=[pl.BlockSpec((S, LANES), lambda i: (0, fblk)), pl.BlockSpec((1, LANES), lambda i: (0, 0)), blk],
        out_specs=(blk, pl.BlockSpec((1, LANES), lambda i: (0, 0))),
        compiler_params=_params("arbitrary"))(rest, bpad, dF)


_NT = (((1,), (1,)), ((), ()))
LOG2E = 1.4426950408889634
ATTN_TILE = 512


def _blocked_rows(a, TA):
    H, S = a.shape
    return a.reshape(H, S // TA, 1, TA)


def _attn_fwd(qkv, nfb, H, name):
    S = qkv.shape[0]
    TA = min(ATTN_TILE, S)
    nb = S // TA
    c = HEAD_DIM ** -0.5 * LOG2E

    def body(q_ref, k_ref, v_ref, nf_ref, o_ref, o32_ref, lse_ref, m_ref, l_ref, acc_ref):
        i = pl.program_id(1)
        m_ref[...] = jnp.full_like(m_ref, -jnp.inf)
        l_ref[...] = jnp.zeros_like(l_ref)
        acc_ref[...] = jnp.zeros_like(acc_ref)

        def block(j, masked):
            rows = pl.ds(pl.multiple_of(j * TA, TA), TA)
            st = (lax.dot_general(k_ref[rows, :], q_ref[...], _NT, preferred_element_type=F32) * c
                  + jnp.tile(nf_ref[0, rows, :], (1, TA // LANES)))
            if masked:
                key = lax.broadcasted_iota(jnp.int32, (TA, TA), 0)
                qry = lax.broadcasted_iota(jnp.int32, (TA, TA), 1)
                st = jnp.where(key <= qry, st, -jnp.inf)
            m_old = m_ref[...]
            m_new = jnp.maximum(m_old, jnp.max(st, axis=0, keepdims=True))
            alpha = jnp.exp2(m_old - m_new)
            pt = jnp.exp2(st - m_new)
            l_ref[...] = alpha * l_ref[...] + jnp.sum(pt, axis=0, keepdims=True)
            acc_ref[...] = alpha * acc_ref[...] + lax.dot_general(v_ref[rows, :], pt.astype(BF16), _TN,
                                                                  preferred_element_type=F32)
            m_ref[...] = m_new

        def loop(j, carry):
            block(j, False)
            return carry

        lax.fori_loop(0, i, loop, 0)
        block(i, True)
        o = jnp.transpose(acc_ref[...] / l_ref[...])
        o32_ref[...] = o
        o_ref[...] = o.astype(BF16)
        lse_ref[0, 0] = m_ref[...] + jnp.log(l_ref[...]) * LOG2E

    qblk = pl.BlockSpec((TA, HEAD_DIM), lambda h, i: (i, h))
    return pl.pallas_call(
        body, name=name, grid=(H, nb),
        in_specs=[qblk,
                  pl.BlockSpec((S, HEAD_DIM), lambda h, i: (0, H + h)),
                  pl.BlockSpec((S, HEAD_DIM), lambda h, i: (0, 2 * H + h)),
                  pl.BlockSpec((1, S, LANES), lambda h, i: (h, 0, 0))],
        out_specs=(qblk, qblk, pl.BlockSpec((1, 1, 1, TA), lambda h, i: (h, i, 0, 0))),
        scratch_shapes=[pltpu.VMEM((1, TA), F32), pltpu.VMEM((1, TA), F32), pltpu.VMEM((HEAD_DIM, TA), F32)],
        out_shape=(jax.ShapeDtypeStruct((S, H * HEAD_DIM), BF16), jax.ShapeDtypeStruct((S, H * HEAD_DIM), F32),
                   jax.ShapeDtypeStruct((H, nb, 1, TA), F32)),
        compiler_params=_params("parallel", "parallel"))(qkv, qkv, qkv, nfb)


def _attn_delta(o, do, H, name):
    S = o.shape[0]
    T = min(512, S)

    def body(o_ref, do_ref, d_ref):
        d_ref[0] = jnp.sum(o_ref[...].astype(F32) * do_ref[...].astype(F32), axis=-1, keepdims=True)

    blk = pl.BlockSpec((T, HEAD_DIM), lambda h, i: (i, h))
    return pl.pallas_call(
        body, name=name, out_shape=jax.ShapeDtypeStruct((H, S, 1), F32), grid=(H, S // T),
        in_specs=[blk, blk], out_specs=pl.BlockSpec((1, T, 1), lambda h, i: (h, i, 0)),
        compiler_params=_params("parallel", "parallel"))(o, do)


def _ds_tile(k, q, v, do, nfb, lse_row, delta_row, c, masked):
    TK, TQ = k.shape[0], q.shape[0]
    st = lax.dot_general(k, q, _NT, preferred_element_type=F32) * c + jnp.tile(nfb, (1, TQ // LANES))
    pt = jnp.exp2(st - lse_row)
    if masked:
        key = lax.broadcasted_iota(jnp.int32, (TK, TQ), 0)
        qry = lax.broadcasted_iota(jnp.int32, (TK, TQ), 1)
        pt = jnp.where(key <= qry, pt, 0.0)
    dpt = lax.dot_general(v, do, _NT, preferred_element_type=F32)
    return pt, pt * (dpt - delta_row)


def _attn_bwd(qkv, nfb, do, lse, delta, H, name):
    S = qkv.shape[0]
    TA = min(ATTN_TILE, S)
    nb = S // TA
    scale = HEAD_DIM ** -0.5
    c = scale * LOG2E

    def body(q_ref, k_ref, v_ref, nf_ref, do_ref, lse_ref, dl_ref, dq_ref, dk_ref, dv_ref, drow_ref, dnf_ref,
             dq_acc, dk_acc, dv_acc, dnf_acc):
        j = pl.program_id(1)

        @pl.when(j == 0)
        def _():
            dq_acc[...] = jnp.zeros_like(dq_acc)
            drow_ref[...] = jnp.zeros_like(drow_ref)

        dk_acc[...] = jnp.zeros_like(dk_acc)
        dv_acc[...] = jnp.zeros_like(dv_acc)
        dnf_acc[...] = jnp.zeros_like(dnf_acc)
        kb = k_ref[...]

        def block(i, masked):
            rows = pl.ds(pl.multiple_of(i * TA, TA), TA)
            qb = q_ref[rows, :]
            dob = do_ref[rows, :]
            pt, dst = _ds_tile(kb, qb, v_ref[...], dob, nf_ref[0], lse_ref[0, i], dl_ref[0, i], c, masked)
            dsb = dst.astype(BF16)
            dv_acc[...] += jnp.dot(pt.astype(BF16), dob, preferred_element_type=F32)
            dk_acc[...] += jnp.dot(dsb, qb, preferred_element_type=F32)
            dq_acc[rows, :] += lax.dot_general(dsb, kb, _TN, preferred_element_type=F32)
            drow_ref[0, i] += jnp.sum(dst, axis=0, keepdims=True)
            part = dst[:, 0:LANES]
            for t in range(1, TA // LANES):
                part = part + dst[:, t * LANES:(t + 1) * LANES]
            dnf_acc[...] += part

        def loop(i, carry):
            block(i, False)
            return carry

        block(j, True)
        lax.fori_loop(j + 1, nb, loop, 0)
        dk_ref[...] = (dk_acc[...] * scale).astype(BF16)
        dv_ref[...] = dv_acc[...].astype(BF16)
        dnf_ref[0] = jnp.sum(dnf_acc[...], axis=-1, keepdims=True)

        @pl.when(j == nb - 1)
        def _():
            dq_ref[...] = (dq_acc[...] * scale).astype(BF16)

    full = pl.BlockSpec((S, HEAD_DIM), lambda h, j: (0, h))
    row_stat = pl.BlockSpec((1, nb, 1, TA), lambda h, j: (h, 0, 0, 0))
    kblk = lambda c0: pl.BlockSpec((TA, HEAD_DIM), lambda h, j: (j, c0 + h))
    shp = jax.ShapeDtypeStruct((S, H * HEAD_DIM), BF16)
    return pl.pallas_call(
        body, name=name, grid=(H, nb),
        in_specs=[full, kblk(H), kblk(2 * H), pl.BlockSpec((1, TA, LANES), lambda h, j: (h, j, 0)), full,
                  row_stat, row_stat],
        out_specs=(full, kblk(0), kblk(0), row_stat, pl.BlockSpec((1, TA, 1), lambda h, j: (h, j, 0))),
        scratch_shapes=[pltpu.VMEM((S, HEAD_DIM), F32), pltpu.VMEM((TA, HEAD_DIM), F32),
                        pltpu.VMEM((TA, HEAD_DIM), F32), pltpu.VMEM((TA, LANES), F32)],
        out_shape=(shp, shp, shp, jax.ShapeDtypeStruct((H, nb, 1, TA), F32), jax.ShapeDtypeStruct((H, S, 1), F32)),
        compiler_params=_params("parallel", "arbitrary"))(qkv, qkv, qkv, nfb, do, lse, delta)


def _taps(ext_ref, w_ref, K, base, r0, rows, cols, reverse=False, init=None):
    acc = init
    for k in range(K):
        wk = w_ref[(K - 1 - k) if reverse else k:((K - 1 - k) if reverse else k) + 1, cols]
        term = wk * ext_ref[base + k + r0:base + k + r0 + rows, cols]
        acc = term if acc is None else acc + term
    return acc


def _prev_blk(T, H):
    return lambda i: jnp.maximum(i * (T // H) - 1, 0)


def _next_blk(T, H, S):
    return lambda i: jnp.minimum((i + 1) * (T // H), S // H - 1)


def _conf_fwd(rest, w, b, lng, lnb, name):
    S = rest.shape[0]
    K, C = w.shape
    H, T = 32, min(256, S)
    RS = min(64, T)
    base = H - (K - 1)

    def body(cv_ref, cg_ref, cvp_ref, cgp_ref, w_ref, b_ref, g_ref, bb_ref, o_ref, ext_ref):
        i = pl.program_id(0)
        ext_ref[0:H, :] = jnp.where(i > 0, cvp_ref[...] * _sigmoid(cgp_ref[...]), 0.0)
        ext_ref[H:H + T, :] = cv_ref[...] * _sigmoid(cg_ref[...])
        for r0 in range(0, T, RS):
            cc = _taps(ext_ref, w_ref, K, base, r0, RS, slice(None), init=jnp.broadcast_to(b_ref[...], (RS, C)))
            xc = cc - jnp.mean(cc, axis=-1, keepdims=True)
            y = xc * lax.rsqrt(jnp.mean(xc * xc, axis=-1, keepdims=True) + LN_EPS) * g_ref[...] + bb_ref[...]
            o_ref[r0:r0 + RS, :] = (y * _sigmoid(y)).astype(BF16)

    pb = _prev_blk(T, H)
    cur = lambda cb: pl.BlockSpec((T, C), lambda i: (i, cb))
    prev = lambda cb: pl.BlockSpec((H, C), lambda i: (pb(i), cb))
    full = lambda a: pl.BlockSpec(a.shape, lambda i: (0, 0))
    return pl.pallas_call(
        body, name=name, out_shape=jax.ShapeDtypeStruct((S, C), BF16), grid=(S // T,),
        in_specs=[cur(0), cur(1), prev(0), prev(1), full(w), full(b), full(lng), full(lnb)],
        out_specs=pl.BlockSpec((T, C), lambda i: (i, 0)),
        scratch_shapes=[pltpu.VMEM((H + T, C), F32)],
        compiler_params=_params("parallel"))(rest, rest, rest, rest, w, b, lng, lnb)


def _conf_bwd(rest, dcs, w, b, lng, lnb, name):
    S = rest.shape[0]
    K, C = w.shape
    H, T = 32, min(256, S)
    RS = 32
    nI = S // T
    base = H - (K - 1)

    def body(cv_ref, cg_ref, cvp_ref, cgp_ref, cvn_ref, cgn_ref, do_ref, don_ref, w_ref, b_ref, g_ref, bb_ref,
             dcvg_ref, dw_ref, dvec_ref, ext_ref, dcc_ref):
        i = pl.program_id(0)
        ext_ref[0:H, :] = jnp.where(i > 0, cvp_ref[...] * _sigmoid(cgp_ref[...]), 0.0)
        ext_ref[H:H + T, :] = cv_ref[...] * _sigmoid(cg_ref[...])
        ext_ref[H + T:H + T + H, :] = cvn_ref[...] * _sigmoid(cgn_ref[...])

        @pl.when(i == 0)
        def _():
            dw_ref[...] = jnp.zeros_like(dw_ref)
            dvec_ref[...] = jnp.zeros_like(dvec_ref)

        db = jnp.zeros((1, C), F32)
        dg = jnp.zeros((1, C), F32)
        dbb = jnp.zeros((1, C), F32)
        for r0 in range(0, T + H, RS):
            cc = _taps(ext_ref, w_ref, K, base, r0, RS, slice(None), init=jnp.broadcast_to(b_ref[...], (RS, C)))
            xc = cc - jnp.mean(cc, axis=-1, keepdims=True)
            r = lax.rsqrt(jnp.mean(xc * xc, axis=-1, keepdims=True) + LN_EPS)
            xh = xc * r
            y = xh * g_ref[...] + bb_ref[...]
            sy = _sigmoid(y)
            if r0 < T:
                d_o = do_ref[r0:r0 + RS, :]
            else:
                d_o = jnp.where(i < nI - 1, don_ref[r0 - T:r0 - T + RS, :], 0.0)
            dy = d_o * (sy * (1.0 + y * (1.0 - sy)))
            dxh = dy * g_ref[...]
            dcc = r * (dxh - jnp.mean(dxh, axis=-1, keepdims=True)
                       - xh * jnp.mean(dxh * xh, axis=-1, keepdims=True))
            dcc_ref[r0:r0 + RS, :] = dcc
            if r0 < T:
                dbb = dbb + jnp.sum(dy, axis=0, keepdims=True)
                dg = dg + jnp.sum(dy * xh, axis=0, keepdims=True)
                db = db + jnp.sum(dcc, axis=0, keepdims=True)
        dvec_ref[0:1, :] += db
        dvec_ref[1:2, :] += dg
        dvec_ref[2:3, :] += dbb
        R2 = min(64, T)
        for k in range(K):
            s = jnp.zeros((1, C), F32)
            for r0 in range(0, T, R2):
                s = s + jnp.sum(dcc_ref[r0:r0 + R2, :] * ext_ref[base + k + r0:base + k + r0 + R2, :],
                                axis=0, keepdims=True)
            dw_ref[k:k + 1, :] += s
        for r0 in range(0, T, R2):
            dci = _taps(dcc_ref, w_ref, K, 0, r0, R2, slice(None), reverse=True)
            cvv = cv_ref[r0:r0 + R2, :]
            sg = _sigmoid(cg_ref[r0:r0 + R2, :])
            dcvg_ref[r0:r0 + R2, 0:C] = (dci * sg).astype(BF16)
            dcvg_ref[r0:r0 + R2, C:2 * C] = (dci * cvv * sg * (1.0 - sg)).astype(BF16)

    pb, nb_ = _prev_blk(T, H), _next_blk(T, H, S)
    cur = lambda cb: pl.BlockSpec((T, C), lambda i: (i, cb))
    prev = lambda cb: pl.BlockSpec((H, C), lambda i: (pb(i), cb))
    nxt = lambda cb: pl.BlockSpec((H, C), lambda i: (nb_(i), cb))
    full = lambda a: pl.BlockSpec(a.shape, lambda i: (0, 0))
    return pl.pallas_call(
        body, name=name,
        out_shape=(jax.ShapeDtypeStruct((S, 2 * C), BF16), jax.ShapeDtypeStruct((32, C), F32),
                   jax.ShapeDtypeStruct((8, C), F32)),
        grid=(nI,),
        in_specs=[cur(0), cur(1), prev(0), prev(1), nxt(0), nxt(1), cur(0), nxt(0),
                  full(w), full(b), full(lng), full(lnb)],
        out_specs=(pl.BlockSpec((T, 2 * C), lambda i: (i, 0)), pl.BlockSpec((32, C), lambda i: (0, 0)),
                   pl.BlockSpec((8, C), lambda i: (0, 0))),
        scratch_shapes=[pltpu.VMEM((H + T + H, C), F32), pltpu.VMEM((T + H, C), F32)],
        compiler_params=_params("arbitrary"))(rest, rest, rest, rest, rest, rest, dcs, dcs, w, b, lng, lnb)


def _sconv_fwd(rest, w, name):
    S = rest.shape[0]
    K, C = w.shape
    H, T = 8, min(256, S)
    RS = min(64, T)
    base = H - (K - 1)

    def body(sx_ref, sb_ref, sc_ref, sxp_ref, scp_ref, w_ref, o_ref, ext_ref):
        i = pl.program_id(0)
        ext_ref[0:H, :] = jnp.where(i > 0, sxp_ref[...] * scp_ref[...], 0.0)
        ext_ref[H:H + T, :] = sx_ref[...] * sc_ref[...]
        for r0 in range(0, T, RS):
            cz = _taps(ext_ref, w_ref, K, base, r0, RS, slice(None))
            o_ref[r0:r0 + RS, :] = (sb_ref[r0:r0 + RS, :] * cz).astype(BF16)

    pb = _prev_blk(T, H)
    cur = lambda cb: pl.BlockSpec((T, C), lambda i: (i, cb))
    prev = lambda cb: pl.BlockSpec((H, C), lambda i: (pb(i), cb))
    return pl.pallas_call(
        body, name=name, out_shape=jax.ShapeDtypeStruct((S, C), BF16), grid=(S // T,),
        in_specs=[cur(2), cur(3), cur(4), prev(2), prev(4), pl.BlockSpec(w.shape, lambda i: (0, 0))],
        out_specs=pl.BlockSpec((T, C), lambda i: (i, 0)),
        scratch_shapes=[pltpu.VMEM((H + T, C), F32)],
        compiler_params=_params("parallel"))(rest, rest, rest, rest, rest, w)


def _sconv_bwd(rest, dcs, w, name):
    S = rest.shape[0]
    K, C = w.shape
    H, T = 8, min(256, S)
    RS = min(64, T)
    nI = S // T
    base = H - (K - 1)

    def body(sx_ref, sb_ref, sc_ref, sxp_ref, scp_ref, sbn_ref, do_ref, don_ref, w_ref,
             dout_ref, dw_ref, ext_ref, dcv_ref):
        i = pl.program_id(0)
        ext_ref[0:H, :] = jnp.where(i > 0, sxp_ref[...] * scp_ref[...], 0.0)
        ext_ref[H:H + T, :] = sx_ref[...] * sc_ref[...]
        dcv_ref[0:T, :] = do_ref[...] * sb_ref[...]
        dcv_ref[T:T + H, :] = jnp.where(i < nI - 1, don_ref[...] * sbn_ref[...], 0.0)

        @pl.when(i == 0)
        def _():
            dw_ref[...] = jnp.zeros_like(dw_ref)

        for k in range(K):
            s = jnp.zeros((1, C), F32)
            for r0 in range(0, T, RS):
                s = s + jnp.sum(dcv_ref[r0:r0 + RS, :] * ext_ref[base + k + r0:base + k + r0 + RS, :],
                                axis=0, keepdims=True)
            dw_ref[k:k + 1, :] += s
        for r0 in range(0, T, RS):
            cz = _taps(ext_ref, w_ref, K, base, r0, RS, slice(None))
            dz = _taps(dcv_ref, w_ref, K, 0, r0, RS, slice(None), reverse=True)
            dout_ref[r0:r0 + RS, 0:C] = (dz * sc_ref[r0:r0 + RS, :]).astype(BF16)
            dout_ref[r0:r0 + RS, C:2 * C] = (do_ref[r0:r0 + RS, :] * cz).astype(BF16)
            dout_ref[r0:r0 + RS, 2 * C:3 * C] = (dz * sx_ref[r0:r0 + RS, :]).astype(BF16)

    pb, nb_ = _prev_blk(T, H), _next_blk(T, H, S)
    cur = lambda cb: pl.BlockSpec((T, C), lambda i: (i, cb))
    prev = lambda cb: pl.BlockSpec((H, C), lambda i: (pb(i), cb))
    nxt = lambda cb: pl.BlockSpec((H, C), lambda i: (nb_(i), cb))
    return pl.pallas_call(
        body, name=name,
        out_shape=(jax.ShapeDtypeStruct((S, 3 * C), BF16), jax.ShapeDtypeStruct((8, C), F32)),
        grid=(nI,),
        in_specs=[cur(2), cur(3), cur(4), prev(2), prev(4), nxt(3), cur(1), nxt(1),
                  pl.BlockSpec(w.shape, lambda i: (0, 0))],
        out_specs=(pl.BlockSpec((T, 3 * C), lambda i: (i, 0)), pl.BlockSpec((8, C), lambda i: (0, 0))),
        scratch_shapes=[pltpu.VMEM((H + T, C), F32), pltpu.VMEM((T + H, C), F32)],
        compiler_params=_params("arbitrary"))(rest, rest, rest, rest, rest, rest, dcs, dcs, w)


FFN_ROWS = 256
FFN_HALO = 16


def _shift_mats(T):
    r = lax.broadcasted_iota(jnp.int32, (T, T), 0)
    c = lax.broadcasted_iota(jnp.int32, (T, T), 1)
    down = [(r == c + k).astype(BF16) for k in (1, 2)]
    up = [(c == r + k).astype(BF16) for k in (1, 2)]
    return down, up


def _edge_rows(strip, shift, first):
    sub = lax.broadcasted_iota(jnp.int32, strip.shape, 0)
    if first:
        return jnp.where(sub < shift, pltpu.roll(strip, shift, 0), 0.0)
    return jnp.where(sub >= 8 - shift, pltpu.roll(strip, 8 - shift, 0), 0.0)


def _conv3_tile(x_ref, prev_ref, w_ref, b_ref, down, has_prev, u_ref, xm_refs=None):
    T = x_ref.shape[0]
    x = x_ref[...]
    xm1 = jnp.dot(down[0], x, preferred_element_type=F32)
    xm2 = jnp.dot(down[1], x, preferred_element_type=F32)
    u_ref[...] = b_ref[...] + w_ref[0:1, :] * xm2 + w_ref[1:2, :] * xm1 + w_ref[2:3, :] * x.astype(F32)
    tail = jnp.where(has_prev, prev_ref[...].astype(F32)[FFN_HALO - 8:, :], 0.0)
    p1, p2 = _edge_rows(tail, 1, True), _edge_rows(tail, 2, True)
    u_ref[0:8, :] += w_ref[0:1, :] * p2 + w_ref[1:2, :] * p1
    if xm_refs is not None:
        xm_refs[0][...] = xm1
        xm_refs[1][...] = xm2
        xm_refs[0][0:8, :] += p1
        xm_refs[1][0:8, :] += p2


def _ffn_fwd(hu, w, b, name):
    S, F2 = hu.shape
    Fd = F2 // 2
    K = w.shape[0]
    assert K == 3
    T = min(FFN_ROWS, S)
    tc = _tile(Fd, 512)
    nJ = Fd // tc

    def body(g_ref, v_ref, gp_ref, vp_ref, wg_ref, wv_ref, bg_ref, bv_ref, o_ref, ug_ref, uv_ref):
        i = pl.program_id(1)
        down, _ = _shift_mats(T)
        _conv3_tile(g_ref, gp_ref, wg_ref, bg_ref, down, i > 0, ug_ref)
        _conv3_tile(v_ref, vp_ref, wv_ref, bv_ref, down, i > 0, uv_ref)
        ug = ug_ref[...]
        o_ref[...] = (ug * _sigmoid(ug) * uv_ref[...]).astype(BF16)

    pb = _prev_blk(T, FFN_HALO)
    cur = lambda off: pl.BlockSpec((T, tc), lambda j, i: (i, j + off))
    prev = lambda off: pl.BlockSpec((FFN_HALO, tc), lambda j, i: (pb(i), j + off))
    wsp = lambda off: pl.BlockSpec((K, tc), lambda j, i: (0, j + off))
    bsp = lambda off: pl.BlockSpec((1, tc), lambda j, i: (0, j + off))
    return pl.pallas_call(
        body, name=name, out_shape=jax.ShapeDtypeStruct((S, Fd), BF16), grid=(nJ, S // T),
        in_specs=[cur(0), cur(nJ), prev(0), prev(nJ), wsp(0), wsp(nJ), bsp(0), bsp(nJ)],
        out_specs=pl.BlockSpec((T, tc), lambda j, i: (i, j)),
        scratch_shapes=[pltpu.VMEM((T, tc), F32), pltpu.VMEM((T, tc), F32)],
        compiler_params=_params("parallel", "parallel"))(hu, hu, hu, hu, w, w, b, b)


def _ffn_bwd(hu, dact, w, b, name):
    S, F2 = hu.shape
    Fd = F2 // 2
    K = w.shape[0]
    assert K == 3
    T = min(FFN_ROWS, S)
    tc = _tile(Fd, 512)
    nJ = Fd // tc
    nI = S // T

    def body(g_ref, v_ref, gp_ref, vp_ref, gn_ref, vn_ref, da_ref, dan_ref, wg_ref, wv_ref, bg_ref, bv_ref,
             dg_ref, dv_ref, dwg_ref, dwv_ref, ug_ref, uv_ref, g1_ref, g2_ref, v1_ref, v2_ref, dh_ref):
        i = pl.program_id(1)
        down, up = _shift_mats(T)
        _conv3_tile(g_ref, gp_ref, wg_ref, bg_ref, down, i > 0, ug_ref, (g1_ref, g2_ref))
        _conv3_tile(v_ref, vp_ref, wv_ref, bv_ref, down, i > 0, uv_ref, (v1_ref, v2_ref))

        @pl.when(i == 0)
        def _():
            dwg_ref[...] = jnp.zeros_like(dwg_ref)
            dwv_ref[...] = jnp.zeros_like(dwv_ref)

        def d_u(ug, uv, d_a):
            sg = _sigmoid(ug)
            return d_a * uv * (sg * (1.0 + ug * (1.0 - sg))), d_a * (ug * sg)

        dug, duv = d_u(ug_ref[...], uv_ref[...], da_ref[...].astype(F32))

        def next_rows(x_ref, xn_ref, w_ref, b_ref):
            strip = jnp.concatenate([x_ref[T - FFN_HALO:, :].astype(F32)[FFN_HALO - 8:, :],
                                     xn_ref[...].astype(F32)[0:8, :]], axis=0)
            return (b_ref[...] + w_ref[0:1, :] * strip[6:14, :] + w_ref[1:2, :] * strip[7:15, :]
                    + w_ref[2:3, :] * strip[8:16, :])

        d_an = jnp.where(i < nI - 1, dan_ref[...].astype(F32)[0:8, :], 0.0)
        dug_n, duv_n = d_u(next_rows(g_ref, gn_ref, wg_ref, bg_ref), next_rows(v_ref, vn_ref, wv_ref, bv_ref), d_an)

        for du, du_n, x_ref, x1_ref, x2_ref, w_ref, dw_ref, out_ref in (
                (dug, dug_n, g_ref, g1_ref, g2_ref, wg_ref, dwg_ref, dg_ref),
                (duv, duv_n, v_ref, v1_ref, v2_ref, wv_ref, dwv_ref, dv_ref)):
            dw_ref[0:1, :] += jnp.sum(du * x2_ref[...], axis=0, keepdims=True)
            dw_ref[1:2, :] += jnp.sum(du * x1_ref[...], axis=0, keepdims=True)
            dw_ref[2:3, :] += jnp.sum(du * x_ref[...].astype(F32), axis=0, keepdims=True)
            dw_ref[3:4, :] += jnp.sum(du, axis=0, keepdims=True)
            dub = du.astype(BF16)
            dh_ref[...] = (w_ref[2:3, :] * du + w_ref[1:2, :] * jnp.dot(up[0], dub, preferred_element_type=F32)
                           + w_ref[0:1, :] * jnp.dot(up[1], dub, preferred_element_type=F32))
            nxt = du_n.astype(BF16).astype(F32)
            dh_ref[T - 8:, :] += w_ref[1:2, :] * _edge_rows(nxt, 1, False) + w_ref[0:1, :] * _edge_rows(nxt, 2, False)
            out_ref[...] = dh_ref[...].astype(BF16)

    pb, nbb = _prev_blk(T, FFN_HALO), _next_blk(T, FFN_HALO, S)
    cur = lambda off: pl.BlockSpec((T, tc), lambda j, i: (i, j + off))
    prev = lambda off: pl.BlockSpec((FFN_HALO, tc), lambda j, i: (pb(i), j + off))
    nxt = lambda off: pl.BlockSpec((FFN_HALO, tc), lambda j, i: (nbb(i), j + off))
    wsp = lambda off: pl.BlockSpec((K, tc), lambda j, i: (0, j + off))
    bsp = lambda off: pl.BlockSpec((1, tc), lambda j, i: (0, j + off))
    half = jax.ShapeDtypeStruct((S, Fd), BF16)
    dws = jax.ShapeDtypeStruct((8, Fd), F32)
    tile = pltpu.VMEM((T, tc), F32)
    return pl.pallas_call(
        body, name=name, out_shape=(half, half, dws, dws), grid=(nJ, nI),
        in_specs=[cur(0), cur(nJ), prev(0), prev(nJ), nxt(0), nxt(nJ), cur(0), nxt(0),
                  wsp(0), wsp(nJ), bsp(0), bsp(nJ)],
        out_specs=(pl.BlockSpec((T, tc), lambda j, i: (i, j)), pl.BlockSpec((T, tc), lambda j, i: (i, j)),
                   pl.BlockSpec((8, tc), lambda j, i: (0, j)), pl.BlockSpec((8, tc), lambda j, i: (0, j))),
        scratch_shapes=[tile] * 7,
        compiler_params=_params("parallel", "arbitrary"))(hu, hu, hu, hu, hu, hu, dact, dact, w, w, b, b)


def _position():
    return lax.axis_index("x"), lax.axis_index("y"), lax.axis_index("c")


def _slot(px, py, pc):
    return 4 * px + 2 * py + pc


def _gather_copies(x_ref, out_ref, sems, r, starting=False):
    send_sems, recv_sems, local_sems = sems
    px, py, pc = _position()
    me, sibling = (px, py, pc), (px, py, 1 - pc)
    chips = [(1 - px, py), (px, 1 - py), (1 - px, 1 - py)]

    def copy(k, block, to, src=None):
        dst = out_ref.at[_slot(*block)]
        return pltpu.make_async_remote_copy(
            src_ref=dst if src is None else src, dst_ref=dst,
            send_sem=send_sems.at[7 * r + k], recv_sem=recv_sems.at[7 * r + k], device_id=to, device_id_type=MESH)

    mine = pltpu.make_async_copy(x_ref, out_ref.at[_slot(*me)], local_sems.at[r])
    first = [copy(0, me, sibling, src=x_ref)] + [copy(1 + n, me, (*chip, pc), src=x_ref)
                                                  for n, chip in enumerate(chips)]
    if starting:
        return mine, first
    passed = [copy(4 + n, (*chip, pc), sibling) for n, chip in enumerate(chips)]
    landed = [copy(1 + n, (*chip, pc), me) for n, chip in enumerate(chips)]
    from_sibling = [copy(0, sibling, me)] + [copy(4 + n, (*chip, 1 - pc), me) for n, chip in enumerate(chips)]
    return mine, first, passed, landed, from_sibling


def _scatter_copies(g_ref, out_ref, sems, r, starting=False):
    send_sems, recv_sems, local_sems = sems
    px, py, pc = _position()
    me = _slot(px, py, pc)
    mine = pltpu.make_async_copy(g_ref.at[me], out_ref.at[me], local_sems.at[r])
    peers = [(px ^ fx, py ^ fy, pc ^ fc) for fx, fy, fc in PEER_FLIPS]

    def copy(k, peer, src_slot, dst_slot):
        return pltpu.make_async_remote_copy(
            src_ref=g_ref.at[src_slot], dst_ref=out_ref.at[dst_slot],
            send_sem=send_sems.at[7 * r + k], recv_sem=recv_sems.at[7 * r + k], device_id=peer, device_id_type=MESH)

    sends = [copy(k, peer, _slot(*peer), me) for k, peer in enumerate(peers)]
    if starting:
        return mine, sends
    arrivals = [copy(k, peer, me, _slot(*peer)) for k, peer in enumerate(peers)]
    return mine, sends, arrivals


def _rider_start(kind, in_ref, out_ref, sems, r):
    if kind == "gather":
        mine, first = _gather_copies(in_ref, out_ref, sems, r, starting=True)
    else:
        mine, first = _scatter_copies(in_ref, out_ref, sems, r, starting=True)
    mine.start()
    for cp in first:
        cp.start()


def _rider_finish(kind, in_ref, out_ref, sems, r):
    if kind == "gather":
        mine, first, passed, landed, from_sibling = _gather_copies(in_ref, out_ref, sems, r)
        for cp, fwd in zip(landed, passed):
            cp.wait_recv()
            fwd.start()
        for cp in from_sibling:
            cp.wait_recv()
        for cp in first + passed:
            cp.wait_send()
    else:
        mine, sends, arrivals = _scatter_copies(in_ref, out_ref, sems, r)
        for cp in arrivals:
            cp.wait_recv()
        for cp in sends:
            cp.wait_send()
    mine.wait()


def _all_gather(x, in_vmem, name):
    space = pltpu.VMEM if in_vmem else pl.ANY

    def body(x_ref, out_ref, send_sems, recv_sems, local_sems):
        sems = (send_sems, recv_sems, local_sems)
        _rider_start("gather", x_ref, out_ref, sems, 0)
        _rider_finish("gather", x_ref, out_ref, sems, 0)

    return pl.pallas_call(
        body, name=name, out_shape=jax.ShapeDtypeStruct((N_DEV,) + x.shape, x.dtype),
        in_specs=[pl.BlockSpec(memory_space=space)], out_specs=pl.BlockSpec(memory_space=space),
        scratch_shapes=[pltpu.SemaphoreType.DMA((7,)), pltpu.SemaphoreType.DMA((7,)), pltpu.SemaphoreType.DMA((1,))],
        compiler_params=pltpu.CompilerParams(vmem_limit_bytes=VMEM_LIMIT_BYTES),
    )(x)


def _adam_sum(stage, w, m, v, layer, prev, name):
    n = stage.shape[0]
    L, R, C = w.shape
    tr = R if R * C <= 256 * 1024 else _row_tile(R, C)
    c1 = 1.0 / (1.0 - ADAM_B1 ** ADAM_STEP)
    c2 = 1.0 / (1.0 - ADAM_B2 ** ADAM_STEP)

    def body(*refs):
        st_ref, w_ref, m_ref, v_ref = refs[:4]
        g_ref, d_ref, nm_ref, nv_ref = refs[-4:]
        g = st_ref[0].astype(F32)
        for s in range(1, n):
            g = g + st_ref[s].astype(F32)
        wv = w_ref[0]
        mn = ADAM_B1 * m_ref[0] + (1.0 - ADAM_B1) * g
        vn = ADAM_B2 * v_ref[0] + (1.0 - ADAM_B2) * (g * g)
        g_ref[0] = g
        nm_ref[0] = mn
        nv_ref[0] = vn
        d_ref[0] = -ADAM_LR * ((mn * c1) / (jnp.sqrt(vn * c2) + ADAM_EPS) + ADAM_WD * wv)

    lay = pl.BlockSpec((1, tr, C), lambda i: (layer, i, 0))
    in_specs = [pl.BlockSpec((n, tr, C), lambda i: (0, i, 0)), lay, lay, lay]
    ins = [stage, w, m, v]
    aliases = {}
    if prev is not None:
        in_specs += [pl.BlockSpec(memory_space=pl.ANY)] * 4
        ins += list(prev)
        aliases = {4: 0, 5: 1, 6: 2, 7: 3}
    shp = jax.ShapeDtypeStruct((L, R, C), F32)
    return pl.pallas_call(
        body, name=name, out_shape=(shp, shp, shp, shp), grid=(R // tr,),
        in_specs=in_specs, out_specs=(lay, lay, lay, lay), input_output_aliases=aliases,
        compiler_params=_params("parallel"))(*ins)


def _row_tile(R, C):
    cpad = -(-C // LANES) * LANES
    want = max(16, (256 * 1024) // cpad)
    best = 16
    for t in range(16, R + 1, 16):
        if R % t == 0 and t <= want:
            best = t
    return best


def _sum_slabs(st, name):
    n, R, C = st.shape
    tr = R if n * R * C * 4 <= (12 << 20) else _row_tile(R, C)

    def body(st_ref, o_ref):
        g = st_ref[0]
        for s in range(1, n):
            g = g + st_ref[s]
        o_ref[...] = g

    return pl.pallas_call(
        body, name=name, out_shape=jax.ShapeDtypeStruct((R, C), F32), grid=(R // tr,),
        in_specs=[pl.BlockSpec((n, tr, C), lambda i: (0, i, 0))], out_specs=pl.BlockSpec((tr, C), lambda i: (i, 0)),
        compiler_params=_params("parallel"))(st)


def _pack(arrs):
    flat = [a.reshape(-1).astype(F32) for a in arrs]
    sizes = [f.shape[0] for f in flat]
    total = sum(sizes)
    padded = -(-total // (16 * LANES)) * (16 * LANES)
    if padded > total:
        flat.append(jnp.zeros((padded - total,), F32))
    return jnp.concatenate(flat).reshape(padded // LANES, LANES), (sizes, [a.shape for a in arrs])


def _unpack(packed, layout, lead=()):
    sizes, shapes = layout
    flat = packed.reshape(lead + (-1,))
    out, off = [], 0
    for sz, shp in zip(sizes, shapes):
        out.append(flat[..., off:off + sz].reshape(lead + tuple(shp)))
        off += sz
    return out


class _NoComm:
    col_slabs = None

    def __init__(self, wts):
        self.wts, self.grads = wts, {}

    def weights(self, l):
        return self.wts[l]

    def gather_rider(self, l, names):
        return None

    def scatter_rider(self, name, l, g):
        self.grads[(name, l)] = g
        return None


def _local_step(x, tgt, ada, mix_norm_g, comm, b_forget, conf_dw_w, conf_dw_b, conf_ln_g, conf_ln_b, sc_dw_w,
                ffn_norm_g, ffn_dw_w, ffn_dw_b, final_norm_g):
    S, D = x.shape
    L = ada.shape[0]

    def mm_gather(a, b, dtype, name, l_next, names):
        rider = comm.gather_rider(l_next, names) if l_next < L else None
        if rider is None:
            return _matmul(a, b, dtype, name=name)
        out, got = _matmul(a, b, dtype, name="cm_" + name, rider=rider)
        comm.gathered(l_next, names, got)
        return out

    def mm_scatter(a, b, dtype, name, wname, l, g):
        rider = comm.scatter_rider(wname, l, g)
        if rider is None:
            return _matmul(a, b, dtype, name=name, b_transposed=True)
        out, got = _matmul(a, b, dtype, name="cm_" + name, rider=rider, b_transposed=True)
        comm.scattered(wname, l, got[0])
        return out

    H = b_forget.shape[1]
    DA = H * HEAD_DIM
    C = conf_dw_b.shape[1]
    NQ = 3 * DA
    NR = 5 * C + LANES
    fblk = (5 * C) // LANES
    row = lambda a: a.reshape(1, -1)
    adav = ada.reshape(L, N_ADA, 1, D)

    saved = []
    xcur, delta, gate = x, None, None
    for l in range(L):
        sh_m, sc_m, g_m, sh_f, sc_f, g_f = [adav[l, n] for n in range(N_ADA)]
        w = comm.weights(l)
        x1, h1 = _site_fwd(xcur, delta, gate, row(mix_norm_g[l]), sc_m, sh_m, name="site_fwd_mix")
        qkv = _matmul(h1, w["w_in_perm"][:, :NQ], BF16, name="mm_qkv")
        rest = _matmul(h1, w["w_in_perm"][:, NQ:], F32, name="mm_rest")
        bpad = jnp.zeros((1, LANES), F32).at[0, :H].set(b_forget[l])
        Fc = _fgate_fwd(rest, bpad, fblk, name="fgate_fwd")
        nf = -LOG2E * jnp.transpose(Fc[:, :H])
        attn, attn32, lse = _attn_fwd(qkv, jnp.broadcast_to(nf[:, :, None], (H, S, LANES)), H, name="attn_fwd")
        conf = _conf_fwd(rest, conf_dw_w[l], row(conf_dw_b[l]), row(conf_ln_g[l]), row(conf_ln_b[l]), name="conf_fwd")
        sconv = _sconv_fwd(rest, sc_dw_w[l], name="sconv_fwd")
        mixcat = jnp.concatenate([attn, conf, sconv], axis=1)
        mixed = _matmul(mixcat, w["w_out"], F32, name="mm_out")
        x2, h2 = _site_fwd(x1, mixed, g_m, row(ffn_norm_g[l]), sc_f, sh_f, name="site_fwd_ffn")
        hu = mm_gather(h2, w["w_up"], BF16, "mm_up", l + 1, ("w_up", "w_down"))
        act = _ffn_fwd(hu, ffn_dw_w[l], row(ffn_dw_b[l]), name="ffn_fwd")
        ffn_out = mm_gather(act, w["w_down"], F32, "mm_down", l + 1, ("w_in", "w_out"))
        saved.append(dict(x1=x1, h1=h1, qkv=qkv, rest=rest, bpad=bpad, nf=nf, attn32=attn32, lse=lse, mixcat=mixcat,
                          mixed=mixed, x2=x2, h2=h2, hu=hu, act=act, ffn_out=ffn_out))
        xcur, delta, gate = x2, ffn_out, g_f

    loss_lanes, dx, d_delta, d_gate, d_gfin = _final_fwd_bwd(xcur, delta, gate, row(final_norm_g), tgt, name="final")
    loss = (0.5 / D) * jnp.sum(loss_lanes)

    grads = dict(final_norm_g=d_gfin[0], ada=[None] * L, mix_norm_g=[None] * L, ffn_norm_g=[None] * L,
                 b_forget=[None] * L, conf_dw_w=[None] * L, conf_dw_b=[None] * L, conf_ln_g=[None] * L,
                 conf_ln_b=[None] * L, sc_dw_w=[None] * L, ffn_dw_w=[None] * L, ffn_dw_b=[None] * L)
    K3 = ffn_dw_w.shape[1]
    for l in reversed(range(L)):
        sv, w = saved[l], comm.weights(l)
        sh_m, sc_m, g_m, sh_f, sc_f, g_f = [adav[l, n] for n in range(N_ADA)]
        d_gf = d_gate
        g_down = _matmul_tn(sv["act"], d_delta, BF16, name="mm_dw_down")
        dact = mm_scatter(d_delta, w["w_down"], BF16, "mm_dact", "w_down", l, g_down)
        dhu_g, dhu_v, dwg, dwv = _ffn_bwd(sv["hu"], dact, ffn_dw_w[l], row(ffn_dw_b[l]), name="ffn_bwd")
        grads["ffn_dw_w"][l] = jnp.concatenate([dwg[:K3], dwv[:K3]], axis=1)
        grads["ffn_dw_b"][l] = jnp.concatenate([dwg[K3], dwv[K3]])
        dhu = jnp.concatenate([dhu_g, dhu_v], axis=1)
        g_up = _matmul_tn(sv["h2"], dhu, BF16, name="mm_dw_up", col_slabs=comm.col_slabs)
        dh2 = mm_scatter(dhu, w["w_up"], F32, "mm_dh2", "w_up", l, g_up)
        dx, d_sh_f, d_a_f, d_mixed, d_gm = _site_bwd(sv["x2"], dh2, dx, row(ffn_norm_g[l]), sc_f,
                                                      sh_f, sv["mixed"], g_m, name="site_bwd_ffn")
        grads["ffn_norm_g"][l] = (d_a_f * (1.0 + sc_f))[0]
        d_sc_f = d_a_f * row(ffn_norm_g[l])
        g_out = _matmul_tn(sv["mixcat"], d_mixed, BF16, name="mm_dw_out")
        dattn = mm_scatter(d_mixed, w["w_out"][:DA], BF16, "mm_dattn", "w_out", l, g_out)
        dcs = _matmul(d_mixed, w["w_out"][DA:], F32, name="mm_dcs", b_transposed=True)
        delta_a = _blocked_rows(_attn_delta(sv["attn32"], dattn, H, name="attn_delta")[:, :, 0], min(ATTN_TILE, S))
        nfb = jnp.broadcast_to(sv["nf"][:, :, None], (H, S, LANES))
        dq, dk, dv, drow, dnf = _attn_bwd(sv["qkv"], nfb, dattn, sv["lse"], delta_a, H, name="attn_bwd")
        dF = jnp.zeros((S, LANES), F32).at[:, :H].set(jnp.transpose(drow.reshape(H, S) - dnf[:, :, 0]))
        dfl, dbf = _fgate_bwd(sv["rest"], sv["bpad"], dF, fblk, name="fgate_bwd")
        grads["b_forget"][l] = dbf[0, :H]
        dcvg, dcw, dcvec = _conf_bwd(sv["rest"], dcs, conf_dw_w[l], row(conf_dw_b[l]), row(conf_ln_g[l]),
                                     row(conf_ln_b[l]), name="conf_bwd")
        grads["conf_dw_w"][l] = dcw[:conf_dw_w.shape[1]]
        grads["conf_dw_b"][l], grads["conf_ln_g"][l], grads["conf_ln_b"][l] = dcvec[0], dcvec[1], dcvec[2]
        dsc3, dsw = _sconv_bwd(sv["rest"], dcs, sc_dw_w[l], name="sconv_bwd")
        grads["sc_dw_w"][l] = dsw[:sc_dw_w.shape[1]]
        dproj = jnp.concatenate([dq, dk, dv, dcvg, dsc3, dfl], axis=1)
        g_in = _matmul_tn(sv["h1"], dproj, BF16, name="mm_dw_in")
        dh1 = mm_scatter(dproj, w["w_in_perm"], F32, "mm_dh1", "w_in_perm", l, g_in)
        if l > 0:
            pv = saved[l - 1]
            g_f_prev = adav[l - 1, 5]
            dx, d_sh_m, d_a_m, d_delta, d_gate = _site_bwd(sv["x1"], dh1, dx, row(mix_norm_g[l]), sc_m, sh_m,
                                                           pv["ffn_out"], g_f_prev, name="site_bwd_mix")
        else:
            dx, d_sh_m, d_a_m = _site_bwd(sv["x1"], dh1, dx, row(mix_norm_g[l]), sc_m, sh_m, None, None,
                                          name="site_bwd_first")
        grads["mix_norm_g"][l] = (d_a_m * (1.0 + sc_m))[0]
        d_sc_m = d_a_m * row(mix_norm_g[l])
        grads["ada"][l] = jnp.concatenate([d_sh_m, d_sc_m, d_gm, d_sh_f, d_sc_f, d_gf], axis=1)[0]
    return loss, dx, grads


def kernel(x, c, ada_w, ada_b, mix_norm_g, w_in, b_forget, conf_dw_w, conf_dw_b, conf_ln_g, conf_ln_b, sc_dw_w, w_out, ffn_norm_g, w_up, ffn_dw_w, ffn_dw_b, w_down, final_norm_g, loss_target, m_ada_w, m_ada_b, m_mix_norm_g, m_w_in, m_b_forget, m_conf_dw_w, m_conf_dw_b, m_conf_ln_g, m_conf_ln_b, m_sc_dw_w, m_w_out, m_ffn_norm_g, m_w_up, m_ffn_dw_w, m_ffn_dw_b, m_w_down, m_final_norm_g, v_ada_w, v_ada_b, v_mix_norm_g, v_w_in, v_b_forget, v_conf_dw_w, v_conf_dw_b, v_conf_ln_g, v_conf_ln_b, v_sc_dw_w, v_w_out, v_ffn_norm_g, v_w_up, v_ffn_dw_w, v_ffn_dw_b, v_w_down, v_final_norm_g):
    L, D, ada_loc = ada_w.shape
    S = x.shape[1]
    H = b_forget.shape[1]
    DA = H * HEAD_DIM
    C = conf_dw_b.shape[1]
    in_loc = w_in.shape[2]
    IN = in_loc * N_DEV
    px, py, pc = _position()
    me = _slot(px, py, pc)

    pk, lay = _pack([c, conf_dw_w, sc_dw_w, ffn_dw_w])
    gathered = _all_gather(pk, True, name="ag_small_fwd")
    c_all, cw_all, sw_all, fw_all = _unpack(gathered, lay, lead=(N_DEV,))
    c_all = c_all[:, 0]
    unshard = lambda a: jnp.moveaxis(a, 0, 2).reshape(a.shape[1], a.shape[2], -1)
    conf_w_full, sc_w_full, ffn_w_full = unshard(cw_all), unshard(sw_all), unshard(fw_all)
    c_act = c_all * jax.nn.sigmoid(c_all)
    c_act16 = jnp.zeros((16, D), F32).at[:N_DEV].set(c_act).astype(BF16)
    ada_cols = jnp.stack([_matmul(c_act16, ada_w[l].astype(BF16), F32, name="mm_ada")[:N_DEV] for l in range(L)])
    ada_g = _all_gather(ada_cols.reshape(L * N_DEV, ada_loc), True, name="ag_ada")
    ada_mine = lax.dynamic_index_in_dim(ada_g.reshape(N_DEV, L, N_DEV, ada_loc), me, axis=2, keepdims=False)
    ada = jnp.moveaxis(ada_mine, 0, 1).reshape(L, N_DEV * ada_loc) + ada_b

    NQ = 3 * DA
    PR = NQ + 5 * C
    shards = dict(w_in=w_in.astype(BF16), w_out=w_out.astype(BF16), w_up=w_up.astype(BF16),
                  w_down=w_down.astype(BF16))

    def shard_cols(g):
        return jnp.moveaxis(g.reshape(g.shape[0], N_DEV, -1), 1, 0)

    def shard_rows(g):
        return g.reshape(N_DEV, -1, g.shape[1])

    class MeshComm:
        col_slabs = N_DEV

        def __init__(self):
            self.got = {0: {n: _all_gather(shards[n][0], False, name="ag_" + n) for n in shards}}
            self.full, self.stage = {}, {}

        def weights(self, l):
            if l not in self.full:
                g = self.got[l]
                wi = jnp.moveaxis(g["w_in"], 0, 1).reshape(D, IN)
                w_in_perm = jnp.concatenate([wi[:, :NQ], wi[:, NQ + H:], wi[:, NQ:NQ + H],
                                             jnp.zeros((D, LANES - H), BF16)], axis=1)
                self.full[l] = dict(w_in_perm=w_in_perm, w_out=g["w_out"].reshape(-1, D),
                                    w_up=jnp.moveaxis(g["w_up"], 0, 1).reshape(D, -1),
                                    w_down=g["w_down"].reshape(-1, D))
            return self.full[l]

        def gather_rider(self, l, names):
            return "gather", [shards[n][l] for n in names]

        def gathered(self, l, names, outs):
            self.got.setdefault(l, {}).update(zip(names, outs))

        def scatter_rider(self, name, l, g):
            if name == "w_in_perm":
                slabs = shard_cols(jnp.concatenate([g[:, :NQ], g[:, PR:PR + H], g[:, NQ:PR]], axis=1))
            elif name == "w_up":
                slabs = g
            else:
                slabs = shard_rows(g)
            return "scatter", [slabs]

        def scattered(self, name, l, out):
            self.stage[(name, l)] = out

    comm = MeshComm()
    loss_loc, dx, gr = _local_step(x[0], loss_target[0], ada, mix_norm_g, comm, b_forget, conf_w_full, conf_dw_b,
                                   conf_ln_g, conf_ln_b, sc_w_full, ffn_norm_g, ffn_w_full, ffn_dw_b, final_norm_g)
    loss = lax.psum(loss_loc, ("x", "y", "c"))

    small_names = ["ada", "mix_norm_g", "ffn_norm_g", "b_forget", "conf_dw_b", "conf_ln_g", "conf_ln_b",
                   "ffn_dw_b", "conf_dw_w", "sc_dw_w", "ffn_dw_w"]
    pk, lay = _pack([jnp.stack(gr[n]) for n in small_names] + [gr["final_norm_g"]])
    parts = _all_gather(pk, True, name="ag_small_bwd")
    tot = _unpack(_sum_slabs(parts, name="sum_small"), lay)
    g_small = dict(zip(small_names + ["final_norm_g"], tot))
    d_ada_all = _unpack(parts, lay, lead=(N_DEV,))[0]
    my_cols = lambda a, n: lax.dynamic_slice_in_dim(a, me * n, n, axis=a.ndim - 1)

    c_act_t = jnp.zeros((D, LANES), F32).at[:, :N_DEV].set(jnp.transpose(c_act)).astype(BF16)
    res = None
    for l in range(L):
        d_loc = jnp.zeros((LANES, ada_loc), F32).at[:N_DEV].set(my_cols(d_ada_all[:, l], ada_loc)).astype(BF16)
        g_l = _matmul(c_act_t, d_loc, F32, name="mm_dada")
        res = _adam_sum(g_l[None], ada_w, m_ada_w, v_ada_w, l, res, name="adam_ada_w")
    out_ada_w = res

    big = {}
    for nm, key, wq, mq, vq in (("w_down", "w_down", w_down, m_w_down, v_w_down), ("w_up", "w_up", w_up, m_w_up, v_w_up),
                                ("w_out", "w_out", w_out, m_w_out, v_w_out), ("w_in", "w_in_perm", w_in, m_w_in, v_w_in)):
        res = None
        for l in reversed(range(L)):
            res = _adam_sum(comm.stage[(key, l)], wq, mq, vq, l, res, name="adam_" + nm)
        big[nm] = res

    K31, K3 = conf_dw_w.shape[1], sc_dw_w.shape[1]
    sm = [("ada_b", ada_b, m_ada_b, v_ada_b, g_small["ada"]),
          ("mix_norm_g", mix_norm_g, m_mix_norm_g, v_mix_norm_g, g_small["mix_norm_g"]),
          ("b_forget", b_forget, m_b_forget, v_b_forget, g_small["b_forget"]),
          ("conf_dw_w", conf_dw_w, m_conf_dw_w, v_conf_dw_w, my_cols(g_small["conf_dw_w"], conf_dw_w.shape[2])),
          ("conf_dw_b", conf_dw_b, m_conf_dw_b, v_conf_dw_b, g_small["conf_dw_b"]),
          ("conf_ln_g", conf_ln_g, m_conf_ln_g, v_conf_ln_g, g_small["conf_ln_g"]),
          ("conf_ln_b", conf_ln_b, m_conf_ln_b, v_conf_ln_b, g_small["conf_ln_b"]),
          ("sc_dw_w", sc_dw_w, m_sc_dw_w, v_sc_dw_w, my_cols(g_small["sc_dw_w"], sc_dw_w.shape[2])),
          ("ffn_norm_g", ffn_norm_g, m_ffn_norm_g, v_ffn_norm_g, g_small["ffn_norm_g"]),
          ("ffn_dw_w", ffn_dw_w, m_ffn_dw_w, v_ffn_dw_w, my_cols(g_small["ffn_dw_w"], ffn_dw_w.shape[2])),
          ("ffn_dw_b", ffn_dw_b, m_ffn_dw_b, v_ffn_dw_b, g_small["ffn_dw_b"]),
          ("final_norm_g", final_norm_g, m_final_norm_g, v_final_norm_g, g_small["final_norm_g"])]
    pw, lay = _pack([t[1] for t in sm])
    pm, _ = _pack([t[2] for t in sm])
    pv, _ = _pack([t[3] for t in sm])
    pg, _ = _pack([t[4] for t in sm])
    sres = _adam_sum(pg[None], pw[None], pm[None], pv[None], 0, None, name="adam_small")
    s_g, s_d, s_m, s_v = [dict(zip([t[0] for t in sm], _unpack(r[0], lay))) for r in sres]

    def pick(idx, name):
        if name == "ada_w":
            return out_ada_w[idx]
        if name in big:
            return big[name][idx]
        return (s_g, s_d, s_m, s_v)[idx][name]

    order = ["ada_w", "ada_b", "mix_norm_g", "w_in", "b_forget", "conf_dw_w", "conf_dw_b", "conf_ln_g", "conf_ln_b",
             "sc_dw_w", "w_out", "ffn_norm_g", "w_up", "ffn_dw_w", "ffn_dw_b", "w_down", "final_norm_g"]
    outs = [loss, dx[None]]
    for idx in range(4):
        outs += [pick(idx, n) for n in order]
    return tuple(outs)
```

```python
import functools

import jax
import jax.numpy as jnp
from jax import lax
from jax.experimental import pallas as pl
from jax.experimental.pallas import tpu as pltpu

F32 = jnp.float32
BF16 = jnp.bfloat16
RMS_EPS = 1e-6
LN_EPS = 1e-5
HEAD_DIM = 128
N_ADA = 6
ADAM_LR = 0.001
ADAM_B1 = 0.9
ADAM_B2 = 0.999
ADAM_EPS = 1e-08
ADAM_WD = 0.01
ADAM_STEP = 10
N_DEV = 8
LANES = 128
VMEM_LIMIT_BYTES = 56 * 1024 * 1024
MM_TILE = 1024
MM_TILE_WIDE = 1536
MM_TILE_N_MAX = 2816
MXU_WIDTH = 256
MM_VMEM_BUDGET = 44 * 1024 * 1024
MESH = pl.DeviceIdType.MESH
PEER_FLIPS = ((0, 0, 1), (1, 0, 0), (0, 1, 0), (1, 1, 0), (1, 0, 1), (0, 1, 1), (1, 1, 1))


def _params(*sem):
    return pltpu.CompilerParams(dimension_semantics=sem, vmem_limit_bytes=VMEM_LIMIT_BYTES)


def _tile(n, cap):
    if n <= cap:
        return n
    for t in range(cap - cap % LANES, 0, -LANES):
        if n % t == 0:
            return t
    raise ValueError(f"no tile for {n}")


def _mm_tile(n):
    t = _tile(n, MM_TILE)
    return t if t == min(n, MM_TILE) else _tile(n, MM_TILE_WIDE)


def _n_tile(n, vmem_bytes):
    for step in (MXU_WIDTH, LANES):
        for t in range(min(n, MM_TILE_N_MAX) // step * step, 0, -step):
            if n % t == 0 and vmem_bytes(t) <= MM_VMEM_BUDGET:
                return t
    return n


def _sigmoid(v):
    return jax.nn.sigmoid(v)


def _matmul(a, b, out_dtype, name, rider=None, b_transposed=False):
    a_parts = a if isinstance(a, tuple) else (a,)
    na = len(a_parts)
    M = a_parts[0].shape[0]
    K = sum(p.shape[1] for p in a_parts)
    N = b.shape[0] if b_transposed else b.shape[1]
    tm = _mm_tile(M)
    out_bytes = jnp.dtype(out_dtype).itemsize
    part_k = a_parts[0].shape[1]

    def blocks_bytes(tk_, tn_):
        return 4 * (na * tm * tk_ + tk_ * tn_) + (4 * tm * tn_ if K > tk_ else 0) + 2 * tm * tn_ * out_bytes

    for tk in [t for t in range(part_k, 0, -LANES) if part_k % t == 0]:
        tn = _n_tile(N, functools.partial(blocks_bytes, tk))
        if blocks_bytes(tk, tn) <= MM_VMEM_BUDGET and (tn >= min(N, 512)):
            break
    nk = K // tk
    half = part_k // tk
    grid = (M // tm, N // tn, nk)
    dims = _NT if b_transposed else (((1,), (0,)), ((), ()))
    kind, arrs = rider if rider is not None else (None, [])
    nr = len(arrs)

    def body(*refs):
        a_refs, refs = refs[:na], refs[na - 1:]
        a_ref, b_ref = a_refs[0], refs[1]
        r_in = refs[2:2 + nr]
        o_ref = refs[2 + nr]
        r_out = refs[3 + nr:3 + 2 * nr]
        rest = refs[3 + 2 * nr:]
        i, j, k = pl.program_id(0), pl.program_id(1), pl.program_id(2)
        if nr:
            sems = rest[-3:]

            @pl.when((i == 0) & (j == 0) & (k == 0))
            def _():
                for r in range(nr):
                    _rider_start(kind, r_in[r], r_out[r], sems, r)

        if nk == 1:
            o_ref[...] = lax.dot_general(a_ref[...], b_ref[...], dims, preferred_element_type=F32).astype(o_ref.dtype)
        else:
            acc_ref = rest[0]

            @pl.when(k == 0)
            def _():
                acc_ref[...] = jnp.zeros_like(acc_ref)

            def accumulate(part_ref):
                acc_ref[...] += lax.dot_general(part_ref[...], b_ref[...], dims, preferred_element_type=F32)

            if na == 1:
                accumulate(a_ref)
            else:
                pl.when(k < half)(lambda: accumulate(a_refs[0]))
                pl.when(k >= half)(lambda: accumulate(a_refs[1]))

            @pl.when(k == nk - 1)
            def _():
                o_ref[...] = acc_ref[...].astype(o_ref.dtype)

        if nr:
            @pl.when((i == grid[0] - 1) & (j == grid[1] - 1) & (k == nk - 1))
            def _():
                for r in range(nr):
                    _rider_finish(kind, r_in[r], r_out[r], sems, r)

    scratch = [] if nk == 1 else [pltpu.VMEM((tm, tn), F32)]
    hbm = pl.BlockSpec(memory_space=pl.ANY)
    if na == 1:
        a_specs = [pl.BlockSpec((tm, tk), lambda i, j, k: (i, k))]
    else:
        a_specs = [pl.BlockSpec((tm, tk), lambda i, j, k: (i, jnp.minimum(k, half - 1))),
                   pl.BlockSpec((tm, tk), lambda i, j, k: (i, jnp.maximum(k - half, 0)))]
    out_shape = jax.ShapeDtypeStruct((M, N), out_dtype)
    out_specs = pl.BlockSpec((tm, tn), lambda i, j, k: (i, j))
    if nr:
        scratch += [pltpu.SemaphoreType.DMA((7 * nr,)), pltpu.SemaphoreType.DMA((7 * nr,)),
                    pltpu.SemaphoreType.DMA((nr,))]
        out_shape = (out_shape,) + tuple(
            jax.ShapeDtypeStruct(x.shape if kind == "scatter" else (N_DEV,) + x.shape, x.dtype) for x in arrs)
        out_specs = (out_specs,) + (hbm,) * nr
    out = pl.pallas_call(
        body, name=name,
        out_shape=out_shape,
        grid=grid,
        in_specs=a_specs + [pl.BlockSpec((tn, tk), lambda i, j, k: (j, k)) if b_transposed
                            else pl.BlockSpec((tk, tn), lambda i, j, k: (k, j))] + [hbm] * nr,
        out_specs=out_specs,
        scratch_shapes=scratch,
        compiler_params=_params(*(("arbitrary",) * 3 if nr else ("parallel", "parallel", "arbitrary"))),
    )(*a_parts, b, *arrs)
    return (out[0], list(out[1:])) if nr else out


_TN = (((0,), (0,)), ((), ()))


def _matmul_tn(a, b, out_dtype, name, col_slabs=None):
    b_parts = b if isinstance(b, tuple) else (b,)
    S, M = a.shape
    N = sum(p.shape[1] for p in b_parts)
    tm = _mm_tile(M)
    out_bytes = jnp.dtype(out_dtype).itemsize

    def blocks_bytes(ts_, tn_):
        return 4 * (ts_ * tm + len(b_parts) * ts_ * tn_) + 4 * tm * tn_ + 2 * tm * tn_ * out_bytes

    for ts in (_tile(S, 2 * MM_TILE), _tile(S, MM_TILE)):
        tn = N // col_slabs if col_slabs else _n_tile(b_parts[0].shape[1], functools.partial(blocks_bytes, ts))
        if blocks_bytes(ts, tn) <= MM_VMEM_BUDGET and tn >= min(b_parts[0].shape[1], MM_TILE):
            break
    ns = S // ts
    half = b_parts[0].shape[1] // tn

    def body(*refs):
        a_ref, b_refs, (o_ref, acc_ref) = refs[0], refs[1:-2], refs[-2:]
        j, k = pl.program_id(1), pl.program_id(2)

        @pl.when(k == 0)
        def _():
            acc_ref[...] = jnp.zeros_like(acc_ref)

        def accumulate(b_ref):
            acc_ref[...] += lax.dot_general(a_ref[...], b_ref[...], _TN, preferred_element_type=F32)

        if len(b_refs) == 1:
            accumulate(b_refs[0])
        else:
            pl.when(j < half)(lambda: accumulate(b_refs[0]))
            pl.when(j >= half)(lambda: accumulate(b_refs[1]))

        @pl.when(k == ns - 1)
        def _():
            if col_slabs is None:
                o_ref[...] = acc_ref[...].astype(o_ref.dtype)
            else:
                o_ref[0] = acc_ref[...].astype(o_ref.dtype)

    if col_slabs is None:
        out_shape = jax.ShapeDtypeStruct((M, N), out_dtype)
        out_spec = pl.BlockSpec((tm, tn), lambda i, j, k: (i, j))
    else:
        out_shape = jax.ShapeDtypeStruct((col_slabs, M, tn), out_dtype)
        out_spec = pl.BlockSpec((1, tm, tn), lambda i, j, k: (j, i, 0))
    if len(b_parts) == 1:
        b_specs = [pl.BlockSpec((ts, tn), lambda i, j, k: (k, j))]
    else:
        b_specs = [pl.BlockSpec((ts, tn), lambda i, j, k: (jnp.where(j < half, k, ns - 1), jnp.minimum(j, half - 1))),
                   pl.BlockSpec((ts, tn), lambda i, j, k: (jnp.where(j < half, 0, k), jnp.maximum(j - half, 0)))]
    return pl.pallas_call(
        body, name=name,
        out_shape=out_shape,
        grid=(M // tm, N // tn, ns),
        in_specs=[pl.BlockSpec((ts, tm), lambda i, j, k: (k, i))] + b_specs,
        out_specs=out_spec,
        scratch_shapes=[pltpu.VMEM((tm, tn), F32)],
        compiler_params=_params("parallel", "parallel", "arbitrary"),
    )(a, *b_parts)


def _site_fwd(x, delta, gate, g, sc, sh, name):
    S, D = x.shape
    T = min(256, S)
    res = delta is not None

    def body(*refs):
        if res:
            x_ref, d_ref, gate_ref, g_ref, sc_ref, sh_ref, xo_ref, h_ref = refs
            xv = x_ref[...] + gate_ref[...] * d_ref[...]
            xo_ref[...] = xv
        else:
            x_ref, g_ref, sc_ref, sh_ref, h_ref = refs
            xv = x_ref[...]
        r = lax.rsqrt(jnp.mean(xv * xv, axis=-1, keepdims=True) + RMS_EPS)
        a = g_ref[...] * (1.0 + sc_ref[...])
        h_ref[...] = (xv * r * a + sh_ref[...]).astype(BF16)

    row = pl.BlockSpec((T, D), lambda i: (i, 0))
    vec = pl.BlockSpec((1, D), lambda i: (0, 0))
    if res:
        ins, in_specs = (x, delta, gate, g, sc, sh), [row, row, vec, vec, vec, vec]
        out_shape = (jax.ShapeDtypeStruct((S, D), F32), jax.ShapeDtypeStruct((S, D), BF16))
        out_specs = (row, row)
    else:
        ins, in_specs = (x, g, sc, sh), [row, vec, vec, vec]
        out_shape = jax.ShapeDtypeStruct((S, D), BF16)
        out_specs = row
    out = pl.pallas_call(body, name=name, out_shape=out_shape, grid=(S // T,), in_specs=in_specs,
                         out_specs=out_specs, compiler_params=_params("parallel"))(*ins)
    return out if res else (x, out)


def _site_bwd(x, dh, dres, g, sc, sh, delta, gate, name):
    S, D = x.shape
    T = min(256, S)
    res = delta is not None

    def body(*refs):
        if res:
            (x_ref, dh_ref, dres_ref, g_ref, sc_ref, delta_ref, gate_ref,
             dx_ref, dsh_ref, da_ref, dd_ref, dgate_ref) = refs
        else:
            x_ref, dh_ref, dres_ref, g_ref, sc_ref, dx_ref, dsh_ref, da_ref = refs
        i = pl.program_id(0)
        xv = x_ref[...]
        dhv = dh_ref[...]
        r = lax.rsqrt(jnp.mean(xv * xv, axis=-1, keepdims=True) + RMS_EPS)
        xh = xv * r
        dxh = dhv * (g_ref[...] * (1.0 + sc_ref[...]))
        dx = r * (dxh - xh * jnp.mean(dxh * xh, axis=-1, keepdims=True)) + dres_ref[...]
        dx_ref[...] = dx

        @pl.when(i == 0)
        def _():
            dsh_ref[...] = jnp.zeros_like(dsh_ref)
            da_ref[...] = jnp.zeros_like(da_ref)
            if res:
                dgate_ref[...] = jnp.zeros_like(dgate_ref)

        dsh_ref[...] += jnp.sum(dhv, axis=0, keepdims=True)
        da_ref[...] += jnp.sum(dhv * xh, axis=0, keepdims=True)
        if res:
            dd_ref[...] = (gate_ref[...] * dx).astype(BF16)
            dgate_ref[...] += jnp.sum(dx * delta_ref[...], axis=0, keepdims=True)

    row = pl.BlockSpec((T, D), lambda i: (i, 0))
    vec = pl.BlockSpec((1, D), lambda i: (0, 0))
    vshape = jax.ShapeDtypeStruct((1, D), F32)
    if res:
        ins, in_specs = (x, dh, dres, g, sc, delta, gate), [row, row, row, vec, vec, row, vec]
        out_shape = (jax.ShapeDtypeStruct((S, D), F32), vshape, vshape, jax.ShapeDtypeStruct((S, D), BF16), vshape)
        out_specs = (row, vec, vec, row, vec)
    else:
        ins, in_specs = (x, dh, dres, g, sc), [row, row, row, vec, vec]
        out_shape = (jax.ShapeDtypeStruct((S, D), F32), vshape, vshape)
        out_specs = (row, vec, vec)
    return pl.pallas_call(body, name=name, out_shape=out_shape, grid=(S // T,), in_specs=in_specs,
                          out_specs=out_specs, compiler_params=_params("arbitrary"))(*ins)


def _final_fwd_bwd(x, delta, gate, gfin, target, name):
    S, D = x.shape
    T = min(256, S)

    def body(x_ref, delta_ref, gate_ref, g_ref, t_ref, loss_ref, dx_ref, dd_ref, dgate_ref, dg_ref):
        i = pl.program_id(0)
        dl = delta_ref[...]
        xv = x_ref[...] + gate_ref[...] * dl
        r = lax.rsqrt(jnp.mean(xv * xv, axis=-1, keepdims=True) + RMS_EPS)
        xh = xv * r
        gv = g_ref[...]
        e = xh * gv - t_ref[...]
        dy = e * (1.0 / D)
        dxh = dy * gv
        dx = r * (dxh - xh * jnp.mean(dxh * xh, axis=-1, keepdims=True))
        dx_ref[...] = dx
        dd_ref[...] = (gate_ref[...] * dx).astype(BF16)

        @pl.when(i == 0)
        def _():
            loss_ref[...] = jnp.zeros_like(loss_ref)
            dgate_ref[...] = jnp.zeros_like(dgate_ref)
            dg_ref[...] = jnp.zeros_like(dg_ref)

        loss_ref[...] += jnp.sum(e * e, axis=0, keepdims=True)
        dgate_ref[...] += jnp.sum(dx * dl, axis=0, keepdims=True)
        dg_ref[...] += jnp.sum(dy * xh, axis=0, keepdims=True)

    row = pl.BlockSpec((T, D), lambda i: (i, 0))
    vec = pl.BlockSpec((1, D), lambda i: (0, 0))
    vshape = jax.ShapeDtypeStruct((1, D), F32)
    return pl.pallas_call(
        body, name=name,
        out_shape=(vshape, jax.ShapeDtypeStruct((S, D), F32), jax.ShapeDtypeStruct((S, D), BF16), vshape, vshape),
        grid=(S // T,), in_specs=[row, row, vec, vec, row], out_specs=(vec, row, row, vec, vec),
        compiler_params=_params("arbitrary"))(x, delta, gate, gfin, target)


def _split3(v):
    hi = v.astype(BF16)
    r1 = v - hi.astype(F32)
    mid = r1.astype(BF16)
    lo = (r1 - mid.astype(F32)).astype(BF16)
    return hi, mid, lo


def _tri_dot(tri, v):
    hi, mid, lo = _split3(v)
    d = functools.partial(jnp.dot, preferred_element_type=F32)
    return d(tri, hi) + d(tri, mid) + d(tri, lo)


def _fgate_fwd(rest, bpad, fblk, name):
    S = rest.shape[0]
    CH = min(256, S)
    nch = S // CH

    def body(f_ref, b_ref, o_ref):
        row = lax.broadcasted_iota(jnp.int32, (CH, CH), 0)
        col = lax.broadcasted_iota(jnp.int32, (CH, CH), 1)
        tri = (row >= col).astype(BF16)

        def step(ci, carry):
            rows = pl.ds(pl.multiple_of(ci * CH, CH), CH)
            z = f_ref[rows, :] + b_ref[...]
            lf = jnp.minimum(z, 0.0) - jnp.log(1.0 + jnp.exp(-jnp.abs(z)))
            o_ref[rows, :] = _tri_dot(tri, lf) + carry
            return carry + jnp.sum(lf, axis=0, keepdims=True)

        lax.fori_loop(0, nch, step, jnp.zeros((1, LANES), F32))

    return pl.pallas_call(
        body, name=name, out_shape=jax.ShapeDtypeStruct((S, LANES), F32), grid=(1,),
        in_specs=[pl.BlockSpec((S, LANES), lambda i: (0, fblk)), pl.BlockSpec((1, LANES), lambda i: (0, 0))],
        out_specs=pl.BlockSpec((S, LANES), lambda i: (0, 0)),
        compiler_params=_params("arbitrary"))(rest, bpad)


def _fgate_bwd(rest, bpad, dF, fblk, name):
    S = rest.shape[0]
    CH = min(256, S)
    nch = S // CH

    def body(f_ref, b_ref, df_ref, o_ref, db_ref):
        row = lax.broadcasted_iota(jnp.int32, (CH, CH), 0)
        col = lax.broadcasted_iota(jnp.int32, (CH, CH), 1)
        tri = (col >= row).astype(BF16)

        def step(n, carry):
            sfx_carry, db = carry
            ci = nch - 1 - n
            rows = pl.ds(pl.multiple_of(ci * CH, CH), CH)
            z = f_ref[rows, :] + b_ref[...]
            dfv = df_ref[rows, :]
            dz = (_tri_dot(tri, dfv) + sfx_carry) * _sigmoid(-z)
            o_ref[rows, :] = dz.astype(BF16)
            return sfx_carry + jnp.sum(dfv, axis=0, keepdims=True), db + jnp.sum(dz, axis=0, keepdims=True)

        zero = jnp.zeros((1, LANES), F32)
        _, db = lax.fori_loop(0, nch, step, (zero, zero))
        db_ref[...] = db

    blk = pl.BlockSpec((S, LANES), lambda i: (0, 0))
    return pl.pallas_call(
        body, name=name,
        out_shape=(jax.ShapeDtypeStruct((S, LANES), BF16), jax.ShapeDtypeStruct((1, LANES), F32)), grid=(1,),
        in_specs=[pl.BlockSpec((S, LANES), lambda i: (0, fblk)), pl.BlockSpec((1, LANES), lambda i: (0, 0)), blk],
        out_specs=(blk, pl.BlockSpec((1, LANES), lambda i: (0, 0))),
        compiler_params=_params("arbitrary"))(rest, bpad, dF)


_NT = (((1,), (1,)), ((), ()))
LOG2E = 1.4426950408889634
ATTN_TILE = 512


def _blocked_rows(a, TA):
    H, S = a.shape
    return a.reshape(H, S // TA, 1, TA)


def _attn_fwd(qkv, nfb, H, name):
    S = qkv.shape[0]
    TA = min(ATTN_TILE, S)
    nb = S // TA
    c = HEAD_DIM ** -0.5 * LOG2E

    def body(q_ref, k_ref, v_ref, nf_ref, o_ref, o32_ref, lse_ref, m_ref, l_ref, acc_ref):
        i = pl.program_id(1)
        m_ref[...] = jnp.full_like(m_ref, -jnp.inf)
        l_ref[...] = jnp.zeros_like(l_ref)
        acc_ref[...] = jnp.zeros_like(acc_ref)

        def block(j, masked):
            rows = pl.ds(pl.multiple_of(j * TA, TA), TA)
            st = (lax.dot_general(k_ref[rows, :], q_ref[...], _NT, preferred_element_type=F32) * c
                  + jnp.tile(nf_ref[0, rows, :], (1, TA // LANES)))
            if masked:
                key = lax.broadcasted_iota(jnp.int32, (TA, TA), 0)
                qry = lax.broadcasted_iota(jnp.int32, (TA, TA), 1)
                st = jnp.where(key <= qry, st, -jnp.inf)
            m_old = m_ref[...]
            m_new = jnp.maximum(m_old, jnp.max(st, axis=0, keepdims=True))
            alpha = jnp.exp2(m_old - m_new)
            pt = jnp.exp2(st - m_new)
            l_ref[...] = alpha * l_ref[...] + jnp.sum(pt, axis=0, keepdims=True)
            acc_ref[...] = alpha * acc_ref[...] + lax.dot_general(v_ref[rows, :], pt.astype(BF16), _TN,
                                                                  preferred_element_type=F32)
            m_ref[...] = m_new

        def loop(j, carry):
            block(j, False)
            return carry

        lax.fori_loop(0, i, loop, 0)
        block(i, True)
        o = jnp.transpose(acc_ref[...] / l_ref[...])
        o32_ref[...] = o
        o_ref[...] = o.astype(BF16)
        lse_ref[0, 0] = m_ref[...] + jnp.log(l_ref[...]) * LOG2E

    qblk = pl.BlockSpec((TA, HEAD_DIM), lambda h, i: (i, h))
    return pl.pallas_call(
        body, name=name, grid=(H, nb),
        in_specs=[qblk,
                  pl.BlockSpec((S, HEAD_DIM), lambda h, i: (0, H + h)),
                  pl.BlockSpec((S, HEAD_DIM), lambda h, i: (0, 2 * H + h)),
                  pl.BlockSpec((1, S, LANES), lambda h, i: (h, 0, 0))],
        out_specs=(qblk, qblk, pl.BlockSpec((1, 1, 1, TA), lambda h, i: (h, i, 0, 0))),
        scratch_shapes=[pltpu.VMEM((1, TA), F32), pltpu.VMEM((1, TA), F32), pltpu.VMEM((HEAD_DIM, TA), F32)],
        out_shape=(jax.ShapeDtypeStruct((S, H * HEAD_DIM), BF16), jax.ShapeDtypeStruct((S, H * HEAD_DIM), F32),
                   jax.ShapeDtypeStruct((H, nb, 1, TA), F32)),
        compiler_params=_params("parallel", "parallel"))(qkv, qkv, qkv, nfb)


def _attn_delta(o, do, H, name):
    S = o.shape[0]
    T = min(512, S)

    def body(o_ref, do_ref, d_ref):
        d_ref[0] = jnp.sum(o_ref[...].astype(F32) * do_ref[...].astype(F32), axis=-1, keepdims=True)

    blk = pl.BlockSpec((T, HEAD_DIM), lambda h, i: (i, h))
    return pl.pallas_call(
        body, name=name, out_shape=jax.ShapeDtypeStruct((H, S, 1), F32), grid=(H, S // T),
        in_specs=[blk, blk], out_specs=pl.BlockSpec((1, T, 1), lambda h, i: (h, i, 0)),
        compiler_params=_params("parallel", "parallel"))(o, do)


def _ds_tile(k, q, v, do, nfb, lse_row, delta_row, c, masked):
    TK, TQ = k.shape[0], q.shape[0]
    st = lax.dot_general(k, q, _NT, preferred_element_type=F32) * c + jnp.tile(nfb, (1, TQ // LANES))
    pt = jnp.exp2(st - lse_row)
    if masked:
        key = lax.broadcasted_iota(jnp.int32, (TK, TQ), 0)
        qry = lax.broadcasted_iota(jnp.int32, (TK, TQ), 1)
        pt = jnp.where(key <= qry, pt, 0.0)
    dpt = lax.dot_general(v, do, _NT, preferred_element_type=F32)
    return pt, pt * (dpt - delta_row)


def _attn_bwd(qkv, nfb, do, lse, delta, H, name):
    S = qkv.shape[0]
    TA = min(ATTN_TILE, S)
    nb = S // TA
    scale = HEAD_DIM ** -0.5
    c = scale * LOG2E

    def body(q_ref, k_ref, v_ref, nf_ref, do_ref, lse_ref, dl_ref, dq_ref, dk_ref, dv_ref, drow_ref, dnf_ref,
             dq_acc, dk_acc, dv_acc, dnf_acc):
        j = pl.program_id(1)

        @pl.when(j == 0)
        def _():
            dq_acc[...] = jnp.zeros_like(dq_acc)
            drow_ref[...] = jnp.zeros_like(drow_ref)

        dk_acc[...] = jnp.zeros_like(dk_acc)
        dv_acc[...] = jnp.zeros_like(dv_acc)
        dnf_acc[...] = jnp.zeros_like(dnf_acc)
        kb = k_ref[...]

        def block(i, masked):
            rows = pl.ds(pl.multiple_of(i * TA, TA), TA)
            qb = q_ref[rows, :]
            dob = do_ref[rows, :]
            pt, dst = _ds_tile(kb, qb, v_ref[...], dob, nf_ref[0], lse_ref[0, i], dl_ref[0, i], c, masked)
            dsb = dst.astype(BF16)
            dv_acc[...] += jnp.dot(pt.astype(BF16), dob, preferred_element_type=F32)
            dk_acc[...] += jnp.dot(dsb, qb, preferred_element_type=F32)
            dq_acc[rows, :] += lax.dot_general(dsb, kb, _TN, preferred_element_type=F32)
            drow_ref[0, i] += jnp.sum(dst, axis=0, keepdims=True)
            part = dst[:, 0:LANES]
            for t in range(1, TA // LANES):
                part = part + dst[:, t * LANES:(t + 1) * LANES]
            dnf_acc[...] += part

        def loop(i, carry):
            block(i, False)
            return carry

        block(j, True)
        lax.fori_loop(j + 1, nb, loop, 0)
        dk_ref[...] = (dk_acc[...] * scale).astype(BF16)
        dv_ref[...] = dv_acc[...].astype(BF16)
        dnf_ref[0] = jnp.sum(dnf_acc[...], axis=-1, keepdims=True)

        @pl.when(j == nb - 1)
        def _():
            dq_ref[...] = (dq_acc[...] * scale).astype(BF16)

    full = pl.BlockSpec((S, HEAD_DIM), lambda h, j: (0, h))
    row_stat = pl.BlockSpec((1, nb, 1, TA), lambda h, j: (h, 0, 0, 0))
    kblk = lambda c0: pl.BlockSpec((TA, HEAD_DIM), lambda h, j: (j, c0 + h))
    shp = jax.ShapeDtypeStruct((S, H * HEAD_DIM), BF16)
    return pl.pallas_call(
        body, name=name, grid=(H, nb),
        in_specs=[full, kblk(H), kblk(2 * H), pl.BlockSpec((1, TA, LANES), lambda h, j: (h, j, 0)), full,
                  row_stat, row_stat],
        out_specs=(full, kblk(0), kblk(0), row_stat, pl.BlockSpec((1, TA, 1), lambda h, j: (h, j, 0))),
        scratch_shapes=[pltpu.VMEM((S, HEAD_DIM), F32), pltpu.VMEM((TA, HEAD_DIM), F32),
                        pltpu.VMEM((TA, HEAD_DIM), F32), pltpu.VMEM((TA, LANES), F32)],
        out_shape=(shp, shp, shp, jax.ShapeDtypeStruct((H, nb, 1, TA), F32), jax.ShapeDtypeStruct((H, S, 1), F32)),
        compiler_params=_params("parallel", "arbitrary"))(qkv, qkv, qkv, nfb, do, lse, delta)


def _taps(ext_ref, w_ref, K, base, r0, rows, cols, reverse=False, init=None):
    acc = init
    for k in range(K):
        wk = w_ref[(K - 1 - k) if reverse else k:((K - 1 - k) if reverse else k) + 1, cols]
        term = wk * ext_ref[base + k + r0:base + k + r0 + rows, cols]
        acc = term if acc is None else acc + term
    return acc


def _prev_blk(T, H):
    return lambda i: jnp.maximum(i * (T // H) - 1, 0)


def _next_blk(T, H, S):
    return lambda i: jnp.minimum((i + 1) * (T // H), S // H - 1)


def _conf_fwd(rest, w, b, lng, lnb, name):
    S = rest.shape[0]
    K, C = w.shape
    H, T = 32, min(256, S)
    RS = min(64, T)
    base = H - (K - 1)

    def body(cv_ref, cg_ref, cvp_ref, cgp_ref, w_ref, b_ref, g_ref, bb_ref, o_ref, ext_ref):
        i = pl.program_id(0)
        ext_ref[0:H, :] = jnp.where(i > 0, cvp_ref[...] * _sigmoid(cgp_ref[...]), 0.0)
        ext_ref[H:H + T, :] = cv_ref[...] * _sigmoid(cg_ref[...])
        for r0 in range(0, T, RS):
            cc = _taps(ext_ref, w_ref, K, base, r0, RS, slice(None), init=jnp.broadcast_to(b_ref[...], (RS, C)))
            xc = cc - jnp.mean(cc, axis=-1, keepdims=True)
            y = xc * lax.rsqrt(jnp.mean(xc * xc, axis=-1, keepdims=True) + LN_EPS) * g_ref[...] + bb_ref[...]
            o_ref[r0:r0 + RS, :] = (y * _sigmoid(y)).astype(BF16)

    pb = _prev_blk(T, H)
    cur = lambda cb: pl.BlockSpec((T, C), lambda i: (i, cb))
    prev = lambda cb: pl.BlockSpec((H, C), lambda i: (pb(i), cb))
    full = lambda a: pl.BlockSpec(a.shape, lambda i: (0, 0))
    return pl.pallas_call(
        body, name=name, out_shape=jax.ShapeDtypeStruct((S, C), BF16), grid=(S // T,),
        in_specs=[cur(0), cur(1), prev(0), prev(1), full(w), full(b), full(lng), full(lnb)],
        out_specs=pl.BlockSpec((T, C), lambda i: (i, 0)),
        scratch_shapes=[pltpu.VMEM((H + T, C), F32)],
        compiler_params=_params("parallel"))(rest, rest, rest, rest, w, b, lng, lnb)


def _conf_bwd(rest, dcs, w, b, lng, lnb, name):
    S = rest.shape[0]
    K, C = w.shape
    H, T = 32, min(256, S)
    RS = 32
    nI = S // T
    base = H - (K - 1)

    def body(cv_ref, cg_ref, cvp_ref, cgp_ref, cvn_ref, cgn_ref, do_ref, don_ref, w_ref, b_ref, g_ref, bb_ref,
             dcvg_ref, dw_ref, dvec_ref, ext_ref, dcc_ref):
        i = pl.program_id(0)
        ext_ref[0:H, :] = jnp.where(i > 0, cvp_ref[...] * _sigmoid(cgp_ref[...]), 0.0)
        ext_ref[H:H + T, :] = cv_ref[...] * _sigmoid(cg_ref[...])
        ext_ref[H + T:H + T + H, :] = cvn_ref[...] * _sigmoid(cgn_ref[...])

        @pl.when(i == 0)
        def _():
            dw_ref[...] = jnp.zeros_like(dw_ref)
            dvec_ref[...] = jnp.zeros_like(dvec_ref)

        db = jnp.zeros((1, C), F32)
        dg = jnp.zeros((1, C), F32)
        dbb = jnp.zeros((1, C), F32)
        for r0 in range(0, T + H, RS):
            cc = _taps(ext_ref, w_ref, K, base, r0, RS, slice(None), init=jnp.broadcast_to(b_ref[...], (RS, C)))
            xc = cc - jnp.mean(cc, axis=-1, keepdims=True)
            r = lax.rsqrt(jnp.mean(xc * xc, axis=-1, keepdims=True) + LN_EPS)
            xh = xc * r
            y = xh * g_ref[...] + bb_ref[...]
            sy = _sigmoid(y)
            if r0 < T:
                d_o = do_ref[r0:r0 + RS, :]
            else:
                d_o = jnp.where(i < nI - 1, don_ref[r0 - T:r0 - T + RS, :], 0.0)
            dy = d_o * (sy * (1.0 + y * (1.0 - sy)))
            dxh = dy * g_ref[...]
            dcc = r * (dxh - jnp.mean(dxh, axis=-1, keepdims=True)
                       - xh * jnp.mean(dxh * xh, axis=-1, keepdims=True))
            dcc_ref[r0:r0 + RS, :] = dcc
            if r0 < T:
                dbb = dbb + jnp.sum(dy, axis=0, keepdims=True)
                dg = dg + jnp.sum(dy * xh, axis=0, keepdims=True)
                db = db + jnp.sum(dcc, axis=0, keepdims=True)
        dvec_ref[0:1, :] += db
        dvec_ref[1:2, :] += dg
        dvec_ref[2:3, :] += dbb
        R2 = min(64, T)
        for k in range(K):
            s = jnp.zeros((1, C), F32)
            for r0 in range(0, T, R2):
                s = s + jnp.sum(dcc_ref[r0:r0 + R2, :] * ext_ref[base + k + r0:base + k + r0 + R2, :],
                                axis=0, keepdims=True)
            dw_ref[k:k + 1, :] += s
        for r0 in range(0, T, R2):
            dci = _taps(dcc_ref, w_ref, K, 0, r0, R2, slice(None), reverse=True)
            cvv = cv_ref[r0:r0 + R2, :]
            sg = _sigmoid(cg_ref[r0:r0 + R2, :])
            dcvg_ref[r0:r0 + R2, 0:C] = (dci * sg).astype(BF16)
            dcvg_ref[r0:r0 + R2, C:2 * C] = (dci * cvv * sg * (1.0 - sg)).astype(BF16)

    pb, nb_ = _prev_blk(T, H), _next_blk(T, H, S)
    cur = lambda cb: pl.BlockSpec((T, C), lambda i: (i, cb))
    prev = lambda cb: pl.BlockSpec((H, C), lambda i: (pb(i), cb))
    nxt = lambda cb: pl.BlockSpec((H, C), lambda i: (nb_(i), cb))
    full = lambda a: pl.BlockSpec(a.shape, lambda i: (0, 0))
    return pl.pallas_call(
        body, name=name,
        out_shape=(jax.ShapeDtypeStruct((S, 2 * C), BF16), jax.ShapeDtypeStruct((32, C), F32),
                   jax.ShapeDtypeStruct((8, C), F32)),
        grid=(nI,),
        in_specs=[cur(0), cur(1), prev(0), prev(1), nxt(0), nxt(1), cur(0), nxt(0),
                  full(w), full(b), full(lng), full(lnb)],
        out_specs=(pl.BlockSpec((T, 2 * C), lambda i: (i, 0)), pl.BlockSpec((32, C), lambda i: (0, 0)),
                   pl.BlockSpec((8, C), lambda i: (0, 0))),
        scratch_shapes=[pltpu.VMEM((H + T + H, C), F32), pltpu.VMEM((T + H, C), F32)],
        compiler_params=_params("arbitrary"))(rest, rest, rest, rest, rest, rest, dcs, dcs, w, b, lng, lnb)


def _sconv_fwd(rest, w, name):
    S = rest.shape[0]
    K, C = w.shape
    H, T = 8, min(256, S)
    RS = min(64, T)
    base = H - (K - 1)

    def body(sx_ref, sb_ref, sc_ref, sxp_ref, scp_ref, w_ref, o_ref, ext_ref):
        i = pl.program_id(0)
        ext_ref[0:H, :] = jnp.where(i > 0, sxp_ref[...] * scp_ref[...], 0.0)
        ext_ref[H:H + T, :] = sx_ref[...] * sc_ref[...]
        for r0 in range(0, T, RS):
            cz = _taps(ext_ref, w_ref, K, base, r0, RS, slice(None))
            o_ref[r0:r0 + RS, :] = (sb_ref[r0:r0 + RS, :] * cz).astype(BF16)

    pb = _prev_blk(T, H)
    cur = lambda cb: pl.BlockSpec((T, C), lambda i: (i, cb))
    prev = lambda cb: pl.BlockSpec((H, C), lambda i: (pb(i), cb))
    return pl.pallas_call(
        body, name=name, out_shape=jax.ShapeDtypeStruct((S, C), BF16), grid=(S // T,),
        in_specs=[cur(2), cur(3), cur(4), prev(2), prev(4), pl.BlockSpec(w.shape, lambda i: (0, 0))],
        out_specs=pl.BlockSpec((T, C), lambda i: (i, 0)),
        scratch_shapes=[pltpu.VMEM((H + T, C), F32)],
        compiler_params=_params("parallel"))(rest, rest, rest, rest, rest, w)


def _sconv_bwd(rest, dcs, w, name):
    S = rest.shape[0]
    K, C = w.shape
    H, T = 8, min(256, S)
    RS = min(64, T)
    nI = S // T
    base = H - (K - 1)

    def body(sx_ref, sb_ref, sc_ref, sxp_ref, scp_ref, sbn_ref, do_ref, don_ref, w_ref,
             dout_ref, dw_ref, ext_ref, dcv_ref):
        i = pl.program_id(0)
        ext_ref[0:H, :] = jnp.where(i > 0, sxp_ref[...] * scp_ref[...], 0.0)
        ext_ref[H:H + T, :] = sx_ref[...] * sc_ref[...]
        dcv_ref[0:T, :] = do_ref[...] * sb_ref[...]
        dcv_ref[T:T + H, :] = jnp.where(i < nI - 1, don_ref[...] * sbn_ref[...], 0.0)

        @pl.when(i == 0)
        def _():
            dw_ref[...] = jnp.zeros_like(dw_ref)

        for k in range(K):
            s = jnp.zeros((1, C), F32)
            for r0 in range(0, T, RS):
                s = s + jnp.sum(dcv_ref[r0:r0 + RS, :] * ext_ref[base + k + r0:base + k + r0 + RS, :],
                                axis=0, keepdims=True)
            dw_ref[k:k + 1, :] += s
        for r0 in range(0, T, RS):
            cz = _taps(ext_ref, w_ref, K, base, r0, RS, slice(None))
            dz = _taps(dcv_ref, w_ref, K, 0, r0, RS, slice(None), reverse=True)
            dout_ref[r0:r0 + RS, 0:C] = (dz * sc_ref[r0:r0 + RS, :]).astype(BF16)
            dout_ref[r0:r0 + RS, C:2 * C] = (do_ref[r0:r0 + RS, :] * cz).astype(BF16)
            dout_ref[r0:r0 + RS, 2 * C:3 * C] = (dz * sx_ref[r0:r0 + RS, :]).astype(BF16)

    pb, nb_ = _prev_blk(T, H), _next_blk(T, H, S)
    cur = lambda cb: pl.BlockSpec((T, C), lambda i: (i, cb))
    prev = lambda cb: pl.BlockSpec((H, C), lambda i: (pb(i), cb))
    nxt = lambda cb: pl.BlockSpec((H, C), lambda i: (nb_(i), cb))
    return pl.pallas_call(
        body, name=name,
        out_shape=(jax.ShapeDtypeStruct((S, 3 * C), BF16), jax.ShapeDtypeStruct((8, C), F32)),
        grid=(nI,),
        in_specs=[cur(2), cur(3), cur(4), prev(2), prev(4), nxt(3), cur(1), nxt(1),
                  pl.BlockSpec(w.shape, lambda i: (0, 0))],
        out_specs=(pl.BlockSpec((T, 3 * C), lambda i: (i, 0)), pl.BlockSpec((8, C), lambda i: (0, 0))),
        scratch_shapes=[pltpu.VMEM((H + T, C), F32), pltpu.VMEM((T + H, C), F32)],
        compiler_params=_params("arbitrary"))(rest, rest, rest, rest, rest, rest, dcs, dcs, w)


FFN_ROWS = 256
FFN_HALO = 16


def _shift_mats(T):
    r = lax.broadcasted_iota(jnp.int32, (T, T), 0)
    c = lax.broadcasted_iota(jnp.int32, (T, T), 1)
    down = [(r == c + k).astype(BF16) for k in (1, 2)]
    up = [(c == r + k).astype(BF16) for k in (1, 2)]
    return down, up


def _edge_rows(strip, shift, first):
    sub = lax.broadcasted_iota(jnp.int32, strip.shape, 0)
    if first:
        return jnp.where(sub < shift, pltpu.roll(strip, shift, 0), 0.0)
    return jnp.where(sub >= 8 - shift, pltpu.roll(strip, 8 - shift, 0), 0.0)


def _conv3_tile(x_ref, prev_ref, w_ref, b_ref, down, has_prev, u_ref, xm_refs=None):
    T = x_ref.shape[0]
    x = x_ref[...]
    xm1 = jnp.dot(down[0], x, preferred_element_type=F32)
    xm2 = jnp.dot(down[1], x, preferred_element_type=F32)
    u_ref[...] = b_ref[...] + w_ref[0:1, :] * xm2 + w_ref[1:2, :] * xm1 + w_ref[2:3, :] * x.astype(F32)
    tail = jnp.where(has_prev, prev_ref[...].astype(F32)[FFN_HALO - 8:, :], 0.0)
    p1, p2 = _edge_rows(tail, 1, True), _edge_rows(tail, 2, True)
    u_ref[0:8, :] += w_ref[0:1, :] * p2 + w_ref[1:2, :] * p1
    if xm_refs is not None:
        xm_refs[0][...] = xm1
        xm_refs[1][...] = xm2
        xm_refs[0][0:8, :] += p1
        xm_refs[1][0:8, :] += p2


def _ffn_fwd(hu, w, b, name):
    S, F2 = hu.shape
    Fd = F2 // 2
    K = w.shape[0]
    assert K == 3
    T = min(FFN_ROWS, S)
    tc = _tile(Fd, 512)
    nJ = Fd // tc

    def body(g_ref, v_ref, gp_ref, vp_ref, wg_ref, wv_ref, bg_ref, bv_ref, o_ref, ug_ref, uv_ref):
        i = pl.program_id(1)
        down, _ = _shift_mats(T)
        _conv3_tile(g_ref, gp_ref, wg_ref, bg_ref, down, i > 0, ug_ref)
        _conv3_tile(v_ref, vp_ref, wv_ref, bv_ref, down, i > 0, uv_ref)
        ug = ug_ref[...]
        o_ref[...] = (ug * _sigmoid(ug) * uv_ref[...]).astype(BF16)

    pb = _prev_blk(T, FFN_HALO)
    cur = lambda off: pl.BlockSpec((T, tc), lambda j, i: (i, j + off))
    prev = lambda off: pl.BlockSpec((FFN_HALO, tc), lambda j, i: (pb(i), j + off))
    wsp = lambda off: pl.BlockSpec((K, tc), lambda j, i: (0, j + off))
    bsp = lambda off: pl.BlockSpec((1, tc), lambda j, i: (0, j + off))
    return pl.pallas_call(
        body, name=name, out_shape=jax.ShapeDtypeStruct((S, Fd), BF16), grid=(nJ, S // T),
        in_specs=[cur(0), cur(nJ), prev(0), prev(nJ), wsp(0), wsp(nJ), bsp(0), bsp(nJ)],
        out_specs=pl.BlockSpec((T, tc), lambda j, i: (i, j)),
        scratch_shapes=[pltpu.VMEM((T, tc), F32), pltpu.VMEM((T, tc), F32)],
        compiler_params=_params("parallel", "parallel"))(hu, hu, hu, hu, w, w, b, b)


def _ffn_bwd(hu, dact, w, b, name):
    S, F2 = hu.shape
    Fd = F2 // 2
    K = w.shape[0]
    assert K == 3
    T = min(FFN_ROWS, S)
    tc = _tile(Fd, 512)
    nJ = Fd // tc
    nI = S // T

    def body(g_ref, v_ref, gp_ref, vp_ref, gn_ref, vn_ref, da_ref, dan_ref, wg_ref, wv_ref, bg_ref, bv_ref,
             dg_ref, dv_ref, dwg_ref, dwv_ref, ug_ref, uv_ref, g1_ref, g2_ref, v1_ref, v2_ref, dh_ref):
        i = pl.program_id(1)
        down, up = _shift_mats(T)
        _conv3_tile(g_ref, gp_ref, wg_ref, bg_ref, down, i > 0, ug_ref, (g1_ref, g2_ref))
        _conv3_tile(v_ref, vp_ref, wv_ref, bv_ref, down, i > 0, uv_ref, (v1_ref, v2_ref))

        @pl.when(i == 0)
        def _():
            dwg_ref[...] = jnp.zeros_like(dwg_ref)
            dwv_ref[...] = jnp.zeros_like(dwv_ref)

        def d_u(ug, uv, d_a):
            sg = _sigmoid(ug)
            return d_a * uv * (sg * (1.0 + ug * (1.0 - sg))), d_a * (ug * sg)

        dug, duv = d_u(ug_ref[...], uv_ref[...], da_ref[...].astype(F32))

        def next_rows(x_ref, xn_ref, w_ref, b_ref):
            strip = jnp.concatenate([x_ref[T - FFN_HALO:, :].astype(F32)[FFN_HALO - 8:, :],
                                     xn_ref[...].astype(F32)[0:8, :]], axis=0)
            return (b_ref[...] + w_ref[0:1, :] * strip[6:14, :] + w_ref[1:2, :] * strip[7:15, :]
                    + w_ref[2:3, :] * strip[8:16, :])

        d_an = jnp.where(i < nI - 1, dan_ref[...].astype(F32)[0:8, :], 0.0)
        dug_n, duv_n = d_u(next_rows(g_ref, gn_ref, wg_ref, bg_ref), next_rows(v_ref, vn_ref, wv_ref, bv_ref), d_an)

        for du, du_n, x_ref, x1_ref, x2_ref, w_ref, dw_ref, out_ref in (
                (dug, dug_n, g_ref, g1_ref, g2_ref, wg_ref, dwg_ref, dg_ref),
                (duv, duv_n, v_ref, v1_ref, v2_ref, wv_ref, dwv_ref, dv_ref)):
            dw_ref[0:1, :] += jnp.sum(du * x2_ref[...], axis=0, keepdims=True)
            dw_ref[1:2, :] += jnp.sum(du * x1_ref[...], axis=0, keepdims=True)
            dw_ref[2:3, :] += jnp.sum(du * x_ref[...].astype(F32), axis=0, keepdims=True)
            dw_ref[3:4, :] += jnp.sum(du, axis=0, keepdims=True)
            dub = du.astype(BF16)
            dh_ref[...] = (w_ref[2:3, :] * du + w_ref[1:2, :] * jnp.dot(up[0], dub, preferred_element_type=F32)
                           + w_ref[0:1, :] * jnp.dot(up[1], dub, preferred_element_type=F32))
            nxt = du_n.astype(BF16).astype(F32)
            dh_ref[T - 8:, :] += w_ref[1:2, :] * _edge_rows(nxt, 1, False) + w_ref[0:1, :] * _edge_rows(nxt, 2, False)
            out_ref[...] = dh_ref[...].astype(BF16)

    pb, nbb = _prev_blk(T, FFN_HALO), _next_blk(T, FFN_HALO, S)
    cur = lambda off: pl.BlockSpec((T, tc), lambda j, i: (i, j + off))
    prev = lambda off: pl.BlockSpec((FFN_HALO, tc), lambda j, i: (pb(i), j + off))
    nxt = lambda off: pl.BlockSpec((FFN_HALO, tc), lambda j, i: (nbb(i), j + off))
    wsp = lambda off: pl.BlockSpec((K, tc), lambda j, i: (0, j + off))
    bsp = lambda off: pl.BlockSpec((1, tc), lambda j, i: (0, j + off))
    half = jax.ShapeDtypeStruct((S, Fd), BF16)
    dws = jax.ShapeDtypeStruct((8, Fd), F32)
    tile = pltpu.VMEM((T, tc), F32)
    return pl.pallas_call(
        body, name=name, out_shape=(half, half, dws, dws), grid=(nJ, nI),
        in_specs=[cur(0), cur(nJ), prev(0), prev(nJ), nxt(0), nxt(nJ), cur(0), nxt(0),
                  wsp(0), wsp(nJ), bsp(0), bsp(nJ)],
        out_specs=(pl.BlockSpec((T, tc), lambda j, i: (i, j)), pl.BlockSpec((T, tc), lambda j, i: (i, j)),
                   pl.BlockSpec((8, tc), lambda j, i: (0, j)), pl.BlockSpec((8, tc), lambda j, i: (0, j))),
        scratch_shapes=[tile] * 7,
        compiler_params=_params("parallel", "arbitrary"))(hu, hu, hu, hu, hu, hu, dact, dact, w, w, b, b)


def _position():
    return lax.axis_index("x"), lax.axis_index("y"), lax.axis_index("c")


def _slot(px, py, pc):
    return 4 * px + 2 * py + pc


def _gather_copies(x_ref, out_ref, sems, r, starting=False):
    send_sems, recv_sems, local_sems = sems
    px, py, pc = _position()
    me, sibling = (px, py, pc), (px, py, 1 - pc)
    chips = [(1 - px, py), (px, 1 - py), (1 - px, 1 - py)]

    def copy(k, block, to, src=None):
        dst = out_ref.at[_slot(*block)]
        return pltpu.make_async_remote_copy(
            src_ref=dst if src is None else src, dst_ref=dst,
            send_sem=send_sems.at[7 * r + k], recv_sem=recv_sems.at[7 * r + k], device_id=to, device_id_type=MESH)

    mine = pltpu.make_async_copy(x_ref, out_ref.at[_slot(*me)], local_sems.at[r])
    first = [copy(0, me, sibling, src=x_ref)] + [copy(1 + n, me, (*chip, pc), src=x_ref)
                                                  for n, chip in enumerate(chips)]
    if starting:
        return mine, first
    passed = [copy(4 + n, (*chip, pc), sibling) for n, chip in enumerate(chips)]
    landed = [copy(1 + n, (*chip, pc), me) for n, chip in enumerate(chips)]
    from_sibling = [copy(0, sibling, me)] + [copy(4 + n, (*chip, 1 - pc), me) for n, chip in enumerate(chips)]
    return mine, first, passed, landed, from_sibling


def _scatter_copies(g_ref, out_ref, sems, r, starting=False):
    send_sems, recv_sems, local_sems = sems
    px, py, pc = _position()
    me = _slot(px, py, pc)
    mine = pltpu.make_async_copy(g_ref.at[me], out_ref.at[me], local_sems.at[r])
    peers = [(px ^ fx, py ^ fy, pc ^ fc) for fx, fy, fc in PEER_FLIPS]

    def copy(k, peer, src_slot, dst_slot):
        return pltpu.make_async_remote_copy(
            src_ref=g_ref.at[src_slot], dst_ref=out_ref.at[dst_slot],
            send_sem=send_sems.at[7 * r + k], recv_sem=recv_sems.at[7 * r + k], device_id=peer, device_id_type=MESH)

    sends = [copy(k, peer, _slot(*peer), me) for k, peer in enumerate(peers)]
    if starting:
        return mine, sends
    arrivals = [copy(k, peer, me, _slot(*peer)) for k, peer in enumerate(peers)]
    return mine, sends, arrivals


def _rider_start(kind, in_ref, out_ref, sems, r):
    if kind == "gather":
        mine, first = _gather_copies(in_ref, out_ref, sems, r, starting=True)
    else:
        mine, first = _scatter_copies(in_ref, out_ref, sems, r, starting=True)
    mine.start()
    for cp in first:
        cp.start()


def _rider_finish(kind, in_ref, out_ref, sems, r):
    if kind == "gather":
        mine, first, passed, landed, from_sibling = _gather_copies(in_ref, out_ref, sems, r)
        for cp, fwd in zip(landed, passed):
            cp.wait_recv()
            fwd.start()
        for cp in from_sibling:
            cp.wait_recv()
        for cp in first + passed:
            cp.wait_send()
    else:
        mine, sends, arrivals = _scatter_copies(in_ref, out_ref, sems, r)
        for cp in arrivals:
            cp.wait_recv()
        for cp in sends:
            cp.wait_send()
    mine.wait()


def _all_gather(x, in_vmem, name):
    space = pltpu.VMEM if in_vmem else pl.ANY

    def body(x_ref, out_ref, send_sems, recv_sems, local_sems):
        sems = (send_sems, recv_sems, local_sems)
        _rider_start("gather", x_ref, out_ref, sems, 0)
        _rider_finish("gather", x_ref, out_ref, sems, 0)

    return pl.pallas_call(
        body, name=name, out_shape=jax.ShapeDtypeStruct((N_DEV,) + x.shape, x.dtype),
        in_specs=[pl.BlockSpec(memory_space=space)], out_specs=pl.BlockSpec(memory_space=space),
        scratch_shapes=[pltpu.SemaphoreType.DMA((7,)), pltpu.SemaphoreType.DMA((7,)), pltpu.SemaphoreType.DMA((1,))],
        compiler_params=pltpu.CompilerParams(vmem_limit_bytes=VMEM_LIMIT_BYTES),
    )(x)


def _adam_sum(stage, w, m, v, layer, prev, name):
    n = stage.shape[0]
    L, R, C = w.shape
    tr = R if R * C <= 256 * 1024 else _row_tile(R, C)
    c1 = 1.0 / (1.0 - ADAM_B1 ** ADAM_STEP)
    c2 = 1.0 / (1.0 - ADAM_B2 ** ADAM_STEP)

    def body(*refs):
        st_ref, w_ref, m_ref, v_ref = refs[:4]
        g_ref, d_ref, nm_ref, nv_ref = refs[-4:]
        g = st_ref[0].astype(F32)
        for s in range(1, n):
            g = g + st_ref[s].astype(F32)
        wv = w_ref[0]
        mn = ADAM_B1 * m_ref[0] + (1.0 - ADAM_B1) * g
        vn = ADAM_B2 * v_ref[0] + (1.0 - ADAM_B2) * (g * g)
        g_ref[0] = g
        nm_ref[0] = mn
        nv_ref[0] = vn
        d_ref[0] = -ADAM_LR * ((mn * c1) / (jnp.sqrt(vn * c2) + ADAM_EPS) + ADAM_WD * wv)

    lay = pl.BlockSpec((1, tr, C), lambda i: (layer, i, 0))
    in_specs = [pl.BlockSpec((n, tr, C), lambda i: (0, i, 0)), lay, lay, lay]
    ins = [stage, w, m, v]
    aliases = {}
    if prev is not None:
        in_specs += [pl.BlockSpec(memory_space=pl.ANY)] * 4
        ins += list(prev)
        aliases = {4: 0, 5: 1, 6: 2, 7: 3}
    shp = jax.ShapeDtypeStruct((L, R, C), F32)
    return pl.pallas_call(
        body, name=name, out_shape=(shp, shp, shp, shp), grid=(R // tr,),
        in_specs=in_specs, out_specs=(lay, lay, lay, lay), input_output_aliases=aliases,
        compiler_params=_params("parallel"))(*ins)


def _row_tile(R, C):
    cpad = -(-C // LANES) * LANES
    want = max(16, (256 * 1024) // cpad)
    best = 16
    for t in range(16, R + 1, 16):
        if R % t == 0 and t <= want:
            best = t
    return best


def _sum_slabs(st, name):
    n, R, C = st.shape
    tr = R if n * R * C * 4 <= (12 << 20) else _row_tile(R, C)

    def body(st_ref, o_ref):
        g = st_ref[0]
        for s in range(1, n):
            g = g + st_ref[s]
        o_ref[...] = g

    return pl.pallas_call(
        body, name=name, out_shape=jax.ShapeDtypeStruct((R, C), F32), grid=(R // tr,),
        in_specs=[pl.BlockSpec((n, tr, C), lambda i: (0, i, 0))], out_specs=pl.BlockSpec((tr, C), lambda i: (i, 0)),
        compiler_params=_params("parallel"))(st)


def _pack(arrs):
    flat = [a.reshape(-1).astype(F32) for a in arrs]
    sizes = [f.shape[0] for f in flat]
    total = sum(sizes)
    padded = -(-total // (16 * LANES)) * (16 * LANES)
    if padded > total:
        flat.append(jnp.zeros((padded - total,), F32))
    return jnp.concatenate(flat).reshape(padded // LANES, LANES), (sizes, [a.shape for a in arrs])


def _unpack(packed, layout, lead=()):
    sizes, shapes = layout
    flat = packed.reshape(lead + (-1,))
    out, off = [], 0
    for sz, shp in zip(sizes, shapes):
        out.append(flat[..., off:off + sz].reshape(lead + tuple(shp)))
        off += sz
    return out


class _NoComm:
    col_slabs = None

    def __init__(self, wts):
        self.wts, self.grads = wts, {}

    def weights(self, l):
        return self.wts[l]

    def gather_rider(self, l, names):
        return None

    def scatter_rider(self, name, l, g):
        self.grads[(name, l)] = g
        return None


def _local_step(x, tgt, ada, mix_norm_g, comm, b_forget, conf_dw_w, conf_dw_b, conf_ln_g, conf_ln_b, sc_dw_w,
                ffn_norm_g, ffn_dw_w, ffn_dw_b, final_norm_g):
    S, D = x.shape
    L = ada.shape[0]

    def mm_gather(a, b, dtype, name, l_next, names):
        rider = comm.gather_rider(l_next, names) if l_next < L else None
        if rider is None:
            return _matmul(a, b, dtype, name=name)
        out, got = _matmul(a, b, dtype, name="cm_" + name, rider=rider)
        comm.gathered(l_next, names, got)
        return out

    def mm_scatter(a, b, dtype, name, wname, l, g):
        rider = comm.scatter_rider(wname, l, g)
        if rider is None:
            return _matmul(a, b, dtype, name=name, b_transposed=True)
        out, got = _matmul(a, b, dtype, name="cm_" + name, rider=rider, b_transposed=True)
        comm.scattered(wname, l, got[0])
        return out

    H = b_forget.shape[1]
    DA = H * HEAD_DIM
    C = conf_dw_b.shape[1]
    NQ = 3 * DA
    NR = 5 * C + LANES
    fblk = (5 * C) // LANES
    row = lambda a: a.reshape(1, -1)
    adav = ada.reshape(L, N_ADA, 1, D)

    saved = []
    xcur, delta, gate = x, None, None
    for l in range(L):
        sh_m, sc_m, g_m, sh_f, sc_f, g_f = [adav[l, n] for n in range(N_ADA)]
        w = comm.weights(l)
        x1, h1 = _site_fwd(xcur, delta, gate, row(mix_norm_g[l]), sc_m, sh_m, name="site_fwd_mix")
        qkv = _matmul(h1, w["w_in_perm"][:, :NQ], BF16, name="mm_qkv")
        rest = _matmul(h1, w["w_in_perm"][:, NQ:], F32, name="mm_rest")
        bpad = jnp.zeros((1, LANES), F32).at[0, :H].set(b_forget[l])
        Fc = _fgate_fwd(rest, bpad, fblk, name="fgate_fwd")
        nf = -LOG2E * jnp.transpose(Fc[:, :H])
        attn, attn32, lse = _attn_fwd(qkv, jnp.broadcast_to(nf[:, :, None], (H, S, LANES)), H, name="attn_fwd")
        conf = _conf_fwd(rest, conf_dw_w[l], row(conf_dw_b[l]), row(conf_ln_g[l]), row(conf_ln_b[l]), name="conf_fwd")
        sconv = _sconv_fwd(rest, sc_dw_w[l], name="sconv_fwd")
        mixcat = jnp.concatenate([attn, conf, sconv], axis=1)
        mixed = _matmul(mixcat, w["w_out"], F32, name="mm_out")
        x2, h2 = _site_fwd(x1, mixed, g_m, row(ffn_norm_g[l]), sc_f, sh_f, name="site_fwd_ffn")
        hu = mm_gather(h2, w["w_up"], BF16, "mm_up", l + 1, ("w_up", "w_down"))
        act = _ffn_fwd(hu, ffn_dw_w[l], row(ffn_dw_b[l]), name="ffn_fwd")
        ffn_out = mm_gather(act, w["w_down"], F32, "mm_down", l + 1, ("w_in", "w_out"))
        saved.append(dict(x1=x1, h1=h1, qkv=qkv, rest=rest, bpad=bpad, nf=nf, attn32=attn32, lse=lse, mixcat=mixcat,
                          mixed=mixed, x2=x2, h2=h2, hu=hu, act=act, ffn_out=ffn_out))
        xcur, delta, gate = x2, ffn_out, g_f

    loss_lanes, dx, d_delta, d_gate, d_gfin = _final_fwd_bwd(xcur, delta, gate, row(final_norm_g), tgt, name="final")
    loss = (0.5 / D) * jnp.sum(loss_lanes)

    grads = dict(final_norm_g=d_gfin[0], ada=[None] * L, mix_norm_g=[None] * L, ffn_norm_g=[None] * L,
                 b_forget=[None] * L, conf_dw_w=[None] * L, conf_dw_b=[None] * L, conf_ln_g=[None] * L,
                 conf_ln_b=[None] * L, sc_dw_w=[None] * L, ffn_dw_w=[None] * L, ffn_dw_b=[None] * L)
    K3 = ffn_dw_w.shape[1]
    for l in reversed(range(L)):
        sv, w = saved[l], comm.weights(l)
        sh_m, sc_m, g_m, sh_f, sc_f, g_f = [adav[l, n] for n in range(N_ADA)]
        d_gf = d_gate
        g_down = _matmul_tn(sv["act"], d_delta, BF16, name="mm_dw_down")
        dact = mm_scatter(d_delta, w["w_down"], BF16, "mm_dact", "w_down", l, g_down)
        dhu_g, dhu_v, dwg, dwv = _ffn_bwd(sv["hu"], dact, ffn_dw_w[l], row(ffn_dw_b[l]), name="ffn_bwd")
        grads["ffn_dw_w"][l] = jnp.concatenate([dwg[:K3], dwv[:K3]], axis=1)
        grads["ffn_dw_b"][l] = jnp.concatenate([dwg[K3], dwv[K3]])
        dhu = (dhu_g, dhu_v)
        g_up = _matmul_tn(sv["h2"], dhu, BF16, name="mm_dw_up", col_slabs=comm.col_slabs)
        dh2 = mm_scatter(dhu, w["w_up"], F32, "mm_dh2", "w_up", l, g_up)
        dx, d_sh_f, d_a_f, d_mixed, d_gm = _site_bwd(sv["x2"], dh2, dx, row(ffn_norm_g[l]), sc_f,
                                                      sh_f, sv["mixed"], g_m, name="site_bwd_ffn")
        grads["ffn_norm_g"][l] = (d_a_f * (1.0 + sc_f))[0]
        d_sc_f = d_a_f * row(ffn_norm_g[l])
        g_out = _matmul_tn(sv["mixcat"], d_mixed, BF16, name="mm_dw_out")
        dattn = mm_scatter(d_mixed, w["w_out"][:DA], BF16, "mm_dattn", "w_out", l, g_out)
        dcs = _matmul(d_mixed, w["w_out"][DA:], F32, name="mm_dcs", b_transposed=True)
        delta_a = _blocked_rows(_attn_delta(sv["attn32"], dattn, H, name="attn_delta")[:, :, 0], min(ATTN_TILE, S))
        nfb = jnp.broadcast_to(sv["nf"][:, :, None], (H, S, LANES))
        dq, dk, dv, drow, dnf = _attn_bwd(sv["qkv"], nfb, dattn, sv["lse"], delta_a, H, name="attn_bwd")
        dF = jnp.zeros((S, LANES), F32).at[:, :H].set(jnp.transpose(drow.reshape(H, S) - dnf[:, :, 0]))
        dfl, dbf = _fgate_bwd(sv["rest"], sv["bpad"], dF, fblk, name="fgate_bwd")
        grads["b_forget"][l] = dbf[0, :H]
        dcvg, dcw, dcvec = _conf_bwd(sv["rest"], dcs, conf_dw_w[l], row(conf_dw_b[l]), row(conf_ln_g[l]),
                                     row(conf_ln_b[l]), name="conf_bwd")
        grads["conf_dw_w"][l] = dcw[:conf_dw_w.shape[1]]
        grads["conf_dw_b"][l], grads["conf_ln_g"][l], grads["conf_ln_b"][l] = dcvec[0], dcvec[1], dcvec[2]
        dsc3, dsw = _sconv_bwd(sv["rest"], dcs, sc_dw_w[l], name="sconv_bwd")
        grads["sc_dw_w"][l] = dsw[:sc_dw_w.shape[1]]
        dproj = jnp.concatenate([dq, dk, dv, dcvg, dsc3, dfl], axis=1)
        g_in = _matmul_tn(sv["h1"], dproj, BF16, name="mm_dw_in")
        dh1 = mm_scatter(dproj, w["w_in_perm"], F32, "mm_dh1", "w_in_perm", l, g_in)
        if l > 0:
            pv = saved[l - 1]
            g_f_prev = adav[l - 1, 5]
            dx, d_sh_m, d_a_m, d_delta, d_gate = _site_bwd(sv["x1"], dh1, dx, row(mix_norm_g[l]), sc_m, sh_m,
                                                           pv["ffn_out"], g_f_prev, name="site_bwd_mix")
        else:
            dx, d_sh_m, d_a_m = _site_bwd(sv["x1"], dh1, dx, row(mix_norm_g[l]), sc_m, sh_m, None, None,
                                          name="site_bwd_first")
        grads["mix_norm_g"][l] = (d_a_m * (1.0 + sc_m))[0]
        d_sc_m = d_a_m * row(mix_norm_g[l])
        grads["ada"][l] = jnp.concatenate([d_sh_m, d_sc_m, d_gm, d_sh_f, d_sc_f, d_gf], axis=1)[0]
    return loss, dx, grads


def kernel(x, c, ada_w, ada_b, mix_norm_g, w_in, b_forget, conf_dw_w, conf_dw_b, conf_ln_g, conf_ln_b, sc_dw_w, w_out, ffn_norm_g, w_up, ffn_dw_w, ffn_dw_b, w_down, final_norm_g, loss_target, m_ada_w, m_ada_b, m_mix_norm_g, m_w_in, m_b_forget, m_conf_dw_w, m_conf_dw_b, m_conf_ln_g, m_conf_ln_b, m_sc_dw_w, m_w_out, m_ffn_norm_g, m_w_up, m_ffn_dw_w, m_ffn_dw_b, m_w_down, m_final_norm_g, v_ada_w, v_ada_b, v_mix_norm_g, v_w_in, v_b_forget, v_conf_dw_w, v_conf_dw_b, v_conf_ln_g, v_conf_ln_b, v_sc_dw_w, v_w_out, v_ffn_norm_g, v_w_up, v_ffn_dw_w, v_ffn_dw_b, v_w_down, v_final_norm_g):
    L, D, ada_loc = ada_w.shape
    S = x.shape[1]
    H = b_forget.shape[1]
    DA = H * HEAD_DIM
    C = conf_dw_b.shape[1]
    in_loc = w_in.shape[2]
    IN = in_loc * N_DEV
    px, py, pc = _position()
    me = _slot(px, py, pc)

    pk, lay = _pack([c, conf_dw_w, sc_dw_w, ffn_dw_w])
    gathered = _all_gather(pk, True, name="ag_small_fwd")
    c_all, cw_all, sw_all, fw_all = _unpack(gathered, lay, lead=(N_DEV,))
    c_all = c_all[:, 0]
    unshard = lambda a: jnp.moveaxis(a, 0, 2).reshape(a.shape[1], a.shape[2], -1)
    conf_w_full, sc_w_full, ffn_w_full = unshard(cw_all), unshard(sw_all), unshard(fw_all)
    c_act = c_all * jax.nn.sigmoid(c_all)
    c_act16 = jnp.zeros((16, D), F32).at[:N_DEV].set(c_act).astype(BF16)
    ada_cols = jnp.stack([_matmul(c_act16, ada_w[l].astype(BF16), F32, name="mm_ada")[:N_DEV] for l in range(L)])
    ada_g = _all_gather(ada_cols.reshape(L * N_DEV, ada_loc), True, name="ag_ada")
    ada_mine = lax.dynamic_index_in_dim(ada_g.reshape(N_DEV, L, N_DEV, ada_loc), me, axis=2, keepdims=False)
    ada = jnp.moveaxis(ada_mine, 0, 1).reshape(L, N_DEV * ada_loc) + ada_b

    NQ = 3 * DA
    PR = NQ + 5 * C
    shards = dict(w_in=w_in.astype(BF16), w_out=w_out.astype(BF16), w_up=w_up.astype(BF16),
                  w_down=w_down.astype(BF16))

    def shard_cols(g):
        return jnp.moveaxis(g.reshape(g.shape[0], N_DEV, -1), 1, 0)

    def shard_rows(g):
        return g.reshape(N_DEV, -1, g.shape[1])

    class MeshComm:
        col_slabs = N_DEV

        def __init__(self):
            self.got = {0: {n: _all_gather(shards[n][0], False, name="ag_" + n) for n in shards}}
            self.full, self.stage = {}, {}

        def weights(self, l):
            if l not in self.full:
                g = self.got[l]
                wi = jnp.moveaxis(g["w_in"], 0, 1).reshape(D, IN)
                w_in_perm = jnp.concatenate([wi[:, :NQ], wi[:, NQ + H:], wi[:, NQ:NQ + H],
                                             jnp.zeros((D, LANES - H), BF16)], axis=1)
                self.full[l] = dict(w_in_perm=w_in_perm, w_out=g["w_out"].reshape(-1, D),
                                    w_up=jnp.moveaxis(g["w_up"], 0, 1).reshape(D, -1),
                                    w_down=g["w_down"].reshape(-1, D))
            return self.full[l]

        def gather_rider(self, l, names):
            return "gather", [shards[n][l] for n in names]

        def gathered(self, l, names, outs):
            self.got.setdefault(l, {}).update(zip(names, outs))

        def scatter_rider(self, name, l, g):
            if name == "w_in_perm":
                slabs = shard_cols(jnp.concatenate([g[:, :NQ], g[:, PR:PR + H], g[:, NQ:PR]], axis=1))
            elif name == "w_up":
                slabs = g
            else:
                slabs = shard_rows(g)
            return "scatter", [slabs]

        def scattered(self, name, l, out):
            self.stage[(name, l)] = out

    comm = MeshComm()
    loss_loc, dx, gr = _local_step(x[0], loss_target[0], ada, mix_norm_g, comm, b_forget, conf_w_full, conf_dw_b,
                                   conf_ln_g, conf_ln_b, sc_w_full, ffn_norm_g, ffn_w_full, ffn_dw_b, final_norm_g)
    loss = lax.psum(loss_loc, ("x", "y", "c"))

    small_names = ["ada", "mix_norm_g", "ffn_norm_g", "b_forget", "conf_dw_b", "conf_ln_g", "conf_ln_b",
                   "ffn_dw_b", "conf_dw_w", "sc_dw_w", "ffn_dw_w"]
    pk, lay = _pack([jnp.stack(gr[n]) for n in small_names] + [gr["final_norm_g"]])
    parts = _all_gather(pk, True, name="ag_small_bwd")
    tot = _unpack(_sum_slabs(parts, name="sum_small"), lay)
    g_small = dict(zip(small_names + ["final_norm_g"], tot))
    d_ada_all = _unpack(parts, lay, lead=(N_DEV,))[0]
    my_cols = lambda a, n: lax.dynamic_slice_in_dim(a, me * n, n, axis=a.ndim - 1)

    c_act_t = jnp.zeros((D, LANES), F32).at[:, :N_DEV].set(jnp.transpose(c_act)).astype(BF16)
    res = None
    for l in range(L):
        d_loc = jnp.zeros((LANES, ada_loc), F32).at[:N_DEV].set(my_cols(d_ada_all[:, l], ada_loc)).astype(BF16)
        g_l = _matmul(c_act_t, d_loc, F32, name="mm_dada")
        res = _adam_sum(g_l[None], ada_w, m_ada_w, v_ada_w, l, res, name="adam_ada_w")
    out_ada_w = res

    big = {}
    for nm, key, wq, mq, vq in (("w_down", "w_down", w_down, m_w_down, v_w_down), ("w_up", "w_up", w_up, m_w_up, v_w_up),
                                ("w_out", "w_out", w_out, m_w_out, v_w_out), ("w_in", "w_in_perm", w_in, m_w_in, v_w_in)):
        res = None
        for l in reversed(range(L)):
            res = _adam_sum(comm.stage[(key, l)], wq, mq, vq, l, res, name="adam_" + nm)
        big[nm] = res

    K31, K3 = conf_dw_w.shape[1], sc_dw_w.shape[1]
    sm = [("ada_b", ada_b, m_ada_b, v_ada_b, g_small["ada"]),
          ("mix_norm_g", mix_norm_g, m_mix_norm_g, v_mix_norm_g, g_small["mix_norm_g"]),
          ("b_forget", b_forget, m_b_forget, v_b_forget, g_small["b_forget"]),
          ("conf_dw_w", conf_dw_w, m_conf_dw_w, v_conf_dw_w, my_cols(g_small["conf_dw_w"], conf_dw_w.shape[2])),
          ("conf_dw_b", conf_dw_b, m_conf_dw_b, v_conf_dw_b, g_small["conf_dw_b"]),
          ("conf_ln_g", conf_ln_g, m_conf_ln_g, v_conf_ln_g, g_small["conf_ln_g"]),
          ("conf_ln_b", conf_ln_b, m_conf_ln_b, v_conf_ln_b, g_small["conf_ln_b"]),
          ("sc_dw_w", sc_dw_w, m_sc_dw_w, v_sc_dw_w, my_cols(g_small["sc_dw_w"], sc_dw_w.shape[2])),
          ("ffn_norm_g", ffn_norm_g, m_ffn_norm_g, v_ffn_norm_g, g_small["ffn_norm_g"]),
          ("ffn_dw_w", ffn_dw_w, m_ffn_dw_w, v_ffn_dw_w, my_cols(g_small["ffn_dw_w"], ffn_dw_w.shape[2])),
          ("ffn_dw_b", ffn_dw_b, m_ffn_dw_b, v_ffn_dw_b, g_small["ffn_dw_b"]),
          ("final_norm_g", final_norm_g, m_final_norm_g, v_final_norm_g, g_small["final_norm_g"])]
    pw, lay = _pack([t[1] for t in sm])
    pm, _ = _pack([t[2] for t in sm])
    pv, _ = _pack([t[3] for t in sm])
    pg, _ = _pack([t[4] for t in sm])
    sres = _adam_sum(pg[None], pw[None], pm[None], pv[None], 0, None, name="adam_small")
    s_g, s_d, s_m, s_v = [dict(zip([t[0] for t in sm], _unpack(r[0], lay))) for r in sres]

    def pick(idx, name):
        if name == "ada_w":
            return out_ada_w[idx]
        if name in big:
            return big[name][idx]
        return (s_g, s_d, s_m, s_v)[idx][name]

    order = ["ada_w", "ada_b", "mix_norm_g", "w_in", "b_forget", "conf_dw_w", "conf_dw_b", "conf_ln_g", "conf_ln_b",
             "sc_dw_w", "w_out", "ffn_norm_g", "w_up", "ffn_dw_w", "ffn_dw_b", "w_down", "final_norm_g"]
    outs = [loss, dx[None]]
    for idx in range(4):
        outs += [pick(idx, n) for n in order]
    return tuple(outs)
```

```python
import functools

import jax
import jax.numpy as jnp
from jax import lax
from jax.experimental import pallas as pl
from jax.experimental.pallas import tpu as pltpu

F32 = jnp.float32
BF16 = jnp.bfloat16
RMS_EPS = 1e-6
LN_EPS = 1e-5
HEAD_DIM = 128
N_ADA = 6
ADAM_LR = 0.001
ADAM_B1 = 0.9
ADAM_B2 = 0.999
ADAM_EPS = 1e-08
ADAM_WD = 0.01
ADAM_STEP = 10
N_DEV = 8
LANES = 128
VMEM_LIMIT_BYTES = 56 * 1024 * 1024
MM_TILE = 1024
MM_TILE_WIDE = 1536
MM_TILE_N_MAX = 2816
MXU_WIDTH = 256
MM_VMEM_BUDGET = 48 * 1024 * 1024
MESH = pl.DeviceIdType.MESH
PEER_FLIPS = ((0, 0, 1), (1, 0, 0), (0, 1, 0), (1, 1, 0), (1, 0, 1), (0, 1, 1), (1, 1, 1))


def _params(*sem):
    return pltpu.CompilerParams(dimension_semantics=sem, vmem_limit_bytes=VMEM_LIMIT_BYTES)


def _tile(n, cap):
    if n <= cap:
        return n
    for t in range(cap - cap % LANES, 0, -LANES):
        if n % t == 0:
            return t
    raise ValueError(f"no tile for {n}")


def _mm_tile(n):
    t = _tile(n, MM_TILE)
    return t if t == min(n, MM_TILE) else _tile(n, MM_TILE_WIDE)


def _n_tile(n, vmem_bytes):
    for step in (MXU_WIDTH, LANES):
        for t in range(min(n, MM_TILE_N_MAX) // step * step, 0, -step):
            if n % t == 0 and vmem_bytes(t) <= MM_VMEM_BUDGET:
                return t
    return n


def _sigmoid(v):
    return jax.nn.sigmoid(v)


def _matmul(a, b, out_dtype, name, rider=None, b_transposed=False):
    a_parts = a if isinstance(a, tuple) else (a,)
    na = len(a_parts)
    M = a_parts[0].shape[0]
    K = sum(p.shape[1] for p in a_parts)
    N = b.shape[0] if b_transposed else b.shape[1]
    tm = _mm_tile(M)
    out_bytes = jnp.dtype(out_dtype).itemsize
    part_k = a_parts[0].shape[1]

    def blocks_bytes(tk_, tn_):
        return 4 * (na * tm * tk_ + tk_ * tn_) + (4 * tm * tn_ if K > tk_ else 0) + 2 * tm * tn_ * out_bytes

    for tk in [t for t in range(part_k, 0, -LANES) if part_k % t == 0]:
        tn = _n_tile(N, functools.partial(blocks_bytes, tk))
        if blocks_bytes(tk, tn) <= MM_VMEM_BUDGET and tn >= min(N, MM_TILE if b_transposed else MM_TILE // 2):
            break
    nk = K // tk
    half = part_k // tk
    grid = (M // tm, N // tn, nk)
    dims = _NT if b_transposed else (((1,), (0,)), ((), ()))
    kind, arrs = rider if rider is not None else (None, [])
    nr = len(arrs)

    def body(*refs):
        a_refs, refs = refs[:na], refs[na - 1:]
        a_ref, b_ref = a_refs[0], refs[1]
        r_in = refs[2:2 + nr]
        o_ref = refs[2 + nr]
        r_out = refs[3 + nr:3 + 2 * nr]
        rest = refs[3 + 2 * nr:]
        i, j, k = pl.program_id(0), pl.program_id(1), pl.program_id(2)
        if nr:
            sems = rest[-3:]

            @pl.when((i == 0) & (j == 0) & (k == 0))
            def _():
                for r in range(nr):
                    _rider_start(kind, r_in[r], r_out[r], sems, r)

        if nk == 1:
            o_ref[...] = lax.dot_general(a_ref[...], b_ref[...], dims, preferred_element_type=F32).astype(o_ref.dtype)
        else:
            acc_ref = rest[0]

            @pl.when(k == 0)
            def _():
                acc_ref[...] = jnp.zeros_like(acc_ref)

            def accumulate(part_ref):
                acc_ref[...] += lax.dot_general(part_ref[...], b_ref[...], dims, preferred_element_type=F32)

            if na == 1:
                accumulate(a_ref)
            else:
                pl.when(k < half)(lambda: accumulate(a_refs[0]))
                pl.when(k >= half)(lambda: accumulate(a_refs[1]))

            @pl.when(k == nk - 1)
            def _():
                o_ref[...] = acc_ref[...].astype(o_ref.dtype)

        if nr:
            @pl.when((i == grid[0] - 1) & (j == grid[1] - 1) & (k == nk - 1))
            def _():
                for r in range(nr):
                    _rider_finish(kind, r_in[r], r_out[r], sems, r)

    scratch = [] if nk == 1 else [pltpu.VMEM((tm, tn), F32)]
    hbm = pl.BlockSpec(memory_space=pl.ANY)
    if na == 1:
        a_specs = [pl.BlockSpec((tm, tk), lambda i, j, k: (i, k))]
    else:
        a_specs = [pl.BlockSpec((tm, tk), lambda i, j, k: (i, jnp.minimum(k, half - 1))),
                   pl.BlockSpec((tm, tk), lambda i, j, k: (i, jnp.maximum(k - half, 0)))]
    out_shape = jax.ShapeDtypeStruct((M, N), out_dtype)
    out_specs = pl.BlockSpec((tm, tn), lambda i, j, k: (i, j))
    if nr:
        scratch += [pltpu.SemaphoreType.DMA((7 * nr,)), pltpu.SemaphoreType.DMA((7 * nr,)),
                    pltpu.SemaphoreType.DMA((nr,))]
        out_shape = (out_shape,) + tuple(
            jax.ShapeDtypeStruct(x.shape if kind == "scatter" else (N_DEV,) + x.shape, x.dtype) for x in arrs)
        out_specs = (out_specs,) + (hbm,) * nr
    out = pl.pallas_call(
        body, name=name,
        out_shape=out_shape,
        grid=grid,
        in_specs=a_specs + [pl.BlockSpec((tn, tk), lambda i, j, k: (j, k)) if b_transposed
                            else pl.BlockSpec((tk, tn), lambda i, j, k: (k, j))] + [hbm] * nr,
        out_specs=out_specs,
        scratch_shapes=scratch,
        compiler_params=_params(*(("arbitrary",) * 3 if nr else ("parallel", "parallel", "arbitrary"))),
    )(*a_parts, b, *arrs)
    return (out[0], list(out[1:])) if nr else out


_TN = (((0,), (0,)), ((), ()))


def _matmul_tn(a, b, out_dtype, name, col_slabs=None):
    b_parts = b if isinstance(b, tuple) else (b,)
    S, M = a.shape
    N = sum(p.shape[1] for p in b_parts)
    tm = _mm_tile(M)
    out_bytes = jnp.dtype(out_dtype).itemsize

    def blocks_bytes(ts_, tn_):
        return 4 * (ts_ * tm + len(b_parts) * ts_ * tn_) + 4 * tm * tn_ + 2 * tm * tn_ * out_bytes

    for ts in (_tile(S, 2 * MM_TILE), _tile(S, MM_TILE)):
        tn = N // col_slabs if col_slabs else _n_tile(b_parts[0].shape[1], functools.partial(blocks_bytes, ts))
        if blocks_bytes(ts, tn) <= MM_VMEM_BUDGET and tn >= min(b_parts[0].shape[1], MM_TILE):
            break
    ns = S // ts
    half = b_parts[0].shape[1] // tn

    def body(*refs):
        a_ref, b_refs, (o_ref, acc_ref) = refs[0], refs[1:-2], refs[-2:]
        j, k = pl.program_id(1), pl.program_id(2)

        @pl.when(k == 0)
        def _():
            acc_ref[...] = jnp.zeros_like(acc_ref)

        def accumulate(b_ref):
            acc_ref[...] += lax.dot_general(a_ref[...], b_ref[...], _TN, preferred_element_type=F32)

        if len(b_refs) == 1:
            accumulate(b_refs[0])
        else:
            pl.when(j < half)(lambda: accumulate(b_refs[0]))
            pl.when(j >= half)(lambda: accumulate(b_refs[1]))

        @pl.when(k == ns - 1)
        def _():
            if col_slabs is None:
                o_ref[...] = acc_ref[...].astype(o_ref.dtype)
            else:
                o_ref[0] = acc_ref[...].astype(o_ref.dtype)

    if col_slabs is None:
        out_shape = jax.ShapeDtypeStruct((M, N), out_dtype)
        out_spec = pl.BlockSpec((tm, tn), lambda i, j, k: (i, j))
    else:
        out_shape = jax.ShapeDtypeStruct((col_slabs, M, tn), out_dtype)
        out_spec = pl.BlockSpec((1, tm, tn), lambda i, j, k: (j, i, 0))
    if len(b_parts) == 1:
        b_specs = [pl.BlockSpec((ts, tn), lambda i, j, k: (k, j))]
    else:
        b_specs = [pl.BlockSpec((ts, tn), lambda i, j, k: (jnp.where(j < half, k, ns - 1), jnp.minimum(j, half - 1))),
                   pl.BlockSpec((ts, tn), lambda i, j, k: (jnp.where(j < half, 0, k), jnp.maximum(j - half, 0)))]
    return pl.pallas_call(
        body, name=name,
        out_shape=out_shape,
        grid=(M // tm, N // tn, ns),
        in_specs=[pl.BlockSpec((ts, tm), lambda i, j, k: (k, i))] + b_specs,
        out_specs=out_spec,
        scratch_shapes=[pltpu.VMEM((tm, tn), F32)],
        compiler_params=_params("parallel", "parallel", "arbitrary"),
    )(a, *b_parts)


def _site_fwd(x, delta, gate, g, sc, sh, name):
    S, D = x.shape
    T = min(256, S)
    res = delta is not None

    def body(*refs):
        if res:
            x_ref, d_ref, gate_ref, g_ref, sc_ref, sh_ref, xo_ref, h_ref = refs
            xv = x_ref[...] + gate_ref[...] * d_ref[...]
            xo_ref[...] = xv
        else:
            x_ref, g_ref, sc_ref, sh_ref, h_ref = refs
            xv = x_ref[...]
        r = lax.rsqrt(jnp.mean(xv * xv, axis=-1, keepdims=True) + RMS_EPS)
        a = g_ref[...] * (1.0 + sc_ref[...])
        h_ref[...] = (xv * r * a + sh_ref[...]).astype(BF16)

    row = pl.BlockSpec((T, D), lambda i: (i, 0))
    vec = pl.BlockSpec((1, D), lambda i: (0, 0))
    if res:
        ins, in_specs = (x, delta, gate, g, sc, sh), [row, row, vec, vec, vec, vec]
        out_shape = (jax.ShapeDtypeStruct((S, D), F32), jax.ShapeDtypeStruct((S, D), BF16))
        out_specs = (row, row)
    else:
        ins, in_specs = (x, g, sc, sh), [row, vec, vec, vec]
        out_shape = jax.ShapeDtypeStruct((S, D), BF16)
        out_specs = row
    out = pl.pallas_call(body, name=name, out_shape=out_shape, grid=(S // T,), in_specs=in_specs,
                         out_specs=out_specs, compiler_params=_params("parallel"))(*ins)
    return out if res else (x, out)


def _site_bwd(x, dh, dres, g, sc, sh, delta, gate, name):
    S, D = x.shape
    T = min(256, S)
    res = delta is not None

    def body(*refs):
        if res:
            (x_ref, dh_ref, dres_ref, g_ref, sc_ref, delta_ref, gate_ref,
             dx_ref, dsh_ref, da_ref, dd_ref, dgate_ref) = refs
        else:
            x_ref, dh_ref, dres_ref, g_ref, sc_ref, dx_ref, dsh_ref, da_ref = refs
        i = pl.program_id(0)
        xv = x_ref[...]
        dhv = dh_ref[...]
        r = lax.rsqrt(jnp.mean(xv * xv, axis=-1, keepdims=True) + RMS_EPS)
        xh = xv * r
        dxh = dhv * (g_ref[...] * (1.0 + sc_ref[...]))
        dx = r * (dxh - xh * jnp.mean(dxh * xh, axis=-1, keepdims=True)) + dres_ref[...]
        dx_ref[...] = dx

        @pl.when(i == 0)
        def _():
            dsh_ref[...] = jnp.zeros_like(dsh_ref)
            da_ref[...] = jnp.zeros_like(da_ref)
            if res:
                dgate_ref[...] = jnp.zeros_like(dgate_ref)

        dsh_ref[...] += jnp.sum(dhv, axis=0, keepdims=True)
        da_ref[...] += jnp.sum(dhv * xh, axis=0, keepdims=True)
        if res:
            dd_ref[...] = (gate_ref[...] * dx).astype(BF16)
            dgate_ref[...] += jnp.sum(dx * delta_ref[...], axis=0, keepdims=True)

    row = pl.BlockSpec((T, D), lambda i: (i, 0))
    vec = pl.BlockSpec((1, D), lambda i: (0, 0))
    vshape = jax.ShapeDtypeStruct((1, D), F32)
    if res:
        ins, in_specs = (x, dh, dres, g, sc, delta, gate), [row, row, row, vec, vec, row, vec]
        out_shape = (jax.ShapeDtypeStruct((S, D), F32), vshape, vshape, jax.ShapeDtypeStruct((S, D), BF16), vshape)
        out_specs = (row, vec, vec, row, vec)
    else:
        ins, in_specs = (x, dh, dres, g, sc), [row, row, row, vec, vec]
        out_shape = (jax.ShapeDtypeStruct((S, D), F32), vshape, vshape)
        out_specs = (row, vec, vec)
    return pl.pallas_call(body, name=name, out_shape=out_shape, grid=(S // T,), in_specs=in_specs,
                          out_specs=out_specs, compiler_params=_params("arbitrary"))(*ins)


def _final_fwd_bwd(x, delta, gate, gfin, target, name):
    S, D = x.shape
    T = min(256, S)

    def body(x_ref, delta_ref, gate_ref, g_ref, t_ref, loss_ref, dx_ref, dd_ref, dgate_ref, dg_ref):
        i = pl.program_id(0)
        dl = delta_ref[...]
        xv = x_ref[...] + gate_ref[...] * dl
        r = lax.rsqrt(jnp.mean(xv * xv, axis=-1, keepdims=True) + RMS_EPS)
        xh = xv * r
        gv = g_ref[...]
        e = xh * gv - t_ref[...]
        dy = e * (1.0 / D)
        dxh = dy * gv
        dx = r * (dxh - xh * jnp.mean(dxh * xh, axis=-1, keepdims=True))
        dx_ref[...] = dx
        dd_ref[...] = (gate_ref[...] * dx).astype(BF16)

        @pl.when(i == 0)
        def _():
            loss_ref[...] = jnp.zeros_like(loss_ref)
            dgate_ref[...] = jnp.zeros_like(dgate_ref)
            dg_ref[...] = jnp.zeros_like(dg_ref)

        loss_ref[...] += jnp.sum(e * e, axis=0, keepdims=True)
        dgate_ref[...] += jnp.sum(dx * dl, axis=0, keepdims=True)
        dg_ref[...] += jnp.sum(dy * xh, axis=0, keepdims=True)

    row = pl.BlockSpec((T, D), lambda i: (i, 0))
    vec = pl.BlockSpec((1, D), lambda i: (0, 0))
    vshape = jax.ShapeDtypeStruct((1, D), F32)
    return pl.pallas_call(
        body, name=name,
        out_shape=(vshape, jax.ShapeDtypeStruct((S, D), F32), jax.ShapeDtypeStruct((S, D), BF16), vshape, vshape),
        grid=(S // T,), in_specs=[row, row, vec, vec, row], out_specs=(vec, row, row, vec, vec),
        compiler_params=_params("arbitrary"))(x, delta, gate, gfin, target)


def _split3(v):
    hi = v.astype(BF16)
    r1 = v - hi.astype(F32)
    mid = r1.astype(BF16)
    lo = (r1 - mid.astype(F32)).astype(BF16)
    return hi, mid, lo


def _tri_dot(tri, v):
    hi, mid, lo = _split3(v)
    d = functools.partial(jnp.dot, preferred_element_type=F32)
    return d(tri, hi) + d(tri, mid) + d(tri, lo)


def _fgate_fwd(rest, bpad, fblk, name):
    S = rest.shape[0]
    CH = min(256, S)
    nch = S // CH

    def body(f_ref, b_ref, o_ref):
        row = lax.broadcasted_iota(jnp.int32, (CH, CH), 0)
        col = lax.broadcasted_iota(jnp.int32, (CH, CH), 1)
        tri = (row >= col).astype(BF16)

        def step(ci, carry):
            rows = pl.ds(pl.multiple_of(ci * CH, CH), CH)
            z = f_ref[rows, :] + b_ref[...]
            lf = jnp.minimum(z, 0.0) - jnp.log(1.0 + jnp.exp(-jnp.abs(z)))
            o_ref[rows, :] = _tri_dot(tri, lf) + carry
            return carry + jnp.sum(lf, axis=0, keepdims=True)

        lax.fori_loop(0, nch, step, jnp.zeros((1, LANES), F32))

    return pl.pallas_call(
        body, name=name, out_shape=jax.ShapeDtypeStruct((S, LANES), F32), grid=(1,),
        in_specs=[pl.BlockSpec((S, LANES), lambda i: (0, fblk)), pl.BlockSpec((1, LANES), lambda i: (0, 0))],
        out_specs=pl.BlockSpec((S, LANES), lambda i: (0, 0)),
        compiler_params=_params("arbitrary"))(rest, bpad)


def _fgate_bwd(rest, bpad, dF, fblk, name):
    S = rest.shape[0]
    CH = min(256, S)
    nch = S // CH

    def body(f_ref, b_ref, df_ref, o_ref, db_ref):
        row = lax.broadcasted_iota(jnp.int32, (CH, CH), 0)
        col = lax.broadcasted_iota(jnp.int32, (CH, CH), 1)
        tri = (col >= row).astype(BF16)

        def step(n, carry):
            sfx_carry, db = carry
            ci = nch - 1 - n
            rows = pl.ds(pl.multiple_of(ci * CH, CH), CH)
            z = f_ref[rows, :] + b_ref[...]
            dfv = df_ref[rows, :]
            dz = (_tri_dot(tri, dfv) + sfx_carry) * _sigmoid(-z)
            o_ref[rows, :] = dz.astype(BF16)
            return sfx_carry + jnp.sum(dfv, axis=0, keepdims=True), db + jnp.sum(dz, axis=0, keepdims=True)

        zero = jnp.zeros((1, LANES), F32)
        _, db = lax.fori_loop(0, nch, step, (zero, zero))
        db_ref[...] = db

    blk = pl.BlockSpec((S, LANES), lambda i: (0, 0))
    return pl.pallas_call(
        body, name=name,
        out_shape=(jax.ShapeDtypeStruct((S, LANES), BF16), jax.ShapeDtypeStruct((1, LANES), F32)), grid=(1,),
        in_specs=[pl.BlockSpec((S, LANES), lambda i: (0, fblk)), pl.BlockSpec((1, LANES), lambda i: (0, 0)), blk],
        out_specs=(blk, pl.BlockSpec((1, LANES), lambda i: (0, 0))),
        compiler_params=_params("arbitrary"))(rest, bpad, dF)


_NT = (((1,), (1,)), ((), ()))
LOG2E = 1.4426950408889634
ATTN_TILE = 512


def _blocked_rows(a, TA):
    H, S = a.shape
    return a.reshape(H, S // TA, 1, TA)


def _attn_fwd(qkv, nfb, H, name):
    S = qkv.shape[0]
    TA = min(ATTN_TILE, S)
    nb = S // TA
    c = HEAD_DIM ** -0.5 * LOG2E

    def body(q_ref, k_ref, v_ref, nf_ref, o_ref, o32_ref, lse_ref, m_ref, l_ref, acc_ref):
        i = pl.program_id(1)
        m_ref[...] = jnp.full_like(m_ref, -jnp.inf)
        l_ref[...] = jnp.zeros_like(l_ref)
        acc_ref[...] = jnp.zeros_like(acc_ref)

        def block(j, masked):
            rows = pl.ds(pl.multiple_of(j * TA, TA), TA)
            st = (lax.dot_general(k_ref[rows, :], q_ref[...], _NT, preferred_element_type=F32) * c
                  + jnp.tile(nf_ref[0, rows, :], (1, TA // LANES)))
            if masked:
                key = lax.broadcasted_iota(jnp.int32, (TA, TA), 0)
                qry = lax.broadcasted_iota(jnp.int32, (TA, TA), 1)
                st = jnp.where(key <= qry, st, -jnp.inf)
            m_old = m_ref[...]
            m_new = jnp.maximum(m_old, jnp.max(st, axis=0, keepdims=True))
            alpha = jnp.exp2(m_old - m_new)
            pt = jnp.exp2(st - m_new)
            l_ref[...] = alpha * l_ref[...] + jnp.sum(pt, axis=0, keepdims=True)
            acc_ref[...] = alpha * acc_ref[...] + lax.dot_general(v_ref[rows, :], pt.astype(BF16), _TN,
                                                                  preferred_element_type=F32)
            m_ref[...] = m_new

        def loop(jj, carry):
            block(2 * jj, False)
            block(2 * jj + 1, False)
            return carry

        lax.fori_loop(0, i // 2, loop, 0)

        @pl.when(i % 2 == 1)
        def _():
            block(i - 1, False)

        block(i, True)
        o = jnp.transpose(acc_ref[...] / l_ref[...])
        o32_ref[...] = o
        o_ref[...] = o.astype(BF16)
        lse_ref[0, 0] = m_ref[...] + jnp.log(l_ref[...]) * LOG2E

    qblk = pl.BlockSpec((TA, HEAD_DIM), lambda h, i: (i, h))
    return pl.pallas_call(
        body, name=name, grid=(H, nb),
        in_specs=[qblk,
                  pl.BlockSpec((S, HEAD_DIM), lambda h, i: (0, H + h)),
                  pl.BlockSpec((S, HEAD_DIM), lambda h, i: (0, 2 * H + h)),
                  pl.BlockSpec((1, S, LANES), lambda h, i: (h, 0, 0))],
        out_specs=(qblk, qblk, pl.BlockSpec((1, 1, 1, TA), lambda h, i: (h, i, 0, 0))),
        scratch_shapes=[pltpu.VMEM((1, TA), F32), pltpu.VMEM((1, TA), F32), pltpu.VMEM((HEAD_DIM, TA), F32)],
        out_shape=(jax.ShapeDtypeStruct((S, H * HEAD_DIM), BF16), jax.ShapeDtypeStruct((S, H * HEAD_DIM), F32),
                   jax.ShapeDtypeStruct((H, nb, 1, TA), F32)),
        compiler_params=_params("parallel", "parallel"))(qkv, qkv, qkv, nfb)


def _attn_delta(o, do, H, name):
    S = o.shape[0]
    T = min(512, S)

    def body(o_ref, do_ref, d_ref):
        d_ref[0] = jnp.sum(o_ref[...].astype(F32) * do_ref[...].astype(F32), axis=-1, keepdims=True)

    blk = pl.BlockSpec((T, HEAD_DIM), lambda h, i: (i, h))
    return pl.pallas_call(
        body, name=name, out_shape=jax.ShapeDtypeStruct((H, S, 1), F32), grid=(H, S // T),
        in_specs=[blk, blk], out_specs=pl.BlockSpec((1, T, 1), lambda h, i: (h, i, 0)),
        compiler_params=_params("parallel", "parallel"))(o, do)


def _ds_tile(k, q, v, do, nfb, lse_row, delta_row, c, masked):
    TK, TQ = k.shape[0], q.shape[0]
    st = lax.dot_general(k, q, _NT, preferred_element_type=F32) * c + jnp.tile(nfb, (1, TQ // LANES))
    pt = jnp.exp2(st - lse_row)
    if masked:
        key = lax.broadcasted_iota(jnp.int32, (TK, TQ), 0)
        qry = lax.broadcasted_iota(jnp.int32, (TK, TQ), 1)
        pt = jnp.where(key <= qry, pt, 0.0)
    dpt = lax.dot_general(v, do, _NT, preferred_element_type=F32)
    return pt, pt * (dpt - delta_row)


def _attn_bwd(qkv, nfb, do, lse, delta, H, name):
    S = qkv.shape[0]
    TA = min(ATTN_TILE, S)
    nb = S // TA
    scale = HEAD_DIM ** -0.5
    c = scale * LOG2E

    def body(q_ref, k_ref, v_ref, nf_ref, do_ref, lse_ref, dl_ref, dq_ref, dk_ref, dv_ref, drow_ref, dnf_ref,
             dq_acc, dk_acc, dv_acc, dnf_acc):
        j = pl.program_id(1)

        @pl.when(j == 0)
        def _():
            dq_acc[...] = jnp.zeros_like(dq_acc)
            drow_ref[...] = jnp.zeros_like(drow_ref)

        dk_acc[...] = jnp.zeros_like(dk_acc)
        dv_acc[...] = jnp.zeros_like(dv_acc)
        dnf_acc[...] = jnp.zeros_like(dnf_acc)
        kb = k_ref[...]

        def block(i, masked):
            rows = pl.ds(pl.multiple_of(i * TA, TA), TA)
            qb = q_ref[rows, :]
            dob = do_ref[rows, :]
            pt, dst = _ds_tile(kb, qb, v_ref[...], dob, nf_ref[0], lse_ref[0, i], dl_ref[0, i], c, masked)
            dsb = dst.astype(BF16)
            dv_acc[...] += jnp.dot(pt.astype(BF16), dob, preferred_element_type=F32)
            dk_acc[...] += jnp.dot(dsb, qb, preferred_element_type=F32)
            dq_acc[rows, :] += lax.dot_general(dsb, kb, _TN, preferred_element_type=F32)
            drow_ref[0, i] += jnp.sum(dst, axis=0, keepdims=True)
            part = dst[:, 0:LANES]
            for t in range(1, TA // LANES):
                part = part + dst[:, t * LANES:(t + 1) * LANES]
            dnf_acc[...] += part

        def loop(ii, carry):
            block(j + 1 + 2 * ii, False)
            block(j + 2 + 2 * ii, False)
            return carry

        block(j, True)
        rest = nb - 1 - j
        lax.fori_loop(0, rest // 2, loop, 0)

        @pl.when(rest % 2 == 1)
        def _():
            block(nb - 1, False)

        dk_ref[...] = (dk_acc[...] * scale).astype(BF16)
        dv_ref[...] = dv_acc[...].astype(BF16)
        dnf_ref[0] = jnp.sum(dnf_acc[...], axis=-1, keepdims=True)

        @pl.when(j == nb - 1)
        def _():
            dq_ref[...] = (dq_acc[...] * scale).astype(BF16)

    full = pl.BlockSpec((S, HEAD_DIM), lambda h, j: (0, h))
    row_stat = pl.BlockSpec((1, nb, 1, TA), lambda h, j: (h, 0, 0, 0))
    kblk = lambda c0: pl.BlockSpec((TA, HEAD_DIM), lambda h, j: (j, c0 + h))
    shp = jax.ShapeDtypeStruct((S, H * HEAD_DIM), BF16)
    return pl.pallas_call(
        body, name=name, grid=(H, nb),
        in_specs=[full, kblk(H), kblk(2 * H), pl.BlockSpec((1, TA, LANES), lambda h, j: (h, j, 0)), full,
                  row_stat, row_stat],
        out_specs=(full, kblk(0), kblk(0), row_stat, pl.BlockSpec((1, TA, 1), lambda h, j: (h, j, 0))),
        scratch_shapes=[pltpu.VMEM((S, HEAD_DIM), F32), pltpu.VMEM((TA, HEAD_DIM), F32),
                        pltpu.VMEM((TA, HEAD_DIM), F32), pltpu.VMEM((TA, LANES), F32)],
        out_shape=(shp, shp, shp, jax.ShapeDtypeStruct((H, nb, 1, TA), F32), jax.ShapeDtypeStruct((H, S, 1), F32)),
        compiler_params=_params("parallel", "arbitrary"))(qkv, qkv, qkv, nfb, do, lse, delta)


def _taps(ext_ref, w_ref, K, base, r0, rows, cols, reverse=False, init=None):
    acc = init
    for k in range(K):
        wk = w_ref[(K - 1 - k) if reverse else k:((K - 1 - k) if reverse else k) + 1, cols]
        term = wk * ext_ref[base + k + r0:base + k + r0 + rows, cols]
        acc = term if acc is None else acc + term
    return acc


def _prev_blk(T, H):
    return lambda i: jnp.maximum(i * (T // H) - 1, 0)


def _next_blk(T, H, S):
    return lambda i: jnp.minimum((i + 1) * (T // H), S // H - 1)


def _conf_fwd(rest, w, b, lng, lnb, name):
    S = rest.shape[0]
    K, C = w.shape
    H, T = 32, min(256, S)
    RS = min(64, T)
    base = H - (K - 1)

    def body(cv_ref, cg_ref, cvp_ref, cgp_ref, w_ref, b_ref, g_ref, bb_ref, o_ref, ext_ref):
        i = pl.program_id(0)
        ext_ref[0:H, :] = jnp.where(i > 0, cvp_ref[...] * _sigmoid(cgp_ref[...]), 0.0)
        ext_ref[H:H + T, :] = cv_ref[...] * _sigmoid(cg_ref[...])
        for r0 in range(0, T, RS):
            cc = _taps(ext_ref, w_ref, K, base, r0, RS, slice(None), init=jnp.broadcast_to(b_ref[...], (RS, C)))
            xc = cc - jnp.mean(cc, axis=-1, keepdims=True)
            y = xc * lax.rsqrt(jnp.mean(xc * xc, axis=-1, keepdims=True) + LN_EPS) * g_ref[...] + bb_ref[...]
            o_ref[r0:r0 + RS, :] = (y * _sigmoid(y)).astype(BF16)

    pb = _prev_blk(T, H)
    cur = lambda cb: pl.BlockSpec((T, C), lambda i: (i, cb))
    prev = lambda cb: pl.BlockSpec((H, C), lambda i: (pb(i), cb))
    full = lambda a: pl.BlockSpec(a.shape, lambda i: (0, 0))
    return pl.pallas_call(
        body, name=name, out_shape=jax.ShapeDtypeStruct((S, C), BF16), grid=(S // T,),
        in_specs=[cur(0), cur(1), prev(0), prev(1), full(w), full(b), full(lng), full(lnb)],
        out_specs=pl.BlockSpec((T, C), lambda i: (i, 0)),
        scratch_shapes=[pltpu.VMEM((H + T, C), F32)],
        compiler_params=_params("parallel"))(rest, rest, rest, rest, w, b, lng, lnb)


def _conf_bwd(rest, dcs, w, b, lng, lnb, name):
    S = rest.shape[0]
    K, C = w.shape
    H, T = 32, min(256, S)
    RS = 32
    nI = S // T
    base = H - (K - 1)

    def body(cv_ref, cg_ref, cvp_ref, cgp_ref, cvn_ref, cgn_ref, do_ref, don_ref, w_ref, b_ref, g_ref, bb_ref,
             dcvg_ref, dw_ref, dvec_ref, ext_ref, dcc_ref):
        i = pl.program_id(0)
        ext_ref[0:H, :] = jnp.where(i > 0, cvp_ref[...] * _sigmoid(cgp_ref[...]), 0.0)
        ext_ref[H:H + T, :] = cv_ref[...] * _sigmoid(cg_ref[...])
        ext_ref[H + T:H + T + H, :] = cvn_ref[...] * _sigmoid(cgn_ref[...])

        @pl.when(i == 0)
        def _():
            dw_ref[...] = jnp.zeros_like(dw_ref)
            dvec_ref[...] = jnp.zeros_like(dvec_ref)

        db = jnp.zeros((1, C), F32)
        dg = jnp.zeros((1, C), F32)
        dbb = jnp.zeros((1, C), F32)
        for r0 in range(0, T + H, RS):
            cc = _taps(ext_ref, w_ref, K, base, r0, RS, slice(None), init=jnp.broadcast_to(b_ref[...], (RS, C)))
            xc = cc - jnp.mean(cc, axis=-1, keepdims=True)
            r = lax.rsqrt(jnp.mean(xc * xc, axis=-1, keepdims=True) + LN_EPS)
            xh = xc * r
            y = xh * g_ref[...] + bb_ref[...]
            sy = _sigmoid(y)
            if r0 < T:
                d_o = do_ref[r0:r0 + RS, :]
            else:
                d_o = jnp.where(i < nI - 1, don_ref[r0 - T:r0 - T + RS, :], 0.0)
            dy = d_o * (sy * (1.0 + y * (1.0 - sy)))
            dxh = dy * g_ref[...]
            dcc = r * (dxh - jnp.mean(dxh, axis=-1, keepdims=True)
                       - xh * jnp.mean(dxh * xh, axis=-1, keepdims=True))
            dcc_ref[r0:r0 + RS, :] = dcc
            if r0 < T:
                dbb = dbb + jnp.sum(dy, axis=0, keepdims=True)
                dg = dg + jnp.sum(dy * xh, axis=0, keepdims=True)
                db = db + jnp.sum(dcc, axis=0, keepdims=True)
        dvec_ref[0:1, :] += db
        dvec_ref[1:2, :] += dg
        dvec_ref[2:3, :] += dbb
        R2 = min(64, T)
        for k in range(K):
            s = jnp.zeros((1, C), F32)
            for r0 in range(0, T, R2):
                s = s + jnp.sum(dcc_ref[r0:r0 + R2, :] * ext_ref[base + k + r0:base + k + r0 + R2, :],
                                axis=0, keepdims=True)
            dw_ref[k:k + 1, :] += s
        for r0 in range(0, T, R2):
            dci = _taps(dcc_ref, w_ref, K, 0, r0, R2, slice(None), reverse=True)
            cvv = cv_ref[r0:r0 + R2, :]
            sg = _sigmoid(cg_ref[r0:r0 + R2, :])
            dcvg_ref[r0:r0 + R2, 0:C] = (dci * sg).astype(BF16)
            dcvg_ref[r0:r0 + R2, C:2 * C] = (dci * cvv * sg * (1.0 - sg)).astype(BF16)

    pb, nb_ = _prev_blk(T, H), _next_blk(T, H, S)
    cur = lambda cb: pl.BlockSpec((T, C), lambda i: (i, cb))
    prev = lambda cb: pl.BlockSpec((H, C), lambda i: (pb(i), cb))
    nxt = lambda cb: pl.BlockSpec((H, C), lambda i: (nb_(i), cb))
    full = lambda a: pl.BlockSpec(a.shape, lambda i: (0, 0))
    return pl.pallas_call(
        body, name=name,
        out_shape=(jax.ShapeDtypeStruct((S, 2 * C), BF16), jax.ShapeDtypeStruct((32, C), F32),
                   jax.ShapeDtypeStruct((8, C), F32)),
        grid=(nI,),
        in_specs=[cur(0), cur(1), prev(0), prev(1), nxt(0), nxt(1), cur(0), nxt(0),
                  full(w), full(b), full(lng), full(lnb)],
        out_specs=(pl.BlockSpec((T, 2 * C), lambda i: (i, 0)), pl.BlockSpec((32, C), lambda i: (0, 0)),
                   pl.BlockSpec((8, C), lambda i: (0, 0))),
        scratch_shapes=[pltpu.VMEM((H + T + H, C), F32), pltpu.VMEM((T + H, C), F32)],
        compiler_params=_params("arbitrary"))(rest, rest, rest, rest, rest, rest, dcs, dcs, w, b, lng, lnb)


def _sconv_fwd(rest, w, name):
    S = rest.shape[0]
    K, C = w.shape
    H, T = 8, min(256, S)
    RS = min(64, T)
    base = H - (K - 1)

    def body(sx_ref, sb_ref, sc_ref, sxp_ref, scp_ref, w_ref, o_ref, ext_ref):
        i = pl.program_id(0)
        ext_ref[0:H, :] = jnp.where(i > 0, sxp_ref[...] * scp_ref[...], 0.0)
        ext_ref[H:H + T, :] = sx_ref[...] * sc_ref[...]
        for r0 in range(0, T, RS):
            cz = _taps(ext_ref, w_ref, K, base, r0, RS, slice(None))
            o_ref[r0:r0 + RS, :] = (sb_ref[r0:r0 + RS, :] * cz).astype(BF16)

    pb = _prev_blk(T, H)
    cur = lambda cb: pl.BlockSpec((T, C), lambda i: (i, cb))
    prev = lambda cb: pl.BlockSpec((H, C), lambda i: (pb(i), cb))
    return pl.pallas_call(
        body, name=name, out_shape=jax.ShapeDtypeStruct((S, C), BF16), grid=(S // T,),
        in_specs=[cur(2), cur(3), cur(4), prev(2), prev(4), pl.BlockSpec(w.shape, lambda i: (0, 0))],
        out_specs=pl.BlockSpec((T, C), lambda i: (i, 0)),
        scratch_shapes=[pltpu.VMEM((H + T, C), F32)],
        compiler_params=_params("parallel"))(rest, rest, rest, rest, rest, w)


def _sconv_bwd(rest, dcs, w, name):
    S = rest.shape[0]
    K, C = w.shape
    H, T = 8, min(256, S)
    RS = min(64, T)
    nI = S // T
    base = H - (K - 1)

    def body(sx_ref, sb_ref, sc_ref, sxp_ref, scp_ref, sbn_ref, do_ref, don_ref, w_ref,
             dout_ref, dw_ref, ext_ref, dcv_ref):
        i = pl.program_id(0)
        ext_ref[0:H, :] = jnp.where(i > 0, sxp_ref[...] * scp_ref[...], 0.0)
        ext_ref[H:H + T, :] = sx_ref[...] * sc_ref[...]
        dcv_ref[0:T, :] = do_ref[...] * sb_ref[...]
        dcv_ref[T:T + H, :] = jnp.where(i < nI - 1, don_ref[...] * sbn_ref[...], 0.0)

        @pl.when(i == 0)
        def _():
            dw_ref[...] = jnp.zeros_like(dw_ref)

        for k in range(K):
            s = jnp.zeros((1, C), F32)
            for r0 in range(0, T, RS):
                s = s + jnp.sum(dcv_ref[r0:r0 + RS, :] * ext_ref[base + k + r0:base + k + r0 + RS, :],
                                axis=0, keepdims=True)
            dw_ref[k:k + 1, :] += s
        for r0 in range(0, T, RS):
            cz = _taps(ext_ref, w_ref, K, base, r0, RS, slice(None))
            dz = _taps(dcv_ref, w_ref, K, 0, r0, RS, slice(None), reverse=True)
            dout_ref[r0:r0 + RS, 0:C] = (dz * sc_ref[r0:r0 + RS, :]).astype(BF16)
            dout_ref[r0:r0 + RS, C:2 * C] = (do_ref[r0:r0 + RS, :] * cz).astype(BF16)
            dout_ref[r0:r0 + RS, 2 * C:3 * C] = (dz * sx_ref[r0:r0 + RS, :]).astype(BF16)

    pb, nb_ = _prev_blk(T, H), _next_blk(T, H, S)
    cur = lambda cb: pl.BlockSpec((T, C), lambda i: (i, cb))
    prev = lambda cb: pl.BlockSpec((H, C), lambda i: (pb(i), cb))
    nxt = lambda cb: pl.BlockSpec((H, C), lambda i: (nb_(i), cb))
    return pl.pallas_call(
        body, name=name,
        out_shape=(jax.ShapeDtypeStruct((S, 3 * C), BF16), jax.ShapeDtypeStruct((8, C), F32)),
        grid=(nI,),
        in_specs=[cur(2), cur(3), cur(4), prev(2), prev(4), nxt(3), cur(1), nxt(1),
                  pl.BlockSpec(w.shape, lambda i: (0, 0))],
        out_specs=(pl.BlockSpec((T, 3 * C), lambda i: (i, 0)), pl.BlockSpec((8, C), lambda i: (0, 0))),
        scratch_shapes=[pltpu.VMEM((H + T, C), F32), pltpu.VMEM((T + H, C), F32)],
        compiler_params=_params("arbitrary"))(rest, rest, rest, rest, rest, rest, dcs, dcs, w)


FFN_ROWS = 256
FFN_HALO = 16


def _shift_mats(T):
    r = lax.broadcasted_iota(jnp.int32, (T, T), 0)
    c = lax.broadcasted_iota(jnp.int32, (T, T), 1)
    down = [(r == c + k).astype(BF16) for k in (1, 2)]
    up = [(c == r + k).astype(BF16) for k in (1, 2)]
    return down, up


def _edge_rows(strip, shift, first):
    sub = lax.broadcasted_iota(jnp.int32, strip.shape, 0)
    if first:
        return jnp.where(sub < shift, pltpu.roll(strip, shift, 0), 0.0)
    return jnp.where(sub >= 8 - shift, pltpu.roll(strip, 8 - shift, 0), 0.0)


def _conv3_tile(x_ref, prev_ref, w_ref, b_ref, down, has_prev, u_ref, xm_refs=None):
    T = x_ref.shape[0]
    x = x_ref[...]
    xm1 = jnp.dot(down[0], x, preferred_element_type=F32)
    xm2 = jnp.dot(down[1], x, preferred_element_type=F32)
    u_ref[...] = b_ref[...] + w_ref[0:1, :] * xm2 + w_ref[1:2, :] * xm1 + w_ref[2:3, :] * x.astype(F32)
    tail = jnp.where(has_prev, prev_ref[...].astype(F32)[FFN_HALO - 8:, :], 0.0)
    p1, p2 = _edge_rows(tail, 1, True), _edge_rows(tail, 2, True)
    u_ref[0:8, :] += w_ref[0:1, :] * p2 + w_ref[1:2, :] * p1
    if xm_refs is not None:
        xm_refs[0][...] = xm1
        xm_refs[1][...] = xm2
        xm_refs[0][0:8, :] += p1
        xm_refs[1][0:8, :] += p2


def _ffn_fwd(hu, w, b, name):
    S, F2 = hu.shape
    Fd = F2 // 2
    K = w.shape[0]
    assert K == 3
    T = min(FFN_ROWS, S)
    tc = _tile(Fd, 512)
    nJ = Fd // tc

    def body(g_ref, v_ref, gp_ref, vp_ref, wg_ref, wv_ref, bg_ref, bv_ref, o_ref, ug_ref, uv_ref):
        i = pl.program_id(1)
        down, _ = _shift_mats(T)
        _conv3_tile(g_ref, gp_ref, wg_ref, bg_ref, down, i > 0, ug_ref)
        _conv3_tile(v_ref, vp_ref, wv_ref, bv_ref, down, i > 0, uv_ref)
        ug = ug_ref[...]
        o_ref[...] = (ug * _sigmoid(ug) * uv_ref[...]).astype(BF16)

    pb = _prev_blk(T, FFN_HALO)
    cur = lambda off: pl.BlockSpec((T, tc), lambda j, i: (i, j + off))
    prev = lambda off: pl.BlockSpec((FFN_HALO, tc), lambda j, i: (pb(i), j + off))
    wsp = lambda off: pl.BlockSpec((K, tc), lambda j, i: (0, j + off))
    bsp = lambda off: pl.BlockSpec((1, tc), lambda j, i: (0, j + off))
    return pl.pallas_call(
        body, name=name, out_shape=jax.ShapeDtypeStruct((S, Fd), BF16), grid=(nJ, S // T),
        in_specs=[cur(0), cur(nJ), prev(0), prev(nJ), wsp(0), wsp(nJ), bsp(0), bsp(nJ)],
        out_specs=pl.BlockSpec((T, tc), lambda j, i: (i, j)),
        scratch_shapes=[pltpu.VMEM((T, tc), F32), pltpu.VMEM((T, tc), F32)],
        compiler_params=_params("parallel", "parallel"))(hu, hu, hu, hu, w, w, b, b)


def _ffn_bwd(hu, dact, w, b, name):
    S, F2 = hu.shape
    Fd = F2 // 2
    K = w.shape[0]
    assert K == 3
    T = min(FFN_ROWS, S)
    tc = _tile(Fd, 512)
    nJ = Fd // tc
    nI = S // T

    def body(g_ref, v_ref, gp_ref, vp_ref, gn_ref, vn_ref, da_ref, dan_ref, wg_ref, wv_ref, bg_ref, bv_ref,
             dg_ref, dv_ref, dwg_ref, dwv_ref, ug_ref, uv_ref, g1_ref, g2_ref, v1_ref, v2_ref, dh_ref):
        i = pl.program_id(1)
        down, up = _shift_mats(T)
        _conv3_tile(g_ref, gp_ref, wg_ref, bg_ref, down, i > 0, ug_ref, (g1_ref, g2_ref))
        _conv3_tile(v_ref, vp_ref, wv_ref, bv_ref, down, i > 0, uv_ref, (v1_ref, v2_ref))

        @pl.when(i == 0)
        def _():
            dwg_ref[...] = jnp.zeros_like(dwg_ref)
            dwv_ref[...] = jnp.zeros_like(dwv_ref)

        def d_u(ug, uv, d_a):
            sg = _sigmoid(ug)
            return d_a * uv * (sg * (1.0 + ug * (1.0 - sg))), d_a * (ug * sg)

        dug, duv = d_u(ug_ref[...], uv_ref[...], da_ref[...].astype(F32))

        def next_rows(x_ref, xn_ref, w_ref, b_ref):
            strip = jnp.concatenate([x_ref[T - FFN_HALO:, :].astype(F32)[FFN_HALO - 8:, :],
                                     xn_ref[...].astype(F32)[0:8, :]], axis=0)
            return (b_ref[...] + w_ref[0:1, :] * strip[6:14, :] + w_ref[1:2, :] * strip[7:15, :]
                    + w_ref[2:3, :] * strip[8:16, :])

        d_an = jnp.where(i < nI - 1, dan_ref[...].astype(F32)[0:8, :], 0.0)
        dug_n, duv_n = d_u(next_rows(g_ref, gn_ref, wg_ref, bg_ref), next_rows(v_ref, vn_ref, wv_ref, bv_ref), d_an)

        for du, du_n, x_ref, x1_ref, x2_ref, w_ref, dw_ref, out_ref in (
                (dug, dug_n, g_ref, g1_ref, g2_ref, wg_ref, dwg_ref, dg_ref),
                (duv, duv_n, v_ref, v1_ref, v2_ref, wv_ref, dwv_ref, dv_ref)):
            dw_ref[0:1, :] += jnp.sum(du * x2_ref[...], axis=0, keepdims=True)
            dw_ref[1:2, :] += jnp.sum(du * x1_ref[...], axis=0, keepdims=True)
            dw_ref[2:3, :] += jnp.sum(du * x_ref[...].astype(F32), axis=0, keepdims=True)
            dw_ref[3:4, :] += jnp.sum(du, axis=0, keepdims=True)
            dub = du.astype(BF16)
            dh_ref[...] = (w_ref[2:3, :] * du + w_ref[1:2, :] * jnp.dot(up[0], dub, preferred_element_type=F32)
                           + w_ref[0:1, :] * jnp.dot(up[1], dub, preferred_element_type=F32))
            nxt = du_n.astype(BF16).astype(F32)
            dh_ref[T - 8:, :] += w_ref[1:2, :] * _edge_rows(nxt, 1, False) + w_ref[0:1, :] * _edge_rows(nxt, 2, False)
            out_ref[...] = dh_ref[...].astype(BF16)

    pb, nbb = _prev_blk(T, FFN_HALO), _next_blk(T, FFN_HALO, S)
    cur = lambda off: pl.BlockSpec((T, tc), lambda j, i: (i, j + off))
    prev = lambda off: pl.BlockSpec((FFN_HALO, tc), lambda j, i: (pb(i), j + off))
    nxt = lambda off: pl.BlockSpec((FFN_HALO, tc), lambda j, i: (nbb(i), j + off))
    wsp = lambda off: pl.BlockSpec((K, tc), lambda j, i: (0, j + off))
    bsp = lambda off: pl.BlockSpec((1, tc), lambda j, i: (0, j + off))
    half = jax.ShapeDtypeStruct((S, Fd), BF16)
    dws = jax.ShapeDtypeStruct((8, Fd), F32)
    tile = pltpu.VMEM((T, tc), F32)
    return pl.pallas_call(
        body, name=name, out_shape=(half, half, dws, dws), grid=(nJ, nI),
        in_specs=[cur(0), cur(nJ), prev(0), prev(nJ), nxt(0), nxt(nJ), cur(0), nxt(0),
                  wsp(0), wsp(nJ), bsp(0), bsp(nJ)],
        out_specs=(pl.BlockSpec((T, tc), lambda j, i: (i, j)), pl.BlockSpec((T, tc), lambda j, i: (i, j)),
                   pl.BlockSpec((8, tc), lambda j, i: (0, j)), pl.BlockSpec((8, tc), lambda j, i: (0, j))),
        scratch_shapes=[tile] * 7,
        compiler_params=_params("parallel", "arbitrary"))(hu, hu, hu, hu, hu, hu, dact, dact, w, w, b, b)


def _position():
    return lax.axis_index("x"), lax.axis_index("y"), lax.axis_index("c")


def _slot(px, py, pc):
    return 4 * px + 2 * py + pc


def _gather_copies(x_ref, out_ref, sems, r, starting=False):
    send_sems, recv_sems, local_sems = sems
    px, py, pc = _position()
    me, sibling = (px, py, pc), (px, py, 1 - pc)
    chips = [(1 - px, py), (px, 1 - py), (1 - px, 1 - py)]

    def copy(k, block, to, src=None):
        dst = out_ref.at[_slot(*block)]
        return pltpu.make_async_remote_copy(
            src_ref=dst if src is None else src, dst_ref=dst,
            send_sem=send_sems.at[7 * r + k], recv_sem=recv_sems.at[7 * r + k], device_id=to, device_id_type=MESH)

    mine = pltpu.make_async_copy(x_ref, out_ref.at[_slot(*me)], local_sems.at[r])
    first = [copy(0, me, sibling, src=x_ref)] + [copy(1 + n, me, (*chip, pc), src=x_ref)
                                                  for n, chip in enumerate(chips)]
    if starting:
        return mine, first
    passed = [copy(4 + n, (*chip, pc), sibling) for n, chip in enumerate(chips)]
    landed = [copy(1 + n, (*chip, pc), me) for n, chip in enumerate(chips)]
    from_sibling = [copy(0, sibling, me)] + [copy(4 + n, (*chip, 1 - pc), me) for n, chip in enumerate(chips)]
    return mine, first, passed, landed, from_sibling


def _scatter_copies(g_ref, out_ref, sems, r, starting=False):
    send_sems, recv_sems, local_sems = sems
    px, py, pc = _position()
    me = _slot(px, py, pc)
    mine = pltpu.make_async_copy(g_ref.at[me], out_ref.at[me], local_sems.at[r])
    peers = [(px ^ fx, py ^ fy, pc ^ fc) for fx, fy, fc in PEER_FLIPS]

    def copy(k, peer, src_slot, dst_slot):
        return pltpu.make_async_remote_copy(
            src_ref=g_ref.at[src_slot], dst_ref=out_ref.at[dst_slot],
            send_sem=send_sems.at[7 * r + k], recv_sem=recv_sems.at[7 * r + k], device_id=peer, device_id_type=MESH)

    sends = [copy(k, peer, _slot(*peer), me) for k, peer in enumerate(peers)]
    if starting:
        return mine, sends
    arrivals = [copy(k, peer, me, _slot(*peer)) for k, peer in enumerate(peers)]
    return mine, sends, arrivals


def _rider_start(kind, in_ref, out_ref, sems, r):
    if kind == "gather":
        mine, first = _gather_copies(in_ref, out_ref, sems, r, starting=True)
    else:
        mine, first = _scatter_copies(in_ref, out_ref, sems, r, starting=True)
    mine.start()
    for cp in first:
        cp.start()


def _rider_finish(kind, in_ref, out_ref, sems, r):
    if kind == "gather":
        mine, first, passed, landed, from_sibling = _gather_copies(in_ref, out_ref, sems, r)
        for cp, fwd in zip(landed, passed):
            cp.wait_recv()
            fwd.start()
        for cp in from_sibling:
            cp.wait_recv()
        for cp in first + passed:
            cp.wait_send()
    else:
        mine, sends, arrivals = _scatter_copies(in_ref, out_ref, sems, r)
        for cp in arrivals:
            cp.wait_recv()
        for cp in sends:
            cp.wait_send()
    mine.wait()


def _all_gather(x, in_vmem, name):
    space = pltpu.VMEM if in_vmem else pl.ANY

    def body(x_ref, out_ref, send_sems, recv_sems, local_sems):
        sems = (send_sems, recv_sems, local_sems)
        _rider_start("gather", x_ref, out_ref, sems, 0)
        _rider_finish("gather", x_ref, out_ref, sems, 0)

    return pl.pallas_call(
        body, name=name, out_shape=jax.ShapeDtypeStruct((N_DEV,) + x.shape, x.dtype),
        in_specs=[pl.BlockSpec(memory_space=space)], out_specs=pl.BlockSpec(memory_space=space),
        scratch_shapes=[pltpu.SemaphoreType.DMA((7,)), pltpu.SemaphoreType.DMA((7,)), pltpu.SemaphoreType.DMA((1,))],
        compiler_params=pltpu.CompilerParams(vmem_limit_bytes=VMEM_LIMIT_BYTES),
    )(x)


def _adam_sum(stage, w, m, v, layer, prev, name):
    n = stage.shape[0]
    L, R, C = w.shape
    tr = R if R * C <= 256 * 1024 else _row_tile(R, C)
    c1 = 1.0 / (1.0 - ADAM_B1 ** ADAM_STEP)
    c2 = 1.0 / (1.0 - ADAM_B2 ** ADAM_STEP)

    def body(*refs):
        st_ref, w_ref, m_ref, v_ref = refs[:4]
        g_ref, d_ref, nm_ref, nv_ref = refs[-4:]
        g = st_ref[0].astype(F32)
        for s in range(1, n):
            g = g + st_ref[s].astype(F32)
        wv = w_ref[0]
        mn = ADAM_B1 * m_ref[0] + (1.0 - ADAM_B1) * g
        vn = ADAM_B2 * v_ref[0] + (1.0 - ADAM_B2) * (g * g)
        g_ref[0] = g
        nm_ref[0] = mn
        nv_ref[0] = vn
        d_ref[0] = -ADAM_LR * ((mn * c1) / (jnp.sqrt(vn * c2) + ADAM_EPS) + ADAM_WD * wv)

    lay = pl.BlockSpec((1, tr, C), lambda i: (layer, i, 0))
    in_specs = [pl.BlockSpec((n, tr, C), lambda i: (0, i, 0)), lay, lay, lay]
    ins = [stage, w, m, v]
    aliases = {}
    if prev is not None:
        in_specs += [pl.BlockSpec(memory_space=pl.ANY)] * 4
        ins += list(prev)
        aliases = {4: 0, 5: 1, 6: 2, 7: 3}
    shp = jax.ShapeDtypeStruct((L, R, C), F32)
    return pl.pallas_call(
        body, name=name, out_shape=(shp, shp, shp, shp), grid=(R // tr,),
        in_specs=in_specs, out_specs=(lay, lay, lay, lay), input_output_aliases=aliases,
        compiler_params=_params("parallel"))(*ins)


def _row_tile(R, C):
    cpad = -(-C // LANES) * LANES
    want = max(16, (256 * 1024) // cpad)
    best = 16
    for t in range(16, R + 1, 16):
        if R % t == 0 and t <= want:
            best = t
    return best


def _sum_slabs(st, name):
    n, R, C = st.shape
    tr = R if n * R * C * 4 <= (12 << 20) else _row_tile(R, C)

    def body(st_ref, o_ref):
        g = st_ref[0]
        for s in range(1, n):
            g = g + st_ref[s]
        o_ref[...] = g

    return pl.pallas_call(
        body, name=name, out_shape=jax.ShapeDtypeStruct((R, C), F32), grid=(R // tr,),
        in_specs=[pl.BlockSpec((n, tr, C), lambda i: (0, i, 0))], out_specs=pl.BlockSpec((tr, C), lambda i: (i, 0)),
        compiler_params=_params("parallel"))(st)


def _pack(arrs):
    flat = [a.reshape(-1).astype(F32) for a in arrs]
    sizes = [f.shape[0] for f in flat]
    total = sum(sizes)
    padded = -(-total // (16 * LANES)) * (16 * LANES)
    if padded > total:
        flat.append(jnp.zeros((padded - total,), F32))
    return jnp.concatenate(flat).reshape(padded // LANES, LANES), (sizes, [a.shape for a in arrs])


def _unpack(packed, layout, lead=()):
    sizes, shapes = layout
    flat = packed.reshape(lead + (-1,))
    out, off = [], 0
    for sz, shp in zip(sizes, shapes):
        out.append(flat[..., off:off + sz].reshape(lead + tuple(shp)))
        off += sz
    return out


class _NoComm:
    col_slabs = None

    def __init__(self, wts):
        self.wts, self.grads = wts, {}

    def weight(self, l, name):
        return self.wts[l][name]

    def gather_rider(self, l, names):
        return None

    def scatter_rider(self, name, l, g):
        self.grads[(name, l)] = g
        return None


def _local_step(x, tgt, ada, mix_norm_g, comm, b_forget, conf_dw_w, conf_dw_b, conf_ln_g, conf_ln_b, sc_dw_w,
                ffn_norm_g, ffn_dw_w, ffn_dw_b, final_norm_g):
    S, D = x.shape
    L = ada.shape[0]

    def mm_gather(a, b, dtype, name, l_next, names):
        rider = comm.gather_rider(l_next, names) if l_next < L else None
        if rider is None:
            return _matmul(a, b, dtype, name=name)
        out, got = _matmul(a, b, dtype, name="cm_" + name, rider=rider)
        comm.gathered(l_next, names, got)
        return out

    def mm_scatter(a, b, dtype, name, wname, l, g):
        rider = comm.scatter_rider(wname, l, g)
        if rider is None:
            return _matmul(a, b, dtype, name=name, b_transposed=True)
        out, got = _matmul(a, b, dtype, name="cm_" + name, rider=rider, b_transposed=True)
        comm.scattered(wname, l, got[0])
        return out

    H = b_forget.shape[1]
    DA = H * HEAD_DIM
    C = conf_dw_b.shape[1]
    NQ = 3 * DA
    NR = 5 * C + LANES
    fblk = (5 * C) // LANES
    row = lambda a: a.reshape(1, -1)
    adav = ada.reshape(L, N_ADA, 1, D)

    saved = []
    xcur, delta, gate = x, None, None
    for l in range(L):
        sh_m, sc_m, g_m, sh_f, sc_f, g_f = [adav[l, n] for n in range(N_ADA)]
        w = functools.partial(comm.weight, l)
        x1, h1 = _site_fwd(xcur, delta, gate, row(mix_norm_g[l]), sc_m, sh_m, name="site_fwd_mix")
        first = 0 if l == 0 else L
        qkv = mm_gather(h1, w("w_in_perm")[:, :NQ], BF16, "mm_qkv", first, ("w_up", "w_down"))
        rest = mm_gather(h1, w("w_in_perm")[:, NQ:], F32, "mm_rest", first, ("w_out",))
        bpad = jnp.zeros((1, LANES), F32).at[0, :H].set(b_forget[l])
        Fc = _fgate_fwd(rest, bpad, fblk, name="fgate_fwd")
        nf = -LOG2E * jnp.transpose(Fc[:, :H])
        attn, attn32, lse = _attn_fwd(qkv, jnp.broadcast_to(nf[:, :, None], (H, S, LANES)), H, name="attn_fwd")
        conf = _conf_fwd(rest, conf_dw_w[l], row(conf_dw_b[l]), row(conf_ln_g[l]), row(conf_ln_b[l]), name="conf_fwd")
        sconv = _sconv_fwd(rest, sc_dw_w[l], name="sconv_fwd")
        mixcat = jnp.concatenate([attn, conf, sconv], axis=1)
        mixed = _matmul(mixcat, w("w_out"), F32, name="mm_out")
        x2, h2 = _site_fwd(x1, mixed, g_m, row(ffn_norm_g[l]), sc_f, sh_f, name="site_fwd_ffn")
        hu = mm_gather(h2, w("w_up"), BF16, "mm_up", l + 1, ("w_up", "w_down"))
        act = _ffn_fwd(hu, ffn_dw_w[l], row(ffn_dw_b[l]), name="ffn_fwd")
        ffn_out = mm_gather(act, w("w_down"), F32, "mm_down", l + 1, ("w_in", "w_out"))
        saved.append(dict(x1=x1, h1=h1, qkv=qkv, rest=rest, bpad=bpad, nf=nf, attn32=attn32, lse=lse, mixcat=mixcat,
                          mixed=mixed, x2=x2, h2=h2, hu=hu, act=act, ffn_out=ffn_out))
        xcur, delta, gate = x2, ffn_out, g_f

    loss_lanes, dx, d_delta, d_gate, d_gfin = _final_fwd_bwd(xcur, delta, gate, row(final_norm_g), tgt, name="final")
    loss = (0.5 / D) * jnp.sum(loss_lanes)

    grads = dict(final_norm_g=d_gfin[0], ada=[None] * L, mix_norm_g=[None] * L, ffn_norm_g=[None] * L,
                 b_forget=[None] * L, conf_dw_w=[None] * L, conf_dw_b=[None] * L, conf_ln_g=[None] * L,
                 conf_ln_b=[None] * L, sc_dw_w=[None] * L, ffn_dw_w=[None] * L, ffn_dw_b=[None] * L)
    K3 = ffn_dw_w.shape[1]
    for l in reversed(range(L)):
        sv, w = saved[l], functools.partial(comm.weight, l)
        sh_m, sc_m, g_m, sh_f, sc_f, g_f = [adav[l, n] for n in range(N_ADA)]
        d_gf = d_gate
        g_down = _matmul_tn(sv["act"], d_delta, BF16, name="mm_dw_down")
        dact = mm_scatter(d_delta, w("w_down"), BF16, "mm_dact", "w_down", l, g_down)
        dhu_g, dhu_v, dwg, dwv = _ffn_bwd(sv["hu"], dact, ffn_dw_w[l], row(ffn_dw_b[l]), name="ffn_bwd")
        grads["ffn_dw_w"][l] = jnp.concatenate([dwg[:K3], dwv[:K3]], axis=1)
        grads["ffn_dw_b"][l] = jnp.concatenate([dwg[K3], dwv[K3]])
        dhu = (dhu_g, dhu_v)
        g_up = _matmul_tn(sv["h2"], dhu, BF16, name="mm_dw_up", col_slabs=comm.col_slabs)
        dh2 = mm_scatter(dhu, w("w_up"), F32, "mm_dh2", "w_up", l, g_up)
        dx, d_sh_f, d_a_f, d_mixed, d_gm = _site_bwd(sv["x2"], dh2, dx, row(ffn_norm_g[l]), sc_f,
                                                      sh_f, sv["mixed"], g_m, name="site_bwd_ffn")
        grads["ffn_norm_g"][l] = (d_a_f * (1.0 + sc_f))[0]
        d_sc_f = d_a_f * row(ffn_norm_g[l])
        g_out = _matmul_tn(sv["mixcat"], d_mixed, BF16, name="mm_dw_out")
        dattn = mm_scatter(d_mixed, w("w_out")[:DA], BF16, "mm_dattn", "w_out", l, g_out)
        dcs = _matmul(d_mixed, w("w_out")[DA:], F32, name="mm_dcs", b_transposed=True)
        delta_a = _blocked_rows(_attn_delta(sv["attn32"], dattn, H, name="attn_delta")[:, :, 0], min(ATTN_TILE, S))
        nfb = jnp.broadcast_to(sv["nf"][:, :, None], (H, S, LANES))
        dq, dk, dv, drow, dnf = _attn_bwd(sv["qkv"], nfb, dattn, sv["lse"], delta_a, H, name="attn_bwd")
        dF = jnp.zeros((S, LANES), F32).at[:, :H].set(jnp.transpose(drow.reshape(H, S) - dnf[:, :, 0]))
        dfl, dbf = _fgate_bwd(sv["rest"], sv["bpad"], dF, fblk, name="fgate_bwd")
        grads["b_forget"][l] = dbf[0, :H]
        dcvg, dcw, dcvec = _conf_bwd(sv["rest"], dcs, conf_dw_w[l], row(conf_dw_b[l]), row(conf_ln_g[l]),
                                     row(conf_ln_b[l]), name="conf_bwd")
        grads["conf_dw_w"][l] = dcw[:conf_dw_w.shape[1]]
        grads["conf_dw_b"][l], grads["conf_ln_g"][l], grads["conf_ln_b"][l] = dcvec[0], dcvec[1], dcvec[2]
        dsc3, dsw = _sconv_bwd(sv["rest"], dcs, sc_dw_w[l], name="sconv_bwd")
        grads["sc_dw_w"][l] = dsw[:sc_dw_w.shape[1]]
        dproj = jnp.concatenate([dq, dk, dv, dcvg, dsc3, dfl], axis=1)
        g_in = _matmul_tn(sv["h1"], dproj, BF16, name="mm_dw_in")
        dh1 = mm_scatter(dproj, w("w_in_perm"), F32, "mm_dh1", "w_in_perm", l, g_in)
        if l > 0:
            pv = saved[l - 1]
            g_f_prev = adav[l - 1, 5]
            dx, d_sh_m, d_a_m, d_delta, d_gate = _site_bwd(sv["x1"], dh1, dx, row(mix_norm_g[l]), sc_m, sh_m,
                                                           pv["ffn_out"], g_f_prev, name="site_bwd_mix")
        else:
            dx, d_sh_m, d_a_m = _site_bwd(sv["x1"], dh1, dx, row(mix_norm_g[l]), sc_m, sh_m, None, None,
                                          name="site_bwd_first")
        grads["mix_norm_g"][l] = (d_a_m * (1.0 + sc_m))[0]
        d_sc_m = d_a_m * row(mix_norm_g[l])
        grads["ada"][l] = jnp.concatenate([d_sh_m, d_sc_m, d_gm, d_sh_f, d_sc_f, d_gf], axis=1)[0]
    return loss, dx, grads


def kernel(x, c, ada_w, ada_b, mix_norm_g, w_in, b_forget, conf_dw_w, conf_dw_b, conf_ln_g, conf_ln_b, sc_dw_w, w_out, ffn_norm_g, w_up, ffn_dw_w, ffn_dw_b, w_down, final_norm_g, loss_target, m_ada_w, m_ada_b, m_mix_norm_g, m_w_in, m_b_forget, m_conf_dw_w, m_conf_dw_b, m_conf_ln_g, m_conf_ln_b, m_sc_dw_w, m_w_out, m_ffn_norm_g, m_w_up, m_ffn_dw_w, m_ffn_dw_b, m_w_down, m_final_norm_g, v_ada_w, v_ada_b, v_mix_norm_g, v_w_in, v_b_forget, v_conf_dw_w, v_conf_dw_b, v_conf_ln_g, v_conf_ln_b, v_sc_dw_w, v_w_out, v_ffn_norm_g, v_w_up, v_ffn_dw_w, v_ffn_dw_b, v_w_down, v_final_norm_g):
    L, D, ada_loc = ada_w.shape
    S = x.shape[1]
    H = b_forget.shape[1]
    DA = H * HEAD_DIM
    C = conf_dw_b.shape[1]
    in_loc = w_in.shape[2]
    IN = in_loc * N_DEV
    px, py, pc = _position()
    me = _slot(px, py, pc)

    pk, lay = _pack([c, conf_dw_w, sc_dw_w, ffn_dw_w])
    gathered = _all_gather(pk, True, name="ag_small_fwd")
    c_all, cw_all, sw_all, fw_all = _unpack(gathered, lay, lead=(N_DEV,))
    c_all = c_all[:, 0]
    unshard = lambda a: jnp.moveaxis(a, 0, 2).reshape(a.shape[1], a.shape[2], -1)
    conf_w_full, sc_w_full, ffn_w_full = unshard(cw_all), unshard(sw_all), unshard(fw_all)
    c_act = c_all * jax.nn.sigmoid(c_all)
    c_act16 = jnp.zeros((16, D), F32).at[:N_DEV].set(c_act).astype(BF16)
    ada_cols = jnp.stack([_matmul(c_act16, ada_w[l].astype(BF16), F32, name="mm_ada")[:N_DEV] for l in range(L)])
    ada_g = _all_gather(ada_cols.reshape(L * N_DEV, ada_loc), True, name="ag_ada")
    ada_mine = lax.dynamic_index_in_dim(ada_g.reshape(N_DEV, L, N_DEV, ada_loc), me, axis=2, keepdims=False)
    ada = jnp.moveaxis(ada_mine, 0, 1).reshape(L, N_DEV * ada_loc) + ada_b

    NQ = 3 * DA
    PR = NQ + 5 * C
    shards = dict(w_in=w_in.astype(BF16), w_out=w_out.astype(BF16), w_up=w_up.astype(BF16),
                  w_down=w_down.astype(BF16))

    def shard_cols(g):
        return jnp.moveaxis(g.reshape(g.shape[0], N_DEV, -1), 1, 0)

    def shard_rows(g):
        return g.reshape(N_DEV, -1, g.shape[1])

    class MeshComm:
        col_slabs = N_DEV

        def __init__(self):
            self.got = {0: {"w_in": _all_gather(shards["w_in"][0], False, name="ag_w_in")}}
            self.full, self.stage = {}, {}

        def weight(self, l, name):
            if (l, name) not in self.full:
                g = self.got[l]
                if name == "w_in_perm":
                    wi = jnp.moveaxis(g["w_in"], 0, 1).reshape(D, IN)
                    full = jnp.concatenate([wi[:, :NQ], wi[:, NQ + H:], wi[:, NQ:NQ + H],
                                            jnp.zeros((D, LANES - H), BF16)], axis=1)
                elif name == "w_up":
                    full = jnp.moveaxis(g["w_up"], 0, 1).reshape(D, -1)
                else:
                    full = g[name].reshape(-1, D)
                self.full[(l, name)] = full
            return self.full[(l, name)]

        def gather_rider(self, l, names):
            return "gather", [shards[n][l] for n in names]

        def gathered(self, l, names, outs):
            self.got.setdefault(l, {}).update(zip(names, outs))

        def scatter_rider(self, name, l, g):
            if name == "w_in_perm":
                slabs = shard_cols(jnp.concatenate([g[:, :NQ], g[:, PR:PR + H], g[:, NQ:PR]], axis=1))
            elif name == "w_up":
                slabs = g
            else:
                slabs = shard_rows(g)
            return "scatter", [slabs]

        def scattered(self, name, l, out):
            self.stage[(name, l)] = out

    comm = MeshComm()
    loss_loc, dx, gr = _local_step(x[0], loss_target[0], ada, mix_norm_g, comm, b_forget, conf_w_full, conf_dw_b,
                                   conf_ln_g, conf_ln_b, sc_w_full, ffn_norm_g, ffn_w_full, ffn_dw_b, final_norm_g)
    loss = lax.psum(loss_loc, ("x", "y", "c"))

    small_names = ["ada", "mix_norm_g", "ffn_norm_g", "b_forget", "conf_dw_b", "conf_ln_g", "conf_ln_b",
                   "ffn_dw_b", "conf_dw_w", "sc_dw_w", "ffn_dw_w"]
    pk, lay = _pack([jnp.stack(gr[n]) for n in small_names] + [gr["final_norm_g"]])
    parts = _all_gather(pk, True, name="ag_small_bwd")
    tot = _unpack(_sum_slabs(parts, name="sum_small"), lay)
    g_small = dict(zip(small_names + ["final_norm_g"], tot))
    d_ada_all = _unpack(parts, lay, lead=(N_DEV,))[0]
    my_cols = lambda a, n: lax.dynamic_slice_in_dim(a, me * n, n, axis=a.ndim - 1)

    c_act_t = jnp.zeros((D, LANES), F32).at[:, :N_DEV].set(jnp.transpose(c_act)).astype(BF16)
    res = None
    for l in range(L):
        d_loc = jnp.zeros((LANES, ada_loc), F32).at[:N_DEV].set(my_cols(d_ada_all[:, l], ada_loc)).astype(BF16)
        g_l = _matmul(c_act_t, d_loc, F32, name="mm_dada")
        res = _adam_sum(g_l[None], ada_w, m_ada_w, v_ada_w, l, res, name="adam_ada_w")
    out_ada_w = res

    big = {}
    for nm, key, wq, mq, vq in (("w_down", "w_down", w_down, m_w_down, v_w_down), ("w_up", "w_up", w_up, m_w_up, v_w_up),
                                ("w_out", "w_out", w_out, m_w_out, v_w_out), ("w_in", "w_in_perm", w_in, m_w_in, v_w_in)):
        res = None
        for l in reversed(range(L)):
            res = _adam_sum(comm.stage[(key, l)], wq, mq, vq, l, res, name="adam_" + nm)
        big[nm] = res

    K31, K3 = conf_dw_w.shape[1], sc_dw_w.shape[1]
    sm = [("ada_b", ada_b, m_ada_b, v_ada_b, g_small["ada"]),
          ("mix_norm_g", mix_norm_g, m_mix_norm_g, v_mix_norm_g, g_small["mix_norm_g"]),
          ("b_forget", b_forget, m_b_forget, v_b_forget, g_small["b_forget"]),
          ("conf_dw_w", conf_dw_w, m_conf_dw_w, v_conf_dw_w, my_cols(g_small["conf_dw_w"], conf_dw_w.shape[2])),
          ("conf_dw_b", conf_dw_b, m_conf_dw_b, v_conf_dw_b, g_small["conf_dw_b"]),
          ("conf_ln_g", conf_ln_g, m_conf_ln_g, v_conf_ln_g, g_small["conf_ln_g"]),
          ("conf_ln_b", conf_ln_b, m_conf_ln_b, v_conf_ln_b, g_small["conf_ln_b"]),
          ("sc_dw_w", sc_dw_w, m_sc_dw_w, v_sc_dw_w, my_cols(g_small["sc_dw_w"], sc_dw_w.shape[2])),
          ("ffn_norm_g", ffn_norm_g, m_ffn_norm_g, v_ffn_norm_g, g_small["ffn_norm_g"]),
          ("ffn_dw_w", ffn_dw_w, m_ffn_dw_w, v_ffn_dw_w, my_cols(g_small["ffn_dw_w"], ffn_dw_w.shape[2])),
          ("ffn_dw_b", ffn_dw_b, m_ffn_dw_b, v_ffn_dw_b, g_small["ffn_dw_b"]),
          ("final_norm_g", final_norm_g, m_final_norm_g, v_final_norm_g, g_small["final_norm_g"])]
    pw, lay = _pack([t[1] for t in sm])
    pm, _ = _pack([t[2] for t in sm])
    pv, _ = _pack([t[3] for t in sm])
    pg, _ = _pack([t[4] for t in sm])
    sres = _adam_sum(pg[None], pw[None], pm[None], pv[None], 0, None, name="adam_small")
    s_g, s_d, s_m, s_v = [dict(zip([t[0] for t in sm], _unpack(r[0], lay))) for r in sres]

    def pick(idx, name):
        if name == "ada_w":
            return out_ada_w[idx]
        if name in big:
            return big[name][idx]
        return (s_g, s_d, s_m, s_v)[idx][name]

    order = ["ada_w", "ada_b", "mix_norm_g", "w_in", "b_forget", "conf_dw_w", "conf_dw_b", "conf_ln_g", "conf_ln_b",
             "sc_dw_w", "w_out", "ffn_norm_g", "w_up", "ffn_dw_w", "ffn_dw_b", "w_down", "final_norm_g"]
    outs = [loss, dx[None]]
    for idx in range(4):
        outs += [pick(idx, n) for n in order]
    return tuple(outs)
```

```python
import functools

import jax
import jax.numpy as jnp
from jax import lax
from jax.experimental import pallas as pl
from jax.experimental.pallas import tpu as pltpu

F32 = jnp.float32
BF16 = jnp.bfloat16
RMS_EPS = 1e-6
LN_EPS = 1e-5
HEAD_DIM = 128
N_ADA = 6
ADAM_LR = 0.001
ADAM_B1 = 0.9
ADAM_B2 = 0.999
ADAM_EPS = 1e-08
ADAM_WD = 0.01
ADAM_STEP = 10
N_DEV = 8
LANES = 128
VMEM_LIMIT_BYTES = 56 * 1024 * 1024
MM_TILE = 1024
MM_TILE_WIDE = 1536
MM_TILE_N_MAX = 2816
MXU_WIDTH = 256
MM_VMEM_BUDGET = 48 * 1024 * 1024
MESH = pl.DeviceIdType.MESH
PEER_FLIPS = ((0, 0, 1), (1, 0, 0), (0, 1, 0), (1, 1, 0), (1, 0, 1), (0, 1, 1), (1, 1, 1))


def _params(*sem):
    return pltpu.CompilerParams(dimension_semantics=sem, vmem_limit_bytes=VMEM_LIMIT_BYTES)


def _tile(n, cap):
    if n <= cap:
        return n
    for t in range(cap - cap % LANES, 0, -LANES):
        if n % t == 0:
            return t
    raise ValueError(f"no tile for {n}")


def _mm_tile(n):
    t = _tile(n, MM_TILE)
    return t if t == min(n, MM_TILE) else _tile(n, MM_TILE_WIDE)


def _n_tile(n, vmem_bytes):
    for step in (MXU_WIDTH, LANES):
        for t in range(min(n, MM_TILE_N_MAX) // step * step, 0, -step):
            if n % t == 0 and vmem_bytes(t) <= MM_VMEM_BUDGET:
                return t
    return n


def _sigmoid(v):
    return jax.nn.sigmoid(v)


def _matmul(a, b, out_dtype, name, rider=None, b_transposed=False):
    a_parts = a if isinstance(a, tuple) else (a,)
    na = len(a_parts)
    M = a_parts[0].shape[0]
    K = sum(p.shape[1] for p in a_parts)
    N = b.shape[0] if b_transposed else b.shape[1]
    tm = _mm_tile(M)
    out_bytes = jnp.dtype(out_dtype).itemsize
    part_k = a_parts[0].shape[1]

    def blocks_bytes(tk_, tn_):
        return 4 * (na * tm * tk_ + tk_ * tn_) + (4 * tm * tn_ if K > tk_ else 0) + 2 * tm * tn_ * out_bytes

    for tk in [t for t in range(part_k, 0, -LANES) if part_k % t == 0]:
        tn = _n_tile(N, functools.partial(blocks_bytes, tk))
        if blocks_bytes(tk, tn) <= MM_VMEM_BUDGET and tn >= min(N, MM_TILE if b_transposed else MM_TILE // 2):
            break
    nk = K // tk
    half = part_k // tk
    grid = (M // tm, N // tn, nk)
    dims = _NT if b_transposed else (((1,), (0,)), ((), ()))
    kind, arrs = rider if rider is not None else (None, [])
    nr = len(arrs)

    def body(*refs):
        a_refs, refs = refs[:na], refs[na - 1:]
        a_ref, b_ref = a_refs[0], refs[1]
        r_in = refs[2:2 + nr]
        o_ref = refs[2 + nr]
        r_out = refs[3 + nr:3 + 2 * nr]
        rest = refs[3 + 2 * nr:]
        i, j, k = pl.program_id(0), pl.program_id(1), pl.program_id(2)
        if nr:
            sems = rest[-3:]

            @pl.when((i == 0) & (j == 0) & (k == 0))
            def _():
                for r in range(nr):
                    _rider_start(kind, r_in[r], r_out[r], sems, r)

        if nk == 1:
            o_ref[...] = lax.dot_general(a_ref[...], b_ref[...], dims, preferred_element_type=F32).astype(o_ref.dtype)
        else:
            acc_ref = rest[0]

            @pl.when(k == 0)
            def _():
                acc_ref[...] = jnp.zeros_like(acc_ref)

            def accumulate(part_ref):
                acc_ref[...] += lax.dot_general(part_ref[...], b_ref[...], dims, preferred_element_type=F32)

            if na == 1:
                accumulate(a_ref)
            else:
                pl.when(k < half)(lambda: accumulate(a_refs[0]))
                pl.when(k >= half)(lambda: accumulate(a_refs[1]))

            @pl.when(k == nk - 1)
            def _():
                o_ref[...] = acc_ref[...].astype(o_ref.dtype)

        if nr:
            @pl.when((i == grid[0] - 1) & (j == grid[1] - 1) & (k == nk - 1))
            def _():
                for r in range(nr):
                    _rider_finish(kind, r_in[r], r_out[r], sems, r)

    scratch = [] if nk == 1 else [pltpu.VMEM((tm, tn), F32)]
    hbm = pl.BlockSpec(memory_space=pl.ANY)
    if na == 1:
        a_specs = [pl.BlockSpec((tm, tk), lambda i, j, k: (i, k))]
    else:
        a_specs = [pl.BlockSpec((tm, tk), lambda i, j, k: (i, jnp.minimum(k, half - 1))),
                   pl.BlockSpec((tm, tk), lambda i, j, k: (i, jnp.maximum(k - half, 0)))]
    out_shape = jax.ShapeDtypeStruct((M, N), out_dtype)
    out_specs = pl.BlockSpec((tm, tn), lambda i, j, k: (i, j))
    if nr:
        scratch += [pltpu.SemaphoreType.DMA((7 * nr,)), pltpu.SemaphoreType.DMA((7 * nr,)),
                    pltpu.SemaphoreType.DMA((nr,))]
        out_shape = (out_shape,) + tuple(
            jax.ShapeDtypeStruct(x.shape if kind == "scatter" else (N_DEV,) + x.shape, x.dtype) for x in arrs)
        out_specs = (out_specs,) + (hbm,) * nr
    out = pl.pallas_call(
        body, name=name,
        out_shape=out_shape,
        grid=grid,
        in_specs=a_specs + [pl.BlockSpec((tn, tk), lambda i, j, k: (j, k)) if b_transposed
                            else pl.BlockSpec((tk, tn), lambda i, j, k: (k, j))] + [hbm] * nr,
        out_specs=out_specs,
        scratch_shapes=scratch,
        compiler_params=_params(*(("arbitrary",) * 3 if nr else ("parallel", "parallel", "arbitrary"))),
    )(*a_parts, b, *arrs)
    return (out[0], list(out[1:])) if nr else out


_TN = (((0,), (0,)), ((), ()))


def _matmul_tn(a, b, out_dtype, name, col_slabs=None):
    b_parts = b if isinstance(b, tuple) else (b,)
    S, M = a.shape
    N = sum(p.shape[1] for p in b_parts)
    tm = _mm_tile(M)
    out_bytes = jnp.dtype(out_dtype).itemsize

    def blocks_bytes(ts_, tn_):
        return 4 * (ts_ * tm + len(b_parts) * ts_ * tn_) + 4 * tm * tn_ + 2 * tm * tn_ * out_bytes

    for ts in (_tile(S, 2 * MM_TILE), _tile(S, MM_TILE)):
        tn = N // col_slabs if col_slabs else _n_tile(b_parts[0].shape[1], functools.partial(blocks_bytes, ts))
        if blocks_bytes(ts, tn) <= MM_VMEM_BUDGET and tn >= min(b_parts[0].shape[1], MM_TILE):
            break
    ns = S // ts
    half = b_parts[0].shape[1] // tn

    def body(*refs):
        a_ref, b_refs, (o_ref, acc_ref) = refs[0], refs[1:-2], refs[-2:]
        j, k = pl.program_id(1), pl.program_id(2)

        @pl.when(k == 0)
        def _():
            acc_ref[...] = jnp.zeros_like(acc_ref)

        def accumulate(b_ref):
            acc_ref[...] += lax.dot_general(a_ref[...], b_ref[...], _TN, preferred_element_type=F32)

        if len(b_refs) == 1:
            accumulate(b_refs[0])
        else:
            pl.when(j < half)(lambda: accumulate(b_refs[0]))
            pl.when(j >= half)(lambda: accumulate(b_refs[1]))

        @pl.when(k == ns - 1)
        def _():
            if col_slabs is None:
                o_ref[...] = acc_ref[...].astype(o_ref.dtype)
            else:
                o_ref[0] = acc_ref[...].astype(o_ref.dtype)

    if col_slabs is None:
        out_shape = jax.ShapeDtypeStruct((M, N), out_dtype)
        out_spec = pl.BlockSpec((tm, tn), lambda i, j, k: (i, j))
    else:
        out_shape = jax.ShapeDtypeStruct((col_slabs, M, tn), out_dtype)
        out_spec = pl.BlockSpec((1, tm, tn), lambda i, j, k: (j, i, 0))
    if len(b_parts) == 1:
        b_specs = [pl.BlockSpec((ts, tn), lambda i, j, k: (k, j))]
    else:
        b_specs = [pl.BlockSpec((ts, tn), lambda i, j, k: (jnp.where(j < half, k, ns - 1), jnp.minimum(j, half - 1))),
                   pl.BlockSpec((ts, tn), lambda i, j, k: (jnp.where(j < half, 0, k), jnp.maximum(j - half, 0)))]
    return pl.pallas_call(
        body, name=name,
        out_shape=out_shape,
        grid=(M // tm, N // tn, ns),
        in_specs=[pl.BlockSpec((ts, tm), lambda i, j, k: (k, i))] + b_specs,
        out_specs=out_spec,
        scratch_shapes=[pltpu.VMEM((tm, tn), F32)],
        compiler_params=_params("parallel", "parallel", "arbitrary"),
    )(a, *b_parts)


def _site_fwd(x, delta, gate, g, sc, sh, name):
    S, D = x.shape
    T = min(256, S)
    res = delta is not None

    def body(*refs):
        if res:
            x_ref, d_ref, gate_ref, g_ref, sc_ref, sh_ref, xo_ref, h_ref = refs
            xv = x_ref[...] + gate_ref[...] * d_ref[...]
            xo_ref[...] = xv
        else:
            x_ref, g_ref, sc_ref, sh_ref, h_ref = refs
            xv = x_ref[...]
        r = lax.rsqrt(jnp.mean(xv * xv, axis=-1, keepdims=True) + RMS_EPS)
        a = g_ref[...] * (1.0 + sc_ref[...])
        h_ref[...] = (xv * r * a + sh_ref[...]).astype(BF16)

    row = pl.BlockSpec((T, D), lambda i: (i, 0))
    vec = pl.BlockSpec((1, D), lambda i: (0, 0))
    if res:
        ins, in_specs = (x, delta, gate, g, sc, sh), [row, row, vec, vec, vec, vec]
        out_shape = (jax.ShapeDtypeStruct((S, D), F32), jax.ShapeDtypeStruct((S, D), BF16))
        out_specs = (row, row)
    else:
        ins, in_specs = (x, g, sc, sh), [row, vec, vec, vec]
        out_shape = jax.ShapeDtypeStruct((S, D), BF16)
        out_specs = row
    out = pl.pallas_call(body, name=name, out_shape=out_shape, grid=(S // T,), in_specs=in_specs,
                         out_specs=out_specs, compiler_params=_params("parallel"))(*ins)
    return out if res else (x, out)


def _site_bwd(x, dh, dres, g, sc, sh, delta, gate, name):
    S, D = x.shape
    T = min(256, S)
    res = delta is not None

    def body(*refs):
        if res:
            (x_ref, dh_ref, dres_ref, g_ref, sc_ref, delta_ref, gate_ref,
             dx_ref, dsh_ref, da_ref, dd_ref, dgate_ref) = refs
        else:
            x_ref, dh_ref, dres_ref, g_ref, sc_ref, dx_ref, dsh_ref, da_ref = refs
        i = pl.program_id(0)
        xv = x_ref[...]
        dhv = dh_ref[...]
        r = lax.rsqrt(jnp.mean(xv * xv, axis=-1, keepdims=True) + RMS_EPS)
        xh = xv * r
        dxh = dhv * (g_ref[...] * (1.0 + sc_ref[...]))
        dx = r * (dxh - xh * jnp.mean(dxh * xh, axis=-1, keepdims=True)) + dres_ref[...]
        dx_ref[...] = dx

        @pl.when(i == 0)
        def _():
            dsh_ref[...] = jnp.zeros_like(dsh_ref)
            da_ref[...] = jnp.zeros_like(da_ref)
            if res:
                dgate_ref[...] = jnp.zeros_like(dgate_ref)

        dsh_ref[...] += jnp.sum(dhv, axis=0, keepdims=True)
        da_ref[...] += jnp.sum(dhv * xh, axis=0, keepdims=True)
        if res:
            dd_ref[...] = (gate_ref[...] * dx).astype(BF16)
            dgate_ref[...] += jnp.sum(dx * delta_ref[...], axis=0, keepdims=True)

    row = pl.BlockSpec((T, D), lambda i: (i, 0))
    vec = pl.BlockSpec((1, D), lambda i: (0, 0))
    vshape = jax.ShapeDtypeStruct((1, D), F32)
    if res:
        ins, in_specs = (x, dh, dres, g, sc, delta, gate), [row, row, row, vec, vec, row, vec]
        out_shape = (jax.ShapeDtypeStruct((S, D), F32), vshape, vshape, jax.ShapeDtypeStruct((S, D), BF16), vshape)
        out_specs = (row, vec, vec, row, vec)
    else:
        ins, in_specs = (x, dh, dres, g, sc), [row, row, row, vec, vec]
        out_shape = (jax.ShapeDtypeStruct((S, D), F32), vshape, vshape)
        out_specs = (row, vec, vec)
    return pl.pallas_call(body, name=name, out_shape=out_shape, grid=(S // T,), in_specs=in_specs,
                          out_specs=out_specs, compiler_params=_params("arbitrary"))(*ins)


def _final_fwd_bwd(x, delta, gate, gfin, target, name):
    S, D = x.shape
    T = min(256, S)

    def body(x_ref, delta_ref, gate_ref, g_ref, t_ref, loss_ref, dx_ref, dd_ref, dgate_ref, dg_ref):
        i = pl.program_id(0)
        dl = delta_ref[...]
        xv = x_ref[...] + gate_ref[...] * dl
        r = lax.rsqrt(jnp.mean(xv * xv, axis=-1, keepdims=True) + RMS_EPS)
        xh = xv * r
        gv = g_ref[...]
        e = xh * gv - t_ref[...]
        dy = e * (1.0 / D)
        dxh = dy * gv
        dx = r * (dxh - xh * jnp.mean(dxh * xh, axis=-1, keepdims=True))
        dx_ref[...] = dx
        dd_ref[...] = (gate_ref[...] * dx).astype(BF16)

        @pl.when(i == 0)
        def _():
            loss_ref[...] = jnp.zeros_like(loss_ref)
            dgate_ref[...] = jnp.zeros_like(dgate_ref)
            dg_ref[...] = jnp.zeros_like(dg_ref)

        loss_ref[...] += jnp.sum(e * e, axis=0, keepdims=True)
        dgate_ref[...] += jnp.sum(dx * dl, axis=0, keepdims=True)
        dg_ref[...] += jnp.sum(dy * xh, axis=0, keepdims=True)

    row = pl.BlockSpec((T, D), lambda i: (i, 0))
    vec = pl.BlockSpec((1, D), lambda i: (0, 0))
    vshape = jax.ShapeDtypeStruct((1, D), F32)
    return pl.pallas_call(
        body, name=name,
        out_shape=(vshape, jax.ShapeDtypeStruct((S, D), F32), jax.ShapeDtypeStruct((S, D), BF16), vshape, vshape),
        grid=(S // T,), in_specs=[row, row, vec, vec, row], out_specs=(vec, row, row, vec, vec),
        compiler_params=_params("arbitrary"))(x, delta, gate, gfin, target)


def _split3(v):
    hi = v.astype(BF16)
    r1 = v - hi.astype(F32)
    mid = r1.astype(BF16)
    lo = (r1 - mid.astype(F32)).astype(BF16)
    return hi, mid, lo


def _tri_dot(tri, v):
    hi, mid, lo = _split3(v)
    d = functools.partial(jnp.dot, preferred_element_type=F32)
    return d(tri, hi) + d(tri, mid) + d(tri, lo)


def _fgate_fwd(rest, bpad, fblk, name):
    S = rest.shape[0]
    CH = min(256, S)
    nch = S // CH

    def body(f_ref, b_ref, o_ref):
        row = lax.broadcasted_iota(jnp.int32, (CH, CH), 0)
        col = lax.broadcasted_iota(jnp.int32, (CH, CH), 1)
        tri = (row >= col).astype(BF16)

        def step(ci, carry):
            rows = pl.ds(pl.multiple_of(ci * CH, CH), CH)
            z = f_ref[rows, :] + b_ref[...]
            lf = jnp.minimum(z, 0.0) - jnp.log(1.0 + jnp.exp(-jnp.abs(z)))
            o_ref[rows, :] = _tri_dot(tri, lf) + carry
            return carry + jnp.sum(lf, axis=0, keepdims=True)

        lax.fori_loop(0, nch, step, jnp.zeros((1, LANES), F32))

    return pl.pallas_call(
        body, name=name, out_shape=jax.ShapeDtypeStruct((S, LANES), F32), grid=(1,),
        in_specs=[pl.BlockSpec((S, LANES), lambda i: (0, fblk)), pl.BlockSpec((1, LANES), lambda i: (0, 0))],
        out_specs=pl.BlockSpec((S, LANES), lambda i: (0, 0)),
        compiler_params=_params("arbitrary"))(rest, bpad)


def _fgate_bwd(rest, bpad, dF, fblk, name):
    S = rest.shape[0]
    CH = min(256, S)
    nch = S // CH

    def body(f_ref, b_ref, df_ref, o_ref, db_ref):
        row = lax.broadcasted_iota(jnp.int32, (CH, CH), 0)
        col = lax.broadcasted_iota(jnp.int32, (CH, CH), 1)
        tri = (col >= row).astype(BF16)

        def step(n, carry):
            sfx_carry, db = carry
            ci = nch - 1 - n
            rows = pl.ds(pl.multiple_of(ci * CH, CH), CH)
            z = f_ref[rows, :] + b_ref[...]
            dfv = df_ref[rows, :]
            dz = (_tri_dot(tri, dfv) + sfx_carry) * _sigmoid(-z)
            o_ref[rows, :] = dz.astype(BF16)
            return sfx_carry + jnp.sum(dfv, axis=0, keepdims=True), db + jnp.sum(dz, axis=0, keepdims=True)

        zero = jnp.zeros((1, LANES), F32)
        _, db = lax.fori_loop(0, nch, step, (zero, zero))
        db_ref[...] = db

    blk = pl.BlockSpec((S, LANES), lambda i: (0, 0))
    return pl.pallas_call(
        body, name=name,
        out_shape=(jax.ShapeDtypeStruct((S, LANES), BF16), jax.ShapeDtypeStruct((1, LANES), F32)), grid=(1,),
        in_specs=[pl.BlockSpec((S, LANES), lambda i: (0, fblk)), pl.BlockSpec((1, LANES), lambda i: (0, 0)), blk],
        out_specs=(blk, pl.BlockSpec((1, LANES), lambda i: (0, 0))),
        compiler_params=_params("arbitrary"))(rest, bpad, dF)


_NT = (((1,), (1,)), ((), ()))
LOG2E = 1.4426950408889634
ATTN_TILE = 512


def _blocked_rows(a, TA):
    H, S = a.shape
    return a.reshape(H, S // TA, 1, TA)


def _attn_fwd(qkv, nfb, H, mix_cols, name):
    S = qkv.shape[0]
    TA = min(ATTN_TILE, S)
    nb = S // TA
    c = HEAD_DIM ** -0.5 * LOG2E

    def body(q_ref, k_ref, v_ref, nf_ref, o_ref, o32_ref, lse_ref, m_ref, l_ref, acc_ref):
        i = pl.program_id(1)
        m_ref[...] = jnp.full_like(m_ref, -jnp.inf)
        l_ref[...] = jnp.zeros_like(l_ref)
        acc_ref[...] = jnp.zeros_like(acc_ref)

        def block(j, masked):
            rows = pl.ds(pl.multiple_of(j * TA, TA), TA)
            st = (lax.dot_general(k_ref[rows, :], q_ref[...], _NT, preferred_element_type=F32) * c
                  + jnp.tile(nf_ref[0, rows, :], (1, TA // LANES)))
            if masked:
                key = lax.broadcasted_iota(jnp.int32, (TA, TA), 0)
                qry = lax.broadcasted_iota(jnp.int32, (TA, TA), 1)
                st = jnp.where(key <= qry, st, -jnp.inf)
            m_old = m_ref[...]
            m_new = jnp.maximum(m_old, jnp.max(st, axis=0, keepdims=True))
            alpha = jnp.exp2(m_old - m_new)
            pt = jnp.exp2(st - m_new)
            l_ref[...] = alpha * l_ref[...] + jnp.sum(pt, axis=0, keepdims=True)
            acc_ref[...] = alpha * acc_ref[...] + lax.dot_general(v_ref[rows, :], pt.astype(BF16), _TN,
                                                                  preferred_element_type=F32)
            m_ref[...] = m_new

        def loop(jj, carry):
            block(2 * jj, False)
            block(2 * jj + 1, False)
            return carry

        lax.fori_loop(0, i // 2, loop, 0)

        @pl.when(i % 2 == 1)
        def _():
            block(i - 1, False)

        block(i, True)
        o = jnp.transpose(acc_ref[...] / l_ref[...])
        o32_ref[...] = o
        o_ref[...] = o.astype(BF16)
        lse_ref[0, 0] = m_ref[...] + jnp.log(l_ref[...]) * LOG2E

    qblk = pl.BlockSpec((TA, HEAD_DIM), lambda h, i: (i, h))
    return pl.pallas_call(
        body, name=name, grid=(H, nb),
        in_specs=[qblk,
                  pl.BlockSpec((S, HEAD_DIM), lambda h, i: (0, H + h)),
                  pl.BlockSpec((S, HEAD_DIM), lambda h, i: (0, 2 * H + h)),
                  pl.BlockSpec((1, S, LANES), lambda h, i: (h, 0, 0))],
        out_specs=(qblk, qblk, pl.BlockSpec((1, 1, 1, TA), lambda h, i: (h, i, 0, 0))),
        scratch_shapes=[pltpu.VMEM((1, TA), F32), pltpu.VMEM((1, TA), F32), pltpu.VMEM((HEAD_DIM, TA), F32)],
        out_shape=(jax.ShapeDtypeStruct((S, mix_cols), BF16), jax.ShapeDtypeStruct((S, H * HEAD_DIM), F32),
                   jax.ShapeDtypeStruct((H, nb, 1, TA), F32)),
        compiler_params=_params("parallel", "parallel"))(qkv, qkv, qkv, nfb)


def _attn_delta(o, do, H, name):
    S = o.shape[0]
    T = min(512, S)

    def body(o_ref, do_ref, d_ref):
        d_ref[0] = jnp.sum(o_ref[...].astype(F32) * do_ref[...].astype(F32), axis=-1, keepdims=True)

    blk = pl.BlockSpec((T, HEAD_DIM), lambda h, i: (i, h))
    return pl.pallas_call(
        body, name=name, out_shape=jax.ShapeDtypeStruct((H, S, 1), F32), grid=(H, S // T),
        in_specs=[blk, blk], out_specs=pl.BlockSpec((1, T, 1), lambda h, i: (h, i, 0)),
        compiler_params=_params("parallel", "parallel"))(o, do)


def _ds_tile(k, q, v, do, nfb, lse_row, delta_row, c, masked):
    TK, TQ = k.shape[0], q.shape[0]
    st = lax.dot_general(k, q, _NT, preferred_element_type=F32) * c + jnp.tile(nfb, (1, TQ // LANES))
    pt = jnp.exp2(st - lse_row)
    if masked:
        key = lax.broadcasted_iota(jnp.int32, (TK, TQ), 0)
        qry = lax.broadcasted_iota(jnp.int32, (TK, TQ), 1)
        pt = jnp.where(key <= qry, pt, 0.0)
    dpt = lax.dot_general(v, do, _NT, preferred_element_type=F32)
    return pt, pt * (dpt - delta_row)


def _attn_bwd(qkv, nfb, do, lse, delta, H, name):
    S = qkv.shape[0]
    TA = min(ATTN_TILE, S)
    nb = S // TA
    scale = HEAD_DIM ** -0.5
    c = scale * LOG2E

    def body(q_ref, k_ref, v_ref, nf_ref, do_ref, lse_ref, dl_ref, dq_ref, dk_ref, dv_ref, drow_ref, dnf_ref,
             dq_acc, dk_acc, dv_acc, dnf_acc):
        j = pl.program_id(1)

        @pl.when(j == 0)
        def _():
            dq_acc[...] = jnp.zeros_like(dq_acc)
            drow_ref[...] = jnp.zeros_like(drow_ref)

        dk_acc[...] = jnp.zeros_like(dk_acc)
        dv_acc[...] = jnp.zeros_like(dv_acc)
        dnf_acc[...] = jnp.zeros_like(dnf_acc)
        kb = k_ref[...]

        def block(i, masked):
            rows = pl.ds(pl.multiple_of(i * TA, TA), TA)
            qb = q_ref[rows, :]
            dob = do_ref[rows, :]
            pt, dst = _ds_tile(kb, qb, v_ref[...], dob, nf_ref[0], lse_ref[0, i], dl_ref[0, i], c, masked)
            dsb = dst.astype(BF16)
            dv_acc[...] += jnp.dot(pt.astype(BF16), dob, preferred_element_type=F32)
            dk_acc[...] += jnp.dot(dsb, qb, preferred_element_type=F32)
            dq_acc[rows, :] += lax.dot_general(dsb, kb, _TN, preferred_element_type=F32)
            drow_ref[0, i] += jnp.sum(dst, axis=0, keepdims=True)
            part = dst[:, 0:LANES]
            for t in range(1, TA // LANES):
                part = part + dst[:, t * LANES:(t + 1) * LANES]
            dnf_acc[...] += part

        def loop(ii, carry):
            block(j + 1 + 2 * ii, False)
            block(j + 2 + 2 * ii, False)
            return carry

        block(j, True)
        rest = nb - 1 - j
        lax.fori_loop(0, rest // 2, loop, 0)

        @pl.when(rest % 2 == 1)
        def _():
            block(nb - 1, False)

        dk_ref[...] = (dk_acc[...] * scale).astype(BF16)
        dv_ref[...] = dv_acc[...].astype(BF16)
        dnf_ref[0] = jnp.sum(dnf_acc[...], axis=-1, keepdims=True)

        @pl.when(j == nb - 1)
        def _():
            dq_ref[...] = (dq_acc[...] * scale).astype(BF16)

    full = pl.BlockSpec((S, HEAD_DIM), lambda h, j: (0, h))
    row_stat = pl.BlockSpec((1, nb, 1, TA), lambda h, j: (h, 0, 0, 0))
    kblk = lambda c0: pl.BlockSpec((TA, HEAD_DIM), lambda h, j: (j, c0 + h))
    shp = jax.ShapeDtypeStruct((S, H * HEAD_DIM), BF16)
    return pl.pallas_call(
        body, name=name, grid=(H, nb),
        in_specs=[full, kblk(H), kblk(2 * H), pl.BlockSpec((1, TA, LANES), lambda h, j: (h, j, 0)), full,
                  row_stat, row_stat],
        out_specs=(full, kblk(0), kblk(0), row_stat, pl.BlockSpec((1, TA, 1), lambda h, j: (h, j, 0))),
        scratch_shapes=[pltpu.VMEM((S, HEAD_DIM), F32), pltpu.VMEM((TA, HEAD_DIM), F32),
                        pltpu.VMEM((TA, HEAD_DIM), F32), pltpu.VMEM((TA, LANES), F32)],
        out_shape=(shp, shp, shp, jax.ShapeDtypeStruct((H, nb, 1, TA), F32), jax.ShapeDtypeStruct((H, S, 1), F32)),
        compiler_params=_params("parallel", "arbitrary"))(qkv, qkv, qkv, nfb, do, lse, delta)


def _taps(ext_ref, w_ref, K, base, r0, rows, cols, reverse=False, init=None):
    acc = init
    for k in range(K):
        wk = w_ref[(K - 1 - k) if reverse else k:((K - 1 - k) if reverse else k) + 1, cols]
        term = wk * ext_ref[base + k + r0:base + k + r0 + rows, cols]
        acc = term if acc is None else acc + term
    return acc


def _prev_blk(T, H):
    return lambda i: jnp.maximum(i * (T // H) - 1, 0)


def _next_blk(T, H, S):
    return lambda i: jnp.minimum((i + 1) * (T // H), S // H - 1)


def _conf_fwd(rest, w, b, lng, lnb, mix, name):
    S = rest.shape[0]
    K, C = w.shape
    H, T = 32, min(256, S)
    RS = min(64, T)
    base = H - (K - 1)

    def body(cv_ref, cg_ref, cvp_ref, cgp_ref, w_ref, b_ref, g_ref, bb_ref, mix_ref, o_ref, cc_ref, ext_ref):
        i = pl.program_id(0)
        ext_ref[0:H, :] = jnp.where(i > 0, cvp_ref[...] * _sigmoid(cgp_ref[...]), 0.0)
        ext_ref[H:H + T, :] = cv_ref[...] * _sigmoid(cg_ref[...])
        for r0 in range(0, T, RS):
            cc = _taps(ext_ref, w_ref, K, base, r0, RS, slice(None), init=jnp.broadcast_to(b_ref[...], (RS, C)))
            cc_ref[r0:r0 + RS, :] = cc
            xc = cc - jnp.mean(cc, axis=-1, keepdims=True)
            y = xc * lax.rsqrt(jnp.mean(xc * xc, axis=-1, keepdims=True) + LN_EPS) * g_ref[...] + bb_ref[...]
            o_ref[r0:r0 + RS, :] = (y * _sigmoid(y)).astype(BF16)

    pb = _prev_blk(T, H)
    cur = lambda cb: pl.BlockSpec((T, C), lambda i: (i, cb))
    prev = lambda cb: pl.BlockSpec((H, C), lambda i: (pb(i), cb))
    full = lambda a: pl.BlockSpec(a.shape, lambda i: (0, 0))
    col_blk = (mix.shape[1] - 2 * C) // C
    return pl.pallas_call(
        body, name=name, grid=(S // T,),
        out_shape=(jax.ShapeDtypeStruct(mix.shape, BF16), jax.ShapeDtypeStruct((S, C), F32)),
        in_specs=[cur(0), cur(1), prev(0), prev(1), full(w), full(b), full(lng), full(lnb),
                  pl.BlockSpec(memory_space=pl.ANY)],
        out_specs=(pl.BlockSpec((T, C), lambda i: (i, col_blk)), pl.BlockSpec((T, C), lambda i: (i, 0))),
        input_output_aliases={8: 0},
        scratch_shapes=[pltpu.VMEM((H + T, C), F32)],
        compiler_params=_params("parallel"))(rest, rest, rest, rest, w, b, lng, lnb, mix)


def _conf_bwd(rest, cc_all, dcs, w, lng, lnb, name):
    S = rest.shape[0]
    K, C = w.shape
    H, T = 32, min(256, S)
    RS = 32
    nI = S // T
    base = H - (K - 1)

    def body(cv_ref, cg_ref, cvp_ref, cgp_ref, cc_ref, ccn_ref, do_ref, don_ref, w_ref, g_ref, bb_ref,
             dcvg_ref, dw_ref, dvec_ref, ext_ref, dcc_ref):
        i = pl.program_id(0)
        ext_ref[0:H, :] = jnp.where(i > 0, cvp_ref[...] * _sigmoid(cgp_ref[...]), 0.0)
        ext_ref[H:H + T, :] = cv_ref[...] * _sigmoid(cg_ref[...])

        @pl.when(i == 0)
        def _():
            dw_ref[...] = jnp.zeros_like(dw_ref)
            dvec_ref[...] = jnp.zeros_like(dvec_ref)

        db = jnp.zeros((1, C), F32)
        dg = jnp.zeros((1, C), F32)
        dbb = jnp.zeros((1, C), F32)
        for r0 in range(0, T + H, RS):
            cc = cc_ref[r0:r0 + RS, :] if r0 < T else ccn_ref[r0 - T:r0 - T + RS, :]
            xc = cc - jnp.mean(cc, axis=-1, keepdims=True)
            r = lax.rsqrt(jnp.mean(xc * xc, axis=-1, keepdims=True) + LN_EPS)
            xh = xc * r
            y = xh * g_ref[...] + bb_ref[...]
            sy = _sigmoid(y)
            if r0 < T:
                d_o = do_ref[r0:r0 + RS, :]
            else:
                d_o = jnp.where(i < nI - 1, don_ref[r0 - T:r0 - T + RS, :], 0.0)
            dy = d_o * (sy * (1.0 + y * (1.0 - sy)))
            dxh = dy * g_ref[...]
            dcc = r * (dxh - jnp.mean(dxh, axis=-1, keepdims=True)
                       - xh * jnp.mean(dxh * xh, axis=-1, keepdims=True))
            dcc_ref[r0:r0 + RS, :] = dcc
            if r0 < T:
                dbb = dbb + jnp.sum(dy, axis=0, keepdims=True)
                dg = dg + jnp.sum(dy * xh, axis=0, keepdims=True)
                db = db + jnp.sum(dcc, axis=0, keepdims=True)
        dvec_ref[0:1, :] += db
        dvec_ref[1:2, :] += dg
        dvec_ref[2:3, :] += dbb
        R2 = min(64, T)
        for k in range(K):
            s = jnp.zeros((1, C), F32)
            for r0 in range(0, T, R2):
                s = s + jnp.sum(dcc_ref[r0:r0 + R2, :] * ext_ref[base + k + r0:base + k + r0 + R2, :],
                                axis=0, keepdims=True)
            dw_ref[k:k + 1, :] += s
        for r0 in range(0, T, R2):
            dci = _taps(dcc_ref, w_ref, K, 0, r0, R2, slice(None), reverse=True)
            cvv = cv_ref[r0:r0 + R2, :]
            sg = _sigmoid(cg_ref[r0:r0 + R2, :])
            dcvg_ref[r0:r0 + R2, 0:C] = (dci * sg).astype(BF16)
            dcvg_ref[r0:r0 + R2, C:2 * C] = (dci * cvv * sg * (1.0 - sg)).astype(BF16)

    pb, nb_ = _prev_blk(T, H), _next_blk(T, H, S)
    cur = lambda cb: pl.BlockSpec((T, C), lambda i: (i, cb))
    prev = lambda cb: pl.BlockSpec((H, C), lambda i: (pb(i), cb))
    nxt = lambda cb: pl.BlockSpec((H, C), lambda i: (nb_(i), cb))
    full = lambda a: pl.BlockSpec(a.shape, lambda i: (0, 0))
    return pl.pallas_call(
        body, name=name,
        out_shape=(jax.ShapeDtypeStruct((S, 2 * C), BF16), jax.ShapeDtypeStruct((32, C), F32),
                   jax.ShapeDtypeStruct((8, C), F32)),
        grid=(nI,),
        in_specs=[cur(0), cur(1), prev(0), prev(1), cur(0), nxt(0), cur(0), nxt(0),
                  full(w), full(lng), full(lnb)],
        out_specs=(pl.BlockSpec((T, 2 * C), lambda i: (i, 0)), pl.BlockSpec((32, C), lambda i: (0, 0)),
                   pl.BlockSpec((8, C), lambda i: (0, 0))),
        scratch_shapes=[pltpu.VMEM((H + T, C), F32), pltpu.VMEM((T + H, C), F32)],
        compiler_params=_params("arbitrary"))(rest, rest, rest, rest, cc_all, cc_all, dcs, dcs, w, lng, lnb)


def _sconv_fwd(rest, w, mix, name):
    S = rest.shape[0]
    K, C = w.shape
    H, T = 8, min(256, S)
    RS = min(64, T)
    base = H - (K - 1)

    def body(sx_ref, sb_ref, sc_ref, sxp_ref, scp_ref, w_ref, mix_ref, o_ref, ext_ref):
        i = pl.program_id(0)
        ext_ref[0:H, :] = jnp.where(i > 0, sxp_ref[...] * scp_ref[...], 0.0)
        ext_ref[H:H + T, :] = sx_ref[...] * sc_ref[...]
        for r0 in range(0, T, RS):
            cz = _taps(ext_ref, w_ref, K, base, r0, RS, slice(None))
            o_ref[r0:r0 + RS, :] = (sb_ref[r0:r0 + RS, :] * cz).astype(BF16)

    pb = _prev_blk(T, H)
    cur = lambda cb: pl.BlockSpec((T, C), lambda i: (i, cb))
    prev = lambda cb: pl.BlockSpec((H, C), lambda i: (pb(i), cb))
    col_blk = (mix.shape[1] - C) // C
    return pl.pallas_call(
        body, name=name, out_shape=jax.ShapeDtypeStruct(mix.shape, BF16), grid=(S // T,),
        in_specs=[cur(2), cur(3), cur(4), prev(2), prev(4), pl.BlockSpec(w.shape, lambda i: (0, 0)),
                  pl.BlockSpec(memory_space=pl.ANY)],
        out_specs=pl.BlockSpec((T, C), lambda i: (i, col_blk)), input_output_aliases={6: 0},
        scratch_shapes=[pltpu.VMEM((H + T, C), F32)],
        compiler_params=_params("parallel"))(rest, rest, rest, rest, rest, w, mix)


def _sconv_bwd(rest, dcs, w, name):
    S = rest.shape[0]
    K, C = w.shape
    H, T = 8, min(256, S)
    RS = min(64, T)
    nI = S // T
    base = H - (K - 1)

    def body(sx_ref, sb_ref, sc_ref, sxp_ref, scp_ref, sbn_ref, do_ref, don_ref, w_ref,
             dout_ref, dw_ref, ext_ref, dcv_ref):
        i = pl.program_id(0)
        ext_ref[0:H, :] = jnp.where(i > 0, sxp_ref[...] * scp_ref[...], 0.0)
        ext_ref[H:H + T, :] = sx_ref[...] * sc_ref[...]
        dcv_ref[0:T, :] = do_ref[...] * sb_ref[...]
        dcv_ref[T:T + H, :] = jnp.where(i < nI - 1, don_ref[...] * sbn_ref[...], 0.0)

        @pl.when(i == 0)
        def _():
            dw_ref[...] = jnp.zeros_like(dw_ref)

        for k in range(K):
            s = jnp.zeros((1, C), F32)
            for r0 in range(0, T, RS):
                s = s + jnp.sum(dcv_ref[r0:r0 + RS, :] * ext_ref[base + k + r0:base + k + r0 + RS, :],
                                axis=0, keepdims=True)
            dw_ref[k:k + 1, :] += s
        for r0 in range(0, T, RS):
            cz = _taps(ext_ref, w_ref, K, base, r0, RS, slice(None))
            dz = _taps(dcv_ref, w_ref, K, 0, r0, RS, slice(None), reverse=True)
            dout_ref[r0:r0 + RS, 0:C] = (dz * sc_ref[r0:r0 + RS, :]).astype(BF16)
            dout_ref[r0:r0 + RS, C:2 * C] = (do_ref[r0:r0 + RS, :] * cz).astype(BF16)
            dout_ref[r0:r0 + RS, 2 * C:3 * C] = (dz * sx_ref[r0:r0 + RS, :]).astype(BF16)

    pb, nb_ = _prev_blk(T, H), _next_blk(T, H, S)
    cur = lambda cb: pl.BlockSpec((T, C), lambda i: (i, cb))
    prev = lambda cb: pl.BlockSpec((H, C), lambda i: (pb(i), cb))
    nxt = lambda cb: pl.BlockSpec((H, C), lambda i: (nb_(i), cb))
    return pl.pallas_call(
        body, name=name,
        out_shape=(jax.ShapeDtypeStruct((S, 3 * C), BF16), jax.ShapeDtypeStruct((8, C), F32)),
        grid=(nI,),
        in_specs=[cur(2), cur(3), cur(4), prev(2), prev(4), nxt(3), cur(1), nxt(1),
                  pl.BlockSpec(w.shape, lambda i: (0, 0))],
        out_specs=(pl.BlockSpec((T, 3 * C), lambda i: (i, 0)), pl.BlockSpec((8, C), lambda i: (0, 0))),
        scratch_shapes=[pltpu.VMEM((H + T, C), F32), pltpu.VMEM((T + H, C), F32)],
        compiler_params=_params("arbitrary"))(rest, rest, rest, rest, rest, rest, dcs, dcs, w)


FFN_ROWS = 256
FFN_HALO = 16


def _shift_mats(T):
    r = lax.broadcasted_iota(jnp.int32, (T, T), 0)
    c = lax.broadcasted_iota(jnp.int32, (T, T), 1)
    return jnp.stack([r == c + 1, r == c + 2, c == r + 1, c == r + 2]).astype(BF16)


def _edge_rows(strip, shift, first):
    sub = lax.broadcasted_iota(jnp.int32, strip.shape, 0)
    if first:
        return jnp.where(sub < shift, pltpu.roll(strip, shift, 0), 0.0)
    return jnp.where(sub >= 8 - shift, pltpu.roll(strip, 8 - shift, 0), 0.0)


def _conv3_tile(x_ref, prev_ref, w_ref, b_ref, down, has_prev, u_ref, xm_refs=None):
    T = x_ref.shape[0]
    x = x_ref[...]
    xm1 = jnp.dot(down[0], x, preferred_element_type=F32)
    xm2 = jnp.dot(down[1], x, preferred_element_type=F32)
    u_ref[...] = b_ref[...] + w_ref[0:1, :] * xm2 + w_ref[1:2, :] * xm1 + w_ref[2:3, :] * x.astype(F32)
    tail = jnp.where(has_prev, prev_ref[...].astype(F32)[FFN_HALO - 8:, :], 0.0)
    p1, p2 = _edge_rows(tail, 1, True), _edge_rows(tail, 2, True)
    u_ref[0:8, :] += w_ref[0:1, :] * p2 + w_ref[1:2, :] * p1
    if xm_refs is not None:
        xm_refs[0][...] = xm1
        xm_refs[1][...] = xm2
        xm_refs[0][0:8, :] += p1
        xm_refs[1][0:8, :] += p2


def _ffn_fwd(hu, w, b, name):
    S, F2 = hu.shape
    Fd = F2 // 2
    K = w.shape[0]
    assert K == 3
    T = min(FFN_ROWS, S)
    tc = _tile(Fd, 512)
    nJ = Fd // tc

    def body(sh_ref, g_ref, v_ref, gp_ref, vp_ref, wg_ref, wv_ref, bg_ref, bv_ref, o_ref, ug_ref, uv_ref):
        i = pl.program_id(1)
        down = (sh_ref[0], sh_ref[1])
        _conv3_tile(g_ref, gp_ref, wg_ref, bg_ref, down, i > 0, ug_ref)
        _conv3_tile(v_ref, vp_ref, wv_ref, bv_ref, down, i > 0, uv_ref)
        ug = ug_ref[...]
        o_ref[...] = (ug * _sigmoid(ug) * uv_ref[...]).astype(BF16)

    pb = _prev_blk(T, FFN_HALO)
    cur = lambda off: pl.BlockSpec((T, tc), lambda j, i: (i, j + off))
    prev = lambda off: pl.BlockSpec((FFN_HALO, tc), lambda j, i: (pb(i), j + off))
    wsp = lambda off: pl.BlockSpec((K, tc), lambda j, i: (0, j + off))
    bsp = lambda off: pl.BlockSpec((1, tc), lambda j, i: (0, j + off))
    return pl.pallas_call(
        body, name=name, out_shape=jax.ShapeDtypeStruct((S, Fd), BF16), grid=(nJ, S // T),
        in_specs=[pl.BlockSpec((4, T, T), lambda j, i: (0, 0, 0)),
                  cur(0), cur(nJ), prev(0), prev(nJ), wsp(0), wsp(nJ), bsp(0), bsp(nJ)],
        out_specs=pl.BlockSpec((T, tc), lambda j, i: (i, j)),
        scratch_shapes=[pltpu.VMEM((T, tc), F32), pltpu.VMEM((T, tc), F32)],
        compiler_params=_params("parallel", "parallel"))(_shift_mats(T), hu, hu, hu, hu, w, w, b, b)


def _ffn_bwd(hu, dact, w, b, name):
    S, F2 = hu.shape
    Fd = F2 // 2
    K = w.shape[0]
    assert K == 3
    T = min(FFN_ROWS, S)
    tc = _tile(Fd, 512)
    nJ = Fd // tc
    nI = S // T

    def body(sh_ref, g_ref, v_ref, gp_ref, vp_ref, gn_ref, vn_ref, da_ref, dan_ref, wg_ref, wv_ref, bg_ref, bv_ref,
             dg_ref, dv_ref, dwg_ref, dwv_ref, ug_ref, uv_ref, g1_ref, g2_ref, v1_ref, v2_ref, dh_ref):
        i = pl.program_id(1)
        down, up = (sh_ref[0], sh_ref[1]), (sh_ref[2], sh_ref[3])
        _conv3_tile(g_ref, gp_ref, wg_ref, bg_ref, down, i > 0, ug_ref, (g1_ref, g2_ref))
        _conv3_tile(v_ref, vp_ref, wv_ref, bv_ref, down, i > 0, uv_ref, (v1_ref, v2_ref))

        @pl.when(i == 0)
        def _():
            dwg_ref[...] = jnp.zeros_like(dwg_ref)
            dwv_ref[...] = jnp.zeros_like(dwv_ref)

        def d_u(ug, uv, d_a):
            sg = _sigmoid(ug)
            return d_a * uv * (sg * (1.0 + ug * (1.0 - sg))), d_a * (ug * sg)

        dug, duv = d_u(ug_ref[...], uv_ref[...], da_ref[...].astype(F32))

        def next_rows(x_ref, xn_ref, w_ref, b_ref):
            strip = jnp.concatenate([x_ref[T - FFN_HALO:, :].astype(F32)[FFN_HALO - 8:, :],
                                     xn_ref[...].astype(F32)[0:8, :]], axis=0)
            return (b_ref[...] + w_ref[0:1, :] * strip[6:14, :] + w_ref[1:2, :] * strip[7:15, :]
                    + w_ref[2:3, :] * strip[8:16, :])

        d_an = jnp.where(i < nI - 1, dan_ref[...].astype(F32)[0:8, :], 0.0)
        dug_n, duv_n = d_u(next_rows(g_ref, gn_ref, wg_ref, bg_ref), next_rows(v_ref, vn_ref, wv_ref, bv_ref), d_an)

        for du, du_n, x_ref, x1_ref, x2_ref, w_ref, dw_ref, out_ref in (
                (dug, dug_n, g_ref, g1_ref, g2_ref, wg_ref, dwg_ref, dg_ref),
                (duv, duv_n, v_ref, v1_ref, v2_ref, wv_ref, dwv_ref, dv_ref)):
            dw_ref[0:1, :] += jnp.sum(du * x2_ref[...], axis=0, keepdims=True)
            dw_ref[1:2, :] += jnp.sum(du * x1_ref[...], axis=0, keepdims=True)
            dw_ref[2:3, :] += jnp.sum(du * x_ref[...].astype(F32), axis=0, keepdims=True)
            dw_ref[3:4, :] += jnp.sum(du, axis=0, keepdims=True)
            dub = du.astype(BF16)
            dh_ref[...] = (w_ref[2:3, :] * du + w_ref[1:2, :] * jnp.dot(up[0], dub, preferred_element_type=F32)
                           + w_ref[0:1, :] * jnp.dot(up[1], dub, preferred_element_type=F32))
            nxt = du_n.astype(BF16).astype(F32)
            dh_ref[T - 8:, :] += w_ref[1:2, :] * _edge_rows(nxt, 1, False) + w_ref[0:1, :] * _edge_rows(nxt, 2, False)
            out_ref[...] = dh_ref[...].astype(BF16)

    pb, nbb = _prev_blk(T, FFN_HALO), _next_blk(T, FFN_HALO, S)
    cur = lambda off: pl.BlockSpec((T, tc), lambda j, i: (i, j + off))
    prev = lambda off: pl.BlockSpec((FFN_HALO, tc), lambda j, i: (pb(i), j + off))
    nxt = lambda off: pl.BlockSpec((FFN_HALO, tc), lambda j, i: (nbb(i), j + off))
    wsp = lambda off: pl.BlockSpec((K, tc), lambda j, i: (0, j + off))
    bsp = lambda off: pl.BlockSpec((1, tc), lambda j, i: (0, j + off))
    half = jax.ShapeDtypeStruct((S, Fd), BF16)
    dws = jax.ShapeDtypeStruct((8, Fd), F32)
    tile = pltpu.VMEM((T, tc), F32)
    return pl.pallas_call(
        body, name=name, out_shape=(half, half, dws, dws), grid=(nJ, nI),
        in_specs=[pl.BlockSpec((4, T, T), lambda j, i: (0, 0, 0)),
                  cur(0), cur(nJ), prev(0), prev(nJ), nxt(0), nxt(nJ), cur(0), nxt(0),
                  wsp(0), wsp(nJ), bsp(0), bsp(nJ)],
        out_specs=(pl.BlockSpec((T, tc), lambda j, i: (i, j)), pl.BlockSpec((T, tc), lambda j, i: (i, j)),
                   pl.BlockSpec((8, tc), lambda j, i: (0, j)), pl.BlockSpec((8, tc), lambda j, i: (0, j))),
        scratch_shapes=[tile] * 7,
        compiler_params=_params("parallel", "arbitrary"))(
            _shift_mats(T), hu, hu, hu, hu, hu, hu, dact, dact, w, w, b, b)


def _position():
    return lax.axis_index("x"), lax.axis_index("y"), lax.axis_index("c")


def _slot(px, py, pc):
    return 4 * px + 2 * py + pc


def _gather_copies(x_ref, out_ref, sems, r, starting=False):
    send_sems, recv_sems, local_sems = sems
    px, py, pc = _position()
    me, sibling = (px, py, pc), (px, py, 1 - pc)
    chips = [(1 - px, py), (px, 1 - py), (1 - px, 1 - py)]

    def copy(k, block, to, src=None):
        dst = out_ref.at[_slot(*block)]
        return pltpu.make_async_remote_copy(
            src_ref=dst if src is None else src, dst_ref=dst,
            send_sem=send_sems.at[7 * r + k], recv_sem=recv_sems.at[7 * r + k], device_id=to, device_id_type=MESH)

    mine = pltpu.make_async_copy(x_ref, out_ref.at[_slot(*me)], local_sems.at[r])
    first = [copy(0, me, sibling, src=x_ref)] + [copy(1 + n, me, (*chip, pc), src=x_ref)
                                                  for n, chip in enumerate(chips)]
    if starting:
        return mine, first
    passed = [copy(4 + n, (*chip, pc), sibling) for n, chip in enumerate(chips)]
    landed = [copy(1 + n, (*chip, pc), me) for n, chip in enumerate(chips)]
    from_sibling = [copy(0, sibling, me)] + [copy(4 + n, (*chip, 1 - pc), me) for n, chip in enumerate(chips)]
    return mine, first, passed, landed, from_sibling


def _scatter_copies(g_ref, out_ref, sems, r, starting=False):
    send_sems, recv_sems, local_sems = sems
    px, py, pc = _position()
    me = _slot(px, py, pc)
    mine = pltpu.make_async_copy(g_ref.at[me], out_ref.at[me], local_sems.at[r])
    peers = [(px ^ fx, py ^ fy, pc ^ fc) for fx, fy, fc in PEER_FLIPS]

    def copy(k, peer, src_slot, dst_slot):
        return pltpu.make_async_remote_copy(
            src_ref=g_ref.at[src_slot], dst_ref=out_ref.at[dst_slot],
            send_sem=send_sems.at[7 * r + k], recv_sem=recv_sems.at[7 * r + k], device_id=peer, device_id_type=MESH)

    sends = [copy(k, peer, _slot(*peer), me) for k, peer in enumerate(peers)]
    if starting:
        return mine, sends
    arrivals = [copy(k, peer, me, _slot(*peer)) for k, peer in enumerate(peers)]
    return mine, sends, arrivals


def _rider_start(kind, in_ref, out_ref, sems, r):
    if kind == "gather":
        mine, first = _gather_copies(in_ref, out_ref, sems, r, starting=True)
    else:
        mine, first = _scatter_copies(in_ref, out_ref, sems, r, starting=True)
    mine.start()
    for cp in first:
        cp.start()


def _rider_finish(kind, in_ref, out_ref, sems, r):
    if kind == "gather":
        mine, first, passed, landed, from_sibling = _gather_copies(in_ref, out_ref, sems, r)
        for cp, fwd in zip(landed, passed):
            cp.wait_recv()
            fwd.start()
        for cp in from_sibling:
            cp.wait_recv()
        for cp in first + passed:
            cp.wait_send()
    else:
        mine, sends, arrivals = _scatter_copies(in_ref, out_ref, sems, r)
        for cp in arrivals:
            cp.wait_recv()
        for cp in sends:
            cp.wait_send()
    mine.wait()


def _all_gather(x, in_vmem, name):
    space = pltpu.VMEM if in_vmem else pl.ANY

    def body(x_ref, out_ref, send_sems, recv_sems, local_sems):
        sems = (send_sems, recv_sems, local_sems)
        _rider_start("gather", x_ref, out_ref, sems, 0)
        _rider_finish("gather", x_ref, out_ref, sems, 0)

    return pl.pallas_call(
        body, name=name, out_shape=jax.ShapeDtypeStruct((N_DEV,) + x.shape, x.dtype),
        in_specs=[pl.BlockSpec(memory_space=space)], out_specs=pl.BlockSpec(memory_space=space),
        scratch_shapes=[pltpu.SemaphoreType.DMA((7,)), pltpu.SemaphoreType.DMA((7,)), pltpu.SemaphoreType.DMA((1,))],
        compiler_params=pltpu.CompilerParams(vmem_limit_bytes=VMEM_LIMIT_BYTES),
    )(x)


def _adam_sum(stage, w, m, v, layer, prev, name):
    n = stage.shape[0]
    L, R, C = w.shape
    tr = R if R * C <= 256 * 1024 else _row_tile(R, C)
    c1 = 1.0 / (1.0 - ADAM_B1 ** ADAM_STEP)
    c2 = 1.0 / (1.0 - ADAM_B2 ** ADAM_STEP)

    def body(*refs):
        st_ref, w_ref, m_ref, v_ref = refs[:4]
        g_ref, d_ref, nm_ref, nv_ref = refs[-4:]
        g = st_ref[0].astype(F32)
        for s in range(1, n):
            g = g + st_ref[s].astype(F32)
        wv = w_ref[0]
        mn = ADAM_B1 * m_ref[0] + (1.0 - ADAM_B1) * g
        vn = ADAM_B2 * v_ref[0] + (1.0 - ADAM_B2) * (g * g)
        g_ref[0] = g
        nm_ref[0] = mn
        nv_ref[0] = vn
        d_ref[0] = -ADAM_LR * ((mn * c1) / (jnp.sqrt(vn * c2) + ADAM_EPS) + ADAM_WD * wv)

    lay = pl.BlockSpec((1, tr, C), lambda i: (layer, i, 0))
    in_specs = [pl.BlockSpec((n, tr, C), lambda i: (0, i, 0)), lay, lay, lay]
    ins = [stage, w, m, v]
    aliases = {}
    if prev is not None:
        in_specs += [pl.BlockSpec(memory_space=pl.ANY)] * 4
        ins += list(prev)
        aliases = {4: 0, 5: 1, 6: 2, 7: 3}
    shp = jax.ShapeDtypeStruct((L, R, C), F32)
    return pl.pallas_call(
        body, name=name, out_shape=(shp, shp, shp, shp), grid=(R // tr,),
        in_specs=in_specs, out_specs=(lay, lay, lay, lay), input_output_aliases=aliases,
        compiler_params=_params("parallel"))(*ins)


def _row_tile(R, C):
    cpad = -(-C // LANES) * LANES
    want = max(16, (256 * 1024) // cpad)
    best = 16
    for t in range(16, R + 1, 16):
        if R % t == 0 and t <= want:
            best = t
    return best


def _sum_slabs(st, name):
    n, R, C = st.shape
    tr = R if n * R * C * 4 <= (12 << 20) else _row_tile(R, C)

    def body(st_ref, o_ref):
        g = st_ref[0]
        for s in range(1, n):
            g = g + st_ref[s]
        o_ref[...] = g

    return pl.pallas_call(
        body, name=name, out_shape=jax.ShapeDtypeStruct((R, C), F32), grid=(R // tr,),
        in_specs=[pl.BlockSpec((n, tr, C), lambda i: (0, i, 0))], out_specs=pl.BlockSpec((tr, C), lambda i: (i, 0)),
        compiler_params=_params("parallel"))(st)


def _pack(arrs):
    flat = [a.reshape(-1).astype(F32) for a in arrs]
    sizes = [f.shape[0] for f in flat]
    total = sum(sizes)
    padded = -(-total // (16 * LANES)) * (16 * LANES)
    if padded > total:
        flat.append(jnp.zeros((padded - total,), F32))
    return jnp.concatenate(flat).reshape(padded // LANES, LANES), (sizes, [a.shape for a in arrs])


def _unpack(packed, layout, lead=()):
    sizes, shapes = layout
    flat = packed.reshape(lead + (-1,))
    out, off = [], 0
    for sz, shp in zip(sizes, shapes):
        out.append(flat[..., off:off + sz].reshape(lead + tuple(shp)))
        off += sz
    return out


class _NoComm:
    col_slabs = None

    def __init__(self, wts):
        self.wts, self.grads = wts, {}

    def weight(self, l, name):
        return self.wts[l][name]

    def gather_rider(self, l, names):
        return None

    def scatter_rider(self, name, l, g):
        self.grads[(name, l)] = g
        return None


def _local_step(x, tgt, ada, mix_norm_g, comm, b_forget, conf_dw_w, conf_dw_b, conf_ln_g, conf_ln_b, sc_dw_w,
                ffn_norm_g, ffn_dw_w, ffn_dw_b, final_norm_g):
    S, D = x.shape
    L = ada.shape[0]

    def mm_gather(a, b, dtype, name, l_next, names):
        rider = comm.gather_rider(l_next, names) if l_next < L else None
        if rider is None:
            return _matmul(a, b, dtype, name=name)
        out, got = _matmul(a, b, dtype, name="cm_" + name, rider=rider)
        comm.gathered(l_next, names, got)
        return out

    def mm_scatter(a, b, dtype, name, wname, l, g):
        rider = comm.scatter_rider(wname, l, g)
        if rider is None:
            return _matmul(a, b, dtype, name=name, b_transposed=True)
        out, got = _matmul(a, b, dtype, name="cm_" + name, rider=rider, b_transposed=True)
        comm.scattered(wname, l, got[0])
        return out

    H = b_forget.shape[1]
    DA = H * HEAD_DIM
    C = conf_dw_b.shape[1]
    NQ = 3 * DA
    NR = 5 * C + LANES
    fblk = (5 * C) // LANES
    row = lambda a: a.reshape(1, -1)
    adav = ada.reshape(L, N_ADA, 1, D)

    saved = []
    xcur, delta, gate = x, None, None
    for l in range(L):
        sh_m, sc_m, g_m, sh_f, sc_f, g_f = [adav[l, n] for n in range(N_ADA)]
        w = functools.partial(comm.weight, l)
        x1, h1 = _site_fwd(xcur, delta, gate, row(mix_norm_g[l]), sc_m, sh_m, name="site_fwd_mix")
        first = 0 if l == 0 else L
        qkv = mm_gather(h1, w("w_in_perm")[:, :NQ], BF16, "mm_qkv", first, ("w_up",))
        rest = mm_gather(h1, w("w_in_perm")[:, NQ:], F32, "mm_rest", first, ("w_down", "w_out"))
        bpad = jnp.zeros((1, LANES), F32).at[0, :H].set(b_forget[l])
        Fc = _fgate_fwd(rest, bpad, fblk, name="fgate_fwd")
        nf = -LOG2E * jnp.transpose(Fc[:, :H])
        mixcat, attn32, lse = _attn_fwd(qkv, jnp.broadcast_to(nf[:, :, None], (H, S, LANES)), H, DA + 2 * C,
                                        name="attn_fwd")
        mixcat, conf_cc = _conf_fwd(rest, conf_dw_w[l], row(conf_dw_b[l]), row(conf_ln_g[l]), row(conf_ln_b[l]),
                                    mixcat, name="conf_fwd")
        mixcat = _sconv_fwd(rest, sc_dw_w[l], mixcat, name="sconv_fwd")
        mixed = _matmul(mixcat, w("w_out"), F32, name="mm_out")
        x2, h2 = _site_fwd(x1, mixed, g_m, row(ffn_norm_g[l]), sc_f, sh_f, name="site_fwd_ffn")
        hu = mm_gather(h2, w("w_up"), BF16, "mm_up", l + 1, ("w_up", "w_down"))
        act = _ffn_fwd(hu, ffn_dw_w[l], row(ffn_dw_b[l]), name="ffn_fwd")
        ffn_out = mm_gather(act, w("w_down"), F32, "mm_down", l + 1, ("w_in", "w_out"))
        saved.append(dict(x1=x1, h1=h1, qkv=qkv, rest=rest, bpad=bpad, nf=nf, attn32=attn32, lse=lse, mixcat=mixcat,
                          mixed=mixed, x2=x2, h2=h2, hu=hu, act=act, ffn_out=ffn_out, conf_cc=conf_cc))
        xcur, delta, gate = x2, ffn_out, g_f

    loss_lanes, dx, d_delta, d_gate, d_gfin = _final_fwd_bwd(xcur, delta, gate, row(final_norm_g), tgt, name="final")
    loss = (0.5 / D) * jnp.sum(loss_lanes)

    grads = dict(final_norm_g=d_gfin[0], ada=[None] * L, mix_norm_g=[None] * L, ffn_norm_g=[None] * L,
                 b_forget=[None] * L, conf_dw_w=[None] * L, conf_dw_b=[None] * L, conf_ln_g=[None] * L,
                 conf_ln_b=[None] * L, sc_dw_w=[None] * L, ffn_dw_w=[None] * L, ffn_dw_b=[None] * L)
    K3 = ffn_dw_w.shape[1]
    for l in reversed(range(L)):
        sv, w = saved[l], functools.partial(comm.weight, l)
        sh_m, sc_m, g_m, sh_f, sc_f, g_f = [adav[l, n] for n in range(N_ADA)]
        d_gf = d_gate
        g_down = _matmul_tn(sv["act"], d_delta, BF16, name="mm_dw_down")
        dact = mm_scatter(d_delta, w("w_down"), BF16, "mm_dact", "w_down", l, g_down)
        dhu_g, dhu_v, dwg, dwv = _ffn_bwd(sv["hu"], dact, ffn_dw_w[l], row(ffn_dw_b[l]), name="ffn_bwd")
        grads["ffn_dw_w"][l] = jnp.concatenate([dwg[:K3], dwv[:K3]], axis=1)
        grads["ffn_dw_b"][l] = jnp.concatenate([dwg[K3], dwv[K3]])
        dhu = (dhu_g, dhu_v)
        g_up = _matmul_tn(sv["h2"], dhu, BF16, name="mm_dw_up", col_slabs=comm.col_slabs)
        dh2 = mm_scatter(dhu, w("w_up"), F32, "mm_dh2", "w_up", l, g_up)
        dx, d_sh_f, d_a_f, d_mixed, d_gm = _site_bwd(sv["x2"], dh2, dx, row(ffn_norm_g[l]), sc_f,
                                                      sh_f, sv["mixed"], g_m, name="site_bwd_ffn")
        grads["ffn_norm_g"][l] = (d_a_f * (1.0 + sc_f))[0]
        d_sc_f = d_a_f * row(ffn_norm_g[l])
        g_out = _matmul_tn(sv["mixcat"], d_mixed, BF16, name="mm_dw_out")
        dattn = mm_scatter(d_mixed, w("w_out")[:DA], BF16, "mm_dattn", "w_out", l, g_out)
        dcs = _matmul(d_mixed, w("w_out")[DA:], F32, name="mm_dcs", b_transposed=True)
        delta_a = _blocked_rows(_attn_delta(sv["attn32"], dattn, H, name="attn_delta")[:, :, 0], min(ATTN_TILE, S))
        nfb = jnp.broadcast_to(sv["nf"][:, :, None], (H, S, LANES))
        dq, dk, dv, drow, dnf = _attn_bwd(sv["qkv"], nfb, dattn, sv["lse"], delta_a, H, name="attn_bwd")
        dF = jnp.zeros((S, LANES), F32).at[:, :H].set(jnp.transpose(drow.reshape(H, S) - dnf[:, :, 0]))
        dfl, dbf = _fgate_bwd(sv["rest"], sv["bpad"], dF, fblk, name="fgate_bwd")
        grads["b_forget"][l] = dbf[0, :H]
        dcvg, dcw, dcvec = _conf_bwd(sv["rest"], sv["conf_cc"], dcs, conf_dw_w[l], row(conf_ln_g[l]),
                                     row(conf_ln_b[l]), name="conf_bwd")
        grads["conf_dw_w"][l] = dcw[:conf_dw_w.shape[1]]
        grads["conf_dw_b"][l], grads["conf_ln_g"][l], grads["conf_ln_b"][l] = dcvec[0], dcvec[1], dcvec[2]
        dsc3, dsw = _sconv_bwd(sv["rest"], dcs, sc_dw_w[l], name="sconv_bwd")
        grads["sc_dw_w"][l] = dsw[:sc_dw_w.shape[1]]
        dproj = jnp.concatenate([dq, dk, dv, dcvg, dsc3, dfl], axis=1)
        g_in = _matmul_tn(sv["h1"], dproj, BF16, name="mm_dw_in")
        dh1 = mm_scatter(dproj, w("w_in_perm"), F32, "mm_dh1", "w_in_perm", l, g_in)
        if l > 0:
            pv = saved[l - 1]
            g_f_prev = adav[l - 1, 5]
            dx, d_sh_m, d_a_m, d_delta, d_gate = _site_bwd(sv["x1"], dh1, dx, row(mix_norm_g[l]), sc_m, sh_m,
                                                           pv["ffn_out"], g_f_prev, name="site_bwd_mix")
        else:
            dx, d_sh_m, d_a_m = _site_bwd(sv["x1"], dh1, dx, row(mix_norm_g[l]), sc_m, sh_m, None, None,
                                          name="site_bwd_first")
        grads["mix_norm_g"][l] = (d_a_m * (1.0 + sc_m))[0]
        d_sc_m = d_a_m * row(mix_norm_g[l])
        grads["ada"][l] = jnp.concatenate([d_sh_m, d_sc_m, d_gm, d_sh_f, d_sc_f, d_gf], axis=1)[0]
    return loss, dx, grads


def kernel(x, c, ada_w, ada_b, mix_norm_g, w_in, b_forget, conf_dw_w, conf_dw_b, conf_ln_g, conf_ln_b, sc_dw_w, w_out, ffn_norm_g, w_up, ffn_dw_w, ffn_dw_b, w_down, final_norm_g, loss_target, m_ada_w, m_ada_b, m_mix_norm_g, m_w_in, m_b_forget, m_conf_dw_w, m_conf_dw_b, m_conf_ln_g, m_conf_ln_b, m_sc_dw_w, m_w_out, m_ffn_norm_g, m_w_up, m_ffn_dw_w, m_ffn_dw_b, m_w_down, m_final_norm_g, v_ada_w, v_ada_b, v_mix_norm_g, v_w_in, v_b_forget, v_conf_dw_w, v_conf_dw_b, v_conf_ln_g, v_conf_ln_b, v_sc_dw_w, v_w_out, v_ffn_norm_g, v_w_up, v_ffn_dw_w, v_ffn_dw_b, v_w_down, v_final_norm_g):
    L, D, ada_loc = ada_w.shape
    S = x.shape[1]
    H = b_forget.shape[1]
    DA = H * HEAD_DIM
    C = conf_dw_b.shape[1]
    in_loc = w_in.shape[2]
    IN = in_loc * N_DEV
    px, py, pc = _position()
    me = _slot(px, py, pc)

    pk, lay = _pack([c, conf_dw_w, sc_dw_w, ffn_dw_w])
    gathered = _all_gather(pk, True, name="ag_small_fwd")
    c_all, cw_all, sw_all, fw_all = _unpack(gathered, lay, lead=(N_DEV,))
    c_all = c_all[:, 0]
    unshard = lambda a: jnp.moveaxis(a, 0, 2).reshape(a.shape[1], a.shape[2], -1)
    conf_w_full, sc_w_full, ffn_w_full = unshard(cw_all), unshard(sw_all), unshard(fw_all)
    c_act = c_all * jax.nn.sigmoid(c_all)
    c_act16 = jnp.zeros((16, D), F32).at[:N_DEV].set(c_act).astype(BF16)
    ada_cols = jnp.stack([_matmul(c_act16, ada_w[l].astype(BF16), F32, name="mm_ada")[:N_DEV] for l in range(L)])
    ada_g = _all_gather(ada_cols.reshape(L * N_DEV, ada_loc), True, name="ag_ada")
    ada_mine = lax.dynamic_index_in_dim(ada_g.reshape(N_DEV, L, N_DEV, ada_loc), me, axis=2, keepdims=False)
    ada = jnp.moveaxis(ada_mine, 0, 1).reshape(L, N_DEV * ada_loc) + ada_b

    NQ = 3 * DA
    PR = NQ + 5 * C
    shards = dict(w_in=w_in.astype(BF16), w_out=w_out.astype(BF16), w_up=w_up.astype(BF16),
                  w_down=w_down.astype(BF16))

    def shard_cols(g):
        return jnp.moveaxis(g.reshape(g.shape[0], N_DEV, -1), 1, 0)

    def shard_rows(g):
        return g.reshape(N_DEV, -1, g.shape[1])

    class MeshComm:
        col_slabs = N_DEV

        def __init__(self):
            self.got = {0: {"w_in": _all_gather(shards["w_in"][0], False, name="ag_w_in")}}
            self.full, self.stage = {}, {}

        def weight(self, l, name):
            if (l, name) not in self.full:
                g = self.got[l]
                if name == "w_in_perm":
                    wi = jnp.moveaxis(g["w_in"], 0, 1).reshape(D, IN)
                    full = jnp.concatenate([wi[:, :NQ], wi[:, NQ + H:], wi[:, NQ:NQ + H],
                                            jnp.zeros((D, LANES - H), BF16)], axis=1)
                elif name == "w_up":
                    full = jnp.moveaxis(g["w_up"], 0, 1).reshape(D, -1)
                else:
                    full = g[name].reshape(-1, D)
                self.full[(l, name)] = full
            return self.full[(l, name)]

        def gather_rider(self, l, names):
            return "gather", [shards[n][l] for n in names]

        def gathered(self, l, names, outs):
            self.got.setdefault(l, {}).update(zip(names, outs))

        def scatter_rider(self, name, l, g):
            if name == "w_in_perm":
                slabs = shard_cols(jnp.concatenate([g[:, :NQ], g[:, PR:PR + H], g[:, NQ:PR]], axis=1))
            elif name == "w_up":
                slabs = g
            else:
                slabs = shard_rows(g)
            return "scatter", [slabs]

        def scattered(self, name, l, out):
            self.stage[(name, l)] = out

    comm = MeshComm()
    loss_loc, dx, gr = _local_step(x[0], loss_target[0], ada, mix_norm_g, comm, b_forget, conf_w_full, conf_dw_b,
                                   conf_ln_g, conf_ln_b, sc_w_full, ffn_norm_g, ffn_w_full, ffn_dw_b, final_norm_g)
    loss = lax.psum(loss_loc, ("x", "y", "c"))

    small_names = ["ada", "mix_norm_g", "ffn_norm_g", "b_forget", "conf_dw_b", "conf_ln_g", "conf_ln_b",
                   "ffn_dw_b", "conf_dw_w", "sc_dw_w", "ffn_dw_w"]
    pk, lay = _pack([jnp.stack(gr[n]) for n in small_names] + [gr["final_norm_g"]])
    parts = _all_gather(pk, True, name="ag_small_bwd")
    tot = _unpack(_sum_slabs(parts, name="sum_small"), lay)
    g_small = dict(zip(small_names + ["final_norm_g"], tot))
    d_ada_all = _unpack(parts, lay, lead=(N_DEV,))[0]
    my_cols = lambda a, n: lax.dynamic_slice_in_dim(a, me * n, n, axis=a.ndim - 1)

    c_act_t = jnp.zeros((D, LANES), F32).at[:, :N_DEV].set(jnp.transpose(c_act)).astype(BF16)
    res = None
    for l in range(L):
        d_loc = jnp.zeros((LANES, ada_loc), F32).at[:N_DEV].set(my_cols(d_ada_all[:, l], ada_loc)).astype(BF16)
        g_l = _matmul(c_act_t, d_loc, F32, name="mm_dada")
        res = _adam_sum(g_l[None], ada_w, m_ada_w, v_ada_w, l, res, name="adam_ada_w")
    out_ada_w = res

    big = {}
    for nm, key, wq, mq, vq in (("w_down", "w_down", w_down, m_w_down, v_w_down), ("w_up", "w_up", w_up, m_w_up, v_w_up),
                                ("w_out", "w_out", w_out, m_w_out, v_w_out), ("w_in", "w_in_perm", w_in, m_w_in, v_w_in)):
        res = None
        for l in reversed(range(L)):
            res = _adam_sum(comm.stage[(key, l)], wq, mq, vq, l, res, name="adam_" + nm)
        big[nm] = res

    K31, K3 = conf_dw_w.shape[1], sc_dw_w.shape[1]
    sm = [("ada_b", ada_b, m_ada_b, v_ada_b, g_small["ada"]),
          ("mix_norm_g", mix_norm_g, m_mix_norm_g, v_mix_norm_g, g_small["mix_norm_g"]),
          ("b_forget", b_forget, m_b_forget, v_b_forget, g_small["b_forget"]),
          ("conf_dw_w", conf_dw_w, m_conf_dw_w, v_conf_dw_w, my_cols(g_small["conf_dw_w"], conf_dw_w.shape[2])),
          ("conf_dw_b", conf_dw_b, m_conf_dw_b, v_conf_dw_b, g_small["conf_dw_b"]),
          ("conf_ln_g", conf_ln_g, m_conf_ln_g, v_conf_ln_g, g_small["conf_ln_g"]),
          ("conf_ln_b", conf_ln_b, m_conf_ln_b, v_conf_ln_b, g_small["conf_ln_b"]),
          ("sc_dw_w", sc_dw_w, m_sc_dw_w, v_sc_dw_w, my_cols(g_small["sc_dw_w"], sc_dw_w.shape[2])),
          ("ffn_norm_g", ffn_norm_g, m_ffn_norm_g, v_ffn_norm_g, g_small["ffn_norm_g"]),
          ("ffn_dw_w", ffn_dw_w, m_ffn_dw_w, v_ffn_dw_w, my_cols(g_small["ffn_dw_w"], ffn_dw_w.shape[2])),
          ("ffn_dw_b", ffn_dw_b, m_ffn_dw_b, v_ffn_dw_b, g_small["ffn_dw_b"]),
          ("final_norm_g", final_norm_g, m_final_norm_g, v_final_norm_g, g_small["final_norm_g"])]
    pw, lay = _pack([t[1] for t in sm])
    pm, _ = _pack([t[2] for t in sm])
    pv, _ = _pack([t[3] for t in sm])
    pg, _ = _pack([t[4] for t in sm])
    sres = _adam_sum(pg[None], pw[None], pm[None], pv[None], 0, None, name="adam_small")
    s_g, s_d, s_m, s_v = [dict(zip([t[0] for t in sm], _unpack(r[0], lay))) for r in sres]

    def pick(idx, name):
        if name == "ada_w":
            return out_ada_w[idx]
        if name in big:
            return big[name][idx]
        return (s_g, s_d, s_m, s_v)[idx][name]

    order = ["ada_w", "ada_b", "mix_norm_g", "w_in", "b_forget", "conf_dw_w", "conf_dw_b", "conf_ln_g", "conf_ln_b",
             "sc_dw_w", "w_out", "ffn_norm_g", "w_up", "ffn_dw_w", "ffn_dw_b", "w_down", "final_norm_g"]
    outs = [loss, dx[None]]
    for idx in range(4):
        outs += [pick(idx, n) for n in order]
    return tuple(outs)
```

```python
import functools

import jax
import jax.numpy as jnp
from jax import lax
from jax.experimental import pallas as pl
from jax.experimental.pallas import tpu as pltpu

F32 = jnp.float32
BF16 = jnp.bfloat16
RMS_EPS = 1e-6
LN_EPS = 1e-5
HEAD_DIM = 128
N_ADA = 6
ADAM_LR = 0.001
ADAM_B1 = 0.9
ADAM_B2 = 0.999
ADAM_EPS = 1e-08
ADAM_WD = 0.01
ADAM_STEP = 10
N_DEV = 8
LANES = 128
VMEM_LIMIT_BYTES = 56 * 1024 * 1024
MM_TILE = 1024
MM_TILE_WIDE = 1536
MM_TILE_N_MAX = 2816
MXU_WIDTH = 256
MM_VMEM_BUDGET = 48 * 1024 * 1024
MESH = pl.DeviceIdType.MESH
PEER_FLIPS = ((0, 0, 1), (1, 0, 0), (0, 1, 0), (1, 1, 0), (1, 0, 1), (0, 1, 1), (1, 1, 1))


def _params(*sem):
    return pltpu.CompilerParams(dimension_semantics=sem, vmem_limit_bytes=VMEM_LIMIT_BYTES)


def _tile(n, cap):
    if n <= cap:
        return n
    for t in range(cap - cap % LANES, 0, -LANES):
        if n % t == 0:
            return t
    raise ValueError(f"no tile for {n}")


def _mm_tile(n):
    t = _tile(n, MM_TILE)
    return t if t == min(n, MM_TILE) else _tile(n, MM_TILE_WIDE)


def _n_tile(n, vmem_bytes):
    for step in (MXU_WIDTH, LANES):
        for t in range(min(n, MM_TILE_N_MAX) // step * step, 0, -step):
            if n % t == 0 and vmem_bytes(t) <= MM_VMEM_BUDGET:
                return t
    return n


def _sigmoid(v):
    return jax.nn.sigmoid(v)


def _matmul(a, b, out_dtype, name, rider=None, b_transposed=False):
    a_parts = a if isinstance(a, tuple) else (a,)
    na = len(a_parts)
    M = a_parts[0].shape[0]
    K = sum(p.shape[1] for p in a_parts)
    N = b.shape[0] if b_transposed else b.shape[1]
    tm = _mm_tile(M)
    out_bytes = jnp.dtype(out_dtype).itemsize
    part_k = a_parts[0].shape[1]

    def blocks_bytes(tk_, tn_):
        return 4 * (na * tm * tk_ + tk_ * tn_) + (4 * tm * tn_ if K > tk_ else 0) + 2 * tm * tn_ * out_bytes

    for tk in [t for t in range(part_k, 0, -LANES) if part_k % t == 0]:
        tn = _n_tile(N, functools.partial(blocks_bytes, tk))
        if blocks_bytes(tk, tn) <= MM_VMEM_BUDGET and tn >= min(N, MM_TILE if b_transposed else MM_TILE // 2):
            break
    nk = K // tk
    half = part_k // tk
    grid = (M // tm, N // tn, nk)
    dims = _NT if b_transposed else (((1,), (0,)), ((), ()))
    kind, arrs = rider if rider is not None else (None, [])
    nr = len(arrs)

    def body(*refs):
        a_refs, refs = refs[:na], refs[na - 1:]
        a_ref, b_ref = a_refs[0], refs[1]
        r_in = refs[2:2 + nr]
        o_ref = refs[2 + nr]
        r_out = refs[3 + nr:3 + 2 * nr]
        rest = refs[3 + 2 * nr:]
        i, j, k = pl.program_id(0), pl.program_id(1), pl.program_id(2)
        if nr:
            sems = rest[-3:]

            @pl.when((i == 0) & (j == 0) & (k == 0))
            def _():
                for r in range(nr):
                    _rider_start(kind, r_in[r], r_out[r], sems, r)

        if nk == 1:
            o_ref[...] = lax.dot_general(a_ref[...], b_ref[...], dims, preferred_element_type=F32).astype(o_ref.dtype)
        else:
            acc_ref = rest[0]

            @pl.when(k == 0)
            def _():
                acc_ref[...] = jnp.zeros_like(acc_ref)

            def accumulate(part_ref):
                acc_ref[...] += lax.dot_general(part_ref[...], b_ref[...], dims, preferred_element_type=F32)

            if na == 1:
                accumulate(a_ref)
            else:
                pl.when(k < half)(lambda: accumulate(a_refs[0]))
                pl.when(k >= half)(lambda: accumulate(a_refs[1]))

            @pl.when(k == nk - 1)
            def _():
                o_ref[...] = acc_ref[...].astype(o_ref.dtype)

        if nr:
            @pl.when((i == grid[0] - 1) & (j == grid[1] - 1) & (k == nk - 1))
            def _():
                for r in range(nr):
                    _rider_finish(kind, r_in[r], r_out[r], sems, r)

    scratch = [] if nk == 1 else [pltpu.VMEM((tm, tn), F32)]
    hbm = pl.BlockSpec(memory_space=pl.ANY)
    if na == 1:
        a_specs = [pl.BlockSpec((tm, tk), lambda i, j, k: (i, k))]
    else:
        a_specs = [pl.BlockSpec((tm, tk), lambda i, j, k: (i, jnp.minimum(k, half - 1))),
                   pl.BlockSpec((tm, tk), lambda i, j, k: (i, jnp.maximum(k - half, 0)))]
    out_shape = jax.ShapeDtypeStruct((M, N), out_dtype)
    out_specs = pl.BlockSpec((tm, tn), lambda i, j, k: (i, j))
    if nr:
        scratch += [pltpu.SemaphoreType.DMA((7 * nr,)), pltpu.SemaphoreType.DMA((7 * nr,)),
                    pltpu.SemaphoreType.DMA((nr,))]
        out_shape = (out_shape,) + tuple(
            jax.ShapeDtypeStruct(x.shape if kind == "scatter" else (N_DEV,) + x.shape, x.dtype) for x in arrs)
        out_specs = (out_specs,) + (hbm,) * nr
    out = pl.pallas_call(
        body, name=name,
        out_shape=out_shape,
        grid=grid,
        in_specs=a_specs + [pl.BlockSpec((tn, tk), lambda i, j, k: (j, k)) if b_transposed
                            else pl.BlockSpec((tk, tn), lambda i, j, k: (k, j))] + [hbm] * nr,
        out_specs=out_specs,
        scratch_shapes=scratch,
        compiler_params=_params(*(("arbitrary",) * 3 if nr else ("parallel", "parallel", "arbitrary"))),
    )(*a_parts, b, *arrs)
    return (out[0], list(out[1:])) if nr else out


_TN = (((0,), (0,)), ((), ()))


def _matmul_tn(a, b, out_dtype, name, col_slabs=None):
    b_parts = b if isinstance(b, tuple) else (b,)
    S, M = a.shape
    N = sum(p.shape[1] for p in b_parts)
    tm = _mm_tile(M)
    out_bytes = jnp.dtype(out_dtype).itemsize

    def blocks_bytes(ts_, tn_):
        return 4 * (ts_ * tm + len(b_parts) * ts_ * tn_) + 4 * tm * tn_ + 2 * tm * tn_ * out_bytes

    for ts in (_tile(S, 2 * MM_TILE), _tile(S, MM_TILE)):
        tn = N // col_slabs if col_slabs else _n_tile(b_parts[0].shape[1], functools.partial(blocks_bytes, ts))
        if blocks_bytes(ts, tn) <= MM_VMEM_BUDGET and tn >= min(b_parts[0].shape[1], MM_TILE):
            break
    ns = S // ts
    half = b_parts[0].shape[1] // tn

    def body(*refs):
        a_ref, b_refs, (o_ref, acc_ref) = refs[0], refs[1:-2], refs[-2:]
        j, k = pl.program_id(1), pl.program_id(2)

        @pl.when(k == 0)
        def _():
            acc_ref[...] = jnp.zeros_like(acc_ref)

        def accumulate(b_ref):
            acc_ref[...] += lax.dot_general(a_ref[...], b_ref[...], _TN, preferred_element_type=F32)

        if len(b_refs) == 1:
            accumulate(b_refs[0])
        else:
            pl.when(j < half)(lambda: accumulate(b_refs[0]))
            pl.when(j >= half)(lambda: accumulate(b_refs[1]))

        @pl.when(k == ns - 1)
        def _():
            if col_slabs is None:
                o_ref[...] = acc_ref[...].astype(o_ref.dtype)
            else:
                o_ref[0] = acc_ref[...].astype(o_ref.dtype)

    if col_slabs is None:
        out_shape = jax.ShapeDtypeStruct((M, N), out_dtype)
        out_spec = pl.BlockSpec((tm, tn), lambda i, j, k: (i, j))
    else:
        out_shape = jax.ShapeDtypeStruct((col_slabs, M, tn), out_dtype)
        out_spec = pl.BlockSpec((1, tm, tn), lambda i, j, k: (j, i, 0))
    if len(b_parts) == 1:
        b_specs = [pl.BlockSpec((ts, tn), lambda i, j, k: (k, j))]
    else:
        b_specs = [pl.BlockSpec((ts, tn), lambda i, j, k: (jnp.where(j < half, k, ns - 1), jnp.minimum(j, half - 1))),
                   pl.BlockSpec((ts, tn), lambda i, j, k: (jnp.where(j < half, 0, k), jnp.maximum(j - half, 0)))]
    return pl.pallas_call(
        body, name=name,
        out_shape=out_shape,
        grid=(M // tm, N // tn, ns),
        in_specs=[pl.BlockSpec((ts, tm), lambda i, j, k: (k, i))] + b_specs,
        out_specs=out_spec,
        scratch_shapes=[pltpu.VMEM((tm, tn), F32)],
        compiler_params=_params("parallel", "parallel", "arbitrary"),
    )(a, *b_parts)


def _site_fwd(x, delta, gate, g, sc, sh, name):
    S, D = x.shape
    T = min(256, S)
    res = delta is not None

    def body(*refs):
        if res:
            x_ref, d_ref, gate_ref, g_ref, sc_ref, sh_ref, xo_ref, h_ref = refs
            xv = x_ref[...] + gate_ref[...] * d_ref[...]
            xo_ref[...] = xv
        else:
            x_ref, g_ref, sc_ref, sh_ref, h_ref = refs
            xv = x_ref[...]
        r = lax.rsqrt(jnp.mean(xv * xv, axis=-1, keepdims=True) + RMS_EPS)
        a = g_ref[...] * (1.0 + sc_ref[...])
        h_ref[...] = (xv * r * a + sh_ref[...]).astype(BF16)

    row = pl.BlockSpec((T, D), lambda i: (i, 0))
    vec = pl.BlockSpec((1, D), lambda i: (0, 0))
    if res:
        ins, in_specs = (x, delta, gate, g, sc, sh), [row, row, vec, vec, vec, vec]
        out_shape = (jax.ShapeDtypeStruct((S, D), F32), jax.ShapeDtypeStruct((S, D), BF16))
        out_specs = (row, row)
    else:
        ins, in_specs = (x, g, sc, sh), [row, vec, vec, vec]
        out_shape = jax.ShapeDtypeStruct((S, D), BF16)
        out_specs = row
    out = pl.pallas_call(body, name=name, out_shape=out_shape, grid=(S // T,), in_specs=in_specs,
                         out_specs=out_specs, compiler_params=_params("parallel"))(*ins)
    return out if res else (x, out)


def _site_bwd(x, dh, dres, g, sc, sh, delta, gate, name):
    S, D = x.shape
    T = min(256, S)
    res = delta is not None

    def body(*refs):
        if res:
            (x_ref, dh_ref, dres_ref, g_ref, sc_ref, delta_ref, gate_ref,
             dx_ref, dsh_ref, da_ref, dd_ref, dgate_ref) = refs
        else:
            x_ref, dh_ref, dres_ref, g_ref, sc_ref, dx_ref, dsh_ref, da_ref = refs
        i = pl.program_id(0)
        xv = x_ref[...]
        dhv = dh_ref[...]
        r = lax.rsqrt(jnp.mean(xv * xv, axis=-1, keepdims=True) + RMS_EPS)
        xh = xv * r
        dxh = dhv * (g_ref[...] * (1.0 + sc_ref[...]))
        dx = r * (dxh - xh * jnp.mean(dxh * xh, axis=-1, keepdims=True)) + dres_ref[...]
        dx_ref[...] = dx

        @pl.when(i == 0)
        def _():
            dsh_ref[...] = jnp.zeros_like(dsh_ref)
            da_ref[...] = jnp.zeros_like(da_ref)
            if res:
                dgate_ref[...] = jnp.zeros_like(dgate_ref)

        dsh_ref[...] += jnp.sum(dhv, axis=0, keepdims=True)
        da_ref[...] += jnp.sum(dhv * xh, axis=0, keepdims=True)
        if res:
            dd_ref[...] = (gate_ref[...] * dx).astype(BF16)
            dgate_ref[...] += jnp.sum(dx * delta_ref[...], axis=0, keepdims=True)

    row = pl.BlockSpec((T, D), lambda i: (i, 0))
    vec = pl.BlockSpec((1, D), lambda i: (0, 0))
    vshape = jax.ShapeDtypeStruct((1, D), F32)
    if res:
        ins, in_specs = (x, dh, dres, g, sc, delta, gate), [row, row, row, vec, vec, row, vec]
        out_shape = (jax.ShapeDtypeStruct((S, D), F32), vshape, vshape, jax.ShapeDtypeStruct((S, D), BF16), vshape)
        out_specs = (row, vec, vec, row, vec)
    else:
        ins, in_specs = (x, dh, dres, g, sc), [row, row, row, vec, vec]
        out_shape = (jax.ShapeDtypeStruct((S, D), F32), vshape, vshape)
        out_specs = (row, vec, vec)
    return pl.pallas_call(body, name=name, out_shape=out_shape, grid=(S // T,), in_specs=in_specs,
                          out_specs=out_specs, compiler_params=_params("arbitrary"))(*ins)


def _final_fwd_bwd(x, delta, gate, gfin, target, name):
    S, D = x.shape
    T = min(256, S)

    def body(x_ref, delta_ref, gate_ref, g_ref, t_ref, loss_ref, dx_ref, dd_ref, dgate_ref, dg_ref):
        i = pl.program_id(0)
        dl = delta_ref[...]
        xv = x_ref[...] + gate_ref[...] * dl
        r = lax.rsqrt(jnp.mean(xv * xv, axis=-1, keepdims=True) + RMS_EPS)
        xh = xv * r
        gv = g_ref[...]
        e = xh * gv - t_ref[...]
        dy = e * (1.0 / D)
        dxh = dy * gv
        dx = r * (dxh - xh * jnp.mean(dxh * xh, axis=-1, keepdims=True))
        dx_ref[...] = dx
        dd_ref[...] = (gate_ref[...] * dx).astype(BF16)

        @pl.when(i == 0)
        def _():
            loss_ref[...] = jnp.zeros_like(loss_ref)
            dgate_ref[...] = jnp.zeros_like(dgate_ref)
            dg_ref[...] = jnp.zeros_like(dg_ref)

        loss_ref[...] += jnp.sum(e * e, axis=0, keepdims=True)
        dgate_ref[...] += jnp.sum(dx * dl, axis=0, keepdims=True)
        dg_ref[...] += jnp.sum(dy * xh, axis=0, keepdims=True)

    row = pl.BlockSpec((T, D), lambda i: (i, 0))
    vec = pl.BlockSpec((1, D), lambda i: (0, 0))
    vshape = jax.ShapeDtypeStruct((1, D), F32)
    return pl.pallas_call(
        body, name=name,
        out_shape=(vshape, jax.ShapeDtypeStruct((S, D), F32), jax.ShapeDtypeStruct((S, D), BF16), vshape, vshape),
        grid=(S // T,), in_specs=[row, row, vec, vec, row], out_specs=(vec, row, row, vec, vec),
        compiler_params=_params("arbitrary"))(x, delta, gate, gfin, target)


def _split3(v):
    hi = v.astype(BF16)
    r1 = v - hi.astype(F32)
    mid = r1.astype(BF16)
    lo = (r1 - mid.astype(F32)).astype(BF16)
    return hi, mid, lo


def _tri_dot(tri, v):
    hi, mid, lo = _split3(v)
    d = functools.partial(jnp.dot, preferred_element_type=F32)
    return d(tri, hi) + d(tri, mid) + d(tri, lo)


def _fgate_fwd(rest, bpad, fblk, name):
    S = rest.shape[0]
    CH = min(256, S)
    nch = S // CH

    def body(f_ref, b_ref, o_ref):
        row = lax.broadcasted_iota(jnp.int32, (CH, CH), 0)
        col = lax.broadcasted_iota(jnp.int32, (CH, CH), 1)
        tri = (row >= col).astype(BF16)

        def step(ci, carry):
            rows = pl.ds(pl.multiple_of(ci * CH, CH), CH)
            z = f_ref[rows, :] + b_ref[...]
            lf = jnp.minimum(z, 0.0) - jnp.log(1.0 + jnp.exp(-jnp.abs(z)))
            o_ref[rows, :] = _tri_dot(tri, lf) + carry
            return carry + jnp.sum(lf, axis=0, keepdims=True)

        lax.fori_loop(0, nch, step, jnp.zeros((1, LANES), F32))

    return pl.pallas_call(
        body, name=name, out_shape=jax.ShapeDtypeStruct((S, LANES), F32), grid=(1,),
        in_specs=[pl.BlockSpec((S, LANES), lambda i: (0, fblk)), pl.BlockSpec((1, LANES), lambda i: (0, 0))],
        out_specs=pl.BlockSpec((S, LANES), lambda i: (0, 0)),
        compiler_params=_params("arbitrary"))(rest, bpad)


def _fgate_bwd(rest, bpad, dF, fblk, name):
    S = rest.shape[0]
    CH = min(256, S)
    nch = S // CH

    def body(f_ref, b_ref, df_ref, o_ref, db_ref):
        row = lax.broadcasted_iota(jnp.int32, (CH, CH), 0)
        col = lax.broadcasted_iota(jnp.int32, (CH, CH), 1)
        tri = (col >= row).astype(BF16)

        def step(n, carry):
            sfx_carry, db = carry
            ci = nch - 1 - n
            rows = pl.ds(pl.multiple_of(ci * CH, CH), CH)
            z = f_ref[rows, :] + b_ref[...]
            dfv = df_ref[rows, :]
            dz = (_tri_dot(tri, dfv) + sfx_carry) * _sigmoid(-z)
            o_ref[rows, :] = dz.astype(BF16)
            return sfx_carry + jnp.sum(dfv, axis=0, keepdims=True), db + jnp.sum(dz, axis=0, keepdims=True)

        zero = jnp.zeros((1, LANES), F32)
        _, db = lax.fori_loop(0, nch, step, (zero, zero))
        db_ref[...] = db

    blk = pl.BlockSpec((S, LANES), lambda i: (0, 0))
    return pl.pallas_call(
        body, name=name,
        out_shape=(jax.ShapeDtypeStruct((S, LANES), BF16), jax.ShapeDtypeStruct((1, LANES), F32)), grid=(1,),
        in_specs=[pl.BlockSpec((S, LANES), lambda i: (0, fblk)), pl.BlockSpec((1, LANES), lambda i: (0, 0)), blk],
        out_specs=(blk, pl.BlockSpec((1, LANES), lambda i: (0, 0))),
        compiler_params=_params("arbitrary"))(rest, bpad, dF)


_NT = (((1,), (1,)), ((), ()))
LOG2E = 1.4426950408889634
ATTN_TILE = 512


def _blocked_rows(a, TA):
    H, S = a.shape
    return a.reshape(H, S // TA, 1, TA)


def _attn_fwd(qkv, nfb, H, mix_cols, name, rider=None):
    S = qkv.shape[0]
    TA = min(ATTN_TILE, S)
    nb = S // TA
    c = HEAD_DIM ** -0.5 * LOG2E

    def body(q_ref, k_ref, v_ref, nf_ref, o_ref, o32_ref, lse_ref, m_ref, l_ref, acc_ref):
        i = pl.program_id(1)
        m_ref[...] = jnp.full_like(m_ref, -jnp.inf)
        l_ref[...] = jnp.zeros_like(l_ref)
        acc_ref[...] = jnp.zeros_like(acc_ref)

        def block(j, masked):
            rows = pl.ds(pl.multiple_of(j * TA, TA), TA)
            st = (lax.dot_general(k_ref[rows, :], q_ref[...], _NT, preferred_element_type=F32) * c
                  + jnp.tile(nf_ref[0, rows, :], (1, TA // LANES)))
            if masked:
                key = lax.broadcasted_iota(jnp.int32, (TA, TA), 0)
                qry = lax.broadcasted_iota(jnp.int32, (TA, TA), 1)
                st = jnp.where(key <= qry, st, -jnp.inf)
            m_old = m_ref[...]
            m_new = jnp.maximum(m_old, jnp.max(st, axis=0, keepdims=True))
            alpha = jnp.exp2(m_old - m_new)
            pt = jnp.exp2(st - m_new)
            l_ref[...] = alpha * l_ref[...] + jnp.sum(pt, axis=0, keepdims=True)
            acc_ref[...] = alpha * acc_ref[...] + lax.dot_general(v_ref[rows, :], pt.astype(BF16), _TN,
                                                                  preferred_element_type=F32)
            m_ref[...] = m_new

        def loop(jj, carry):
            block(2 * jj, False)
            block(2 * jj + 1, False)
            return carry

        lax.fori_loop(0, i // 2, loop, 0)

        @pl.when(i % 2 == 1)
        def _():
            block(i - 1, False)

        block(i, True)
        o = jnp.transpose(acc_ref[...] / l_ref[...])
        o32_ref[...] = o
        o_ref[...] = o.astype(BF16)
        lse_ref[0, 0] = m_ref[...] + jnp.log(l_ref[...]) * LOG2E

    qblk = pl.BlockSpec((TA, HEAD_DIM), lambda h, i: (i, h))
    call = dict(
        body=body, n_in=4, n_out=3, grid=(H, nb),
        in_specs=[qblk,
                  pl.BlockSpec((S, HEAD_DIM), lambda h, i: (0, H + h)),
                  pl.BlockSpec((S, HEAD_DIM), lambda h, i: (0, 2 * H + h)),
                  pl.BlockSpec((1, S, LANES), lambda h, i: (h, 0, 0))],
        out_specs=[qblk, qblk, pl.BlockSpec((1, 1, 1, TA), lambda h, i: (h, i, 0, 0))],
        scratch_shapes=[pltpu.VMEM((1, TA), F32), pltpu.VMEM((1, TA), F32), pltpu.VMEM((HEAD_DIM, TA), F32)],
        out_shape=[jax.ShapeDtypeStruct((S, mix_cols), BF16), jax.ShapeDtypeStruct((S, H * HEAD_DIM), F32),
                   jax.ShapeDtypeStruct((H, nb, 1, TA), F32)])
    return _call(call, name, ("parallel", "parallel"), rider, [qkv, qkv, qkv, nfb])


def _attn_delta(o, do, H, name):
    S = o.shape[0]
    T = min(512, S)

    def body(o_ref, do_ref, d_ref):
        d_ref[0] = jnp.sum(o_ref[...].astype(F32) * do_ref[...].astype(F32), axis=-1, keepdims=True)

    blk = pl.BlockSpec((T, HEAD_DIM), lambda h, i: (i, h))
    return pl.pallas_call(
        body, name=name, out_shape=jax.ShapeDtypeStruct((H, S, 1), F32), grid=(H, S // T),
        in_specs=[blk, blk], out_specs=pl.BlockSpec((1, T, 1), lambda h, i: (h, i, 0)),
        compiler_params=_params("parallel", "parallel"))(o, do)


def _ds_tile(k, q, v, do, nfb, lse_row, delta_row, c, masked):
    TK, TQ = k.shape[0], q.shape[0]
    st = lax.dot_general(k, q, _NT, preferred_element_type=F32) * c + jnp.tile(nfb, (1, TQ // LANES))
    pt = jnp.exp2(st - lse_row)
    if masked:
        key = lax.broadcasted_iota(jnp.int32, (TK, TQ), 0)
        qry = lax.broadcasted_iota(jnp.int32, (TK, TQ), 1)
        pt = jnp.where(key <= qry, pt, 0.0)
    dpt = lax.dot_general(v, do, _NT, preferred_element_type=F32)
    return pt, pt * (dpt - delta_row)


def _attn_bwd(qkv, nfb, do, lse, delta, H, name, rider=None):
    S = qkv.shape[0]
    TA = min(ATTN_TILE, S)
    nb = S // TA
    scale = HEAD_DIM ** -0.5
    c = scale * LOG2E

    def body(q_ref, k_ref, v_ref, nf_ref, do_ref, lse_ref, dl_ref, dq_ref, dk_ref, dv_ref, drow_ref, dnf_ref,
             dq_acc, dk_acc, dv_acc, dnf_acc):
        j = pl.program_id(1)

        @pl.when(j == 0)
        def _():
            dq_acc[...] = jnp.zeros_like(dq_acc)
            drow_ref[...] = jnp.zeros_like(drow_ref)

        dk_acc[...] = jnp.zeros_like(dk_acc)
        dv_acc[...] = jnp.zeros_like(dv_acc)
        dnf_acc[...] = jnp.zeros_like(dnf_acc)
        kb = k_ref[...]

        def block(i, masked):
            rows = pl.ds(pl.multiple_of(i * TA, TA), TA)
            qb = q_ref[rows, :]
            dob = do_ref[rows, :]
            pt, dst = _ds_tile(kb, qb, v_ref[...], dob, nf_ref[0], lse_ref[0, i], dl_ref[0, i], c, masked)
            dsb = dst.astype(BF16)
            dv_acc[...] += jnp.dot(pt.astype(BF16), dob, preferred_element_type=F32)
            dk_acc[...] += jnp.dot(dsb, qb, preferred_element_type=F32)
            dq_acc[rows, :] += lax.dot_general(dsb, kb, _TN, preferred_element_type=F32)
            drow_ref[0, i] += jnp.sum(dst, axis=0, keepdims=True)
            part = dst[:, 0:LANES]
            for t in range(1, TA // LANES):
                part = part + dst[:, t * LANES:(t + 1) * LANES]
            dnf_acc[...] += part

        def loop(ii, carry):
            block(j + 1 + 2 * ii, False)
            block(j + 2 + 2 * ii, False)
            return carry

        block(j, True)
        rest = nb - 1 - j
        lax.fori_loop(0, rest // 2, loop, 0)

        @pl.when(rest % 2 == 1)
        def _():
            block(nb - 1, False)

        dk_ref[...] = (dk_acc[...] * scale).astype(BF16)
        dv_ref[...] = dv_acc[...].astype(BF16)
        dnf_ref[0] = jnp.sum(dnf_acc[...], axis=-1, keepdims=True)

        @pl.when(j == nb - 1)
        def _():
            dq_ref[...] = (dq_acc[...] * scale).astype(BF16)

    full = pl.BlockSpec((S, HEAD_DIM), lambda h, j: (0, h))
    row_stat = pl.BlockSpec((1, nb, 1, TA), lambda h, j: (h, 0, 0, 0))
    kblk = lambda c0: pl.BlockSpec((TA, HEAD_DIM), lambda h, j: (j, c0 + h))
    shp = jax.ShapeDtypeStruct((S, H * HEAD_DIM), BF16)
    call = dict(
        body=body, n_in=7, n_out=5, grid=(H, nb),
        in_specs=[full, kblk(H), kblk(2 * H), pl.BlockSpec((1, TA, LANES), lambda h, j: (h, j, 0)), full,
                  row_stat, row_stat],
        out_specs=[full, kblk(0), kblk(0), row_stat, pl.BlockSpec((1, TA, 1), lambda h, j: (h, j, 0))],
        scratch_shapes=[pltpu.VMEM((S, HEAD_DIM), F32), pltpu.VMEM((TA, HEAD_DIM), F32),
                        pltpu.VMEM((TA, HEAD_DIM), F32), pltpu.VMEM((TA, LANES), F32)],
        out_shape=[shp, shp, shp, jax.ShapeDtypeStruct((H, nb, 1, TA), F32), jax.ShapeDtypeStruct((H, S, 1), F32)])
    return _call(call, name, ("parallel", "arbitrary"), rider, [qkv, qkv, qkv, nfb, do, lse, delta])


def _taps(ext_ref, w_ref, K, base, r0, rows, cols, reverse=False, init=None):
    acc = init
    for k in range(K):
        wk = w_ref[(K - 1 - k) if reverse else k:((K - 1 - k) if reverse else k) + 1, cols]
        term = wk * ext_ref[base + k + r0:base + k + r0 + rows, cols]
        acc = term if acc is None else acc + term
    return acc


def _prev_blk(T, H):
    return lambda i: jnp.maximum(i * (T // H) - 1, 0)


def _next_blk(T, H, S):
    return lambda i: jnp.minimum((i + 1) * (T // H), S // H - 1)


def _conf_fwd(rest, w, b, lng, lnb, mix, name):
    S = rest.shape[0]
    K, C = w.shape
    H, T = 32, min(256, S)
    RS = min(64, T)
    base = H - (K - 1)

    def body(cv_ref, cg_ref, cvp_ref, cgp_ref, w_ref, b_ref, g_ref, bb_ref, mix_ref, o_ref, cc_ref, ext_ref):
        i = pl.program_id(0)
        ext_ref[0:H, :] = jnp.where(i > 0, cvp_ref[...] * _sigmoid(cgp_ref[...]), 0.0)
        ext_ref[H:H + T, :] = cv_ref[...] * _sigmoid(cg_ref[...])
        for r0 in range(0, T, RS):
            cc = _taps(ext_ref, w_ref, K, base, r0, RS, slice(None), init=jnp.broadcast_to(b_ref[...], (RS, C)))
            cc_ref[r0:r0 + RS, :] = cc
            xc = cc - jnp.mean(cc, axis=-1, keepdims=True)
            y = xc * lax.rsqrt(jnp.mean(xc * xc, axis=-1, keepdims=True) + LN_EPS) * g_ref[...] + bb_ref[...]
            o_ref[r0:r0 + RS, :] = (y * _sigmoid(y)).astype(BF16)

    pb = _prev_blk(T, H)
    cur = lambda cb: pl.BlockSpec((T, C), lambda i: (i, cb))
    prev = lambda cb: pl.BlockSpec((H, C), lambda i: (pb(i), cb))
    full = lambda a: pl.BlockSpec(a.shape, lambda i: (0, 0))
    col_blk = (mix.shape[1] - 2 * C) // C
    return pl.pallas_call(
        body, name=name, grid=(S // T,),
        out_shape=(jax.ShapeDtypeStruct(mix.shape, BF16), jax.ShapeDtypeStruct((S, C), F32)),
        in_specs=[cur(0), cur(1), prev(0), prev(1), full(w), full(b), full(lng), full(lnb),
                  pl.BlockSpec(memory_space=pl.ANY)],
        out_specs=(pl.BlockSpec((T, C), lambda i: (i, col_blk)), pl.BlockSpec((T, C), lambda i: (i, 0))),
        input_output_aliases={8: 0},
        scratch_shapes=[pltpu.VMEM((H + T, C), F32)],
        compiler_params=_params("parallel"))(rest, rest, rest, rest, w, b, lng, lnb, mix)


def _conf_bwd(rest, cc_all, dcs, w, lng, lnb, name):
    S = rest.shape[0]
    K, C = w.shape
    H, T = 32, min(256, S)
    RS = 32
    nI = S // T
    base = H - (K - 1)

    def body(cv_ref, cg_ref, cvp_ref, cgp_ref, cc_ref, ccn_ref, do_ref, don_ref, w_ref, g_ref, bb_ref,
             dcvg_ref, dw_ref, dvec_ref, ext_ref, dcc_ref):
        i = pl.program_id(0)
        ext_ref[0:H, :] = jnp.where(i > 0, cvp_ref[...] * _sigmoid(cgp_ref[...]), 0.0)
        ext_ref[H:H + T, :] = cv_ref[...] * _sigmoid(cg_ref[...])

        @pl.when(i == 0)
        def _():
            dw_ref[...] = jnp.zeros_like(dw_ref)
            dvec_ref[...] = jnp.zeros_like(dvec_ref)

        db = jnp.zeros((1, C), F32)
        dg = jnp.zeros((1, C), F32)
        dbb = jnp.zeros((1, C), F32)
        for r0 in range(0, T + H, RS):
            cc = cc_ref[r0:r0 + RS, :] if r0 < T else ccn_ref[r0 - T:r0 - T + RS, :]
            xc = cc - jnp.mean(cc, axis=-1, keepdims=True)
            r = lax.rsqrt(jnp.mean(xc * xc, axis=-1, keepdims=True) + LN_EPS)
            xh = xc * r
            y = xh * g_ref[...] + bb_ref[...]
            sy = _sigmoid(y)
            if r0 < T:
                d_o = do_ref[r0:r0 + RS, :]
            else:
                d_o = jnp.where(i < nI - 1, don_ref[r0 - T:r0 - T + RS, :], 0.0)
            dy = d_o * (sy * (1.0 + y * (1.0 - sy)))
            dxh = dy * g_ref[...]
            dcc = r * (dxh - jnp.mean(dxh, axis=-1, keepdims=True)
                       - xh * jnp.mean(dxh * xh, axis=-1, keepdims=True))
            dcc_ref[r0:r0 + RS, :] = dcc
            if r0 < T:
                dbb = dbb + jnp.sum(dy, axis=0, keepdims=True)
                dg = dg + jnp.sum(dy * xh, axis=0, keepdims=True)
                db = db + jnp.sum(dcc, axis=0, keepdims=True)
        dvec_ref[0:1, :] += db
        dvec_ref[1:2, :] += dg
        dvec_ref[2:3, :] += dbb
        R2 = min(64, T)
        for k in range(K):
            s = jnp.zeros((1, C), F32)
            for r0 in range(0, T, R2):
                s = s + jnp.sum(dcc_ref[r0:r0 + R2, :] * ext_ref[base + k + r0:base + k + r0 + R2, :],
                                axis=0, keepdims=True)
            dw_ref[k:k + 1, :] += s
        for r0 in range(0, T, R2):
            dci = _taps(dcc_ref, w_ref, K, 0, r0, R2, slice(None), reverse=True)
            cvv = cv_ref[r0:r0 + R2, :]
            sg = _sigmoid(cg_ref[r0:r0 + R2, :])
            dcvg_ref[r0:r0 + R2, 0:C] = (dci * sg).astype(BF16)
            dcvg_ref[r0:r0 + R2, C:2 * C] = (dci * cvv * sg * (1.0 - sg)).astype(BF16)

    pb, nb_ = _prev_blk(T, H), _next_blk(T, H, S)
    cur = lambda cb: pl.BlockSpec((T, C), lambda i: (i, cb))
    prev = lambda cb: pl.BlockSpec((H, C), lambda i: (pb(i), cb))
    nxt = lambda cb: pl.BlockSpec((H, C), lambda i: (nb_(i), cb))
    full = lambda a: pl.BlockSpec(a.shape, lambda i: (0, 0))
    return pl.pallas_call(
        body, name=name,
        out_shape=(jax.ShapeDtypeStruct((S, 2 * C), BF16), jax.ShapeDtypeStruct((32, C), F32),
                   jax.ShapeDtypeStruct((8, C), F32)),
        grid=(nI,),
        in_specs=[cur(0), cur(1), prev(0), prev(1), cur(0), nxt(0), cur(0), nxt(0),
                  full(w), full(lng), full(lnb)],
        out_specs=(pl.BlockSpec((T, 2 * C), lambda i: (i, 0)), pl.BlockSpec((32, C), lambda i: (0, 0)),
                   pl.BlockSpec((8, C), lambda i: (0, 0))),
        scratch_shapes=[pltpu.VMEM((H + T, C), F32), pltpu.VMEM((T + H, C), F32)],
        compiler_params=_params("arbitrary"))(rest, rest, rest, rest, cc_all, cc_all, dcs, dcs, w, lng, lnb)


def _sconv_fwd(rest, w, mix, name):
    S = rest.shape[0]
    K, C = w.shape
    H, T = 8, min(256, S)
    RS = min(64, T)
    base = H - (K - 1)

    def body(sx_ref, sb_ref, sc_ref, sxp_ref, scp_ref, w_ref, mix_ref, o_ref, ext_ref):
        i = pl.program_id(0)
        ext_ref[0:H, :] = jnp.where(i > 0, sxp_ref[...] * scp_ref[...], 0.0)
        ext_ref[H:H + T, :] = sx_ref[...] * sc_ref[...]
        for r0 in range(0, T, RS):
            cz = _taps(ext_ref, w_ref, K, base, r0, RS, slice(None))
            o_ref[r0:r0 + RS, :] = (sb_ref[r0:r0 + RS, :] * cz).astype(BF16)

    pb = _prev_blk(T, H)
    cur = lambda cb: pl.BlockSpec((T, C), lambda i: (i, cb))
    prev = lambda cb: pl.BlockSpec((H, C), lambda i: (pb(i), cb))
    col_blk = (mix.shape[1] - C) // C
    return pl.pallas_call(
        body, name=name, out_shape=jax.ShapeDtypeStruct(mix.shape, BF16), grid=(S // T,),
        in_specs=[cur(2), cur(3), cur(4), prev(2), prev(4), pl.BlockSpec(w.shape, lambda i: (0, 0)),
                  pl.BlockSpec(memory_space=pl.ANY)],
        out_specs=pl.BlockSpec((T, C), lambda i: (i, col_blk)), input_output_aliases={6: 0},
        scratch_shapes=[pltpu.VMEM((H + T, C), F32)],
        compiler_params=_params("parallel"))(rest, rest, rest, rest, rest, w, mix)


def _sconv_bwd(rest, dcs, w, name):
    S = rest.shape[0]
    K, C = w.shape
    H, T = 8, min(256, S)
    RS = min(64, T)
    nI = S // T
    base = H - (K - 1)

    def body(sx_ref, sb_ref, sc_ref, sxp_ref, scp_ref, sbn_ref, do_ref, don_ref, w_ref,
             dout_ref, dw_ref, ext_ref, dcv_ref):
        i = pl.program_id(0)
        ext_ref[0:H, :] = jnp.where(i > 0, sxp_ref[...] * scp_ref[...], 0.0)
        ext_ref[H:H + T, :] = sx_ref[...] * sc_ref[...]
        dcv_ref[0:T, :] = do_ref[...] * sb_ref[...]
        dcv_ref[T:T + H, :] = jnp.where(i < nI - 1, don_ref[...] * sbn_ref[...], 0.0)

        @pl.when(i == 0)
        def _():
            dw_ref[...] = jnp.zeros_like(dw_ref)

        for k in range(K):
            s = jnp.zeros((1, C), F32)
            for r0 in range(0, T, RS):
                s = s + jnp.sum(dcv_ref[r0:r0 + RS, :] * ext_ref[base + k + r0:base + k + r0 + RS, :],
                                axis=0, keepdims=True)
            dw_ref[k:k + 1, :] += s
        for r0 in range(0, T, RS):
            cz = _taps(ext_ref, w_ref, K, base, r0, RS, slice(None))
            dz = _taps(dcv_ref, w_ref, K, 0, r0, RS, slice(None), reverse=True)
            dout_ref[r0:r0 + RS, 0:C] = (dz * sc_ref[r0:r0 + RS, :]).astype(BF16)
            dout_ref[r0:r0 + RS, C:2 * C] = (do_ref[r0:r0 + RS, :] * cz).astype(BF16)
            dout_ref[r0:r0 + RS, 2 * C:3 * C] = (dz * sx_ref[r0:r0 + RS, :]).astype(BF16)

    pb, nb_ = _prev_blk(T, H), _next_blk(T, H, S)
    cur = lambda cb: pl.BlockSpec((T, C), lambda i: (i, cb))
    prev = lambda cb: pl.BlockSpec((H, C), lambda i: (pb(i), cb))
    nxt = lambda cb: pl.BlockSpec((H, C), lambda i: (nb_(i), cb))
    return pl.pallas_call(
        body, name=name,
        out_shape=(jax.ShapeDtypeStruct((S, 3 * C), BF16), jax.ShapeDtypeStruct((8, C), F32)),
        grid=(nI,),
        in_specs=[cur(2), cur(3), cur(4), prev(2), prev(4), nxt(3), cur(1), nxt(1),
                  pl.BlockSpec(w.shape, lambda i: (0, 0))],
        out_specs=(pl.BlockSpec((T, 3 * C), lambda i: (i, 0)), pl.BlockSpec((8, C), lambda i: (0, 0))),
        scratch_shapes=[pltpu.VMEM((H + T, C), F32), pltpu.VMEM((T + H, C), F32)],
        compiler_params=_params("arbitrary"))(rest, rest, rest, rest, rest, rest, dcs, dcs, w)


FFN_ROWS = 256
FFN_HALO = 16


def _shift_mats(T):
    r = lax.broadcasted_iota(jnp.int32, (T, T), 0)
    c = lax.broadcasted_iota(jnp.int32, (T, T), 1)
    return jnp.stack([r == c + 1, r == c + 2, c == r + 1, c == r + 2]).astype(BF16)


def _edge_rows(strip, shift, first):
    sub = lax.broadcasted_iota(jnp.int32, strip.shape, 0)
    if first:
        return jnp.where(sub < shift, pltpu.roll(strip, shift, 0), 0.0)
    return jnp.where(sub >= 8 - shift, pltpu.roll(strip, 8 - shift, 0), 0.0)


def _conv3_tile(x_ref, prev_ref, w_ref, b_ref, down, has_prev, u_ref, xm_refs=None):
    T = x_ref.shape[0]
    x = x_ref[...]
    xm1 = jnp.dot(down[0], x, preferred_element_type=F32)
    xm2 = jnp.dot(down[1], x, preferred_element_type=F32)
    u_ref[...] = b_ref[...] + w_ref[0:1, :] * xm2 + w_ref[1:2, :] * xm1 + w_ref[2:3, :] * x.astype(F32)
    tail = jnp.where(has_prev, prev_ref[...].astype(F32)[FFN_HALO - 8:, :], 0.0)
    p1, p2 = _edge_rows(tail, 1, True), _edge_rows(tail, 2, True)
    u_ref[0:8, :] += w_ref[0:1, :] * p2 + w_ref[1:2, :] * p1
    if xm_refs is not None:
        xm_refs[0][...] = xm1
        xm_refs[1][...] = xm2
        xm_refs[0][0:8, :] += p1
        xm_refs[1][0:8, :] += p2


def _ffn_fwd(hu, w, b, name):
    S, F2 = hu.shape
    Fd = F2 // 2
    K = w.shape[0]
    assert K == 3
    T = min(FFN_ROWS, S)
    tc = _tile(Fd, 512)
    nJ = Fd // tc

    def body(sh_ref, g_ref, v_ref, gp_ref, vp_ref, wg_ref, wv_ref, bg_ref, bv_ref, o_ref, ug_ref, uv_ref):
        i = pl.program_id(1)
        down = (sh_ref[0], sh_ref[1])
        _conv3_tile(g_ref, gp_ref, wg_ref, bg_ref, down, i > 0, ug_ref)
        _conv3_tile(v_ref, vp_ref, wv_ref, bv_ref, down, i > 0, uv_ref)
        ug = ug_ref[...]
        o_ref[...] = (ug * _sigmoid(ug) * uv_ref[...]).astype(BF16)

    pb = _prev_blk(T, FFN_HALO)
    cur = lambda off: pl.BlockSpec((T, tc), lambda j, i: (i, j + off))
    prev = lambda off: pl.BlockSpec((FFN_HALO, tc), lambda j, i: (pb(i), j + off))
    wsp = lambda off: pl.BlockSpec((K, tc), lambda j, i: (0, j + off))
    bsp = lambda off: pl.BlockSpec((1, tc), lambda j, i: (0, j + off))
    return pl.pallas_call(
        body, name=name, out_shape=jax.ShapeDtypeStruct((S, Fd), BF16), grid=(nJ, S // T),
        in_specs=[pl.BlockSpec((4, T, T), lambda j, i: (0, 0, 0)),
                  cur(0), cur(nJ), prev(0), prev(nJ), wsp(0), wsp(nJ), bsp(0), bsp(nJ)],
        out_specs=pl.BlockSpec((T, tc), lambda j, i: (i, j)),
        scratch_shapes=[pltpu.VMEM((T, tc), F32), pltpu.VMEM((T, tc), F32)],
        compiler_params=_params("parallel", "parallel"))(_shift_mats(T), hu, hu, hu, hu, w, w, b, b)


def _ffn_bwd(hu, dact, w, b, name):
    S, F2 = hu.shape
    Fd = F2 // 2
    K = w.shape[0]
    assert K == 3
    T = min(FFN_ROWS, S)
    tc = _tile(Fd, 512)
    nJ = Fd // tc
    nI = S // T

    def body(sh_ref, g_ref, v_ref, gp_ref, vp_ref, gn_ref, vn_ref, da_ref, dan_ref, wg_ref, wv_ref, bg_ref, bv_ref,
             dg_ref, dv_ref, dwg_ref, dwv_ref, ug_ref, uv_ref, g1_ref, g2_ref, v1_ref, v2_ref, dh_ref):
        i = pl.program_id(1)
        down, up = (sh_ref[0], sh_ref[1]), (sh_ref[2], sh_ref[3])
        _conv3_tile(g_ref, gp_ref, wg_ref, bg_ref, down, i > 0, ug_ref, (g1_ref, g2_ref))
        _conv3_tile(v_ref, vp_ref, wv_ref, bv_ref, down, i > 0, uv_ref, (v1_ref, v2_ref))

        @pl.when(i == 0)
        def _():
            dwg_ref[...] = jnp.zeros_like(dwg_ref)
            dwv_ref[...] = jnp.zeros_like(dwv_ref)

        def d_u(ug, uv, d_a):
            sg = _sigmoid(ug)
            return d_a * uv * (sg * (1.0 + ug * (1.0 - sg))), d_a * (ug * sg)

        dug, duv = d_u(ug_ref[...], uv_ref[...], da_ref[...].astype(F32))

        def next_rows(x_ref, xn_ref, w_ref, b_ref):
            strip = jnp.concatenate([x_ref[T - FFN_HALO:, :].astype(F32)[FFN_HALO - 8:, :],
                                     xn_ref[...].astype(F32)[0:8, :]], axis=0)
            return (b_ref[...] + w_ref[0:1, :] * strip[6:14, :] + w_ref[1:2, :] * strip[7:15, :]
                    + w_ref[2:3, :] * strip[8:16, :])

        d_an = jnp.where(i < nI - 1, dan_ref[...].astype(F32)[0:8, :], 0.0)
        dug_n, duv_n = d_u(next_rows(g_ref, gn_ref, wg_ref, bg_ref), next_rows(v_ref, vn_ref, wv_ref, bv_ref), d_an)

        for du, du_n, x_ref, x1_ref, x2_ref, w_ref, dw_ref, out_ref in (
                (dug, dug_n, g_ref, g1_ref, g2_ref, wg_ref, dwg_ref, dg_ref),
                (duv, duv_n, v_ref, v1_ref, v2_ref, wv_ref, dwv_ref, dv_ref)):
            dw_ref[0:1, :] += jnp.sum(du * x2_ref[...], axis=0, keepdims=True)
            dw_ref[1:2, :] += jnp.sum(du * x1_ref[...], axis=0, keepdims=True)
            dw_ref[2:3, :] += jnp.sum(du * x_ref[...].astype(F32), axis=0, keepdims=True)
            dw_ref[3:4, :] += jnp.sum(du, axis=0, keepdims=True)
            dub = du.astype(BF16)
            dh_ref[...] = (w_ref[2:3, :] * du + w_ref[1:2, :] * jnp.dot(up[0], dub, preferred_element_type=F32)
                           + w_ref[0:1, :] * jnp.dot(up[1], dub, preferred_element_type=F32))
            nxt = du_n.astype(BF16).astype(F32)
            dh_ref[T - 8:, :] += w_ref[1:2, :] * _edge_rows(nxt, 1, False) + w_ref[0:1, :] * _edge_rows(nxt, 2, False)
            out_ref[...] = dh_ref[...].astype(BF16)

    pb, nbb = _prev_blk(T, FFN_HALO), _next_blk(T, FFN_HALO, S)
    cur = lambda off: pl.BlockSpec((T, tc), lambda j, i: (i, j + off))
    prev = lambda off: pl.BlockSpec((FFN_HALO, tc), lambda j, i: (pb(i), j + off))
    nxt = lambda off: pl.BlockSpec((FFN_HALO, tc), lambda j, i: (nbb(i), j + off))
    wsp = lambda off: pl.BlockSpec((K, tc), lambda j, i: (0, j + off))
    bsp = lambda off: pl.BlockSpec((1, tc), lambda j, i: (0, j + off))
    half = jax.ShapeDtypeStruct((S, Fd), BF16)
    dws = jax.ShapeDtypeStruct((8, Fd), F32)
    tile = pltpu.VMEM((T, tc), F32)
    return pl.pallas_call(
        body, name=name, out_shape=(half, half, dws, dws), grid=(nJ, nI),
        in_specs=[pl.BlockSpec((4, T, T), lambda j, i: (0, 0, 0)),
                  cur(0), cur(nJ), prev(0), prev(nJ), nxt(0), nxt(nJ), cur(0), nxt(0),
                  wsp(0), wsp(nJ), bsp(0), bsp(nJ)],
        out_specs=(pl.BlockSpec((T, tc), lambda j, i: (i, j)), pl.BlockSpec((T, tc), lambda j, i: (i, j)),
                   pl.BlockSpec((8, tc), lambda j, i: (0, j)), pl.BlockSpec((8, tc), lambda j, i: (0, j))),
        scratch_shapes=[tile] * 7,
        compiler_params=_params("parallel", "arbitrary"))(
            _shift_mats(T), hu, hu, hu, hu, hu, hu, dact, dact, w, w, b, b)


def _position():
    return lax.axis_index("x"), lax.axis_index("y"), lax.axis_index("c")


def _slot(px, py, pc):
    return 4 * px + 2 * py + pc


def _gather_copies(x_ref, out_ref, sems, r, starting=False):
    send_sems, recv_sems, local_sems = sems
    px, py, pc = _position()
    me, sibling = (px, py, pc), (px, py, 1 - pc)
    chips = [(1 - px, py), (px, 1 - py), (1 - px, 1 - py)]

    def copy(k, block, to, src=None):
        dst = out_ref.at[_slot(*block)]
        return pltpu.make_async_remote_copy(
            src_ref=dst if src is None else src, dst_ref=dst,
            send_sem=send_sems.at[7 * r + k], recv_sem=recv_sems.at[7 * r + k], device_id=to, device_id_type=MESH)

    mine = pltpu.make_async_copy(x_ref, out_ref.at[_slot(*me)], local_sems.at[r])
    first = [copy(0, me, sibling, src=x_ref)] + [copy(1 + n, me, (*chip, pc), src=x_ref)
                                                  for n, chip in enumerate(chips)]
    if starting:
        return mine, first
    passed = [copy(4 + n, (*chip, pc), sibling) for n, chip in enumerate(chips)]
    landed = [copy(1 + n, (*chip, pc), me) for n, chip in enumerate(chips)]
    from_sibling = [copy(0, sibling, me)] + [copy(4 + n, (*chip, 1 - pc), me) for n, chip in enumerate(chips)]
    return mine, first, passed, landed, from_sibling


def _scatter_copies(g_ref, out_ref, sems, r, starting=False):
    send_sems, recv_sems, local_sems = sems
    px, py, pc = _position()
    me = _slot(px, py, pc)
    mine = pltpu.make_async_copy(g_ref.at[me], out_ref.at[me], local_sems.at[r])
    peers = [(px ^ fx, py ^ fy, pc ^ fc) for fx, fy, fc in PEER_FLIPS]

    def copy(k, peer, src_slot, dst_slot):
        return pltpu.make_async_remote_copy(
            src_ref=g_ref.at[src_slot], dst_ref=out_ref.at[dst_slot],
            send_sem=send_sems.at[7 * r + k], recv_sem=recv_sems.at[7 * r + k], device_id=peer, device_id_type=MESH)

    sends = [copy(k, peer, _slot(*peer), me) for k, peer in enumerate(peers)]
    if starting:
        return mine, sends
    arrivals = [copy(k, peer, me, _slot(*peer)) for k, peer in enumerate(peers)]
    return mine, sends, arrivals


def _rider_start(kind, in_ref, out_ref, sems, r):
    if kind == "gather":
        mine, first = _gather_copies(in_ref, out_ref, sems, r, starting=True)
    else:
        mine, first = _scatter_copies(in_ref, out_ref, sems, r, starting=True)
    mine.start()
    for cp in first:
        cp.start()


def _rider_finish(kind, in_ref, out_ref, sems, r):
    if kind == "gather":
        mine, first, passed, landed, from_sibling = _gather_copies(in_ref, out_ref, sems, r)
        for cp, fwd in zip(landed, passed):
            cp.wait_recv()
            fwd.start()
        for cp in from_sibling:
            cp.wait_recv()
        for cp in first + passed:
            cp.wait_send()
    else:
        mine, sends, arrivals = _scatter_copies(in_ref, out_ref, sems, r)
        for cp in arrivals:
            cp.wait_recv()
        for cp in sends:
            cp.wait_send()
    mine.wait()


def _ride(call, rider):
    kind, arrs = rider
    nr, n_in, n_out, grid, body = len(arrs), call["n_in"], call["n_out"], call["grid"], call["body"]

    def wrapped(*refs):
        ins, r_in = refs[:n_in], refs[n_in:n_in + nr]
        outs, r_out = refs[n_in + nr:n_in + nr + n_out], refs[n_in + nr + n_out:n_in + 2 * nr + n_out]
        scratch, sems = refs[n_in + 2 * nr + n_out:-3], refs[-3:]
        ids = [pl.program_id(a) for a in range(len(grid))]
        first = functools.reduce(lambda p, q: p & q, [i == 0 for i in ids])
        last = functools.reduce(lambda p, q: p & q, [i == n - 1 for i, n in zip(ids, grid)])

        @pl.when(first)
        def _():
            for r in range(nr):
                _rider_start(kind, r_in[r], r_out[r], sems, r)

        body(*ins, *outs, *scratch)

        @pl.when(last)
        def _():
            for r in range(nr):
                _rider_finish(kind, r_in[r], r_out[r], sems, r)

    hbm = pl.BlockSpec(memory_space=pl.ANY)
    return dict(
        body=wrapped, grid=grid,
        in_specs=call["in_specs"] + [hbm] * nr,
        out_specs=tuple(call["out_specs"]) + (hbm,) * nr,
        out_shape=tuple(call["out_shape"]) + tuple(
            jax.ShapeDtypeStruct(x.shape if kind == "scatter" else (N_DEV,) + x.shape, x.dtype) for x in arrs),
        scratch_shapes=call["scratch_shapes"] + [pltpu.SemaphoreType.DMA((7 * nr,)), pltpu.SemaphoreType.DMA((7 * nr,)),
                                                 pltpu.SemaphoreType.DMA((nr,))])


def _call(call, name, semantics, rider, operands):
    if rider is not None:
        call, semantics, operands = _ride(call, rider), ("arbitrary",) * len(call["grid"]), operands + list(rider[1])
    out = pl.pallas_call(call["body"], name=name, grid=call["grid"], in_specs=call["in_specs"],
                         out_specs=tuple(call["out_specs"]), out_shape=tuple(call["out_shape"]),
                         scratch_shapes=call["scratch_shapes"], compiler_params=_params(*semantics))(*operands)
    if rider is None:
        return out, []
    n = len(out) - len(rider[1])
    return out[:n], list(out[n:])


def _all_gather(x, in_vmem, name):
    space = pltpu.VMEM if in_vmem else pl.ANY

    def body(x_ref, out_ref, send_sems, recv_sems, local_sems):
        sems = (send_sems, recv_sems, local_sems)
        _rider_start("gather", x_ref, out_ref, sems, 0)
        _rider_finish("gather", x_ref, out_ref, sems, 0)

    return pl.pallas_call(
        body, name=name, out_shape=jax.ShapeDtypeStruct((N_DEV,) + x.shape, x.dtype),
        in_specs=[pl.BlockSpec(memory_space=space)], out_specs=pl.BlockSpec(memory_space=space),
        scratch_shapes=[pltpu.SemaphoreType.DMA((7,)), pltpu.SemaphoreType.DMA((7,)), pltpu.SemaphoreType.DMA((1,))],
        compiler_params=pltpu.CompilerParams(vmem_limit_bytes=VMEM_LIMIT_BYTES),
    )(x)


def _adam_sum(stage, w, m, v, layer, prev, name):
    n = stage.shape[0]
    L, R, C = w.shape
    tr = R if R * C <= 256 * 1024 else _row_tile(R, C)
    c1 = 1.0 / (1.0 - ADAM_B1 ** ADAM_STEP)
    c2 = 1.0 / (1.0 - ADAM_B2 ** ADAM_STEP)

    def body(*refs):
        st_ref, w_ref, m_ref, v_ref = refs[:4]
        g_ref, d_ref, nm_ref, nv_ref = refs[-4:]
        g = st_ref[0].astype(F32)
        for s in range(1, n):
            g = g + st_ref[s].astype(F32)
        wv = w_ref[0]
        mn = ADAM_B1 * m_ref[0] + (1.0 - ADAM_B1) * g
        vn = ADAM_B2 * v_ref[0] + (1.0 - ADAM_B2) * (g * g)
        g_ref[0] = g
        nm_ref[0] = mn
        nv_ref[0] = vn
        d_ref[0] = -ADAM_LR * ((mn * c1) / (jnp.sqrt(vn * c2) + ADAM_EPS) + ADAM_WD * wv)

    lay = pl.BlockSpec((1, tr, C), lambda i: (layer, i, 0))
    in_specs = [pl.BlockSpec((n, tr, C), lambda i: (0, i, 0)), lay, lay, lay]
    ins = [stage, w, m, v]
    aliases = {}
    if prev is not None:
        in_specs += [pl.BlockSpec(memory_space=pl.ANY)] * 4
        ins += list(prev)
        aliases = {4: 0, 5: 1, 6: 2, 7: 3}
    shp = jax.ShapeDtypeStruct((L, R, C), F32)
    return pl.pallas_call(
        body, name=name, out_shape=(shp, shp, shp, shp), grid=(R // tr,),
        in_specs=in_specs, out_specs=(lay, lay, lay, lay), input_output_aliases=aliases,
        compiler_params=_params("parallel"))(*ins)


def _row_tile(R, C):
    cpad = -(-C // LANES) * LANES
    want = max(16, (256 * 1024) // cpad)
    best = 16
    for t in range(16, R + 1, 16):
        if R % t == 0 and t <= want:
            best = t
    return best


def _sum_slabs(st, name):
    n, R, C = st.shape
    tr = R if n * R * C * 4 <= (12 << 20) else _row_tile(R, C)

    def body(st_ref, o_ref):
        g = st_ref[0]
        for s in range(1, n):
            g = g + st_ref[s]
        o_ref[...] = g

    return pl.pallas_call(
        body, name=name, out_shape=jax.ShapeDtypeStruct((R, C), F32), grid=(R // tr,),
        in_specs=[pl.BlockSpec((n, tr, C), lambda i: (0, i, 0))], out_specs=pl.BlockSpec((tr, C), lambda i: (i, 0)),
        compiler_params=_params("parallel"))(st)


def _pack(arrs):
    flat = [a.reshape(-1).astype(F32) for a in arrs]
    sizes = [f.shape[0] for f in flat]
    total = sum(sizes)
    padded = -(-total // (16 * LANES)) * (16 * LANES)
    if padded > total:
        flat.append(jnp.zeros((padded - total,), F32))
    return jnp.concatenate(flat).reshape(padded // LANES, LANES), (sizes, [a.shape for a in arrs])


def _unpack(packed, layout, lead=()):
    sizes, shapes = layout
    flat = packed.reshape(lead + (-1,))
    out, off = [], 0
    for sz, shp in zip(sizes, shapes):
        out.append(flat[..., off:off + sz].reshape(lead + tuple(shp)))
        off += sz
    return out


class _NoComm:
    col_slabs = None

    def __init__(self, wts):
        self.wts, self.grads = wts, {}

    def weight(self, l, name):
        return self.wts[l][name]

    def gather_rider(self, l, names):
        return None

    def scatter_rider(self, name, l, g):
        self.grads[(name, l)] = g
        return None


def _local_step(x, tgt, ada, mix_norm_g, comm, b_forget, conf_dw_w, conf_dw_b, conf_ln_g, conf_ln_b, sc_dw_w,
                ffn_norm_g, ffn_dw_w, ffn_dw_b, final_norm_g):
    S, D = x.shape
    L = ada.shape[0]

    def mm_gather(a, b, dtype, name, l_next, names):
        rider = comm.gather_rider(l_next, names) if l_next < L else None
        if rider is None:
            return _matmul(a, b, dtype, name=name)
        out, got = _matmul(a, b, dtype, name="cm_" + name, rider=rider)
        comm.gathered(l_next, names, got)
        return out

    def mm_scatter(a, b, dtype, name, wname, l, g):
        rider = comm.scatter_rider(wname, l, g)
        if rider is None:
            return _matmul(a, b, dtype, name=name, b_transposed=True)
        out, got = _matmul(a, b, dtype, name="cm_" + name, rider=rider, b_transposed=True)
        comm.scattered(wname, l, got[0])
        return out

    H = b_forget.shape[1]
    DA = H * HEAD_DIM
    C = conf_dw_b.shape[1]
    NQ = 3 * DA
    NR = 5 * C + LANES
    fblk = (5 * C) // LANES
    row = lambda a: a.reshape(1, -1)
    adav = ada.reshape(L, N_ADA, 1, D)

    saved = []
    xcur, delta, gate = x, None, None
    for l in range(L):
        sh_m, sc_m, g_m, sh_f, sc_f, g_f = [adav[l, n] for n in range(N_ADA)]
        w = functools.partial(comm.weight, l)
        x1, h1 = _site_fwd(xcur, delta, gate, row(mix_norm_g[l]), sc_m, sh_m, name="site_fwd_mix")
        qkv = _matmul(h1, w("w_in_perm")[:, :NQ], BF16, name="mm_qkv")
        rest = _matmul(h1, w("w_in_perm")[:, NQ:], F32, name="mm_rest")
        bpad = jnp.zeros((1, LANES), F32).at[0, :H].set(b_forget[l])
        Fc = _fgate_fwd(rest, bpad, fblk, name="fgate_fwd")
        nf = -LOG2E * jnp.transpose(Fc[:, :H])
        g_l, g_names = (0, ("w_up", "w_down", "w_out")) if l == 0 else (l + 1, ("w_up", "w_down"))
        rider = comm.gather_rider(g_l, g_names) if g_l < L else None
        (mixcat, attn32, lse), got = _attn_fwd(qkv, jnp.broadcast_to(nf[:, :, None], (H, S, LANES)), H, DA + 2 * C,
                                               name="attn_fwd" if rider is None else "cm_attn_fwd", rider=rider)
        if rider is not None:
            comm.gathered(g_l, g_names, got)
        mixcat, conf_cc = _conf_fwd(rest, conf_dw_w[l], row(conf_dw_b[l]), row(conf_ln_g[l]), row(conf_ln_b[l]),
                                    mixcat, name="conf_fwd")
        mixcat = _sconv_fwd(rest, sc_dw_w[l], mixcat, name="sconv_fwd")
        mixed = _matmul(mixcat, w("w_out"), F32, name="mm_out")
        x2, h2 = _site_fwd(x1, mixed, g_m, row(ffn_norm_g[l]), sc_f, sh_f, name="site_fwd_ffn")
        hu = mm_gather(h2, w("w_up"), BF16, "mm_up", l + 1, ("w_up", "w_down") if l == 0 else ("w_in", "w_out"))
        act = _ffn_fwd(hu, ffn_dw_w[l], row(ffn_dw_b[l]), name="ffn_fwd")
        ffn_out = mm_gather(act, w("w_down"), F32, "mm_down", l + 1 if l == 0 else L, ("w_in", "w_out"))
        saved.append(dict(x1=x1, h1=h1, qkv=qkv, rest=rest, bpad=bpad, nf=nf, attn32=attn32, lse=lse, mixcat=mixcat,
                          mixed=mixed, x2=x2, h2=h2, hu=hu, act=act, ffn_out=ffn_out, conf_cc=conf_cc))
        xcur, delta, gate = x2, ffn_out, g_f

    loss_lanes, dx, d_delta, d_gate, d_gfin = _final_fwd_bwd(xcur, delta, gate, row(final_norm_g), tgt, name="final")
    loss = (0.5 / D) * jnp.sum(loss_lanes)

    grads = dict(final_norm_g=d_gfin[0], ada=[None] * L, mix_norm_g=[None] * L, ffn_norm_g=[None] * L,
                 b_forget=[None] * L, conf_dw_w=[None] * L, conf_dw_b=[None] * L, conf_ln_g=[None] * L,
                 conf_ln_b=[None] * L, sc_dw_w=[None] * L, ffn_dw_w=[None] * L, ffn_dw_b=[None] * L)
    K3 = ffn_dw_w.shape[1]
    for l in reversed(range(L)):
        sv, w = saved[l], functools.partial(comm.weight, l)
        sh_m, sc_m, g_m, sh_f, sc_f, g_f = [adav[l, n] for n in range(N_ADA)]
        d_gf = d_gate
        g_down = _matmul_tn(sv["act"], d_delta, BF16, name="mm_dw_down")
        dact = _matmul(d_delta, w("w_down"), BF16, name="mm_dact", b_transposed=True)
        dhu_g, dhu_v, dwg, dwv = _ffn_bwd(sv["hu"], dact, ffn_dw_w[l], row(ffn_dw_b[l]), name="ffn_bwd")
        grads["ffn_dw_w"][l] = jnp.concatenate([dwg[:K3], dwv[:K3]], axis=1)
        grads["ffn_dw_b"][l] = jnp.concatenate([dwg[K3], dwv[K3]])
        dhu = (dhu_g, dhu_v)
        g_up = _matmul_tn(sv["h2"], dhu, BF16, name="mm_dw_up", col_slabs=comm.col_slabs)
        dh2 = mm_scatter(dhu, w("w_up"), F32, "mm_dh2", "w_down", l, g_down)
        dx, d_sh_f, d_a_f, d_mixed, d_gm = _site_bwd(sv["x2"], dh2, dx, row(ffn_norm_g[l]), sc_f,
                                                      sh_f, sv["mixed"], g_m, name="site_bwd_ffn")
        grads["ffn_norm_g"][l] = (d_a_f * (1.0 + sc_f))[0]
        d_sc_f = d_a_f * row(ffn_norm_g[l])
        g_out = _matmul_tn(sv["mixcat"], d_mixed, BF16, name="mm_dw_out")
        dattn = _matmul(d_mixed, w("w_out")[:DA], BF16, name="mm_dattn", b_transposed=True)
        dcs = _matmul(d_mixed, w("w_out")[DA:], F32, name="mm_dcs", b_transposed=True)
        delta_a = _blocked_rows(_attn_delta(sv["attn32"], dattn, H, name="attn_delta")[:, :, 0], min(ATTN_TILE, S))
        nfb = jnp.broadcast_to(sv["nf"][:, :, None], (H, S, LANES))
        r_up, r_out = comm.scatter_rider("w_up", l, g_up), comm.scatter_rider("w_out", l, g_out)
        rider = None if r_up is None else ("scatter", r_up[1] + r_out[1])
        (dq, dk, dv, drow, dnf), got = _attn_bwd(sv["qkv"], nfb, dattn, sv["lse"], delta_a, H,
                                                 name="attn_bwd" if rider is None else "cm_attn_bwd", rider=rider)
        if rider is not None:
            comm.scattered("w_up", l, got[0])
            comm.scattered("w_out", l, got[1])
        dF = jnp.zeros((S, LANES), F32).at[:, :H].set(jnp.transpose(drow.reshape(H, S) - dnf[:, :, 0]))
        dfl, dbf = _fgate_bwd(sv["rest"], sv["bpad"], dF, fblk, name="fgate_bwd")
        grads["b_forget"][l] = dbf[0, :H]
        dcvg, dcw, dcvec = _conf_bwd(sv["rest"], sv["conf_cc"], dcs, conf_dw_w[l], row(conf_ln_g[l]),
                                     row(conf_ln_b[l]), name="conf_bwd")
        grads["conf_dw_w"][l] = dcw[:conf_dw_w.shape[1]]
        grads["conf_dw_b"][l], grads["conf_ln_g"][l], grads["conf_ln_b"][l] = dcvec[0], dcvec[1], dcvec[2]
        dsc3, dsw = _sconv_bwd(sv["rest"], dcs, sc_dw_w[l], name="sconv_bwd")
        grads["sc_dw_w"][l] = dsw[:sc_dw_w.shape[1]]
        dproj = jnp.concatenate([dq, dk, dv, dcvg, dsc3, dfl], axis=1)
        g_in = _matmul_tn(sv["h1"], dproj, BF16, name="mm_dw_in")
        dh1 = mm_scatter(dproj, w("w_in_perm"), F32, "mm_dh1", "w_in_perm", l, g_in)
        if l > 0:
            pv = saved[l - 1]
            g_f_prev = adav[l - 1, 5]
            dx, d_sh_m, d_a_m, d_delta, d_gate = _site_bwd(sv["x1"], dh1, dx, row(mix_norm_g[l]), sc_m, sh_m,
                                                           pv["ffn_out"], g_f_prev, name="site_bwd_mix")
        else:
            dx, d_sh_m, d_a_m = _site_bwd(sv["x1"], dh1, dx, row(mix_norm_g[l]), sc_m, sh_m, None, None,
                                          name="site_bwd_first")
        grads["mix_norm_g"][l] = (d_a_m * (1.0 + sc_m))[0]
        d_sc_m = d_a_m * row(mix_norm_g[l])
        grads["ada"][l] = jnp.concatenate([d_sh_m, d_sc_m, d_gm, d_sh_f, d_sc_f, d_gf], axis=1)[0]
    return loss, dx, grads


def kernel(x, c, ada_w, ada_b, mix_norm_g, w_in, b_forget, conf_dw_w, conf_dw_b, conf_ln_g, conf_ln_b, sc_dw_w, w_out, ffn_norm_g, w_up, ffn_dw_w, ffn_dw_b, w_down, final_norm_g, loss_target, m_ada_w, m_ada_b, m_mix_norm_g, m_w_in, m_b_forget, m_conf_dw_w, m_conf_dw_b, m_conf_ln_g, m_conf_ln_b, m_sc_dw_w, m_w_out, m_ffn_norm_g, m_w_up, m_ffn_dw_w, m_ffn_dw_b, m_w_down, m_final_norm_g, v_ada_w, v_ada_b, v_mix_norm_g, v_w_in, v_b_forget, v_conf_dw_w, v_conf_dw_b, v_conf_ln_g, v_conf_ln_b, v_sc_dw_w, v_w_out, v_ffn_norm_g, v_w_up, v_ffn_dw_w, v_ffn_dw_b, v_w_down, v_final_norm_g):
    L, D, ada_loc = ada_w.shape
    S = x.shape[1]
    H = b_forget.shape[1]
    DA = H * HEAD_DIM
    C = conf_dw_b.shape[1]
    in_loc = w_in.shape[2]
    IN = in_loc * N_DEV
    px, py, pc = _position()
    me = _slot(px, py, pc)

    pk, lay = _pack([c, conf_dw_w, sc_dw_w, ffn_dw_w])
    gathered = _all_gather(pk, True, name="ag_small_fwd")
    c_all, cw_all, sw_all, fw_all = _unpack(gathered, lay, lead=(N_DEV,))
    c_all = c_all[:, 0]
    unshard = lambda a: jnp.moveaxis(a, 0, 2).reshape(a.shape[1], a.shape[2], -1)
    conf_w_full, sc_w_full, ffn_w_full = unshard(cw_all), unshard(sw_all), unshard(fw_all)
    c_act = c_all * jax.nn.sigmoid(c_all)
    c_act16 = jnp.zeros((16, D), F32).at[:N_DEV].set(c_act).astype(BF16)
    ada_cols = jnp.stack([_matmul(c_act16, ada_w[l].astype(BF16), F32, name="mm_ada")[:N_DEV] for l in range(L)])
    ada_g = _all_gather(ada_cols.reshape(L * N_DEV, ada_loc), True, name="ag_ada")
    ada_mine = lax.dynamic_index_in_dim(ada_g.reshape(N_DEV, L, N_DEV, ada_loc), me, axis=2, keepdims=False)
    ada = jnp.moveaxis(ada_mine, 0, 1).reshape(L, N_DEV * ada_loc) + ada_b

    NQ = 3 * DA
    PR = NQ + 5 * C
    shards = dict(w_in=w_in.astype(BF16), w_out=w_out.astype(BF16), w_up=w_up.astype(BF16),
                  w_down=w_down.astype(BF16))

    def shard_cols(g):
        return jnp.moveaxis(g.reshape(g.shape[0], N_DEV, -1), 1, 0)

    def shard_rows(g):
        return g.reshape(N_DEV, -1, g.shape[1])

    class MeshComm:
        col_slabs = N_DEV

        def __init__(self):
            self.got = {0: {"w_in": _all_gather(shards["w_in"][0], False, name="ag_w_in")}}
            self.full, self.stage = {}, {}

        def weight(self, l, name):
            if (l, name) not in self.full:
                g = self.got[l]
                if name == "w_in_perm":
                    wi = jnp.moveaxis(g["w_in"], 0, 1).reshape(D, IN)
                    full = jnp.concatenate([wi[:, :NQ], wi[:, NQ + H:], wi[:, NQ:NQ + H],
                                            jnp.zeros((D, LANES - H), BF16)], axis=1)
                elif name == "w_up":
                    full = jnp.moveaxis(g["w_up"], 0, 1).reshape(D, -1)
                else:
                    full = g[name].reshape(-1, D)
                self.full[(l, name)] = full
            return self.full[(l, name)]

        def gather_rider(self, l, names):
            return "gather", [shards[n][l] for n in names]

        def gathered(self, l, names, outs):
            self.got.setdefault(l, {}).update(zip(names, outs))

        def scatter_rider(self, name, l, g):
            if name == "w_in_perm":
                slabs = shard_cols(jnp.concatenate([g[:, :NQ], g[:, PR:PR + H], g[:, NQ:PR]], axis=1))
            elif name == "w_up":
                slabs = g
            else:
                slabs = shard_rows(g)
            return "scatter", [slabs]

        def scattered(self, name, l, out):
            self.stage[(name, l)] = out

    comm = MeshComm()
    loss_loc, dx, gr = _local_step(x[0], loss_target[0], ada, mix_norm_g, comm, b_forget, conf_w_full, conf_dw_b,
                                   conf_ln_g, conf_ln_b, sc_w_full, ffn_norm_g, ffn_w_full, ffn_dw_b, final_norm_g)
    loss = lax.psum(loss_loc, ("x", "y", "c"))

    small_names = ["ada", "mix_norm_g", "ffn_norm_g", "b_forget", "conf_dw_b", "conf_ln_g", "conf_ln_b",
                   "ffn_dw_b", "conf_dw_w", "sc_dw_w", "ffn_dw_w"]
    pk, lay = _pack([jnp.stack(gr[n]) for n in small_names] + [gr["final_norm_g"]])
    parts = _all_gather(pk, True, name="ag_small_bwd")
    tot = _unpack(_sum_slabs(parts, name="sum_small"), lay)
    g_small = dict(zip(small_names + ["final_norm_g"], tot))
    d_ada_all = _unpack(parts, lay, lead=(N_DEV,))[0]
    my_cols = lambda a, n: lax.dynamic_slice_in_dim(a, me * n, n, axis=a.ndim - 1)

    c_act_t = jnp.zeros((D, LANES), F32).at[:, :N_DEV].set(jnp.transpose(c_act)).astype(BF16)
    res = None
    for l in range(L):
        d_loc = jnp.zeros((LANES, ada_loc), F32).at[:N_DEV].set(my_cols(d_ada_all[:, l], ada_loc)).astype(BF16)
        g_l = _matmul(c_act_t, d_loc, F32, name="mm_dada")
        res = _adam_sum(g_l[None], ada_w, m_ada_w, v_ada_w, l, res, name="adam_ada_w")
    out_ada_w = res

    big = {}
    for nm, key, wq, mq, vq in (("w_down", "w_down", w_down, m_w_down, v_w_down), ("w_up", "w_up", w_up, m_w_up, v_w_up),
                                ("w_out", "w_out", w_out, m_w_out, v_w_out), ("w_in", "w_in_perm", w_in, m_w_in, v_w_in)):
        res = None
        for l in reversed(range(L)):
            res = _adam_sum(comm.stage[(key, l)], wq, mq, vq, l, res, name="adam_" + nm)
        big[nm] = res

    K31, K3 = conf_dw_w.shape[1], sc_dw_w.shape[1]
    sm = [("ada_b", ada_b, m_ada_b, v_ada_b, g_small["ada"]),
          ("mix_norm_g", mix_norm_g, m_mix_norm_g, v_mix_norm_g, g_small["mix_norm_g"]),
          ("b_forget", b_forget, m_b_forget, v_b_forget, g_small["b_forget"]),
          ("conf_dw_w", conf_dw_w, m_conf_dw_w, v_conf_dw_w, my_cols(g_small["conf_dw_w"], conf_dw_w.shape[2])),
          ("conf_dw_b", conf_dw_b, m_conf_dw_b, v_conf_dw_b, g_small["conf_dw_b"]),
          ("conf_ln_g", conf_ln_g, m_conf_ln_g, v_conf_ln_g, g_small["conf_ln_g"]),
          ("conf_ln_b", conf_ln_b, m_conf_ln_b, v_conf_ln_b, g_small["conf_ln_b"]),
          ("sc_dw_w", sc_dw_w, m_sc_dw_w, v_sc_dw_w, my_cols(g_small["sc_dw_w"], sc_dw_w.shape[2])),
          ("ffn_norm_g", ffn_norm_g, m_ffn_norm_g, v_ffn_norm_g, g_small["ffn_norm_g"]),
          ("ffn_dw_w", ffn_dw_w, m_ffn_dw_w, v_ffn_dw_w, my_cols(g_small["ffn_dw_w"], ffn_dw_w.shape[2])),
          ("ffn_dw_b", ffn_dw_b, m_ffn_dw_b, v_ffn_dw_b, g_small["ffn_dw_b"]),
          ("final_norm_g", final_norm_g, m_final_norm_g, v_final_norm_g, g_small["final_norm_g"])]
    pw, lay = _pack([t[1] for t in sm])
    pm, _ = _pack([t[2] for t in sm])
    pv, _ = _pack([t[3] for t in sm])
    pg, _ = _pack([t[4] for t in sm])
    sres = _adam_sum(pg[None], pw[None], pm[None], pv[None], 0, None, name="adam_small")
    s_g, s_d, s_m, s_v = [dict(zip([t[0] for t in sm], _unpack(r[0], lay))) for r in sres]

    def pick(idx, name):
        if name == "ada_w":
            return out_ada_w[idx]
        if name in big:
            return big[name][idx]
        return (s_g, s_d, s_m, s_v)[idx][name]

    order = ["ada_w", "ada_b", "mix_norm_g", "w_in", "b_forget", "conf_dw_w", "conf_dw_b", "conf_ln_g", "conf_ln_b",
             "sc_dw_w", "w_out", "ffn_norm_g", "w_up", "ffn_dw_w", "ffn_dw_b", "w_down", "final_norm_g"]
    outs = [loss, dx[None]]
    for idx in range(4):
        outs += [pick(idx, n) for n in order]
    return tuple(outs)
```

```python
import functools

import jax
import jax.numpy as jnp
from jax import lax
from jax.experimental import pallas as pl
from jax.experimental.pallas import tpu as pltpu

F32 = jnp.float32
BF16 = jnp.bfloat16
RMS_EPS = 1e-6
LN_EPS = 1e-5
HEAD_DIM = 128
N_ADA = 6
ADAM_LR = 0.001
ADAM_B1 = 0.9
ADAM_B2 = 0.999
ADAM_EPS = 1e-08
ADAM_WD = 0.01
ADAM_STEP = 10
N_DEV = 8
LANES = 128
VMEM_LIMIT_BYTES = 56 * 1024 * 1024
MM_TILE = 1024
MM_TILE_WIDE = 1536
MM_TILE_N_MAX = 2816
MXU_WIDTH = 256
MM_VMEM_BUDGET = 48 * 1024 * 1024
MESH = pl.DeviceIdType.MESH
PEER_FLIPS = ((0, 0, 1), (1, 0, 0), (0, 1, 0), (1, 1, 0), (1, 0, 1), (0, 1, 1), (1, 1, 1))


def _params(*sem):
    return pltpu.CompilerParams(dimension_semantics=sem, vmem_limit_bytes=VMEM_LIMIT_BYTES)


def _tile(n, cap):
    if n <= cap:
        return n
    for t in range(cap - cap % LANES, 0, -LANES):
        if n % t == 0:
            return t
    raise ValueError(f"no tile for {n}")


def _mm_tile(n):
    t = _tile(n, MM_TILE)
    return t if t == min(n, MM_TILE) else _tile(n, MM_TILE_WIDE)


def _n_tile(n, vmem_bytes):
    for step in (MXU_WIDTH, LANES):
        for t in range(min(n, MM_TILE_N_MAX) // step * step, 0, -step):
            if n % t == 0 and vmem_bytes(t) <= MM_VMEM_BUDGET:
                return t
    return n


def _sigmoid(v):
    return jax.nn.sigmoid(v)


def _matmul(a, b, out_dtype, name, rider=None, b_transposed=False):
    a_parts = a if isinstance(a, tuple) else (a,)
    na = len(a_parts)
    M = a_parts[0].shape[0]
    K = sum(p.shape[1] for p in a_parts)
    N = b.shape[0] if b_transposed else b.shape[1]
    tm = _mm_tile(M)
    out_bytes = jnp.dtype(out_dtype).itemsize
    part_k = a_parts[0].shape[1]

    def blocks_bytes(tk_, tn_):
        return 4 * (na * tm * tk_ + tk_ * tn_) + (4 * tm * tn_ if K > tk_ else 0) + 2 * tm * tn_ * out_bytes

    for tk in [t for t in range(part_k, 0, -LANES) if part_k % t == 0]:
        tn = _n_tile(N, functools.partial(blocks_bytes, tk))
        if blocks_bytes(tk, tn) <= MM_VMEM_BUDGET and tn >= min(N, MM_TILE if b_transposed else MM_TILE // 2):
            break
    nk = K // tk
    half = part_k // tk
    grid = (M // tm, N // tn, nk)
    dims = _NT if b_transposed else (((1,), (0,)), ((), ()))
    kind, arrs = rider if rider is not None else (None, [])
    nr = len(arrs)

    def body(*refs):
        a_refs, refs = refs[:na], refs[na - 1:]
        a_ref, b_ref = a_refs[0], refs[1]
        r_in = refs[2:2 + nr]
        o_ref = refs[2 + nr]
        r_out = refs[3 + nr:3 + 2 * nr]
        rest = refs[3 + 2 * nr:]
        i, j, k = pl.program_id(0), pl.program_id(1), pl.program_id(2)
        if nr:
            sems = rest[-3:]

            @pl.when((i == 0) & (j == 0) & (k == 0))
            def _():
                for r in range(nr):
                    _rider_start(kind, r_in[r], r_out[r], sems, r)

            if _late_step(grid[0]):
                @pl.when((i == _late_step(grid[0])) & (j == 0) & (k == 0))
                def _():
                    for r in range(nr):
                        _rider_forward(kind, r_in[r], r_out[r], sems, r)

        if nk == 1:
            o_ref[...] = lax.dot_general(a_ref[...], b_ref[...], dims, preferred_element_type=F32).astype(o_ref.dtype)
        else:
            acc_ref = rest[0]

            @pl.when(k == 0)
            def _():
                acc_ref[...] = jnp.zeros_like(acc_ref)

            def accumulate(part_ref):
                acc_ref[...] += lax.dot_general(part_ref[...], b_ref[...], dims, preferred_element_type=F32)

            if na == 1:
                accumulate(a_ref)
            else:
                pl.when(k < half)(lambda: accumulate(a_refs[0]))
                pl.when(k >= half)(lambda: accumulate(a_refs[1]))

            @pl.when(k == nk - 1)
            def _():
                o_ref[...] = acc_ref[...].astype(o_ref.dtype)

        if nr:
            @pl.when((i == grid[0] - 1) & (j == grid[1] - 1) & (k == nk - 1))
            def _():
                for r in range(nr):
                    _rider_finish(kind, r_in[r], r_out[r], sems, r, forwarded=bool(_late_step(grid[0])))

    scratch = [] if nk == 1 else [pltpu.VMEM((tm, tn), F32)]
    hbm = pl.BlockSpec(memory_space=pl.ANY)
    if na == 1:
        a_specs = [pl.BlockSpec((tm, tk), lambda i, j, k: (i, k))]
    else:
        a_specs = [pl.BlockSpec((tm, tk), lambda i, j, k: (i, jnp.minimum(k, half - 1))),
                   pl.BlockSpec((tm, tk), lambda i, j, k: (i, jnp.maximum(k - half, 0)))]
    out_shape = jax.ShapeDtypeStruct((M, N), out_dtype)
    out_specs = pl.BlockSpec((tm, tn), lambda i, j, k: (i, j))
    if nr:
        scratch += [pltpu.SemaphoreType.DMA((7 * nr,)), pltpu.SemaphoreType.DMA((7 * nr,)),
                    pltpu.SemaphoreType.DMA((nr,))]
        out_shape = (out_shape,) + tuple(
            jax.ShapeDtypeStruct(x.shape if kind == "scatter" else (N_DEV,) + x.shape, x.dtype) for x in arrs)
        out_specs = (out_specs,) + (hbm,) * nr
    out = pl.pallas_call(
        body, name=name,
        out_shape=out_shape,
        grid=grid,
        in_specs=a_specs + [pl.BlockSpec((tn, tk), lambda i, j, k: (j, k)) if b_transposed
                            else pl.BlockSpec((tk, tn), lambda i, j, k: (k, j))] + [hbm] * nr,
        out_specs=out_specs,
        scratch_shapes=scratch,
        compiler_params=_params(*(("arbitrary",) * 3 if nr else ("parallel", "parallel", "arbitrary"))),
    )(*a_parts, b, *arrs)
    return (out[0], list(out[1:])) if nr else out


_TN = (((0,), (0,)), ((), ()))


def _matmul_tn(a, b, out_dtype, name, col_slabs=None):
    b_parts = b if isinstance(b, tuple) else (b,)
    S, M = a.shape
    N = sum(p.shape[1] for p in b_parts)
    tm = _mm_tile(M)
    out_bytes = jnp.dtype(out_dtype).itemsize

    def blocks_bytes(ts_, tn_):
        return 4 * (ts_ * tm + len(b_parts) * ts_ * tn_) + 4 * tm * tn_ + 2 * tm * tn_ * out_bytes

    for ts in (_tile(S, 2 * MM_TILE), _tile(S, MM_TILE)):
        tn = N // col_slabs if col_slabs else _n_tile(b_parts[0].shape[1], functools.partial(blocks_bytes, ts))
        if blocks_bytes(ts, tn) <= MM_VMEM_BUDGET and tn >= min(b_parts[0].shape[1], MM_TILE):
            break
    ns = S // ts
    half = b_parts[0].shape[1] // tn

    def body(*refs):
        a_ref, b_refs, (o_ref, acc_ref) = refs[0], refs[1:-2], refs[-2:]
        j, k = pl.program_id(1), pl.program_id(2)

        @pl.when(k == 0)
        def _():
            acc_ref[...] = jnp.zeros_like(acc_ref)

        def accumulate(b_ref):
            acc_ref[...] += lax.dot_general(a_ref[...], b_ref[...], _TN, preferred_element_type=F32)

        if len(b_refs) == 1:
            accumulate(b_refs[0])
        else:
            pl.when(j < half)(lambda: accumulate(b_refs[0]))
            pl.when(j >= half)(lambda: accumulate(b_refs[1]))

        @pl.when(k == ns - 1)
        def _():
            if col_slabs is None:
                o_ref[...] = acc_ref[...].astype(o_ref.dtype)
            else:
                o_ref[0] = acc_ref[...].astype(o_ref.dtype)

    if col_slabs is None:
        out_shape = jax.ShapeDtypeStruct((M, N), out_dtype)
        out_spec = pl.BlockSpec((tm, tn), lambda i, j, k: (i, j))
    else:
        out_shape = jax.ShapeDtypeStruct((col_slabs, M, tn), out_dtype)
        out_spec = pl.BlockSpec((1, tm, tn), lambda i, j, k: (j, i, 0))
    if len(b_parts) == 1:
        b_specs = [pl.BlockSpec((ts, tn), lambda i, j, k: (k, j))]
    else:
        b_specs = [pl.BlockSpec((ts, tn), lambda i, j, k: (jnp.where(j < half, k, ns - 1), jnp.minimum(j, half - 1))),
                   pl.BlockSpec((ts, tn), lambda i, j, k: (jnp.where(j < half, 0, k), jnp.maximum(j - half, 0)))]
    return pl.pallas_call(
        body, name=name,
        out_shape=out_shape,
        grid=(M // tm, N // tn, ns),
        in_specs=[pl.BlockSpec((ts, tm), lambda i, j, k: (k, i))] + b_specs,
        out_specs=out_spec,
        scratch_shapes=[pltpu.VMEM((tm, tn), F32)],
        compiler_params=_params("parallel", "parallel", "arbitrary"),
    )(a, *b_parts)


def _site_fwd(x, delta, gate, g, sc, sh, name):
    S, D = x.shape
    T = min(256, S)
    res = delta is not None

    def body(*refs):
        if res:
            x_ref, d_ref, gate_ref, g_ref, sc_ref, sh_ref, xo_ref, h_ref = refs
            xv = x_ref[...] + gate_ref[...] * d_ref[...]
            xo_ref[...] = xv
        else:
            x_ref, g_ref, sc_ref, sh_ref, h_ref = refs
            xv = x_ref[...]
        r = lax.rsqrt(jnp.mean(xv * xv, axis=-1, keepdims=True) + RMS_EPS)
        a = g_ref[...] * (1.0 + sc_ref[...])
        h_ref[...] = (xv * r * a + sh_ref[...]).astype(BF16)

    row = pl.BlockSpec((T, D), lambda i: (i, 0))
    vec = pl.BlockSpec((1, D), lambda i: (0, 0))
    if res:
        ins, in_specs = (x, delta, gate, g, sc, sh), [row, row, vec, vec, vec, vec]
        out_shape = (jax.ShapeDtypeStruct((S, D), F32), jax.ShapeDtypeStruct((S, D), BF16))
        out_specs = (row, row)
    else:
        ins, in_specs = (x, g, sc, sh), [row, vec, vec, vec]
        out_shape = jax.ShapeDtypeStruct((S, D), BF16)
        out_specs = row
    out = pl.pallas_call(body, name=name, out_shape=out_shape, grid=(S // T,), in_specs=in_specs,
                         out_specs=out_specs, compiler_params=_params("parallel"))(*ins)
    return out if res else (x, out)


def _site_bwd(x, dh, dres, g, sc, sh, delta, gate, name):
    S, D = x.shape
    T = min(256, S)
    res = delta is not None

    def body(*refs):
        if res:
            (x_ref, dh_ref, dres_ref, g_ref, sc_ref, delta_ref, gate_ref,
             dx_ref, dsh_ref, da_ref, dd_ref, dgate_ref) = refs
        else:
            x_ref, dh_ref, dres_ref, g_ref, sc_ref, dx_ref, dsh_ref, da_ref = refs
        i = pl.program_id(0)
        xv = x_ref[...]
        dhv = dh_ref[...]
        r = lax.rsqrt(jnp.mean(xv * xv, axis=-1, keepdims=True) + RMS_EPS)
        xh = xv * r
        dxh = dhv * (g_ref[...] * (1.0 + sc_ref[...]))
        dx = r * (dxh - xh * jnp.mean(dxh * xh, axis=-1, keepdims=True)) + dres_ref[...]
        dx_ref[...] = dx

        @pl.when(i == 0)
        def _():
            dsh_ref[...] = jnp.zeros_like(dsh_ref)
            da_ref[...] = jnp.zeros_like(da_ref)
            if res:
                dgate_ref[...] = jnp.zeros_like(dgate_ref)

        dsh_ref[...] += jnp.sum(dhv, axis=0, keepdims=True)
        da_ref[...] += jnp.sum(dhv * xh, axis=0, keepdims=True)
        if res:
            dd_ref[...] = (gate_ref[...] * dx).astype(BF16)
            dgate_ref[...] += jnp.sum(dx * delta_ref[...], axis=0, keepdims=True)

    row = pl.BlockSpec((T, D), lambda i: (i, 0))
    vec = pl.BlockSpec((1, D), lambda i: (0, 0))
    vshape = jax.ShapeDtypeStruct((1, D), F32)
    if res:
        ins, in_specs = (x, dh, dres, g, sc, delta, gate), [row, row, row, vec, vec, row, vec]
        out_shape = (jax.ShapeDtypeStruct((S, D), F32), vshape, vshape, jax.ShapeDtypeStruct((S, D), BF16), vshape)
        out_specs = (row, vec, vec, row, vec)
    else:
        ins, in_specs = (x, dh, dres, g, sc), [row, row, row, vec, vec]
        out_shape = (jax.ShapeDtypeStruct((S, D), F32), vshape, vshape)
        out_specs = (row, vec, vec)
    return pl.pallas_call(body, name=name, out_shape=out_shape, grid=(S // T,), in_specs=in_specs,
                          out_specs=out_specs, compiler_params=_params("arbitrary"))(*ins)


def _final_fwd_bwd(x, delta, gate, gfin, target, name):
    S, D = x.shape
    T = min(256, S)

    def body(x_ref, delta_ref, gate_ref, g_ref, t_ref, loss_ref, dx_ref, dd_ref, dgate_ref, dg_ref):
        i = pl.program_id(0)
        dl = delta_ref[...]
        xv = x_ref[...] + gate_ref[...] * dl
        r = lax.rsqrt(jnp.mean(xv * xv, axis=-1, keepdims=True) + RMS_EPS)
        xh = xv * r
        gv = g_ref[...]
        e = xh * gv - t_ref[...]
        dy = e * (1.0 / D)
        dxh = dy * gv
        dx = r * (dxh - xh * jnp.mean(dxh * xh, axis=-1, keepdims=True))
        dx_ref[...] = dx
        dd_ref[...] = (gate_ref[...] * dx).astype(BF16)

        @pl.when(i == 0)
        def _():
            loss_ref[...] = jnp.zeros_like(loss_ref)
            dgate_ref[...] = jnp.zeros_like(dgate_ref)
            dg_ref[...] = jnp.zeros_like(dg_ref)

        loss_ref[...] += jnp.sum(e * e, axis=0, keepdims=True)
        dgate_ref[...] += jnp.sum(dx * dl, axis=0, keepdims=True)
        dg_ref[...] += jnp.sum(dy * xh, axis=0, keepdims=True)

    row = pl.BlockSpec((T, D), lambda i: (i, 0))
    vec = pl.BlockSpec((1, D), lambda i: (0, 0))
    vshape = jax.ShapeDtypeStruct((1, D), F32)
    return pl.pallas_call(
        body, name=name,
        out_shape=(vshape, jax.ShapeDtypeStruct((S, D), F32), jax.ShapeDtypeStruct((S, D), BF16), vshape, vshape),
        grid=(S // T,), in_specs=[row, row, vec, vec, row], out_specs=(vec, row, row, vec, vec),
        compiler_params=_params("arbitrary"))(x, delta, gate, gfin, target)


def _split3(v):
    hi = v.astype(BF16)
    r1 = v - hi.astype(F32)
    mid = r1.astype(BF16)
    lo = (r1 - mid.astype(F32)).astype(BF16)
    return hi, mid, lo


def _tri_dot(tri, v):
    hi, mid, lo = _split3(v)
    d = functools.partial(jnp.dot, preferred_element_type=F32)
    return d(tri, hi) + d(tri, mid) + d(tri, lo)


def _fgate_fwd(rest, bpad, fblk, name):
    S = rest.shape[0]
    CH = min(256, S)
    nch = S // CH

    def body(f_ref, b_ref, o_ref):
        row = lax.broadcasted_iota(jnp.int32, (CH, CH), 0)
        col = lax.broadcasted_iota(jnp.int32, (CH, CH), 1)
        tri = (row >= col).astype(BF16)

        def step(ci, carry):
            rows = pl.ds(pl.multiple_of(ci * CH, CH), CH)
            z = f_ref[rows, :] + b_ref[...]
            lf = jnp.minimum(z, 0.0) - jnp.log(1.0 + jnp.exp(-jnp.abs(z)))
            o_ref[rows, :] = _tri_dot(tri, lf) + carry
            return carry + jnp.sum(lf, axis=0, keepdims=True)

        lax.fori_loop(0, nch, step, jnp.zeros((1, LANES), F32))

    return pl.pallas_call(
        body, name=name, out_shape=jax.ShapeDtypeStruct((S, LANES), F32), grid=(1,),
        in_specs=[pl.BlockSpec((S, LANES), lambda i: (0, fblk)), pl.BlockSpec((1, LANES), lambda i: (0, 0))],
        out_specs=pl.BlockSpec((S, LANES), lambda i: (0, 0)),
        compiler_params=_params("arbitrary"))(rest, bpad)


def _fgate_bwd(rest, bpad, dF, fblk, name):
    S = rest.shape[0]
    CH = min(256, S)
    nch = S // CH

    def body(f_ref, b_ref, df_ref, o_ref, db_ref):
        row = lax.broadcasted_iota(jnp.int32, (CH, CH), 0)
        col = lax.broadcasted_iota(jnp.int32, (CH, CH), 1)
        tri = (col >= row).astype(BF16)

        def step(n, carry):
            sfx_carry, db = carry
            ci = nch - 1 - n
            rows = pl.ds(pl.multiple_of(ci * CH, CH), CH)
            z = f_ref[rows, :] + b_ref[...]
            dfv = df_ref[rows, :]
            dz = (_tri_dot(tri, dfv) + sfx_carry) * _sigmoid(-z)
            o_ref[rows, :] = dz.astype(BF16)
            return sfx_carry + jnp.sum(dfv, axis=0, keepdims=True), db + jnp.sum(dz, axis=0, keepdims=True)

        zero = jnp.zeros((1, LANES), F32)
        _, db = lax.fori_loop(0, nch, step, (zero, zero))
        db_ref[...] = db

    blk = pl.BlockSpec((S, LANES), lambda i: (0, 0))
    return pl.pallas_call(
        body, name=name,
        out_shape=(jax.ShapeDtypeStruct((S, LANES), BF16), jax.ShapeDtypeStruct((1, LANES), F32)), grid=(1,),
        in_specs=[pl.BlockSpec((S, LANES), lambda i: (0, fblk)), pl.BlockSpec((1, LANES), lambda i: (0, 0)), blk],
        out_specs=(blk, pl.BlockSpec((1, LANES), lambda i: (0, 0))),
        compiler_params=_params("arbitrary"))(rest, bpad, dF)


_NT = (((1,), (1,)), ((), ()))
LOG2E = 1.4426950408889634
ATTN_TILE = 512


def _blocked_rows(a, TA):
    H, S = a.shape
    return a.reshape(H, S // TA, 1, TA)


def _attn_fwd(qkv, nfb, H, mix_cols, name, rider=None):
    S = qkv.shape[0]
    TA = min(ATTN_TILE, S)
    nb = S // TA
    c = HEAD_DIM ** -0.5 * LOG2E

    def body(q_ref, k_ref, v_ref, nf_ref, o_ref, o32_ref, lse_ref, m_ref, l_ref, acc_ref):
        i = pl.program_id(1)
        m_ref[...] = jnp.full_like(m_ref, -jnp.inf)
        l_ref[...] = jnp.zeros_like(l_ref)
        acc_ref[...] = jnp.zeros_like(acc_ref)

        def block(j, masked):
            rows = pl.ds(pl.multiple_of(j * TA, TA), TA)
            st = (lax.dot_general(k_ref[rows, :], q_ref[...], _NT, preferred_element_type=F32) * c
                  + jnp.tile(nf_ref[0, rows, :], (1, TA // LANES)))
            if masked:
                key = lax.broadcasted_iota(jnp.int32, (TA, TA), 0)
                qry = lax.broadcasted_iota(jnp.int32, (TA, TA), 1)
                st = jnp.where(key <= qry, st, -jnp.inf)
            m_old = m_ref[...]
            m_new = jnp.maximum(m_old, jnp.max(st, axis=0, keepdims=True))
            alpha = jnp.exp2(m_old - m_new)
            pt = jnp.exp2(st - m_new)
            l_ref[...] = alpha * l_ref[...] + jnp.sum(pt, axis=0, keepdims=True)
            acc_ref[...] = alpha * acc_ref[...] + lax.dot_general(v_ref[rows, :], pt.astype(BF16), _TN,
                                                                  preferred_element_type=F32)
            m_ref[...] = m_new

        def loop(jj, carry):
            block(2 * jj, False)
            block(2 * jj + 1, False)
            return carry

        lax.fori_loop(0, i // 2, loop, 0)

        @pl.when(i % 2 == 1)
        def _():
            block(i - 1, False)

        block(i, True)
        o = jnp.transpose(acc_ref[...] / l_ref[...])
        o32_ref[...] = o
        o_ref[...] = o.astype(BF16)
        lse_ref[0, 0] = m_ref[...] + jnp.log(l_ref[...]) * LOG2E

    qblk = pl.BlockSpec((TA, HEAD_DIM), lambda h, i: (i, h))
    call = dict(
        body=body, n_in=4, n_out=3, grid=(H, nb),
        in_specs=[qblk,
                  pl.BlockSpec((S, HEAD_DIM), lambda h, i: (0, H + h)),
                  pl.BlockSpec((S, HEAD_DIM), lambda h, i: (0, 2 * H + h)),
                  pl.BlockSpec((1, S, LANES), lambda h, i: (h, 0, 0))],
        out_specs=[qblk, qblk, pl.BlockSpec((1, 1, 1, TA), lambda h, i: (h, i, 0, 0))],
        scratch_shapes=[pltpu.VMEM((1, TA), F32), pltpu.VMEM((1, TA), F32), pltpu.VMEM((HEAD_DIM, TA), F32)],
        out_shape=[jax.ShapeDtypeStruct((S, mix_cols), BF16), jax.ShapeDtypeStruct((S, H * HEAD_DIM), F32),
                   jax.ShapeDtypeStruct((H, nb, 1, TA), F32)])
    return _call(call, name, ("parallel", "parallel"), rider, [qkv, qkv, qkv, nfb])


def _attn_delta(o, do, H, name):
    S = o.shape[0]
    T = min(512, S)

    def body(o_ref, do_ref, d_ref):
        d_ref[0] = jnp.sum(o_ref[...].astype(F32) * do_ref[...].astype(F32), axis=-1, keepdims=True)

    blk = pl.BlockSpec((T, HEAD_DIM), lambda h, i: (i, h))
    return pl.pallas_call(
        body, name=name, out_shape=jax.ShapeDtypeStruct((H, S, 1), F32), grid=(H, S // T),
        in_specs=[blk, blk], out_specs=pl.BlockSpec((1, T, 1), lambda h, i: (h, i, 0)),
        compiler_params=_params("parallel", "parallel"))(o, do)


def _ds_tile(k, q, v, do, nfb, lse_row, delta_row, c, masked):
    TK, TQ = k.shape[0], q.shape[0]
    st = lax.dot_general(k, q, _NT, preferred_element_type=F32) * c + jnp.tile(nfb, (1, TQ // LANES))
    pt = jnp.exp2(st - lse_row)
    if masked:
        key = lax.broadcasted_iota(jnp.int32, (TK, TQ), 0)
        qry = lax.broadcasted_iota(jnp.int32, (TK, TQ), 1)
        pt = jnp.where(key <= qry, pt, 0.0)
    dpt = lax.dot_general(v, do, _NT, preferred_element_type=F32)
    return pt, pt * (dpt - delta_row)


def _attn_bwd(qkv, nfb, do, lse, delta, H, name, rider=None):
    S = qkv.shape[0]
    TA = min(ATTN_TILE, S)
    nb = S // TA
    scale = HEAD_DIM ** -0.5
    c = scale * LOG2E

    def body(q_ref, k_ref, v_ref, nf_ref, do_ref, lse_ref, dl_ref, dq_ref, dk_ref, dv_ref, drow_ref, dnf_ref,
             dq_acc, dk_acc, dv_acc, dnf_acc):
        j = pl.program_id(1)

        @pl.when(j == 0)
        def _():
            dq_acc[...] = jnp.zeros_like(dq_acc)
            drow_ref[...] = jnp.zeros_like(drow_ref)

        dk_acc[...] = jnp.zeros_like(dk_acc)
        dv_acc[...] = jnp.zeros_like(dv_acc)
        dnf_acc[...] = jnp.zeros_like(dnf_acc)
        kb = k_ref[...]

        def block(i, masked):
            rows = pl.ds(pl.multiple_of(i * TA, TA), TA)
            qb = q_ref[rows, :]
            dob = do_ref[rows, :]
            pt, dst = _ds_tile(kb, qb, v_ref[...], dob, nf_ref[0], lse_ref[0, i], dl_ref[0, i], c, masked)
            dsb = dst.astype(BF16)
            dv_acc[...] += jnp.dot(pt.astype(BF16), dob, preferred_element_type=F32)
            dk_acc[...] += jnp.dot(dsb, qb, preferred_element_type=F32)
            dq_acc[rows, :] += lax.dot_general(dsb, kb, _TN, preferred_element_type=F32)
            drow_ref[0, i] += jnp.sum(dst, axis=0, keepdims=True)
            part = dst[:, 0:LANES]
            for t in range(1, TA // LANES):
                part = part + dst[:, t * LANES:(t + 1) * LANES]
            dnf_acc[...] += part

        def loop(ii, carry):
            block(j + 1 + 2 * ii, False)
            block(j + 2 + 2 * ii, False)
            return carry

        block(j, True)
        rest = nb - 1 - j
        lax.fori_loop(0, rest // 2, loop, 0)

        @pl.when(rest % 2 == 1)
        def _():
            block(nb - 1, False)

        dk_ref[...] = (dk_acc[...] * scale).astype(BF16)
        dv_ref[...] = dv_acc[...].astype(BF16)
        dnf_ref[0] = jnp.sum(dnf_acc[...], axis=-1, keepdims=True)

        @pl.when(j == nb - 1)
        def _():
            dq_ref[...] = (dq_acc[...] * scale).astype(BF16)

    full = pl.BlockSpec((S, HEAD_DIM), lambda h, j: (0, h))
    row_stat = pl.BlockSpec((1, nb, 1, TA), lambda h, j: (h, 0, 0, 0))
    kblk = lambda c0: pl.BlockSpec((TA, HEAD_DIM), lambda h, j: (j, c0 + h))
    shp = jax.ShapeDtypeStruct((S, H * HEAD_DIM), BF16)
    call = dict(
        body=body, n_in=7, n_out=5, grid=(H, nb),
        in_specs=[full, kblk(H), kblk(2 * H), pl.BlockSpec((1, TA, LANES), lambda h, j: (h, j, 0)), full,
                  row_stat, row_stat],
        out_specs=[full, kblk(0), kblk(0), row_stat, pl.BlockSpec((1, TA, 1), lambda h, j: (h, j, 0))],
        scratch_shapes=[pltpu.VMEM((S, HEAD_DIM), F32), pltpu.VMEM((TA, HEAD_DIM), F32),
                        pltpu.VMEM((TA, HEAD_DIM), F32), pltpu.VMEM((TA, LANES), F32)],
        out_shape=[shp, shp, shp, jax.ShapeDtypeStruct((H, nb, 1, TA), F32), jax.ShapeDtypeStruct((H, S, 1), F32)])
    return _call(call, name, ("parallel", "arbitrary"), rider, [qkv, qkv, qkv, nfb, do, lse, delta])


def _taps(ext_ref, w_ref, K, base, r0, rows, cols, reverse=False, init=None):
    acc = init
    for k in range(K):
        wk = w_ref[(K - 1 - k) if reverse else k:((K - 1 - k) if reverse else k) + 1, cols]
        term = wk * ext_ref[base + k + r0:base + k + r0 + rows, cols]
        acc = term if acc is None else acc + term
    return acc


def _prev_blk(T, H):
    return lambda i: jnp.maximum(i * (T // H) - 1, 0)


def _next_blk(T, H, S):
    return lambda i: jnp.minimum((i + 1) * (T // H), S // H - 1)


def _conf_fwd(rest, w, b, lng, lnb, mix, name):
    S = rest.shape[0]
    K, C = w.shape
    H, T = 32, min(256, S)
    RS = min(64, T)
    base = H - (K - 1)

    def body(cv_ref, cg_ref, cvp_ref, cgp_ref, w_ref, b_ref, g_ref, bb_ref, mix_ref, o_ref, cc_ref, ext_ref):
        i = pl.program_id(0)
        ext_ref[0:H, :] = jnp.where(i > 0, cvp_ref[...] * _sigmoid(cgp_ref[...]), 0.0)
        ext_ref[H:H + T, :] = cv_ref[...] * _sigmoid(cg_ref[...])
        for r0 in range(0, T, RS):
            cc = _taps(ext_ref, w_ref, K, base, r0, RS, slice(None), init=jnp.broadcast_to(b_ref[...], (RS, C)))
            cc_ref[r0:r0 + RS, :] = cc
            xc = cc - jnp.mean(cc, axis=-1, keepdims=True)
            y = xc * lax.rsqrt(jnp.mean(xc * xc, axis=-1, keepdims=True) + LN_EPS) * g_ref[...] + bb_ref[...]
            o_ref[r0:r0 + RS, :] = (y * _sigmoid(y)).astype(BF16)

    pb = _prev_blk(T, H)
    cur = lambda cb: pl.BlockSpec((T, C), lambda i: (i, cb))
    prev = lambda cb: pl.BlockSpec((H, C), lambda i: (pb(i), cb))
    full = lambda a: pl.BlockSpec(a.shape, lambda i: (0, 0))
    col_blk = (mix.shape[1] - 2 * C) // C
    return pl.pallas_call(
        body, name=name, grid=(S // T,),
        out_shape=(jax.ShapeDtypeStruct(mix.shape, BF16), jax.ShapeDtypeStruct((S, C), F32)),
        in_specs=[cur(0), cur(1), prev(0), prev(1), full(w), full(b), full(lng), full(lnb),
                  pl.BlockSpec(memory_space=pl.ANY)],
        out_specs=(pl.BlockSpec((T, C), lambda i: (i, col_blk)), pl.BlockSpec((T, C), lambda i: (i, 0))),
        input_output_aliases={8: 0},
        scratch_shapes=[pltpu.VMEM((H + T, C), F32)],
        compiler_params=_params("parallel"))(rest, rest, rest, rest, w, b, lng, lnb, mix)


def _conf_bwd(rest, cc_all, dcs, w, lng, lnb, name):
    S = rest.shape[0]
    K, C = w.shape
    H, T = 32, min(256, S)
    RS = 32
    nI = S // T
    base = H - (K - 1)

    def body(cv_ref, cg_ref, cvp_ref, cgp_ref, cc_ref, ccn_ref, do_ref, don_ref, w_ref, g_ref, bb_ref,
             dcvg_ref, dw_ref, dvec_ref, ext_ref, dcc_ref):
        i = pl.program_id(0)
        ext_ref[0:H, :] = jnp.where(i > 0, cvp_ref[...] * _sigmoid(cgp_ref[...]), 0.0)
        ext_ref[H:H + T, :] = cv_ref[...] * _sigmoid(cg_ref[...])

        @pl.when(i == 0)
        def _():
            dw_ref[...] = jnp.zeros_like(dw_ref)
            dvec_ref[...] = jnp.zeros_like(dvec_ref)

        db = jnp.zeros((1, C), F32)
        dg = jnp.zeros((1, C), F32)
        dbb = jnp.zeros((1, C), F32)
        for r0 in range(0, T + H, RS):
            cc = cc_ref[r0:r0 + RS, :] if r0 < T else ccn_ref[r0 - T:r0 - T + RS, :]
            xc = cc - jnp.mean(cc, axis=-1, keepdims=True)
            r = lax.rsqrt(jnp.mean(xc * xc, axis=-1, keepdims=True) + LN_EPS)
            xh = xc * r
            y = xh * g_ref[...] + bb_ref[...]
            sy = _sigmoid(y)
            if r0 < T:
                d_o = do_ref[r0:r0 + RS, :]
            else:
                d_o = jnp.where(i < nI - 1, don_ref[r0 - T:r0 - T + RS, :], 0.0)
            dy = d_o * (sy * (1.0 + y * (1.0 - sy)))
            dxh = dy * g_ref[...]
            dcc = r * (dxh - jnp.mean(dxh, axis=-1, keepdims=True)
                       - xh * jnp.mean(dxh * xh, axis=-1, keepdims=True))
            dcc_ref[r0:r0 + RS, :] = dcc
            if r0 < T:
                dbb = dbb + jnp.sum(dy, axis=0, keepdims=True)
                dg = dg + jnp.sum(dy * xh, axis=0, keepdims=True)
                db = db + jnp.sum(dcc, axis=0, keepdims=True)
        dvec_ref[0:1, :] += db
        dvec_ref[1:2, :] += dg
        dvec_ref[2:3, :] += dbb
        R2 = min(64, T)
        for k in range(K):
            s = jnp.zeros((1, C), F32)
            for r0 in range(0, T, R2):
                s = s + jnp.sum(dcc_ref[r0:r0 + R2, :] * ext_ref[base + k + r0:base + k + r0 + R2, :],
                                axis=0, keepdims=True)
            dw_ref[k:k + 1, :] += s
        for r0 in range(0, T, R2):
            dci = _taps(dcc_ref, w_ref, K, 0, r0, R2, slice(None), reverse=True)
            cvv = cv_ref[r0:r0 + R2, :]
            sg = _sigmoid(cg_ref[r0:r0 + R2, :])
            dcvg_ref[r0:r0 + R2, 0:C] = (dci * sg).astype(BF16)
            dcvg_ref[r0:r0 + R2, C:2 * C] = (dci * cvv * sg * (1.0 - sg)).astype(BF16)

    pb, nb_ = _prev_blk(T, H), _next_blk(T, H, S)
    cur = lambda cb: pl.BlockSpec((T, C), lambda i: (i, cb))
    prev = lambda cb: pl.BlockSpec((H, C), lambda i: (pb(i), cb))
    nxt = lambda cb: pl.BlockSpec((H, C), lambda i: (nb_(i), cb))
    full = lambda a: pl.BlockSpec(a.shape, lambda i: (0, 0))
    return pl.pallas_call(
        body, name=name,
        out_shape=(jax.ShapeDtypeStruct((S, 2 * C), BF16), jax.ShapeDtypeStruct((32, C), F32),
                   jax.ShapeDtypeStruct((8, C), F32)),
        grid=(nI,),
        in_specs=[cur(0), cur(1), prev(0), prev(1), cur(0), nxt(0), cur(0), nxt(0),
                  full(w), full(lng), full(lnb)],
        out_specs=(pl.BlockSpec((T, 2 * C), lambda i: (i, 0)), pl.BlockSpec((32, C), lambda i: (0, 0)),
                   pl.BlockSpec((8, C), lambda i: (0, 0))),
        scratch_shapes=[pltpu.VMEM((H + T, C), F32), pltpu.VMEM((T + H, C), F32)],
        compiler_params=_params("arbitrary"))(rest, rest, rest, rest, cc_all, cc_all, dcs, dcs, w, lng, lnb)


def _sconv_fwd(rest, w, mix, name):
    S = rest.shape[0]
    K, C = w.shape
    H, T = 8, min(256, S)
    RS = min(64, T)
    base = H - (K - 1)

    def body(sx_ref, sb_ref, sc_ref, sxp_ref, scp_ref, w_ref, mix_ref, o_ref, ext_ref):
        i = pl.program_id(0)
        ext_ref[0:H, :] = jnp.where(i > 0, sxp_ref[...] * scp_ref[...], 0.0)
        ext_ref[H:H + T, :] = sx_ref[...] * sc_ref[...]
        for r0 in range(0, T, RS):
            cz = _taps(ext_ref, w_ref, K, base, r0, RS, slice(None))
            o_ref[r0:r0 + RS, :] = (sb_ref[r0:r0 + RS, :] * cz).astype(BF16)

    pb = _prev_blk(T, H)
    cur = lambda cb: pl.BlockSpec((T, C), lambda i: (i, cb))
    prev = lambda cb: pl.BlockSpec((H, C), lambda i: (pb(i), cb))
    col_blk = (mix.shape[1] - C) // C
    return pl.pallas_call(
        body, name=name, out_shape=jax.ShapeDtypeStruct(mix.shape, BF16), grid=(S // T,),
        in_specs=[cur(2), cur(3), cur(4), prev(2), prev(4), pl.BlockSpec(w.shape, lambda i: (0, 0)),
                  pl.BlockSpec(memory_space=pl.ANY)],
        out_specs=pl.BlockSpec((T, C), lambda i: (i, col_blk)), input_output_aliases={6: 0},
        scratch_shapes=[pltpu.VMEM((H + T, C), F32)],
        compiler_params=_params("parallel"))(rest, rest, rest, rest, rest, w, mix)


def _sconv_bwd(rest, dcs, w, name):
    S = rest.shape[0]
    K, C = w.shape
    H, T = 8, min(256, S)
    RS = min(64, T)
    nI = S // T
    base = H - (K - 1)

    def body(sx_ref, sb_ref, sc_ref, sxp_ref, scp_ref, sbn_ref, do_ref, don_ref, w_ref,
             dout_ref, dw_ref, ext_ref, dcv_ref):
        i = pl.program_id(0)
        ext_ref[0:H, :] = jnp.where(i > 0, sxp_ref[...] * scp_ref[...], 0.0)
        ext_ref[H:H + T, :] = sx_ref[...] * sc_ref[...]
        dcv_ref[0:T, :] = do_ref[...] * sb_ref[...]
        dcv_ref[T:T + H, :] = jnp.where(i < nI - 1, don_ref[...] * sbn_ref[...], 0.0)

        @pl.when(i == 0)
        def _():
            dw_ref[...] = jnp.zeros_like(dw_ref)

        for k in range(K):
            s = jnp.zeros((1, C), F32)
            for r0 in range(0, T, RS):
                s = s + jnp.sum(dcv_ref[r0:r0 + RS, :] * ext_ref[base + k + r0:base + k + r0 + RS, :],
                                axis=0, keepdims=True)
            dw_ref[k:k + 1, :] += s
        for r0 in range(0, T, RS):
            cz = _taps(ext_ref, w_ref, K, base, r0, RS, slice(None))
            dz = _taps(dcv_ref, w_ref, K, 0, r0, RS, slice(None), reverse=True)
            dout_ref[r0:r0 + RS, 0:C] = (dz * sc_ref[r0:r0 + RS, :]).astype(BF16)
            dout_ref[r0:r0 + RS, C:2 * C] = (do_ref[r0:r0 + RS, :] * cz).astype(BF16)
            dout_ref[r0:r0 + RS, 2 * C:3 * C] = (dz * sx_ref[r0:r0 + RS, :]).astype(BF16)

    pb, nb_ = _prev_blk(T, H), _next_blk(T, H, S)
    cur = lambda cb: pl.BlockSpec((T, C), lambda i: (i, cb))
    prev = lambda cb: pl.BlockSpec((H, C), lambda i: (pb(i), cb))
    nxt = lambda cb: pl.BlockSpec((H, C), lambda i: (nb_(i), cb))
    return pl.pallas_call(
        body, name=name,
        out_shape=(jax.ShapeDtypeStruct((S, 3 * C), BF16), jax.ShapeDtypeStruct((8, C), F32)),
        grid=(nI,),
        in_specs=[cur(2), cur(3), cur(4), prev(2), prev(4), nxt(3), cur(1), nxt(1),
                  pl.BlockSpec(w.shape, lambda i: (0, 0))],
        out_specs=(pl.BlockSpec((T, 3 * C), lambda i: (i, 0)), pl.BlockSpec((8, C), lambda i: (0, 0))),
        scratch_shapes=[pltpu.VMEM((H + T, C), F32), pltpu.VMEM((T + H, C), F32)],
        compiler_params=_params("arbitrary"))(rest, rest, rest, rest, rest, rest, dcs, dcs, w)


FFN_ROWS = 256
FFN_HALO = 16


def _shift_mats(T):
    r = lax.broadcasted_iota(jnp.int32, (T, T), 0)
    c = lax.broadcasted_iota(jnp.int32, (T, T), 1)
    return jnp.stack([r == c + 1, r == c + 2, c == r + 1, c == r + 2]).astype(BF16)


def _edge_rows(strip, shift, first):
    sub = lax.broadcasted_iota(jnp.int32, strip.shape, 0)
    if first:
        return jnp.where(sub < shift, pltpu.roll(strip, shift, 0), 0.0)
    return jnp.where(sub >= 8 - shift, pltpu.roll(strip, 8 - shift, 0), 0.0)


def _conv3_tile(x_ref, prev_ref, w_ref, b_ref, down, has_prev, u_ref, xm_refs=None):
    T = x_ref.shape[0]
    x = x_ref[...]
    xm1 = jnp.dot(down[0], x, preferred_element_type=F32)
    xm2 = jnp.dot(down[1], x, preferred_element_type=F32)
    u_ref[...] = b_ref[...] + w_ref[0:1, :] * xm2 + w_ref[1:2, :] * xm1 + w_ref[2:3, :] * x.astype(F32)
    tail = jnp.where(has_prev, prev_ref[...].astype(F32)[FFN_HALO - 8:, :], 0.0)
    p1, p2 = _edge_rows(tail, 1, True), _edge_rows(tail, 2, True)
    u_ref[0:8, :] += w_ref[0:1, :] * p2 + w_ref[1:2, :] * p1
    if xm_refs is not None:
        xm_refs[0][...] = xm1
        xm_refs[1][...] = xm2
        xm_refs[0][0:8, :] += p1
        xm_refs[1][0:8, :] += p2


def _ffn_fwd(hu, w, b, name):
    S, F2 = hu.shape
    Fd = F2 // 2
    K = w.shape[0]
    assert K == 3
    T = min(FFN_ROWS, S)
    tc = _tile(Fd, 512)
    nJ = Fd // tc

    def body(sh_ref, g_ref, v_ref, gp_ref, vp_ref, wg_ref, wv_ref, bg_ref, bv_ref, o_ref, ug_ref, uv_ref):
        i = pl.program_id(1)
        down = (sh_ref[0], sh_ref[1])
        _conv3_tile(g_ref, gp_ref, wg_ref, bg_ref, down, i > 0, ug_ref)
        _conv3_tile(v_ref, vp_ref, wv_ref, bv_ref, down, i > 0, uv_ref)
        ug = ug_ref[...]
        o_ref[...] = (ug * _sigmoid(ug) * uv_ref[...]).astype(BF16)

    pb = _prev_blk(T, FFN_HALO)
    cur = lambda off: pl.BlockSpec((T, tc), lambda j, i: (i, j + off))
    prev = lambda off: pl.BlockSpec((FFN_HALO, tc), lambda j, i: (pb(i), j + off))
    wsp = lambda off: pl.BlockSpec((K, tc), lambda j, i: (0, j + off))
    bsp = lambda off: pl.BlockSpec((1, tc), lambda j, i: (0, j + off))
    return pl.pallas_call(
        body, name=name, out_shape=jax.ShapeDtypeStruct((S, Fd), BF16), grid=(nJ, S // T),
        in_specs=[pl.BlockSpec((4, T, T), lambda j, i: (0, 0, 0)),
                  cur(0), cur(nJ), prev(0), prev(nJ), wsp(0), wsp(nJ), bsp(0), bsp(nJ)],
        out_specs=pl.BlockSpec((T, tc), lambda j, i: (i, j)),
        scratch_shapes=[pltpu.VMEM((T, tc), F32), pltpu.VMEM((T, tc), F32)],
        compiler_params=_params("parallel", "parallel"))(_shift_mats(T), hu, hu, hu, hu, w, w, b, b)


def _ffn_bwd(hu, dact, w, b, name):
    S, F2 = hu.shape
    Fd = F2 // 2
    K = w.shape[0]
    assert K == 3
    T = min(FFN_ROWS, S)
    tc = _tile(Fd, 512)
    nJ = Fd // tc
    nI = S // T

    def body(sh_ref, g_ref, v_ref, gp_ref, vp_ref, gn_ref, vn_ref, da_ref, dan_ref, wg_ref, wv_ref, bg_ref, bv_ref,
             dg_ref, dv_ref, dwg_ref, dwv_ref, ug_ref, uv_ref, g1_ref, g2_ref, v1_ref, v2_ref, dh_ref):
        i = pl.program_id(1)
        down, up = (sh_ref[0], sh_ref[1]), (sh_ref[2], sh_ref[3])
        _conv3_tile(g_ref, gp_ref, wg_ref, bg_ref, down, i > 0, ug_ref, (g1_ref, g2_ref))
        _conv3_tile(v_ref, vp_ref, wv_ref, bv_ref, down, i > 0, uv_ref, (v1_ref, v2_ref))

        @pl.when(i == 0)
        def _():
            dwg_ref[...] = jnp.zeros_like(dwg_ref)
            dwv_ref[...] = jnp.zeros_like(dwv_ref)

        def d_u(ug, uv, d_a):
            sg = _sigmoid(ug)
            return d_a * uv * (sg * (1.0 + ug * (1.0 - sg))), d_a * (ug * sg)

        dug, duv = d_u(ug_ref[...], uv_ref[...], da_ref[...].astype(F32))

        def next_rows(x_ref, xn_ref, w_ref, b_ref):
            strip = jnp.concatenate([x_ref[T - FFN_HALO:, :].astype(F32)[FFN_HALO - 8:, :],
                                     xn_ref[...].astype(F32)[0:8, :]], axis=0)
            return (b_ref[...] + w_ref[0:1, :] * strip[6:14, :] + w_ref[1:2, :] * strip[7:15, :]
                    + w_ref[2:3, :] * strip[8:16, :])

        d_an = jnp.where(i < nI - 1, dan_ref[...].astype(F32)[0:8, :], 0.0)
        dug_n, duv_n = d_u(next_rows(g_ref, gn_ref, wg_ref, bg_ref), next_rows(v_ref, vn_ref, wv_ref, bv_ref), d_an)

        for du, du_n, x_ref, x1_ref, x2_ref, w_ref, dw_ref, out_ref in (
                (dug, dug_n, g_ref, g1_ref, g2_ref, wg_ref, dwg_ref, dg_ref),
                (duv, duv_n, v_ref, v1_ref, v2_ref, wv_ref, dwv_ref, dv_ref)):
            dw_ref[0:1, :] += jnp.sum(du * x2_ref[...], axis=0, keepdims=True)
            dw_ref[1:2, :] += jnp.sum(du * x1_ref[...], axis=0, keepdims=True)
            dw_ref[2:3, :] += jnp.sum(du * x_ref[...].astype(F32), axis=0, keepdims=True)
            dw_ref[3:4, :] += jnp.sum(du, axis=0, keepdims=True)
            dub = du.astype(BF16)
            dh_ref[...] = (w_ref[2:3, :] * du + w_ref[1:2, :] * jnp.dot(up[0], dub, preferred_element_type=F32)
                           + w_ref[0:1, :] * jnp.dot(up[1], dub, preferred_element_type=F32))
            nxt = du_n.astype(BF16).astype(F32)
            dh_ref[T - 8:, :] += w_ref[1:2, :] * _edge_rows(nxt, 1, False) + w_ref[0:1, :] * _edge_rows(nxt, 2, False)
            out_ref[...] = dh_ref[...].astype(BF16)

    pb, nbb = _prev_blk(T, FFN_HALO), _next_blk(T, FFN_HALO, S)
    cur = lambda off: pl.BlockSpec((T, tc), lambda j, i: (i, j + off))
    prev = lambda off: pl.BlockSpec((FFN_HALO, tc), lambda j, i: (pb(i), j + off))
    nxt = lambda off: pl.BlockSpec((FFN_HALO, tc), lambda j, i: (nbb(i), j + off))
    wsp = lambda off: pl.BlockSpec((K, tc), lambda j, i: (0, j + off))
    bsp = lambda off: pl.BlockSpec((1, tc), lambda j, i: (0, j + off))
    half = jax.ShapeDtypeStruct((S, Fd), BF16)
    dws = jax.ShapeDtypeStruct((8, Fd), F32)
    tile = pltpu.VMEM((T, tc), F32)
    return pl.pallas_call(
        body, name=name, out_shape=(half, half, dws, dws), grid=(nJ, nI),
        in_specs=[pl.BlockSpec((4, T, T), lambda j, i: (0, 0, 0)),
                  cur(0), cur(nJ), prev(0), prev(nJ), nxt(0), nxt(nJ), cur(0), nxt(0),
                  wsp(0), wsp(nJ), bsp(0), bsp(nJ)],
        out_specs=(pl.BlockSpec((T, tc), lambda j, i: (i, j)), pl.BlockSpec((T, tc), lambda j, i: (i, j)),
                   pl.BlockSpec((8, tc), lambda j, i: (0, j)), pl.BlockSpec((8, tc), lambda j, i: (0, j))),
        scratch_shapes=[tile] * 7,
        compiler_params=_params("parallel", "arbitrary"))(
            _shift_mats(T), hu, hu, hu, hu, hu, hu, dact, dact, w, w, b, b)


def _position():
    return lax.axis_index("x"), lax.axis_index("y"), lax.axis_index("c")


def _slot(px, py, pc):
    return 4 * px + 2 * py + pc


def _gather_copies(x_ref, out_ref, sems, r, starting=False):
    send_sems, recv_sems, local_sems = sems
    px, py, pc = _position()
    me, sibling = (px, py, pc), (px, py, 1 - pc)
    chips = [(1 - px, py), (px, 1 - py), (1 - px, 1 - py)]

    def copy(k, block, to, src=None):
        dst = out_ref.at[_slot(*block)]
        return pltpu.make_async_remote_copy(
            src_ref=dst if src is None else src, dst_ref=dst,
            send_sem=send_sems.at[7 * r + k], recv_sem=recv_sems.at[7 * r + k], device_id=to, device_id_type=MESH)

    mine = pltpu.make_async_copy(x_ref, out_ref.at[_slot(*me)], local_sems.at[r])
    first = [copy(0, me, sibling, src=x_ref)] + [copy(1 + n, me, (*chip, pc), src=x_ref)
                                                  for n, chip in enumerate(chips)]
    if starting:
        return mine, first
    passed = [copy(4 + n, (*chip, pc), sibling) for n, chip in enumerate(chips)]
    landed = [copy(1 + n, (*chip, pc), me) for n, chip in enumerate(chips)]
    from_sibling = [copy(0, sibling, me)] + [copy(4 + n, (*chip, 1 - pc), me) for n, chip in enumerate(chips)]
    return mine, first, passed, landed, from_sibling


def _scatter_copies(g_ref, out_ref, sems, r, starting=False):
    send_sems, recv_sems, local_sems = sems
    px, py, pc = _position()
    me = _slot(px, py, pc)
    mine = pltpu.make_async_copy(g_ref.at[me], out_ref.at[me], local_sems.at[r])
    peers = [(px ^ fx, py ^ fy, pc ^ fc) for fx, fy, fc in PEER_FLIPS]

    def copy(k, peer, src_slot, dst_slot):
        return pltpu.make_async_remote_copy(
            src_ref=g_ref.at[src_slot], dst_ref=out_ref.at[dst_slot],
            send_sem=send_sems.at[7 * r + k], recv_sem=recv_sems.at[7 * r + k], device_id=peer, device_id_type=MESH)

    sends = [copy(k, peer, _slot(*peer), me) for k, peer in enumerate(peers)]
    if starting:
        return mine, sends
    arrivals = [copy(k, peer, me, _slot(*peer)) for k, peer in enumerate(peers)]
    return mine, sends, arrivals


def _rider_start(kind, in_ref, out_ref, sems, r):
    if kind == "gather":
        mine, first = _gather_copies(in_ref, out_ref, sems, r, starting=True)
    else:
        mine, first = _scatter_copies(in_ref, out_ref, sems, r, starting=True)
    mine.start()
    for cp in first:
        cp.start()


def _rider_forward(kind, in_ref, out_ref, sems, r):
    if kind == "gather":
        px, py, pc = _position()
        send_sems, recv_sems, _ = sems
        for n, chip in enumerate([(1 - px, py), (px, 1 - py), (1 - px, 1 - py)]):
            rows = out_ref.at[_slot(*chip, pc)]
            copy = lambda k, to: pltpu.make_async_remote_copy(
                src_ref=rows, dst_ref=rows, send_sem=send_sems.at[7 * r + k], recv_sem=recv_sems.at[7 * r + k],
                device_id=to, device_id_type=MESH)
            copy(1 + n, (px, py, pc)).wait_recv()
            copy(4 + n, (px, py, 1 - pc)).start()


def _rider_finish(kind, in_ref, out_ref, sems, r, forwarded=False):
    if kind == "gather":
        mine, first, passed, landed, from_sibling = _gather_copies(in_ref, out_ref, sems, r)
        if not forwarded:
            for cp, fwd in zip(landed, passed):
                cp.wait_recv()
                fwd.start()
        for cp in from_sibling:
            cp.wait_recv()
        for cp in first + passed:
            cp.wait_send()
    else:
        mine, sends, arrivals = _scatter_copies(in_ref, out_ref, sems, r)
        for cp in arrivals:
            cp.wait_recv()
        for cp in sends:
            cp.wait_send()
    mine.wait()


def _late_step(n):
    return (3 * n) // 4 if n >= 4 else 0


def _ride(call, rider):
    kind, arrs = rider
    nr, n_in, n_out, grid, body = len(arrs), call["n_in"], call["n_out"], call["grid"], call["body"]

    def wrapped(*refs):
        ins, r_in = refs[:n_in], refs[n_in:n_in + nr]
        outs, r_out = refs[n_in + nr:n_in + nr + n_out], refs[n_in + nr + n_out:n_in + 2 * nr + n_out]
        scratch, sems = refs[n_in + 2 * nr + n_out:-3], refs[-3:]
        ids = [pl.program_id(a) for a in range(len(grid))]
        first = functools.reduce(lambda p, q: p & q, [i == 0 for i in ids])
        last = functools.reduce(lambda p, q: p & q, [i == n - 1 for i, n in zip(ids, grid)])
        late = functools.reduce(lambda p, q: p & q, [ids[0] == _late_step(grid[0])] + [i == 0 for i in ids[1:]])

        @pl.when(first)
        def _():
            for r in range(nr):
                _rider_start(kind, r_in[r], r_out[r], sems, r)

        if _late_step(grid[0]):
            @pl.when(late)
            def _():
                for r in range(nr):
                    _rider_forward(kind, r_in[r], r_out[r], sems, r)

        body(*ins, *outs, *scratch)

        @pl.when(last)
        def _():
            for r in range(nr):
                _rider_finish(kind, r_in[r], r_out[r], sems, r, forwarded=bool(_late_step(grid[0])))

    hbm = pl.BlockSpec(memory_space=pl.ANY)
    return dict(
        body=wrapped, grid=grid,
        in_specs=call["in_specs"] + [hbm] * nr,
        out_specs=tuple(call["out_specs"]) + (hbm,) * nr,
        out_shape=tuple(call["out_shape"]) + tuple(
            jax.ShapeDtypeStruct(x.shape if kind == "scatter" else (N_DEV,) + x.shape, x.dtype) for x in arrs),
        scratch_shapes=call["scratch_shapes"] + [pltpu.SemaphoreType.DMA((7 * nr,)), pltpu.SemaphoreType.DMA((7 * nr,)),
                                                 pltpu.SemaphoreType.DMA((nr,))])


def _call(call, name, semantics, rider, operands):
    if rider is not None:
        call, semantics, operands = _ride(call, rider), ("arbitrary",) * len(call["grid"]), operands + list(rider[1])
    out = pl.pallas_call(call["body"], name=name, grid=call["grid"], in_specs=call["in_specs"],
                         out_specs=tuple(call["out_specs"]), out_shape=tuple(call["out_shape"]),
                         scratch_shapes=call["scratch_shapes"], compiler_params=_params(*semantics))(*operands)
    if rider is None:
        return out, []
    n = len(out) - len(rider[1])
    return out[:n], list(out[n:])


def _all_gather(x, in_vmem, name):
    space = pltpu.VMEM if in_vmem else pl.ANY

    def body(x_ref, out_ref, send_sems, recv_sems, local_sems):
        sems = (send_sems, recv_sems, local_sems)
        _rider_start("gather", x_ref, out_ref, sems, 0)
        _rider_finish("gather", x_ref, out_ref, sems, 0)

    return pl.pallas_call(
        body, name=name, out_shape=jax.ShapeDtypeStruct((N_DEV,) + x.shape, x.dtype),
        in_specs=[pl.BlockSpec(memory_space=space)], out_specs=pl.BlockSpec(memory_space=space),
        scratch_shapes=[pltpu.SemaphoreType.DMA((7,)), pltpu.SemaphoreType.DMA((7,)), pltpu.SemaphoreType.DMA((1,))],
        compiler_params=pltpu.CompilerParams(vmem_limit_bytes=VMEM_LIMIT_BYTES),
    )(x)


def _adam_sum(stage, w, m, v, layer, prev, name):
    n = stage.shape[0]
    L, R, C = w.shape
    tr = R if R * C <= 256 * 1024 else _row_tile(R, C)
    c1 = 1.0 / (1.0 - ADAM_B1 ** ADAM_STEP)
    c2 = 1.0 / (1.0 - ADAM_B2 ** ADAM_STEP)

    def body(*refs):
        st_ref, w_ref, m_ref, v_ref = refs[:4]
        g_ref, d_ref, nm_ref, nv_ref = refs[-4:]
        g = st_ref[0].astype(F32)
        for s in range(1, n):
            g = g + st_ref[s].astype(F32)
        wv = w_ref[0]
        mn = ADAM_B1 * m_ref[0] + (1.0 - ADAM_B1) * g
        vn = ADAM_B2 * v_ref[0] + (1.0 - ADAM_B2) * (g * g)
        g_ref[0] = g
        nm_ref[0] = mn
        nv_ref[0] = vn
        d_ref[0] = -ADAM_LR * ((mn * c1) / (jnp.sqrt(vn * c2) + ADAM_EPS) + ADAM_WD * wv)

    lay = pl.BlockSpec((1, tr, C), lambda i: (layer, i, 0))
    in_specs = [pl.BlockSpec((n, tr, C), lambda i: (0, i, 0)), lay, lay, lay]
    ins = [stage, w, m, v]
    aliases = {}
    if prev is not None:
        in_specs += [pl.BlockSpec(memory_space=pl.ANY)] * 4
        ins += list(prev)
        aliases = {4: 0, 5: 1, 6: 2, 7: 3}
    shp = jax.ShapeDtypeStruct((L, R, C), F32)
    return pl.pallas_call(
        body, name=name, out_shape=(shp, shp, shp, shp), grid=(R // tr,),
        in_specs=in_specs, out_specs=(lay, lay, lay, lay), input_output_aliases=aliases,
        compiler_params=_params("parallel"))(*ins)


def _row_tile(R, C):
    cpad = -(-C // LANES) * LANES
    want = max(16, (256 * 1024) // cpad)
    best = 16
    for t in range(16, R + 1, 16):
        if R % t == 0 and t <= want:
            best = t
    return best


def _sum_slabs(st, name):
    n, R, C = st.shape
    tr = R if n * R * C * 4 <= (12 << 20) else _row_tile(R, C)

    def body(st_ref, o_ref):
        g = st_ref[0]
        for s in range(1, n):
            g = g + st_ref[s]
        o_ref[...] = g

    return pl.pallas_call(
        body, name=name, out_shape=jax.ShapeDtypeStruct((R, C), F32), grid=(R // tr,),
        in_specs=[pl.BlockSpec((n, tr, C), lambda i: (0, i, 0))], out_specs=pl.BlockSpec((tr, C), lambda i: (i, 0)),
        compiler_params=_params("parallel"))(st)


def _pack(arrs):
    flat = [a.reshape(-1).astype(F32) for a in arrs]
    sizes = [f.shape[0] for f in flat]
    total = sum(sizes)
    padded = -(-total // (16 * LANES)) * (16 * LANES)
    if padded > total:
        flat.append(jnp.zeros((padded - total,), F32))
    return jnp.concatenate(flat).reshape(padded // LANES, LANES), (sizes, [a.shape for a in arrs])


def _unpack(packed, layout, lead=()):
    sizes, shapes = layout
    flat = packed.reshape(lead + (-1,))
    out, off = [], 0
    for sz, shp in zip(sizes, shapes):
        out.append(flat[..., off:off + sz].reshape(lead + tuple(shp)))
        off += sz
    return out


class _NoComm:
    col_slabs = None

    def __init__(self, wts):
        self.wts, self.grads = wts, {}

    def weight(self, l, name):
        return self.wts[l][name]

    def gather_rider(self, l, names):
        return None

    def scatter_rider(self, name, l, g):
        self.grads[(name, l)] = g
        return None


def _local_step(x, tgt, ada, mix_norm_g, comm, b_forget, conf_dw_w, conf_dw_b, conf_ln_g, conf_ln_b, sc_dw_w,
                ffn_norm_g, ffn_dw_w, ffn_dw_b, final_norm_g):
    S, D = x.shape
    L = ada.shape[0]

    def mm_gather(a, b, dtype, name, l_next, names):
        rider = comm.gather_rider(l_next, names) if l_next < L else None
        if rider is None:
            return _matmul(a, b, dtype, name=name)
        out, got = _matmul(a, b, dtype, name="cm_" + name, rider=rider)
        comm.gathered(l_next, names, got)
        return out

    def mm_scatter(a, b, dtype, name, wname, l, g):
        rider = comm.scatter_rider(wname, l, g)
        if rider is None:
            return _matmul(a, b, dtype, name=name, b_transposed=True)
        out, got = _matmul(a, b, dtype, name="cm_" + name, rider=rider, b_transposed=True)
        comm.scattered(wname, l, got[0])
        return out

    H = b_forget.shape[1]
    DA = H * HEAD_DIM
    C = conf_dw_b.shape[1]
    NQ = 3 * DA
    NR = 5 * C + LANES
    fblk = (5 * C) // LANES
    row = lambda a: a.reshape(1, -1)
    adav = ada.reshape(L, N_ADA, 1, D)

    saved = []
    xcur, delta, gate = x, None, None
    for l in range(L):
        sh_m, sc_m, g_m, sh_f, sc_f, g_f = [adav[l, n] for n in range(N_ADA)]
        w = functools.partial(comm.weight, l)
        x1, h1 = _site_fwd(xcur, delta, gate, row(mix_norm_g[l]), sc_m, sh_m, name="site_fwd_mix")
        qkv = _matmul(h1, w("w_in_perm")[:, :NQ], BF16, name="mm_qkv")
        rest = _matmul(h1, w("w_in_perm")[:, NQ:], F32, name="mm_rest")
        bpad = jnp.zeros((1, LANES), F32).at[0, :H].set(b_forget[l])
        Fc = _fgate_fwd(rest, bpad, fblk, name="fgate_fwd")
        nf = -LOG2E * jnp.transpose(Fc[:, :H])
        g_l, g_names = (0, ("w_up", "w_down", "w_out")) if l == 0 else (l + 1, ("w_up", "w_down"))
        rider = comm.gather_rider(g_l, g_names) if g_l < L else None
        (mixcat, attn32, lse), got = _attn_fwd(qkv, jnp.broadcast_to(nf[:, :, None], (H, S, LANES)), H, DA + 2 * C,
                                               name="attn_fwd" if rider is None else "cm_attn_fwd", rider=rider)
        if rider is not None:
            comm.gathered(g_l, g_names, got)
        mixcat, conf_cc = _conf_fwd(rest, conf_dw_w[l], row(conf_dw_b[l]), row(conf_ln_g[l]), row(conf_ln_b[l]),
                                    mixcat, name="conf_fwd")
        mixcat = _sconv_fwd(rest, sc_dw_w[l], mixcat, name="sconv_fwd")
        mixed = _matmul(mixcat, w("w_out"), F32, name="mm_out")
        x2, h2 = _site_fwd(x1, mixed, g_m, row(ffn_norm_g[l]), sc_f, sh_f, name="site_fwd_ffn")
        hu = mm_gather(h2, w("w_up"), BF16, "mm_up", l + 1, ("w_up", "w_down") if l == 0 else ("w_in", "w_out"))
        act = _ffn_fwd(hu, ffn_dw_w[l], row(ffn_dw_b[l]), name="ffn_fwd")
        ffn_out = mm_gather(act, w("w_down"), F32, "mm_down", l + 1 if l == 0 else L, ("w_in", "w_out"))
        saved.append(dict(x1=x1, h1=h1, qkv=qkv, rest=rest, bpad=bpad, nf=nf, attn32=attn32, lse=lse, mixcat=mixcat,
                          mixed=mixed, x2=x2, h2=h2, hu=hu, act=act, ffn_out=ffn_out, conf_cc=conf_cc))
        xcur, delta, gate = x2, ffn_out, g_f

    loss_lanes, dx, d_delta, d_gate, d_gfin = _final_fwd_bwd(xcur, delta, gate, row(final_norm_g), tgt, name="final")
    loss = (0.5 / D) * jnp.sum(loss_lanes)

    grads = dict(final_norm_g=d_gfin[0], ada=[None] * L, mix_norm_g=[None] * L, ffn_norm_g=[None] * L,
                 b_forget=[None] * L, conf_dw_w=[None] * L, conf_dw_b=[None] * L, conf_ln_g=[None] * L,
                 conf_ln_b=[None] * L, sc_dw_w=[None] * L, ffn_dw_w=[None] * L, ffn_dw_b=[None] * L)
    K3 = ffn_dw_w.shape[1]
    for l in reversed(range(L)):
        sv, w = saved[l], functools.partial(comm.weight, l)
        sh_m, sc_m, g_m, sh_f, sc_f, g_f = [adav[l, n] for n in range(N_ADA)]
        d_gf = d_gate
        g_down = _matmul_tn(sv["act"], d_delta, BF16, name="mm_dw_down")
        dact = _matmul(d_delta, w("w_down"), BF16, name="mm_dact", b_transposed=True)
        dhu_g, dhu_v, dwg, dwv = _ffn_bwd(sv["hu"], dact, ffn_dw_w[l], row(ffn_dw_b[l]), name="ffn_bwd")
        grads["ffn_dw_w"][l] = jnp.concatenate([dwg[:K3], dwv[:K3]], axis=1)
        grads["ffn_dw_b"][l] = jnp.concatenate([dwg[K3], dwv[K3]])
        dhu = (dhu_g, dhu_v)
        g_up = _matmul_tn(sv["h2"], dhu, BF16, name="mm_dw_up", col_slabs=comm.col_slabs)
        dh2 = mm_scatter(dhu, w("w_up"), F32, "mm_dh2", "w_down", l, g_down)
        dx, d_sh_f, d_a_f, d_mixed, d_gm = _site_bwd(sv["x2"], dh2, dx, row(ffn_norm_g[l]), sc_f,
                                                      sh_f, sv["mixed"], g_m, name="site_bwd_ffn")
        grads["ffn_norm_g"][l] = (d_a_f * (1.0 + sc_f))[0]
        d_sc_f = d_a_f * row(ffn_norm_g[l])
        g_out = _matmul_tn(sv["mixcat"], d_mixed, BF16, name="mm_dw_out")
        dattn = _matmul(d_mixed, w("w_out")[:DA], BF16, name="mm_dattn", b_transposed=True)
        dcs = _matmul(d_mixed, w("w_out")[DA:], F32, name="mm_dcs", b_transposed=True)
        delta_a = _blocked_rows(_attn_delta(sv["attn32"], dattn, H, name="attn_delta")[:, :, 0], min(ATTN_TILE, S))
        nfb = jnp.broadcast_to(sv["nf"][:, :, None], (H, S, LANES))
        r_up, r_out = comm.scatter_rider("w_up", l, g_up), comm.scatter_rider("w_out", l, g_out)
        rider = None if r_up is None else ("scatter", r_up[1] + r_out[1])
        (dq, dk, dv, drow, dnf), got = _attn_bwd(sv["qkv"], nfb, dattn, sv["lse"], delta_a, H,
                                                 name="attn_bwd" if rider is None else "cm_attn_bwd", rider=rider)
        if rider is not None:
            comm.scattered("w_up", l, got[0])
            comm.scattered("w_out", l, got[1])
        dF = jnp.zeros((S, LANES), F32).at[:, :H].set(jnp.transpose(drow.reshape(H, S) - dnf[:, :, 0]))
        dfl, dbf = _fgate_bwd(sv["rest"], sv["bpad"], dF, fblk, name="fgate_bwd")
        grads["b_forget"][l] = dbf[0, :H]
        dcvg, dcw, dcvec = _conf_bwd(sv["rest"], sv["conf_cc"], dcs, conf_dw_w[l], row(conf_ln_g[l]),
                                     row(conf_ln_b[l]), name="conf_bwd")
        grads["conf_dw_w"][l] = dcw[:conf_dw_w.shape[1]]
        grads["conf_dw_b"][l], grads["conf_ln_g"][l], grads["conf_ln_b"][l] = dcvec[0], dcvec[1], dcvec[2]
        dsc3, dsw = _sconv_bwd(sv["rest"], dcs, sc_dw_w[l], name="sconv_bwd")
        grads["sc_dw_w"][l] = dsw[:sc_dw_w.shape[1]]
        dproj = jnp.concatenate([dq, dk, dv, dcvg, dsc3, dfl], axis=1)
        g_in = _matmul_tn(sv["h1"], dproj, BF16, name="mm_dw_in")
        dh1 = mm_scatter(dproj, w("w_in_perm"), F32, "mm_dh1", "w_in_perm", l, g_in)
        if l > 0:
            pv = saved[l - 1]
            g_f_prev = adav[l - 1, 5]
            dx, d_sh_m, d_a_m, d_delta, d_gate = _site_bwd(sv["x1"], dh1, dx, row(mix_norm_g[l]), sc_m, sh_m,
                                                           pv["ffn_out"], g_f_prev, name="site_bwd_mix")
        else:
            dx, d_sh_m, d_a_m = _site_bwd(sv["x1"], dh1, dx, row(mix_norm_g[l]), sc_m, sh_m, None, None,
                                          name="site_bwd_first")
        grads["mix_norm_g"][l] = (d_a_m * (1.0 + sc_m))[0]
        d_sc_m = d_a_m * row(mix_norm_g[l])
        grads["ada"][l] = jnp.concatenate([d_sh_m, d_sc_m, d_gm, d_sh_f, d_sc_f, d_gf], axis=1)[0]
    return loss, dx, grads


def kernel(x, c, ada_w, ada_b, mix_norm_g, w_in, b_forget, conf_dw_w, conf_dw_b, conf_ln_g, conf_ln_b, sc_dw_w, w_out, ffn_norm_g, w_up, ffn_dw_w, ffn_dw_b, w_down, final_norm_g, loss_target, m_ada_w, m_ada_b, m_mix_norm_g, m_w_in, m_b_forget, m_conf_dw_w, m_conf_dw_b, m_conf_ln_g, m_conf_ln_b, m_sc_dw_w, m_w_out, m_ffn_norm_g, m_w_up, m_ffn_dw_w, m_ffn_dw_b, m_w_down, m_final_norm_g, v_ada_w, v_ada_b, v_mix_norm_g, v_w_in, v_b_forget, v_conf_dw_w, v_conf_dw_b, v_conf_ln_g, v_conf_ln_b, v_sc_dw_w, v_w_out, v_ffn_norm_g, v_w_up, v_ffn_dw_w, v_ffn_dw_b, v_w_down, v_final_norm_g):
    L, D, ada_loc = ada_w.shape
    S = x.shape[1]
    H = b_forget.shape[1]
    DA = H * HEAD_DIM
    C = conf_dw_b.shape[1]
    in_loc = w_in.shape[2]
    IN = in_loc * N_DEV
    px, py, pc = _position()
    me = _slot(px, py, pc)

    pk, lay = _pack([c, conf_dw_w, sc_dw_w, ffn_dw_w])
    gathered = _all_gather(pk, True, name="ag_small_fwd")
    c_all, cw_all, sw_all, fw_all = _unpack(gathered, lay, lead=(N_DEV,))
    c_all = c_all[:, 0]
    unshard = lambda a: jnp.moveaxis(a, 0, 2).reshape(a.shape[1], a.shape[2], -1)
    conf_w_full, sc_w_full, ffn_w_full = unshard(cw_all), unshard(sw_all), unshard(fw_all)
    c_act = c_all * jax.nn.sigmoid(c_all)
    c_act16 = jnp.zeros((16, D), F32).at[:N_DEV].set(c_act).astype(BF16)
    ada_cols = jnp.stack([_matmul(c_act16, ada_w[l].astype(BF16), F32, name="mm_ada")[:N_DEV] for l in range(L)])
    ada_g = _all_gather(ada_cols.reshape(L * N_DEV, ada_loc), True, name="ag_ada")
    ada_mine = lax.dynamic_index_in_dim(ada_g.reshape(N_DEV, L, N_DEV, ada_loc), me, axis=2, keepdims=False)
    ada = jnp.moveaxis(ada_mine, 0, 1).reshape(L, N_DEV * ada_loc) + ada_b

    NQ = 3 * DA
    PR = NQ + 5 * C
    shards = dict(w_in=w_in.astype(BF16), w_out=w_out.astype(BF16), w_up=w_up.astype(BF16),
                  w_down=w_down.astype(BF16))

    def shard_cols(g):
        return jnp.moveaxis(g.reshape(g.shape[0], N_DEV, -1), 1, 0)

    def shard_rows(g):
        return g.reshape(N_DEV, -1, g.shape[1])

    class MeshComm:
        col_slabs = N_DEV

        def __init__(self):
            self.got = {0: {"w_in": _all_gather(shards["w_in"][0], False, name="ag_w_in")}}
            self.full, self.stage = {}, {}

        def weight(self, l, name):
            if (l, name) not in self.full:
                g = self.got[l]
                if name == "w_in_perm":
                    wi = jnp.moveaxis(g["w_in"], 0, 1).reshape(D, IN)
                    full = jnp.concatenate([wi[:, :NQ], wi[:, NQ + H:], wi[:, NQ:NQ + H],
                                            jnp.zeros((D, LANES - H), BF16)], axis=1)
                elif name == "w_up":
                    full = jnp.moveaxis(g["w_up"], 0, 1).reshape(D, -1)
                else:
                    full = g[name].reshape(-1, D)
                self.full[(l, name)] = full
            return self.full[(l, name)]

        def gather_rider(self, l, names):
            return "gather", [shards[n][l] for n in names]

        def gathered(self, l, names, outs):
            self.got.setdefault(l, {}).update(zip(names, outs))

        def scatter_rider(self, name, l, g):
            if name == "w_in_perm":
                slabs = shard_cols(jnp.concatenate([g[:, :NQ], g[:, PR:PR + H], g[:, NQ:PR]], axis=1))
            elif name == "w_up":
                slabs = g
            else:
                slabs = shard_rows(g)
            return "scatter", [slabs]

        def scattered(self, name, l, out):
            self.stage[(name, l)] = out

    comm = MeshComm()
    loss_loc, dx, gr = _local_step(x[0], loss_target[0], ada, mix_norm_g, comm, b_forget, conf_w_full, conf_dw_b,
                                   conf_ln_g, conf_ln_b, sc_w_full, ffn_norm_g, ffn_w_full, ffn_dw_b, final_norm_g)
    loss = lax.psum(loss_loc, ("x", "y", "c"))

    small_names = ["ada", "mix_norm_g", "ffn_norm_g", "b_forget", "conf_dw_b", "conf_ln_g", "conf_ln_b",
                   "ffn_dw_b", "conf_dw_w", "sc_dw_w", "ffn_dw_w"]
    pk, lay = _pack([jnp.stack(gr[n]) for n in small_names] + [gr["final_norm_g"]])
    parts = _all_gather(pk, True, name="ag_small_bwd")
    tot = _unpack(_sum_slabs(parts, name="sum_small"), lay)
    g_small = dict(zip(small_names + ["final_norm_g"], tot))
    d_ada_all = _unpack(parts, lay, lead=(N_DEV,))[0]
    my_cols = lambda a, n: lax.dynamic_slice_in_dim(a, me * n, n, axis=a.ndim - 1)

    c_act_t = jnp.zeros((D, LANES), F32).at[:, :N_DEV].set(jnp.transpose(c_act)).astype(BF16)
    res = None
    for l in range(L):
        d_loc = jnp.zeros((LANES, ada_loc), F32).at[:N_DEV].set(my_cols(d_ada_all[:, l], ada_loc)).astype(BF16)
        g_l = _matmul(c_act_t, d_loc, F32, name="mm_dada")
        res = _adam_sum(g_l[None], ada_w, m_ada_w, v_ada_w, l, res, name="adam_ada_w")
    out_ada_w = res

    big = {}
    for nm, key, wq, mq, vq in (("w_down", "w_down", w_down, m_w_down, v_w_down), ("w_up", "w_up", w_up, m_w_up, v_w_up),
                                ("w_out", "w_out", w_out, m_w_out, v_w_out), ("w_in", "w_in_perm", w_in, m_w_in, v_w_in)):
        res = None
        for l in reversed(range(L)):
            res = _adam_sum(comm.stage[(key, l)], wq, mq, vq, l, res, name="adam_" + nm)
        big[nm] = res

    K31, K3 = conf_dw_w.shape[1], sc_dw_w.shape[1]
    sm = [("ada_b", ada_b, m_ada_b, v_ada_b, g_small["ada"]),
          ("mix_norm_g", mix_norm_g, m_mix_norm_g, v_mix_norm_g, g_small["mix_norm_g"]),
          ("b_forget", b_forget, m_b_forget, v_b_forget, g_small["b_forget"]),
          ("conf_dw_w", conf_dw_w, m_conf_dw_w, v_conf_dw_w, my_cols(g_small["conf_dw_w"], conf_dw_w.shape[2])),
          ("conf_dw_b", conf_dw_b, m_conf_dw_b, v_conf_dw_b, g_small["conf_dw_b"]),
          ("conf_ln_g", conf_ln_g, m_conf_ln_g, v_conf_ln_g, g_small["conf_ln_g"]),
          ("conf_ln_b", conf_ln_b, m_conf_ln_b, v_conf_ln_b, g_small["conf_ln_b"]),
          ("sc_dw_w", sc_dw_w, m_sc_dw_w, v_sc_dw_w, my_cols(g_small["sc_dw_w"], sc_dw_w.shape[2])),
          ("ffn_norm_g", ffn_norm_g, m_ffn_norm_g, v_ffn_norm_g, g_small["ffn_norm_g"]),
          ("ffn_dw_w", ffn_dw_w, m_ffn_dw_w, v_ffn_dw_w, my_cols(g_small["ffn_dw_w"], ffn_dw_w.shape[2])),
          ("ffn_dw_b", ffn_dw_b, m_ffn_dw_b, v_ffn_dw_b, g_small["ffn_dw_b"]),
          ("final_norm_g", final_norm_g, m_final_norm_g, v_final_norm_g, g_small["final_norm_g"])]
    pw, lay = _pack([t[1] for t in sm])
    pm, _ = _pack([t[2] for t in sm])
    pv, _ = _pack([t[3] for t in sm])
    pg, _ = _pack([t[4] for t in sm])
    sres = _adam_sum(pg[None], pw[None], pm[None], pv[None], 0, None, name="adam_small")
    s_g, s_d, s_m, s_v = [dict(zip([t[0] for t in sm], _unpack(r[0], lay))) for r in sres]

    def pick(idx, name):
        if name == "ada_w":
            return out_ada_w[idx]
        if name in big:
            return big[name][idx]
        return (s_g, s_d, s_m, s_v)[idx][name]

    order = ["ada_w", "ada_b", "mix_norm_g", "w_in", "b_forget", "conf_dw_w", "conf_dw_b", "conf_ln_g", "conf_ln_b",
             "sc_dw_w", "w_out", "ffn_norm_g", "w_up", "ffn_dw_w", "ffn_dw_b", "w_down", "final_norm_g"]
    outs = [loss, dx[None]]
    for idx in range(4):
        outs += [pick(idx, n) for n in order]
    return tuple(outs)
```

```python
import functools

import jax
import jax.numpy as jnp
from jax import lax
from jax.experimental import pallas as pl
from jax.experimental.pallas import tpu as pltpu

F32 = jnp.float32
BF16 = jnp.bfloat16
RMS_EPS = 1e-6
LN_EPS = 1e-5
HEAD_DIM = 128
N_ADA = 6
ADAM_LR = 0.001
ADAM_B1 = 0.9
ADAM_B2 = 0.999
ADAM_EPS = 1e-08
ADAM_WD = 0.01
ADAM_STEP = 10
N_DEV = 8
LANES = 128
VMEM_LIMIT_BYTES = 56 * 1024 * 1024
MM_TILE = 1024
MM_TILE_WIDE = 1536
MM_TILE_N_MAX = 2816
MXU_WIDTH = 256
MM_VMEM_BUDGET = 48 * 1024 * 1024
MESH = pl.DeviceIdType.MESH
PEER_FLIPS = ((0, 0, 1), (1, 0, 0), (0, 1, 0), (1, 1, 0), (1, 0, 1), (0, 1, 1), (1, 1, 1))


def _params(*sem):
    return pltpu.CompilerParams(dimension_semantics=sem, vmem_limit_bytes=VMEM_LIMIT_BYTES)


def _tile(n, cap):
    if n <= cap:
        return n
    for t in range(cap - cap % LANES, 0, -LANES):
        if n % t == 0:
            return t
    raise ValueError(f"no tile for {n}")


def _mm_tile(n):
    t = _tile(n, MM_TILE)
    return t if t == min(n, MM_TILE) else _tile(n, MM_TILE_WIDE)


def _n_tile(n, vmem_bytes):
    for step in (MXU_WIDTH, LANES):
        for t in range(min(n, MM_TILE_N_MAX) // step * step, 0, -step):
            if n % t == 0 and vmem_bytes(t) <= MM_VMEM_BUDGET:
                return t
    return n


def _sigmoid(v):
    return jax.nn.sigmoid(v)


def _matmul(a, b, out_dtype, name, rider=None, b_transposed=False):
    a_parts = a if isinstance(a, tuple) else (a,)
    na = len(a_parts)
    M = a_parts[0].shape[0]
    K = sum(p.shape[1] for p in a_parts)
    N = b.shape[0] if b_transposed else b.shape[1]
    tm = _mm_tile(M)
    out_bytes = jnp.dtype(out_dtype).itemsize
    part_k = a_parts[0].shape[1]

    def blocks_bytes(tk_, tn_):
        return 4 * (na * tm * tk_ + tk_ * tn_) + (4 * tm * tn_ if K > tk_ else 0) + 2 * tm * tn_ * out_bytes

    for tk in [t for t in range(part_k, 0, -LANES) if part_k % t == 0]:
        tn = _n_tile(N, functools.partial(blocks_bytes, tk))
        if blocks_bytes(tk, tn) <= MM_VMEM_BUDGET and tn >= min(N, MM_TILE if b_transposed else MM_TILE // 2):
            break
    nk = K // tk
    half = part_k // tk
    grid = (M // tm, N // tn, nk)
    dims = _NT if b_transposed else (((1,), (0,)), ((), ()))
    kind, arrs = rider if rider is not None else (None, [])
    nr = len(arrs)

    def body(*refs):
        a_refs, refs = refs[:na], refs[na - 1:]
        a_ref, b_ref = a_refs[0], refs[1]
        r_in = refs[2:2 + nr]
        o_ref = refs[2 + nr]
        r_out = refs[3 + nr:3 + 2 * nr]
        rest = refs[3 + 2 * nr:]
        i, j, k = pl.program_id(0), pl.program_id(1), pl.program_id(2)
        if nr:
            sems = rest[-3:]

            @pl.when((i == 0) & (j == 0) & (k == 0))
            def _():
                for r in range(nr):
                    _rider_start(kind, r_in[r], r_out[r], sems, r)

            if _late_step(grid[0]):
                @pl.when((i == _late_step(grid[0])) & (j == 0) & (k == 0))
                def _():
                    for r in range(nr):
                        _rider_forward(kind, r_in[r], r_out[r], sems, r)

        if nk == 1:
            o_ref[...] = lax.dot_general(a_ref[...], b_ref[...], dims, preferred_element_type=F32).astype(o_ref.dtype)
        else:
            acc_ref = rest[0]

            @pl.when(k == 0)
            def _():
                acc_ref[...] = jnp.zeros_like(acc_ref)

            def accumulate(part_ref):
                acc_ref[...] += lax.dot_general(part_ref[...], b_ref[...], dims, preferred_element_type=F32)

            if na == 1:
                accumulate(a_ref)
            else:
                pl.when(k < half)(lambda: accumulate(a_refs[0]))
                pl.when(k >= half)(lambda: accumulate(a_refs[1]))

            @pl.when(k == nk - 1)
            def _():
                o_ref[...] = acc_ref[...].astype(o_ref.dtype)

        if nr:
            @pl.when((i == grid[0] - 1) & (j == grid[1] - 1) & (k == nk - 1))
            def _():
                for r in range(nr):
                    _rider_finish(kind, r_in[r], r_out[r], sems, r, forwarded=bool(_late_step(grid[0])))

    scratch = [] if nk == 1 else [pltpu.VMEM((tm, tn), F32)]
    hbm = pl.BlockSpec(memory_space=pl.ANY)
    if na == 1:
        a_specs = [pl.BlockSpec((tm, tk), lambda i, j, k: (i, k))]
    else:
        a_specs = [pl.BlockSpec((tm, tk), lambda i, j, k: (i, jnp.minimum(k, half - 1))),
                   pl.BlockSpec((tm, tk), lambda i, j, k: (i, jnp.maximum(k - half, 0)))]
    out_shape = jax.ShapeDtypeStruct((M, N), out_dtype)
    out_specs = pl.BlockSpec((tm, tn), lambda i, j, k: (i, j))
    if nr:
        scratch += [pltpu.SemaphoreType.DMA((7 * nr,)), pltpu.SemaphoreType.DMA((7 * nr,)),
                    pltpu.SemaphoreType.DMA((nr,))]
        out_shape = (out_shape,) + tuple(
            jax.ShapeDtypeStruct(x.shape if kind == "scatter" else (N_DEV,) + x.shape, x.dtype) for x in arrs)
        out_specs = (out_specs,) + (hbm,) * nr
    out = pl.pallas_call(
        body, name=name,
        out_shape=out_shape,
        grid=grid,
        in_specs=a_specs + [pl.BlockSpec((tn, tk), lambda i, j, k: (j, k)) if b_transposed
                            else pl.BlockSpec((tk, tn), lambda i, j, k: (k, j))] + [hbm] * nr,
        out_specs=out_specs,
        scratch_shapes=scratch,
        compiler_params=_params(*(("arbitrary",) * 3 if nr else ("parallel", "parallel", "arbitrary"))),
    )(*a_parts, b, *arrs)
    return (out[0], list(out[1:])) if nr else out


_TN = (((0,), (0,)), ((), ()))


def _matmul_tn(a, b, out_dtype, name, col_slabs=None):
    b_parts = b if isinstance(b, tuple) else (b,)
    S, M = a.shape
    N = sum(p.shape[1] for p in b_parts)
    tm = _mm_tile(M)
    out_bytes = jnp.dtype(out_dtype).itemsize

    def blocks_bytes(ts_, tn_):
        return 4 * (ts_ * tm + len(b_parts) * ts_ * tn_) + 4 * tm * tn_ + 2 * tm * tn_ * out_bytes

    for ts in (_tile(S, 2 * MM_TILE), _tile(S, MM_TILE)):
        tn = N // col_slabs if col_slabs else _n_tile(b_parts[0].shape[1], functools.partial(blocks_bytes, ts))
        if blocks_bytes(ts, tn) <= MM_VMEM_BUDGET and tn >= min(b_parts[0].shape[1], MM_TILE):
            break
    ns = S // ts
    half = b_parts[0].shape[1] // tn

    def body(*refs):
        a_ref, b_refs, (o_ref, acc_ref) = refs[0], refs[1:-2], refs[-2:]
        j, k = pl.program_id(1), pl.program_id(2)

        @pl.when(k == 0)
        def _():
            acc_ref[...] = jnp.zeros_like(acc_ref)

        def accumulate(b_ref):
            acc_ref[...] += lax.dot_general(a_ref[...], b_ref[...], _TN, preferred_element_type=F32)

        if len(b_refs) == 1:
            accumulate(b_refs[0])
        else:
            pl.when(j < half)(lambda: accumulate(b_refs[0]))
            pl.when(j >= half)(lambda: accumulate(b_refs[1]))

        @pl.when(k == ns - 1)
        def _():
            if col_slabs is None:
                o_ref[...] = acc_ref[...].astype(o_ref.dtype)
            else:
                o_ref[0] = acc_ref[...].astype(o_ref.dtype)

    if col_slabs is None:
        out_shape = jax.ShapeDtypeStruct((M, N), out_dtype)
        out_spec = pl.BlockSpec((tm, tn), lambda i, j, k: (i, j))
    else:
        out_shape = jax.ShapeDtypeStruct((col_slabs, M, tn), out_dtype)
        out_spec = pl.BlockSpec((1, tm, tn), lambda i, j, k: (j, i, 0))
    if len(b_parts) == 1:
        b_specs = [pl.BlockSpec((ts, tn), lambda i, j, k: (k, j))]
    else:
        b_specs = [pl.BlockSpec((ts, tn), lambda i, j, k: (jnp.where(j < half, k, ns - 1), jnp.minimum(j, half - 1))),
                   pl.BlockSpec((ts, tn), lambda i, j, k: (jnp.where(j < half, 0, k), jnp.maximum(j - half, 0)))]
    return pl.pallas_call(
        body, name=name,
        out_shape=out_shape,
        grid=(M // tm, N // tn, ns),
        in_specs=[pl.BlockSpec((ts, tm), lambda i, j, k: (k, i))] + b_specs,
        out_specs=out_spec,
        scratch_shapes=[pltpu.VMEM((tm, tn), F32)],
        compiler_params=_params("parallel", "parallel", "arbitrary"),
    )(a, *b_parts)


def _site_fwd(x, delta, gate, g, sc, sh, name):
    S, D = x.shape
    T = min(256, S)
    res = delta is not None

    def body(*refs):
        if res:
            x_ref, d_ref, gate_ref, g_ref, sc_ref, sh_ref, xo_ref, h_ref = refs
            xv = x_ref[...] + gate_ref[...] * d_ref[...]
            xo_ref[...] = xv
        else:
            x_ref, g_ref, sc_ref, sh_ref, h_ref = refs
            xv = x_ref[...]
        r = lax.rsqrt(jnp.mean(xv * xv, axis=-1, keepdims=True) + RMS_EPS)
        a = g_ref[...] * (1.0 + sc_ref[...])
        h_ref[...] = (xv * r * a + sh_ref[...]).astype(BF16)

    row = pl.BlockSpec((T, D), lambda i: (i, 0))
    vec = pl.BlockSpec((1, D), lambda i: (0, 0))
    if res:
        ins, in_specs = (x, delta, gate, g, sc, sh), [row, row, vec, vec, vec, vec]
        out_shape = (jax.ShapeDtypeStruct((S, D), F32), jax.ShapeDtypeStruct((S, D), BF16))
        out_specs = (row, row)
    else:
        ins, in_specs = (x, g, sc, sh), [row, vec, vec, vec]
        out_shape = jax.ShapeDtypeStruct((S, D), BF16)
        out_specs = row
    out = pl.pallas_call(body, name=name, out_shape=out_shape, grid=(S // T,), in_specs=in_specs,
                         out_specs=out_specs, compiler_params=_params("parallel"))(*ins)
    return out if res else (x, out)


def _site_bwd(x, dh, dres, g, sc, sh, delta, gate, name):
    S, D = x.shape
    T = min(256, S)
    res = delta is not None

    def body(*refs):
        if res:
            (x_ref, dh_ref, dres_ref, g_ref, sc_ref, delta_ref, gate_ref,
             dx_ref, dsh_ref, da_ref, dd_ref, dgate_ref) = refs
        else:
            x_ref, dh_ref, dres_ref, g_ref, sc_ref, dx_ref, dsh_ref, da_ref = refs
        i = pl.program_id(0)
        xv = x_ref[...]
        dhv = dh_ref[...]
        r = lax.rsqrt(jnp.mean(xv * xv, axis=-1, keepdims=True) + RMS_EPS)
        xh = xv * r
        dxh = dhv * (g_ref[...] * (1.0 + sc_ref[...]))
        dx = r * (dxh - xh * jnp.mean(dxh * xh, axis=-1, keepdims=True)) + dres_ref[...]
        dx_ref[...] = dx

        @pl.when(i == 0)
        def _():
            dsh_ref[...] = jnp.zeros_like(dsh_ref)
            da_ref[...] = jnp.zeros_like(da_ref)
            if res:
                dgate_ref[...] = jnp.zeros_like(dgate_ref)

        dsh_ref[...] += jnp.sum(dhv, axis=0, keepdims=True)
        da_ref[...] += jnp.sum(dhv * xh, axis=0, keepdims=True)
        if res:
            dd_ref[...] = (gate_ref[...] * dx).astype(BF16)
            dgate_ref[...] += jnp.sum(dx * delta_ref[...], axis=0, keepdims=True)

    row = pl.BlockSpec((T, D), lambda i: (i, 0))
    vec = pl.BlockSpec((1, D), lambda i: (0, 0))
    vshape = jax.ShapeDtypeStruct((1, D), F32)
    if res:
        ins, in_specs = (x, dh, dres, g, sc, delta, gate), [row, row, row, vec, vec, row, vec]
        out_shape = (jax.ShapeDtypeStruct((S, D), F32), vshape, vshape, jax.ShapeDtypeStruct((S, D), BF16), vshape)
        out_specs = (row, vec, vec, row, vec)
    else:
        ins, in_specs = (x, dh, dres, g, sc), [row, row, row, vec, vec]
        out_shape = (jax.ShapeDtypeStruct((S, D), F32), vshape, vshape)
        out_specs = (row, vec, vec)
    return pl.pallas_call(body, name=name, out_shape=out_shape, grid=(S // T,), in_specs=in_specs,
                          out_specs=out_specs, compiler_params=_params("arbitrary"))(*ins)


def _final_fwd_bwd(x, delta, gate, gfin, target, name):
    S, D = x.shape
    T = min(256, S)

    def body(x_ref, delta_ref, gate_ref, g_ref, t_ref, loss_ref, dx_ref, dd_ref, dgate_ref, dg_ref):
        i = pl.program_id(0)
        dl = delta_ref[...]
        xv = x_ref[...] + gate_ref[...] * dl
        r = lax.rsqrt(jnp.mean(xv * xv, axis=-1, keepdims=True) + RMS_EPS)
        xh = xv * r
        gv = g_ref[...]
        e = xh * gv - t_ref[...]
        dy = e * (1.0 / D)
        dxh = dy * gv
        dx = r * (dxh - xh * jnp.mean(dxh * xh, axis=-1, keepdims=True))
        dx_ref[...] = dx
        dd_ref[...] = (gate_ref[...] * dx).astype(BF16)

        @pl.when(i == 0)
        def _():
            loss_ref[...] = jnp.zeros_like(loss_ref)
            dgate_ref[...] = jnp.zeros_like(dgate_ref)
            dg_ref[...] = jnp.zeros_like(dg_ref)

        loss_ref[...] += jnp.sum(e * e, axis=0, keepdims=True)
        dgate_ref[...] += jnp.sum(dx * dl, axis=0, keepdims=True)
        dg_ref[...] += jnp.sum(dy * xh, axis=0, keepdims=True)

    row = pl.BlockSpec((T, D), lambda i: (i, 0))
    vec = pl.BlockSpec((1, D), lambda i: (0, 0))
    vshape = jax.ShapeDtypeStruct((1, D), F32)
    return pl.pallas_call(
        body, name=name,
        out_shape=(vshape, jax.ShapeDtypeStruct((S, D), F32), jax.ShapeDtypeStruct((S, D), BF16), vshape, vshape),
        grid=(S // T,), in_specs=[row, row, vec, vec, row], out_specs=(vec, row, row, vec, vec),
        compiler_params=_params("arbitrary"))(x, delta, gate, gfin, target)


def _split3(v):
    hi = v.astype(BF16)
    r1 = v - hi.astype(F32)
    mid = r1.astype(BF16)
    lo = (r1 - mid.astype(F32)).astype(BF16)
    return hi, mid, lo


def _tri_dot(tri, v):
    hi, mid, lo = _split3(v)
    d = functools.partial(jnp.dot, preferred_element_type=F32)
    return d(tri, hi) + d(tri, mid) + d(tri, lo)


def _fgate_fwd(rest, bpad, fblk, name):
    S = rest.shape[0]
    CH = min(256, S)
    nch = S // CH

    def body(f_ref, b_ref, o_ref):
        row = lax.broadcasted_iota(jnp.int32, (CH, CH), 0)
        col = lax.broadcasted_iota(jnp.int32, (CH, CH), 1)
        tri = (row >= col).astype(BF16)

        def step(ci, carry):
            rows = pl.ds(pl.multiple_of(ci * CH, CH), CH)
            z = f_ref[rows, :] + b_ref[...]
            lf = jnp.minimum(z, 0.0) - jnp.log(1.0 + jnp.exp(-jnp.abs(z)))
            o_ref[rows, :] = _tri_dot(tri, lf) + carry
            return carry + jnp.sum(lf, axis=0, keepdims=True)

        lax.fori_loop(0, nch, step, jnp.zeros((1, LANES), F32))

    return pl.pallas_call(
        body, name=name, out_shape=jax.ShapeDtypeStruct((S, LANES), F32), grid=(1,),
        in_specs=[pl.BlockSpec((S, LANES), lambda i: (0, fblk)), pl.BlockSpec((1, LANES), lambda i: (0, 0))],
        out_specs=pl.BlockSpec((S, LANES), lambda i: (0, 0)),
        compiler_params=_params("arbitrary"))(rest, bpad)


def _fgate_bwd(rest, bpad, dF, fblk, name):
    S = rest.shape[0]
    CH = min(256, S)
    nch = S // CH

    def body(f_ref, b_ref, df_ref, o_ref, db_ref):
        row = lax.broadcasted_iota(jnp.int32, (CH, CH), 0)
        col = lax.broadcasted_iota(jnp.int32, (CH, CH), 1)
        tri = (col >= row).astype(BF16)

        def step(n, carry):
            sfx_carry, db = carry
            ci = nch - 1 - n
            rows = pl.ds(pl.multiple_of(ci * CH, CH), CH)
            z = f_ref[rows, :] + b_ref[...]
            dfv = df_ref[rows, :]
            dz = (_tri_dot(tri, dfv) + sfx_carry) * _sigmoid(-z)
            o_ref[rows, :] = dz.astype(BF16)
            return sfx_carry + jnp.sum(dfv, axis=0, keepdims=True), db + jnp.sum(dz, axis=0, keepdims=True)

        zero = jnp.zeros((1, LANES), F32)
        _, db = lax.fori_loop(0, nch, step, (zero, zero))
        db_ref[...] = db

    blk = pl.BlockSpec((S, LANES), lambda i: (0, 0))
    return pl.pallas_call(
        body, name=name,
        out_shape=(jax.ShapeDtypeStruct((S, LANES), BF16), jax.ShapeDtypeStruct((1, LANES), F32)), grid=(1,),
        in_specs=[pl.BlockSpec((S, LANES), lambda i: (0, fblk)), pl.BlockSpec((1, LANES), lambda i: (0, 0)), blk],
        out_specs=(blk, pl.BlockSpec((1, LANES), lambda i: (0, 0))),
        compiler_params=_params("arbitrary"))(rest, bpad, dF)


_NT = (((1,), (1,)), ((), ()))
LOG2E = 1.4426950408889634
ATTN_TILE = 512


def _blocked_rows(a, TA):
    H, S = a.shape
    return a.reshape(H, S // TA, 1, TA)


def _attn_fwd(qkv, nfb, H, mix_cols, name, rider=None):
    S = qkv.shape[0]
    TA = min(ATTN_TILE, S)
    nb = S // TA
    c = HEAD_DIM ** -0.5 * LOG2E

    def body(q_ref, k_ref, v_ref, nf_ref, o_ref, o32_ref, lse_ref, m_ref, l_ref, acc_ref):
        i = pl.program_id(1)
        m_ref[...] = jnp.full_like(m_ref, -jnp.inf)
        l_ref[...] = jnp.zeros_like(l_ref)
        acc_ref[...] = jnp.zeros_like(acc_ref)

        def block(j, masked):
            rows = pl.ds(pl.multiple_of(j * TA, TA), TA)
            st = (lax.dot_general(k_ref[rows, :], q_ref[...], _NT, preferred_element_type=F32) * c
                  + jnp.tile(nf_ref[0, rows, :], (1, TA // LANES)))
            if masked:
                key = lax.broadcasted_iota(jnp.int32, (TA, TA), 0)
                qry = lax.broadcasted_iota(jnp.int32, (TA, TA), 1)
                st = jnp.where(key <= qry, st, -jnp.inf)
            m_old = m_ref[...]
            m_new = jnp.maximum(m_old, jnp.max(st, axis=0, keepdims=True))
            alpha = jnp.exp2(m_old - m_new)
            pt = jnp.exp2(st - m_new)
            l_ref[...] = alpha * l_ref[...] + jnp.sum(pt, axis=0, keepdims=True)
            acc_ref[...] = alpha * acc_ref[...] + lax.dot_general(v_ref[rows, :], pt.astype(BF16), _TN,
                                                                  preferred_element_type=F32)
            m_ref[...] = m_new

        def loop(jj, carry):
            block(2 * jj, False)
            block(2 * jj + 1, False)
            return carry

        lax.fori_loop(0, i // 2, loop, 0)

        @pl.when(i % 2 == 1)
        def _():
            block(i - 1, False)

        block(i, True)
        o = jnp.transpose(acc_ref[...] / l_ref[...])
        o32_ref[...] = o
        o_ref[...] = o.astype(BF16)
        lse_ref[0, 0] = m_ref[...] + jnp.log(l_ref[...]) * LOG2E

    qblk = pl.BlockSpec((TA, HEAD_DIM), lambda h, i: (i, h))
    call = dict(
        body=body, n_in=4, n_out=3, grid=(H, nb),
        in_specs=[qblk,
                  pl.BlockSpec((S, HEAD_DIM), lambda h, i: (0, H + h)),
                  pl.BlockSpec((S, HEAD_DIM), lambda h, i: (0, 2 * H + h)),
                  pl.BlockSpec((1, S, LANES), lambda h, i: (h, 0, 0))],
        out_specs=[qblk, qblk, pl.BlockSpec((1, 1, 1, TA), lambda h, i: (h, i, 0, 0))],
        scratch_shapes=[pltpu.VMEM((1, TA), F32), pltpu.VMEM((1, TA), F32), pltpu.VMEM((HEAD_DIM, TA), F32)],
        out_shape=[jax.ShapeDtypeStruct((S, mix_cols), BF16), jax.ShapeDtypeStruct((S, H * HEAD_DIM), F32),
                   jax.ShapeDtypeStruct((H, nb, 1, TA), F32)])
    return _call(call, name, ("parallel", "parallel"), rider, [qkv, qkv, qkv, nfb])


def _attn_delta(o, do, H, name):
    S = o.shape[0]
    T = min(512, S)

    def body(o_ref, do_ref, d_ref):
        d_ref[0] = jnp.sum(o_ref[...].astype(F32) * do_ref[...].astype(F32), axis=-1, keepdims=True)

    blk = pl.BlockSpec((T, HEAD_DIM), lambda h, i: (i, h))
    return pl.pallas_call(
        body, name=name, out_shape=jax.ShapeDtypeStruct((H, S, 1), F32), grid=(H, S // T),
        in_specs=[blk, blk], out_specs=pl.BlockSpec((1, T, 1), lambda h, i: (h, i, 0)),
        compiler_params=_params("parallel", "parallel"))(o, do)


def _ds_tile(k, q, v, do, nfb, lse_row, delta_row, c, masked):
    TK, TQ = k.shape[0], q.shape[0]
    st = lax.dot_general(k, q, _NT, preferred_element_type=F32) * c + jnp.tile(nfb, (1, TQ // LANES))
    pt = jnp.exp2(st - lse_row)
    if masked:
        key = lax.broadcasted_iota(jnp.int32, (TK, TQ), 0)
        qry = lax.broadcasted_iota(jnp.int32, (TK, TQ), 1)
        pt = jnp.where(key <= qry, pt, 0.0)
    dpt = lax.dot_general(v, do, _NT, preferred_element_type=F32)
    return pt, pt * (dpt - delta_row)


def _attn_bwd(qkv, nfb, do, lse, delta, H, name, rider=None):
    S = qkv.shape[0]
    TA = min(ATTN_TILE, S)
    nb = S // TA
    scale = HEAD_DIM ** -0.5
    c = scale * LOG2E

    def body(q_ref, k_ref, v_ref, nf_ref, do_ref, lse_ref, dl_ref, dq_ref, dk_ref, dv_ref, drow_ref, dnf_ref,
             dq_acc, dk_acc, dv_acc, dnf_acc):
        j = pl.program_id(1)

        @pl.when(j == 0)
        def _():
            dq_acc[...] = jnp.zeros_like(dq_acc)
            drow_ref[...] = jnp.zeros_like(drow_ref)

        dk_acc[...] = jnp.zeros_like(dk_acc)
        dv_acc[...] = jnp.zeros_like(dv_acc)
        dnf_acc[...] = jnp.zeros_like(dnf_acc)
        kb = k_ref[...]

        def block(i, masked):
            rows = pl.ds(pl.multiple_of(i * TA, TA), TA)
            qb = q_ref[rows, :]
            dob = do_ref[rows, :]
            pt, dst = _ds_tile(kb, qb, v_ref[...], dob, nf_ref[0], lse_ref[0, i], dl_ref[0, i], c, masked)
            dsb = dst.astype(BF16)
            dv_acc[...] += jnp.dot(pt.astype(BF16), dob, preferred_element_type=F32)
            dk_acc[...] += jnp.dot(dsb, qb, preferred_element_type=F32)
            dq_acc[rows, :] += lax.dot_general(dsb, kb, _TN, preferred_element_type=F32)
            drow_ref[0, i] += jnp.sum(dst, axis=0, keepdims=True)
            part = dst[:, 0:LANES]
            for t in range(1, TA // LANES):
                part = part + dst[:, t * LANES:(t + 1) * LANES]
            dnf_acc[...] += part

        def loop(ii, carry):
            block(j + 1 + 2 * ii, False)
            block(j + 2 + 2 * ii, False)
            return carry

        block(j, True)
        rest = nb - 1 - j
        lax.fori_loop(0, rest // 2, loop, 0)

        @pl.when(rest % 2 == 1)
        def _():
            block(nb - 1, False)

        dk_ref[...] = (dk_acc[...] * scale).astype(BF16)
        dv_ref[...] = dv_acc[...].astype(BF16)
        dnf_ref[0] = jnp.sum(dnf_acc[...], axis=-1, keepdims=True)

        @pl.when(j == nb - 1)
        def _():
            dq_ref[...] = (dq_acc[...] * scale).astype(BF16)

    full = pl.BlockSpec((S, HEAD_DIM), lambda h, j: (0, h))
    row_stat = pl.BlockSpec((1, nb, 1, TA), lambda h, j: (h, 0, 0, 0))
    kblk = lambda c0: pl.BlockSpec((TA, HEAD_DIM), lambda h, j: (j, c0 + h))
    shp = jax.ShapeDtypeStruct((S, H * HEAD_DIM), BF16)
    call = dict(
        body=body, n_in=7, n_out=5, grid=(H, nb),
        in_specs=[full, kblk(H), kblk(2 * H), pl.BlockSpec((1, TA, LANES), lambda h, j: (h, j, 0)), full,
                  row_stat, row_stat],
        out_specs=[full, kblk(0), kblk(0), row_stat, pl.BlockSpec((1, TA, 1), lambda h, j: (h, j, 0))],
        scratch_shapes=[pltpu.VMEM((S, HEAD_DIM), F32), pltpu.VMEM((TA, HEAD_DIM), F32),
                        pltpu.VMEM((TA, HEAD_DIM), F32), pltpu.VMEM((TA, LANES), F32)],
        out_shape=[shp, shp, shp, jax.ShapeDtypeStruct((H, nb, 1, TA), F32), jax.ShapeDtypeStruct((H, S, 1), F32)])
    return _call(call, name, ("parallel", "arbitrary"), rider, [qkv, qkv, qkv, nfb, do, lse, delta])


def _taps(ext_ref, w_ref, K, base, r0, rows, cols, reverse=False, init=None):
    acc = init
    for k in range(K):
        wk = w_ref[(K - 1 - k) if reverse else k:((K - 1 - k) if reverse else k) + 1, cols]
        term = wk * ext_ref[base + k + r0:base + k + r0 + rows, cols]
        acc = term if acc is None else acc + term
    return acc


def _prev_blk(T, H):
    return lambda i: jnp.maximum(i * (T // H) - 1, 0)


def _next_blk(T, H, S):
    return lambda i: jnp.minimum((i + 1) * (T // H), S // H - 1)


def _conf_fwd(rest, w, b, lng, lnb, mix, name):
    S = rest.shape[0]
    K, C = w.shape
    H, T = 32, min(256, S)
    RS = min(64, T)
    base = H - (K - 1)

    def body(cv_ref, cg_ref, cvp_ref, cgp_ref, w_ref, b_ref, g_ref, bb_ref, mix_ref, o_ref, cc_ref, ext_ref):
        i = pl.program_id(0)
        ext_ref[0:H, :] = jnp.where(i > 0, cvp_ref[...] * _sigmoid(cgp_ref[...]), 0.0)
        ext_ref[H:H + T, :] = cv_ref[...] * _sigmoid(cg_ref[...])
        for r0 in range(0, T, RS):
            cc = _taps(ext_ref, w_ref, K, base, r0, RS, slice(None), init=jnp.broadcast_to(b_ref[...], (RS, C)))
            cc_ref[r0:r0 + RS, :] = cc
            xc = cc - jnp.mean(cc, axis=-1, keepdims=True)
            y = xc * lax.rsqrt(jnp.mean(xc * xc, axis=-1, keepdims=True) + LN_EPS) * g_ref[...] + bb_ref[...]
            o_ref[r0:r0 + RS, :] = (y * _sigmoid(y)).astype(BF16)

    pb = _prev_blk(T, H)
    cur = lambda cb: pl.BlockSpec((T, C), lambda i: (i, cb))
    prev = lambda cb: pl.BlockSpec((H, C), lambda i: (pb(i), cb))
    full = lambda a: pl.BlockSpec(a.shape, lambda i: (0, 0))
    col_blk = (mix.shape[1] - 2 * C) // C
    return pl.pallas_call(
        body, name=name, grid=(S // T,),
        out_shape=(jax.ShapeDtypeStruct(mix.shape, BF16), jax.ShapeDtypeStruct((S, C), F32)),
        in_specs=[cur(0), cur(1), prev(0), prev(1), full(w), full(b), full(lng), full(lnb),
                  pl.BlockSpec(memory_space=pl.ANY)],
        out_specs=(pl.BlockSpec((T, C), lambda i: (i, col_blk)), pl.BlockSpec((T, C), lambda i: (i, 0))),
        input_output_aliases={8: 0},
        scratch_shapes=[pltpu.VMEM((H + T, C), F32)],
        compiler_params=_params("parallel"))(rest, rest, rest, rest, w, b, lng, lnb, mix)


def _conf_bwd(rest, cc_all, dcs, w, lng, lnb, name):
    S = rest.shape[0]
    K, C = w.shape
    H, T = 32, min(256, S)
    RS = 32
    nI = S // T
    base = H - (K - 1)

    def body(cv_ref, cg_ref, cvp_ref, cgp_ref, cc_ref, ccn_ref, do_ref, don_ref, w_ref, g_ref, bb_ref,
             dcvg_ref, dw_ref, dvec_ref, ext_ref, dcc_ref):
        i = pl.program_id(0)
        ext_ref[0:H, :] = jnp.where(i > 0, cvp_ref[...] * _sigmoid(cgp_ref[...]), 0.0)
        ext_ref[H:H + T, :] = cv_ref[...] * _sigmoid(cg_ref[...])

        @pl.when(i == 0)
        def _():
            dw_ref[...] = jnp.zeros_like(dw_ref)
            dvec_ref[...] = jnp.zeros_like(dvec_ref)

        db = jnp.zeros((1, C), F32)
        dg = jnp.zeros((1, C), F32)
        dbb = jnp.zeros((1, C), F32)
        for r0 in range(0, T + H, RS):
            cc = cc_ref[r0:r0 + RS, :] if r0 < T else ccn_ref[r0 - T:r0 - T + RS, :]
            xc = cc - jnp.mean(cc, axis=-1, keepdims=True)
            r = lax.rsqrt(jnp.mean(xc * xc, axis=-1, keepdims=True) + LN_EPS)
            xh = xc * r
            y = xh * g_ref[...] + bb_ref[...]
            sy = _sigmoid(y)
            if r0 < T:
                d_o = do_ref[r0:r0 + RS, :]
            else:
                d_o = jnp.where(i < nI - 1, don_ref[r0 - T:r0 - T + RS, :], 0.0)
            dy = d_o * (sy * (1.0 + y * (1.0 - sy)))
            dxh = dy * g_ref[...]
            dcc = r * (dxh - jnp.mean(dxh, axis=-1, keepdims=True)
                       - xh * jnp.mean(dxh * xh, axis=-1, keepdims=True))
            dcc_ref[r0:r0 + RS, :] = dcc
            if r0 < T:
                dbb = dbb + jnp.sum(dy, axis=0, keepdims=True)
                dg = dg + jnp.sum(dy * xh, axis=0, keepdims=True)
                db = db + jnp.sum(dcc, axis=0, keepdims=True)
        dvec_ref[0:1, :] += db
        dvec_ref[1:2, :] += dg
        dvec_ref[2:3, :] += dbb
        R2 = min(64, T)
        for k in range(K):
            s = jnp.zeros((1, C), F32)
            for r0 in range(0, T, R2):
                s = s + jnp.sum(dcc_ref[r0:r0 + R2, :] * ext_ref[base + k + r0:base + k + r0 + R2, :],
                                axis=0, keepdims=True)
            dw_ref[k:k + 1, :] += s
        for r0 in range(0, T, R2):
            dci = _taps(dcc_ref, w_ref, K, 0, r0, R2, slice(None), reverse=True)
            cvv = cv_ref[r0:r0 + R2, :]
            sg = _sigmoid(cg_ref[r0:r0 + R2, :])
            dcvg_ref[r0:r0 + R2, 0:C] = (dci * sg).astype(BF16)
            dcvg_ref[r0:r0 + R2, C:2 * C] = (dci * cvv * sg * (1.0 - sg)).astype(BF16)

    pb, nb_ = _prev_blk(T, H), _next_blk(T, H, S)
    cur = lambda cb: pl.BlockSpec((T, C), lambda i: (i, cb))
    prev = lambda cb: pl.BlockSpec((H, C), lambda i: (pb(i), cb))
    nxt = lambda cb: pl.BlockSpec((H, C), lambda i: (nb_(i), cb))
    full = lambda a: pl.BlockSpec(a.shape, lambda i: (0, 0))
    return pl.pallas_call(
        body, name=name,
        out_shape=(jax.ShapeDtypeStruct((S, 2 * C), BF16), jax.ShapeDtypeStruct((32, C), F32),
                   jax.ShapeDtypeStruct((8, C), F32)),
        grid=(nI,),
        in_specs=[cur(0), cur(1), prev(0), prev(1), cur(0), nxt(0), cur(0), nxt(0),
                  full(w), full(lng), full(lnb)],
        out_specs=(pl.BlockSpec((T, 2 * C), lambda i: (i, 0)), pl.BlockSpec((32, C), lambda i: (0, 0)),
                   pl.BlockSpec((8, C), lambda i: (0, 0))),
        scratch_shapes=[pltpu.VMEM((H + T, C), F32), pltpu.VMEM((T + H, C), F32)],
        compiler_params=_params("arbitrary"))(rest, rest, rest, rest, cc_all, cc_all, dcs, dcs, w, lng, lnb)


def _sconv_fwd(rest, w, mix, name):
    S = rest.shape[0]
    K, C = w.shape
    H, T = 8, min(256, S)
    RS = min(64, T)
    base = H - (K - 1)

    def body(sx_ref, sb_ref, sc_ref, sxp_ref, scp_ref, w_ref, mix_ref, o_ref, ext_ref):
        i = pl.program_id(0)
        ext_ref[0:H, :] = jnp.where(i > 0, sxp_ref[...] * scp_ref[...], 0.0)
        ext_ref[H:H + T, :] = sx_ref[...] * sc_ref[...]
        for r0 in range(0, T, RS):
            cz = _taps(ext_ref, w_ref, K, base, r0, RS, slice(None))
            o_ref[r0:r0 + RS, :] = (sb_ref[r0:r0 + RS, :] * cz).astype(BF16)

    pb = _prev_blk(T, H)
    cur = lambda cb: pl.BlockSpec((T, C), lambda i: (i, cb))
    prev = lambda cb: pl.BlockSpec((H, C), lambda i: (pb(i), cb))
    col_blk = (mix.shape[1] - C) // C
    return pl.pallas_call(
        body, name=name, out_shape=jax.ShapeDtypeStruct(mix.shape, BF16), grid=(S // T,),
        in_specs=[cur(2), cur(3), cur(4), prev(2), prev(4), pl.BlockSpec(w.shape, lambda i: (0, 0)),
                  pl.BlockSpec(memory_space=pl.ANY)],
        out_specs=pl.BlockSpec((T, C), lambda i: (i, col_blk)), input_output_aliases={6: 0},
        scratch_shapes=[pltpu.VMEM((H + T, C), F32)],
        compiler_params=_params("parallel"))(rest, rest, rest, rest, rest, w, mix)


def _sconv_bwd(rest, dcs, w, name):
    S = rest.shape[0]
    K, C = w.shape
    H, T = 8, min(256, S)
    RS = min(64, T)
    nI = S // T
    base = H - (K - 1)

    def body(sx_ref, sb_ref, sc_ref, sxp_ref, scp_ref, sbn_ref, do_ref, don_ref, w_ref,
             dout_ref, dw_ref, ext_ref, dcv_ref):
        i = pl.program_id(0)
        ext_ref[0:H, :] = jnp.where(i > 0, sxp_ref[...] * scp_ref[...], 0.0)
        ext_ref[H:H + T, :] = sx_ref[...] * sc_ref[...]
        dcv_ref[0:T, :] = do_ref[...] * sb_ref[...]
        dcv_ref[T:T + H, :] = jnp.where(i < nI - 1, don_ref[...] * sbn_ref[...], 0.0)

        @pl.when(i == 0)
        def _():
            dw_ref[...] = jnp.zeros_like(dw_ref)

        for k in range(K):
            s = jnp.zeros((1, C), F32)
            for r0 in range(0, T, RS):
                s = s + jnp.sum(dcv_ref[r0:r0 + RS, :] * ext_ref[base + k + r0:base + k + r0 + RS, :],
                                axis=0, keepdims=True)
            dw_ref[k:k + 1, :] += s
        for r0 in range(0, T, RS):
            cz = _taps(ext_ref, w_ref, K, base, r0, RS, slice(None))
            dz = _taps(dcv_ref, w_ref, K, 0, r0, RS, slice(None), reverse=True)
            dout_ref[r0:r0 + RS, 0:C] = (dz * sc_ref[r0:r0 + RS, :]).astype(BF16)
            dout_ref[r0:r0 + RS, C:2 * C] = (do_ref[r0:r0 + RS, :] * cz).astype(BF16)
            dout_ref[r0:r0 + RS, 2 * C:3 * C] = (dz * sx_ref[r0:r0 + RS, :]).astype(BF16)

    pb, nb_ = _prev_blk(T, H), _next_blk(T, H, S)
    cur = lambda cb: pl.BlockSpec((T, C), lambda i: (i, cb))
    prev = lambda cb: pl.BlockSpec((H, C), lambda i: (pb(i), cb))
    nxt = lambda cb: pl.BlockSpec((H, C), lambda i: (nb_(i), cb))
    return pl.pallas_call(
        body, name=name,
        out_shape=(jax.ShapeDtypeStruct((S, 3 * C), BF16), jax.ShapeDtypeStruct((8, C), F32)),
        grid=(nI,),
        in_specs=[cur(2), cur(3), cur(4), prev(2), prev(4), nxt(3), cur(1), nxt(1),
                  pl.BlockSpec(w.shape, lambda i: (0, 0))],
        out_specs=(pl.BlockSpec((T, 3 * C), lambda i: (i, 0)), pl.BlockSpec((8, C), lambda i: (0, 0))),
        scratch_shapes=[pltpu.VMEM((H + T, C), F32), pltpu.VMEM((T + H, C), F32)],
        compiler_params=_params("arbitrary"))(rest, rest, rest, rest, rest, rest, dcs, dcs, w)


FFN_ROWS = 256
FFN_HALO = 16


def _shift_mats(T):
    r = lax.broadcasted_iota(jnp.int32, (T, T), 0)
    c = lax.broadcasted_iota(jnp.int32, (T, T), 1)
    return jnp.stack([r == c + 1, r == c + 2, c == r + 1, c == r + 2]).astype(BF16)


def _edge_rows(strip, shift, first):
    sub = lax.broadcasted_iota(jnp.int32, strip.shape, 0)
    if first:
        return jnp.where(sub < shift, pltpu.roll(strip, shift, 0), 0.0)
    return jnp.where(sub >= 8 - shift, pltpu.roll(strip, 8 - shift, 0), 0.0)


def _conv3_tile(x_ref, prev_ref, w_ref, b_ref, down, has_prev, u_ref, xm_refs=None):
    T = x_ref.shape[0]
    x = x_ref[...]
    xm1 = jnp.dot(down[0], x, preferred_element_type=F32)
    xm2 = jnp.dot(down[1], x, preferred_element_type=F32)
    u_ref[...] = b_ref[...] + w_ref[0:1, :] * xm2 + w_ref[1:2, :] * xm1 + w_ref[2:3, :] * x.astype(F32)
    tail = jnp.where(has_prev, prev_ref[...].astype(F32)[FFN_HALO - 8:, :], 0.0)
    p1, p2 = _edge_rows(tail, 1, True), _edge_rows(tail, 2, True)
    u_ref[0:8, :] += w_ref[0:1, :] * p2 + w_ref[1:2, :] * p1
    if xm_refs is not None:
        xm_refs[0][...] = xm1
        xm_refs[1][...] = xm2
        xm_refs[0][0:8, :] += p1
        xm_refs[1][0:8, :] += p2


def _ffn_fwd(hu, w, b, name):
    S, F2 = hu.shape
    Fd = F2 // 2
    K = w.shape[0]
    assert K == 3
    T = min(FFN_ROWS, S)
    tc = _tile(Fd, 512)
    nJ = Fd // tc

    def body(sh_ref, g_ref, v_ref, gp_ref, vp_ref, wg_ref, wv_ref, bg_ref, bv_ref, o_ref, ug_ref, uv_ref):
        i = pl.program_id(1)
        down = (sh_ref[0], sh_ref[1])
        _conv3_tile(g_ref, gp_ref, wg_ref, bg_ref, down, i > 0, ug_ref)
        _conv3_tile(v_ref, vp_ref, wv_ref, bv_ref, down, i > 0, uv_ref)
        ug = ug_ref[...]
        o_ref[...] = (ug * _sigmoid(ug) * uv_ref[...]).astype(BF16)

    pb = _prev_blk(T, FFN_HALO)
    cur = lambda off: pl.BlockSpec((T, tc), lambda j, i: (i, j + off))
    prev = lambda off: pl.BlockSpec((FFN_HALO, tc), lambda j, i: (pb(i), j + off))
    wsp = lambda off: pl.BlockSpec((K, tc), lambda j, i: (0, j + off))
    bsp = lambda off: pl.BlockSpec((1, tc), lambda j, i: (0, j + off))
    return pl.pallas_call(
        body, name=name, out_shape=jax.ShapeDtypeStruct((S, Fd), BF16), grid=(nJ, S // T),
        in_specs=[pl.BlockSpec((4, T, T), lambda j, i: (0, 0, 0)),
                  cur(0), cur(nJ), prev(0), prev(nJ), wsp(0), wsp(nJ), bsp(0), bsp(nJ)],
        out_specs=pl.BlockSpec((T, tc), lambda j, i: (i, j)),
        scratch_shapes=[pltpu.VMEM((T, tc), F32), pltpu.VMEM((T, tc), F32)],
        compiler_params=_params("parallel", "parallel"))(_shift_mats(T), hu, hu, hu, hu, w, w, b, b)


def _ffn_bwd(hu, dact, w, b, name):
    S, F2 = hu.shape
    Fd = F2 // 2
    K = w.shape[0]
    assert K == 3
    T = min(FFN_ROWS, S)
    tc = _tile(Fd, 512)
    nJ = Fd // tc
    nI = S // T

    def body(sh_ref, g_ref, v_ref, gp_ref, vp_ref, gn_ref, vn_ref, da_ref, dan_ref, wg_ref, wv_ref, bg_ref, bv_ref,
             dg_ref, dv_ref, dwg_ref, dwv_ref, ug_ref, uv_ref, g1_ref, g2_ref, v1_ref, v2_ref, dh_ref):
        i = pl.program_id(1)
        down, up = (sh_ref[0], sh_ref[1]), (sh_ref[2], sh_ref[3])
        _conv3_tile(g_ref, gp_ref, wg_ref, bg_ref, down, i > 0, ug_ref, (g1_ref, g2_ref))
        _conv3_tile(v_ref, vp_ref, wv_ref, bv_ref, down, i > 0, uv_ref, (v1_ref, v2_ref))

        @pl.when(i == 0)
        def _():
            dwg_ref[...] = jnp.zeros_like(dwg_ref)
            dwv_ref[...] = jnp.zeros_like(dwv_ref)

        def d_u(ug, uv, d_a):
            sg = _sigmoid(ug)
            return d_a * uv * (sg * (1.0 + ug * (1.0 - sg))), d_a * (ug * sg)

        dug, duv = d_u(ug_ref[...], uv_ref[...], da_ref[...].astype(F32))

        def next_rows(x_ref, xn_ref, w_ref, b_ref):
            strip = jnp.concatenate([x_ref[T - FFN_HALO:, :].astype(F32)[FFN_HALO - 8:, :],
                                     xn_ref[...].astype(F32)[0:8, :]], axis=0)
            return (b_ref[...] + w_ref[0:1, :] * strip[6:14, :] + w_ref[1:2, :] * strip[7:15, :]
                    + w_ref[2:3, :] * strip[8:16, :])

        d_an = jnp.where(i < nI - 1, dan_ref[...].astype(F32)[0:8, :], 0.0)
        dug_n, duv_n = d_u(next_rows(g_ref, gn_ref, wg_ref, bg_ref), next_rows(v_ref, vn_ref, wv_ref, bv_ref), d_an)

        for du, du_n, x_ref, x1_ref, x2_ref, w_ref, dw_ref, out_ref in (
                (dug, dug_n, g_ref, g1_ref, g2_ref, wg_ref, dwg_ref, dg_ref),
                (duv, duv_n, v_ref, v1_ref, v2_ref, wv_ref, dwv_ref, dv_ref)):
            dw_ref[0:1, :] += jnp.sum(du * x2_ref[...], axis=0, keepdims=True)
            dw_ref[1:2, :] += jnp.sum(du * x1_ref[...], axis=0, keepdims=True)
            dw_ref[2:3, :] += jnp.sum(du * x_ref[...].astype(F32), axis=0, keepdims=True)
            dw_ref[3:4, :] += jnp.sum(du, axis=0, keepdims=True)
            dub = du.astype(BF16)
            dh_ref[...] = (w_ref[2:3, :] * du + w_ref[1:2, :] * jnp.dot(up[0], dub, preferred_element_type=F32)
                           + w_ref[0:1, :] * jnp.dot(up[1], dub, preferred_element_type=F32))
            nxt = du_n.astype(BF16).astype(F32)
            dh_ref[T - 8:, :] += w_ref[1:2, :] * _edge_rows(nxt, 1, False) + w_ref[0:1, :] * _edge_rows(nxt, 2, False)
            out_ref[...] = dh_ref[...].astype(BF16)

    pb, nbb = _prev_blk(T, FFN_HALO), _next_blk(T, FFN_HALO, S)
    cur = lambda off: pl.BlockSpec((T, tc), lambda j, i: (i, j + off))
    prev = lambda off: pl.BlockSpec((FFN_HALO, tc), lambda j, i: (pb(i), j + off))
    nxt = lambda off: pl.BlockSpec((FFN_HALO, tc), lambda j, i: (nbb(i), j + off))
    wsp = lambda off: pl.BlockSpec((K, tc), lambda j, i: (0, j + off))
    bsp = lambda off: pl.BlockSpec((1, tc), lambda j, i: (0, j + off))
    half = jax.ShapeDtypeStruct((S, Fd), BF16)
    dws = jax.ShapeDtypeStruct((8, Fd), F32)
    tile = pltpu.VMEM((T, tc), F32)
    return pl.pallas_call(
        body, name=name, out_shape=(half, half, dws, dws), grid=(nJ, nI),
        in_specs=[pl.BlockSpec((4, T, T), lambda j, i: (0, 0, 0)),
                  cur(0), cur(nJ), prev(0), prev(nJ), nxt(0), nxt(nJ), cur(0), nxt(0),
                  wsp(0), wsp(nJ), bsp(0), bsp(nJ)],
        out_specs=(pl.BlockSpec((T, tc), lambda j, i: (i, j)), pl.BlockSpec((T, tc), lambda j, i: (i, j)),
                   pl.BlockSpec((8, tc), lambda j, i: (0, j)), pl.BlockSpec((8, tc), lambda j, i: (0, j))),
        scratch_shapes=[tile] * 7,
        compiler_params=_params("parallel", "arbitrary"))(
            _shift_mats(T), hu, hu, hu, hu, hu, hu, dact, dact, w, w, b, b)


def _position():
    return lax.axis_index("x"), lax.axis_index("y"), lax.axis_index("c")


def _slot(px, py, pc):
    return 4 * px + 2 * py + pc


def _gather_copies(x_ref, out_ref, sems, r, starting=False):
    send_sems, recv_sems, local_sems = sems
    px, py, pc = _position()
    me, sibling = (px, py, pc), (px, py, 1 - pc)
    chips = [(1 - px, py), (px, 1 - py), (1 - px, 1 - py)]

    def copy(k, block, to, src=None):
        dst = out_ref.at[_slot(*block)]
        return pltpu.make_async_remote_copy(
            src_ref=dst if src is None else src, dst_ref=dst,
            send_sem=send_sems.at[7 * r + k], recv_sem=recv_sems.at[7 * r + k], device_id=to, device_id_type=MESH)

    mine = pltpu.make_async_copy(x_ref, out_ref.at[_slot(*me)], local_sems.at[r])
    first = [copy(0, me, sibling, src=x_ref)] + [copy(1 + n, me, (*chip, pc), src=x_ref)
                                                  for n, chip in enumerate(chips)]
    if starting:
        return mine, first
    passed = [copy(4 + n, (*chip, pc), sibling) for n, chip in enumerate(chips)]
    landed = [copy(1 + n, (*chip, pc), me) for n, chip in enumerate(chips)]
    from_sibling = [copy(0, sibling, me)] + [copy(4 + n, (*chip, 1 - pc), me) for n, chip in enumerate(chips)]
    return mine, first, passed, landed, from_sibling


def _scatter_copies(g_ref, out_ref, sems, r, starting=False):
    send_sems, recv_sems, local_sems = sems
    px, py, pc = _position()
    me = _slot(px, py, pc)
    mine = pltpu.make_async_copy(g_ref.at[me], out_ref.at[me], local_sems.at[r])
    peers = [(px ^ fx, py ^ fy, pc ^ fc) for fx, fy, fc in PEER_FLIPS]

    def copy(k, peer, src_slot, dst_slot):
        return pltpu.make_async_remote_copy(
            src_ref=g_ref.at[src_slot], dst_ref=out_ref.at[dst_slot],
            send_sem=send_sems.at[7 * r + k], recv_sem=recv_sems.at[7 * r + k], device_id=peer, device_id_type=MESH)

    sends = [copy(k, peer, _slot(*peer), me) for k, peer in enumerate(peers)]
    if starting:
        return mine, sends
    arrivals = [copy(k, peer, me, _slot(*peer)) for k, peer in enumerate(peers)]
    return mine, sends, arrivals


def _rider_start(kind, in_ref, out_ref, sems, r):
    if kind == "gather":
        mine, first = _gather_copies(in_ref, out_ref, sems, r, starting=True)
    else:
        mine, first = _scatter_copies(in_ref, out_ref, sems, r, starting=True)
    mine.start()
    for cp in first:
        cp.start()


def _rider_forward(kind, in_ref, out_ref, sems, r):
    if kind == "gather":
        px, py, pc = _position()
        send_sems, recv_sems, _ = sems
        for n, chip in enumerate([(1 - px, py), (px, 1 - py), (1 - px, 1 - py)]):
            rows = out_ref.at[_slot(*chip, pc)]
            copy = lambda k, to: pltpu.make_async_remote_copy(
                src_ref=rows, dst_ref=rows, send_sem=send_sems.at[7 * r + k], recv_sem=recv_sems.at[7 * r + k],
                device_id=to, device_id_type=MESH)
            copy(1 + n, (px, py, pc)).wait_recv()
            copy(4 + n, (px, py, 1 - pc)).start()


def _rider_finish(kind, in_ref, out_ref, sems, r, forwarded=False):
    if kind == "gather":
        mine, first, passed, landed, from_sibling = _gather_copies(in_ref, out_ref, sems, r)
        if not forwarded:
            for cp, fwd in zip(landed, passed):
                cp.wait_recv()
                fwd.start()
        for cp in from_sibling:
            cp.wait_recv()
        for cp in first + passed:
            cp.wait_send()
    else:
        mine, sends, arrivals = _scatter_copies(in_ref, out_ref, sems, r)
        for cp in arrivals:
            cp.wait_recv()
        for cp in sends:
            cp.wait_send()
    mine.wait()


def _late_step(n):
    return (3 * n) // 4 if n >= 4 else 0


def _ride(call, rider):
    kind, arrs = rider
    nr, n_in, n_out, grid, body = len(arrs), call["n_in"], call["n_out"], call["grid"], call["body"]

    def wrapped(*refs):
        ins, r_in = refs[:n_in], refs[n_in:n_in + nr]
        outs, r_out = refs[n_in + nr:n_in + nr + n_out], refs[n_in + nr + n_out:n_in + 2 * nr + n_out]
        scratch, sems = refs[n_in + 2 * nr + n_out:-3], refs[-3:]
        ids = [pl.program_id(a) for a in range(len(grid))]
        first = functools.reduce(lambda p, q: p & q, [i == 0 for i in ids])
        last = functools.reduce(lambda p, q: p & q, [i == n - 1 for i, n in zip(ids, grid)])
        late = functools.reduce(lambda p, q: p & q, [ids[0] == _late_step(grid[0])] + [i == 0 for i in ids[1:]])

        @pl.when(first)
        def _():
            for r in range(nr):
                _rider_start(kind, r_in[r], r_out[r], sems, r)

        if _late_step(grid[0]):
            @pl.when(late)
            def _():
                for r in range(nr):
                    _rider_forward(kind, r_in[r], r_out[r], sems, r)

        body(*ins, *outs, *scratch)

        @pl.when(last)
        def _():
            for r in range(nr):
                _rider_finish(kind, r_in[r], r_out[r], sems, r, forwarded=bool(_late_step(grid[0])))

    hbm = pl.BlockSpec(memory_space=pl.ANY)
    return dict(
        body=wrapped, grid=grid,
        in_specs=call["in_specs"] + [hbm] * nr,
        out_specs=tuple(call["out_specs"]) + (hbm,) * nr,
        out_shape=tuple(call["out_shape"]) + tuple(
            jax.ShapeDtypeStruct(x.shape if kind == "scatter" else (N_DEV,) + x.shape, x.dtype) for x in arrs),
        scratch_shapes=call["scratch_shapes"] + [pltpu.SemaphoreType.DMA((7 * nr,)), pltpu.SemaphoreType.DMA((7 * nr,)),
                                                 pltpu.SemaphoreType.DMA((nr,))])


def _call(call, name, semantics, rider, operands):
    if rider is not None:
        call, semantics, operands = _ride(call, rider), ("arbitrary",) * len(call["grid"]), operands + list(rider[1])
    out = pl.pallas_call(call["body"], name=name, grid=call["grid"], in_specs=call["in_specs"],
                         out_specs=tuple(call["out_specs"]), out_shape=tuple(call["out_shape"]),
                         scratch_shapes=call["scratch_shapes"], compiler_params=_params(*semantics))(*operands)
    if rider is None:
        return out, []
    n = len(out) - len(rider[1])
    return out[:n], list(out[n:])


def _all_gather(x, in_vmem, name):
    space = pltpu.VMEM if in_vmem else pl.ANY

    def body(x_ref, out_ref, send_sems, recv_sems, local_sems):
        sems = (send_sems, recv_sems, local_sems)
        _rider_start("gather", x_ref, out_ref, sems, 0)
        _rider_finish("gather", x_ref, out_ref, sems, 0)

    return pl.pallas_call(
        body, name=name, out_shape=jax.ShapeDtypeStruct((N_DEV,) + x.shape, x.dtype),
        in_specs=[pl.BlockSpec(memory_space=space)], out_specs=pl.BlockSpec(memory_space=space),
        scratch_shapes=[pltpu.SemaphoreType.DMA((7,)), pltpu.SemaphoreType.DMA((7,)), pltpu.SemaphoreType.DMA((1,))],
        compiler_params=pltpu.CompilerParams(vmem_limit_bytes=VMEM_LIMIT_BYTES),
    )(x)


def _adam_sum(stage, w, m, v, layer, prev, name):
    n = stage.shape[0]
    L, R, C = w.shape
    tr = R if R * C <= 256 * 1024 else _row_tile(R, C)
    c1 = 1.0 / (1.0 - ADAM_B1 ** ADAM_STEP)
    c2 = 1.0 / (1.0 - ADAM_B2 ** ADAM_STEP)

    def body(*refs):
        st_ref, w_ref, m_ref, v_ref = refs[:4]
        g_ref, d_ref, nm_ref, nv_ref = refs[-4:]
        g = st_ref[0].astype(F32)
        for s in range(1, n):
            g = g + st_ref[s].astype(F32)
        wv = w_ref[0]
        mn = ADAM_B1 * m_ref[0] + (1.0 - ADAM_B1) * g
        vn = ADAM_B2 * v_ref[0] + (1.0 - ADAM_B2) * (g * g)
        g_ref[0] = g
        nm_ref[0] = mn
        nv_ref[0] = vn
        d_ref[0] = -ADAM_LR * ((mn * c1) / (jnp.sqrt(vn * c2) + ADAM_EPS) + ADAM_WD * wv)

    lay = pl.BlockSpec((1, tr, C), lambda i: (layer, i, 0))
    in_specs = [pl.BlockSpec((n, tr, C), lambda i: (0, i, 0)), lay, lay, lay]
    ins = [stage, w, m, v]
    aliases = {}
    if prev is not None:
        in_specs += [pl.BlockSpec(memory_space=pl.ANY)] * 4
        ins += list(prev)
        aliases = {4: 0, 5: 1, 6: 2, 7: 3}
    shp = jax.ShapeDtypeStruct((L, R, C), F32)
    return pl.pallas_call(
        body, name=name, out_shape=(shp, shp, shp, shp), grid=(R // tr,),
        in_specs=in_specs, out_specs=(lay, lay, lay, lay), input_output_aliases=aliases,
        compiler_params=_params("parallel"))(*ins)


def _row_tile(R, C):
    cpad = -(-C // LANES) * LANES
    want = max(16, (256 * 1024) // cpad)
    best = 16
    for t in range(16, R + 1, 16):
        if R % t == 0 and t <= want:
            best = t
    return best


def _sum_slabs(st, name):
    n, R, C = st.shape
    tr = R if n * R * C * 4 <= (12 << 20) else _row_tile(R, C)

    def body(st_ref, o_ref):
        g = st_ref[0]
        for s in range(1, n):
            g = g + st_ref[s]
        o_ref[...] = g

    return pl.pallas_call(
        body, name=name, out_shape=jax.ShapeDtypeStruct((R, C), F32), grid=(R // tr,),
        in_specs=[pl.BlockSpec((n, tr, C), lambda i: (0, i, 0))], out_specs=pl.BlockSpec((tr, C), lambda i: (i, 0)),
        compiler_params=_params("parallel"))(st)


def _pack(arrs):
    flat = [a.reshape(-1).astype(F32) for a in arrs]
    sizes = [f.shape[0] for f in flat]
    total = sum(sizes)
    padded = -(-total // (16 * LANES)) * (16 * LANES)
    if padded > total:
        flat.append(jnp.zeros((padded - total,), F32))
    return jnp.concatenate(flat).reshape(padded // LANES, LANES), (sizes, [a.shape for a in arrs])


def _unpack(packed, layout, lead=()):
    sizes, shapes = layout
    flat = packed.reshape(lead + (-1,))
    out, off = [], 0
    for sz, shp in zip(sizes, shapes):
        out.append(flat[..., off:off + sz].reshape(lead + tuple(shp)))
        off += sz
    return out


class _NoComm:
    col_slabs = None

    def __init__(self, wts):
        self.wts, self.grads = wts, {}

    def weight(self, l, name):
        return self.wts[l][name]

    def gather_rider(self, l, names):
        return None

    def scatter_rider(self, name, l, g):
        self.grads[(name, l)] = g
        return None


def _local_step(x, tgt, ada, mix_norm_g, comm, b_forget, conf_dw_w, conf_dw_b, conf_ln_g, conf_ln_b, sc_dw_w,
                ffn_norm_g, ffn_dw_w, ffn_dw_b, final_norm_g):
    S, D = x.shape
    L = ada.shape[0]

    def mm_gather(a, b, dtype, name, l_next, names):
        rider = comm.gather_rider(l_next, names) if l_next < L else None
        if rider is None:
            return _matmul(a, b, dtype, name=name)
        out, got = _matmul(a, b, dtype, name="cm_" + name, rider=rider)
        comm.gathered(l_next, names, got)
        return out

    def mm_scatter(a, b, dtype, name, wname, l, g):
        rider = comm.scatter_rider(wname, l, g)
        if rider is None:
            return _matmul(a, b, dtype, name=name, b_transposed=True)
        out, got = _matmul(a, b, dtype, name="cm_" + name, rider=rider, b_transposed=True)
        comm.scattered(wname, l, got[0])
        return out

    H = b_forget.shape[1]
    DA = H * HEAD_DIM
    C = conf_dw_b.shape[1]
    NQ = 3 * DA
    NR = 5 * C + LANES
    fblk = (5 * C) // LANES
    row = lambda a: a.reshape(1, -1)
    adav = ada.reshape(L, N_ADA, 1, D)

    saved = []
    xcur, delta, gate = x, None, None
    for l in range(L):
        sh_m, sc_m, g_m, sh_f, sc_f, g_f = [adav[l, n] for n in range(N_ADA)]
        w = functools.partial(comm.weight, l)
        x1, h1 = _site_fwd(xcur, delta, gate, row(mix_norm_g[l]), sc_m, sh_m, name="site_fwd_mix")
        qkv = _matmul(h1, w("w_in_perm")[:, :NQ], BF16, name="mm_qkv")
        rest = _matmul(h1, w("w_in_perm")[:, NQ:], F32, name="mm_rest")
        bpad = jnp.zeros((1, LANES), F32).at[0, :H].set(b_forget[l])
        Fc = _fgate_fwd(rest, bpad, fblk, name="fgate_fwd")
        nf = -LOG2E * jnp.transpose(Fc[:, :H])
        g_l, g_names = (0, ("w_up", "w_down", "w_out")) if l == 0 else (l + 1, ("w_up", "w_down"))
        rider = comm.gather_rider(g_l, g_names) if g_l < L else None
        (mixcat, attn32, lse), got = _attn_fwd(qkv, jnp.broadcast_to(nf[:, :, None], (H, S, LANES)), H, DA + 2 * C,
                                               name="attn_fwd" if rider is None else "cm_attn_fwd", rider=rider)
        if rider is not None:
            comm.gathered(g_l, g_names, got)
        mixcat, conf_cc = _conf_fwd(rest, conf_dw_w[l], row(conf_dw_b[l]), row(conf_ln_g[l]), row(conf_ln_b[l]),
                                    mixcat, name="conf_fwd")
        mixcat = _sconv_fwd(rest, sc_dw_w[l], mixcat, name="sconv_fwd")
        mixed = _matmul(mixcat, w("w_out"), F32, name="mm_out")
        x2, h2 = _site_fwd(x1, mixed, g_m, row(ffn_norm_g[l]), sc_f, sh_f, name="site_fwd_ffn")
        hu = mm_gather(h2, w("w_up"), BF16, "mm_up", l + 1, ("w_up", "w_down") if l == 0 else ("w_in", "w_out"))
        act = _ffn_fwd(hu, ffn_dw_w[l], row(ffn_dw_b[l]), name="ffn_fwd")
        ffn_out = mm_gather(act, w("w_down"), F32, "mm_down", l + 1 if l == 0 else L, ("w_in", "w_out"))
        saved.append(dict(x1=x1, h1=h1, qkv=qkv, rest=rest, bpad=bpad, nf=nf, attn32=attn32, lse=lse, mixcat=mixcat,
                          mixed=mixed, x2=x2, h2=h2, hu=hu, act=act, ffn_out=ffn_out, conf_cc=conf_cc))
        xcur, delta, gate = x2, ffn_out, g_f

    loss_lanes, dx, d_delta, d_gate, d_gfin = _final_fwd_bwd(xcur, delta, gate, row(final_norm_g), tgt, name="final")
    loss = (0.5 / D) * jnp.sum(loss_lanes)

    grads = dict(final_norm_g=d_gfin[0], ada=[None] * L, mix_norm_g=[None] * L, ffn_norm_g=[None] * L,
                 b_forget=[None] * L, conf_dw_w=[None] * L, conf_dw_b=[None] * L, conf_ln_g=[None] * L,
                 conf_ln_b=[None] * L, sc_dw_w=[None] * L, ffn_dw_w=[None] * L, ffn_dw_b=[None] * L)
    K3 = ffn_dw_w.shape[1]
    pending = []
    for l in reversed(range(L)):
        sv, w = saved[l], functools.partial(comm.weight, l)
        sh_m, sc_m, g_m, sh_f, sc_f, g_f = [adav[l, n] for n in range(N_ADA)]
        d_gf = d_gate
        g_down = _matmul_tn(sv["act"], d_delta, BF16, name="mm_dw_down")
        dact = _matmul(d_delta, w("w_down"), BF16, name="mm_dact", b_transposed=True)
        dhu_g, dhu_v, dwg, dwv = _ffn_bwd(sv["hu"], dact, ffn_dw_w[l], row(ffn_dw_b[l]), name="ffn_bwd")
        grads["ffn_dw_w"][l] = jnp.concatenate([dwg[:K3], dwv[:K3]], axis=1)
        grads["ffn_dw_b"][l] = jnp.concatenate([dwg[K3], dwv[K3]])
        dhu = (dhu_g, dhu_v)
        g_up = _matmul_tn(sv["h2"], dhu, BF16, name="mm_dw_up", col_slabs=comm.col_slabs)
        dh2 = mm_scatter(dhu, w("w_up"), F32, "mm_dh2", "w_down", l, g_down)
        dx, d_sh_f, d_a_f, d_mixed, d_gm = _site_bwd(sv["x2"], dh2, dx, row(ffn_norm_g[l]), sc_f,
                                                      sh_f, sv["mixed"], g_m, name="site_bwd_ffn")
        grads["ffn_norm_g"][l] = (d_a_f * (1.0 + sc_f))[0]
        d_sc_f = d_a_f * row(ffn_norm_g[l])
        g_out = _matmul_tn(sv["mixcat"], d_mixed, BF16, name="mm_dw_out")
        dattn = _matmul(d_mixed, w("w_out")[:DA], BF16, name="mm_dattn", b_transposed=True)
        dcs = _matmul(d_mixed, w("w_out")[DA:], F32, name="mm_dcs", b_transposed=True)
        delta_a = _blocked_rows(_attn_delta(sv["attn32"], dattn, H, name="attn_delta")[:, :, 0], min(ATTN_TILE, S))
        nfb = jnp.broadcast_to(sv["nf"][:, :, None], (H, S, LANES))
        riding = [("w_up", l, g_up), ("w_out", l, g_out)] + pending
        parts = [comm.scatter_rider(*item) for item in riding]
        rider = None if parts[0] is None else ("scatter", [p[1][0] for p in parts])
        (dq, dk, dv, drow, dnf), got = _attn_bwd(sv["qkv"], nfb, dattn, sv["lse"], delta_a, H,
                                                 name="attn_bwd" if rider is None else "cm_attn_bwd", rider=rider)
        if rider is not None:
            for (wname, wl, _), out in zip(riding, got):
                comm.scattered(wname, wl, out)
        pending = []
        dF = jnp.zeros((S, LANES), F32).at[:, :H].set(jnp.transpose(drow.reshape(H, S) - dnf[:, :, 0]))
        dfl, dbf = _fgate_bwd(sv["rest"], sv["bpad"], dF, fblk, name="fgate_bwd")
        grads["b_forget"][l] = dbf[0, :H]
        dcvg, dcw, dcvec = _conf_bwd(sv["rest"], sv["conf_cc"], dcs, conf_dw_w[l], row(conf_ln_g[l]),
                                     row(conf_ln_b[l]), name="conf_bwd")
        grads["conf_dw_w"][l] = dcw[:conf_dw_w.shape[1]]
        grads["conf_dw_b"][l], grads["conf_ln_g"][l], grads["conf_ln_b"][l] = dcvec[0], dcvec[1], dcvec[2]
        dsc3, dsw = _sconv_bwd(sv["rest"], dcs, sc_dw_w[l], name="sconv_bwd")
        grads["sc_dw_w"][l] = dsw[:sc_dw_w.shape[1]]
        dproj = jnp.concatenate([dq, dk, dv, dcvg, dsc3, dfl], axis=1)
        g_in = _matmul_tn(sv["h1"], dproj, BF16, name="mm_dw_in")
        if l > 0:
            dh1 = _matmul(dproj, w("w_in_perm"), F32, name="mm_dh1", b_transposed=True)
            pending = [("w_in_perm", l, g_in)]
        else:
            dh1 = mm_scatter(dproj, w("w_in_perm"), F32, "mm_dh1", "w_in_perm", l, g_in)
        if l > 0:
            pv = saved[l - 1]
            g_f_prev = adav[l - 1, 5]
            dx, d_sh_m, d_a_m, d_delta, d_gate = _site_bwd(sv["x1"], dh1, dx, row(mix_norm_g[l]), sc_m, sh_m,
                                                           pv["ffn_out"], g_f_prev, name="site_bwd_mix")
        else:
            dx, d_sh_m, d_a_m = _site_bwd(sv["x1"], dh1, dx, row(mix_norm_g[l]), sc_m, sh_m, None, None,
                                          name="site_bwd_first")
        grads["mix_norm_g"][l] = (d_a_m * (1.0 + sc_m))[0]
        d_sc_m = d_a_m * row(mix_norm_g[l])
        grads["ada"][l] = jnp.concatenate([d_sh_m, d_sc_m, d_gm, d_sh_f, d_sc_f, d_gf], axis=1)[0]
    return loss, dx, grads


def kernel(x, c, ada_w, ada_b, mix_norm_g, w_in, b_forget, conf_dw_w, conf_dw_b, conf_ln_g, conf_ln_b, sc_dw_w, w_out, ffn_norm_g, w_up, ffn_dw_w, ffn_dw_b, w_down, final_norm_g, loss_target, m_ada_w, m_ada_b, m_mix_norm_g, m_w_in, m_b_forget, m_conf_dw_w, m_conf_dw_b, m_conf_ln_g, m_conf_ln_b, m_sc_dw_w, m_w_out, m_ffn_norm_g, m_w_up, m_ffn_dw_w, m_ffn_dw_b, m_w_down, m_final_norm_g, v_ada_w, v_ada_b, v_mix_norm_g, v_w_in, v_b_forget, v_conf_dw_w, v_conf_dw_b, v_conf_ln_g, v_conf_ln_b, v_sc_dw_w, v_w_out, v_ffn_norm_g, v_w_up, v_ffn_dw_w, v_ffn_dw_b, v_w_down, v_final_norm_g):
    L, D, ada_loc = ada_w.shape
    S = x.shape[1]
    H = b_forget.shape[1]
    DA = H * HEAD_DIM
    C = conf_dw_b.shape[1]
    in_loc = w_in.shape[2]
    IN = in_loc * N_DEV
    px, py, pc = _position()
    me = _slot(px, py, pc)

    pk, lay = _pack([c, conf_dw_w, sc_dw_w, ffn_dw_w])
    gathered = _all_gather(pk, True, name="ag_small_fwd")
    c_all, cw_all, sw_all, fw_all = _unpack(gathered, lay, lead=(N_DEV,))
    c_all = c_all[:, 0]
    unshard = lambda a: jnp.moveaxis(a, 0, 2).reshape(a.shape[1], a.shape[2], -1)
    conf_w_full, sc_w_full, ffn_w_full = unshard(cw_all), unshard(sw_all), unshard(fw_all)
    c_act = c_all * jax.nn.sigmoid(c_all)
    c_act16 = jnp.zeros((16, D), F32).at[:N_DEV].set(c_act).astype(BF16)
    ada_cols = jnp.stack([_matmul(c_act16, ada_w[l].astype(BF16), F32, name="mm_ada")[:N_DEV] for l in range(L)])
    ada_g = _all_gather(ada_cols.reshape(L * N_DEV, ada_loc), True, name="ag_ada")
    ada_mine = lax.dynamic_index_in_dim(ada_g.reshape(N_DEV, L, N_DEV, ada_loc), me, axis=2, keepdims=False)
    ada = jnp.moveaxis(ada_mine, 0, 1).reshape(L, N_DEV * ada_loc) + ada_b

    NQ = 3 * DA
    PR = NQ + 5 * C
    shards = dict(w_in=w_in.astype(BF16), w_out=w_out.astype(BF16), w_up=w_up.astype(BF16),
                  w_down=w_down.astype(BF16))

    def shard_cols(g):
        return jnp.moveaxis(g.reshape(g.shape[0], N_DEV, -1), 1, 0)

    def shard_rows(g):
        return g.reshape(N_DEV, -1, g.shape[1])

    class MeshComm:
        col_slabs = N_DEV

        def __init__(self):
            self.got = {0: {"w_in": _all_gather(shards["w_in"][0], False, name="ag_w_in")}}
            self.full, self.stage = {}, {}

        def weight(self, l, name):
            if (l, name) not in self.full:
                g = self.got[l]
                if name == "w_in_perm":
                    wi = jnp.moveaxis(g["w_in"], 0, 1).reshape(D, IN)
                    full = jnp.concatenate([wi[:, :NQ], wi[:, NQ + H:], wi[:, NQ:NQ + H],
                                            jnp.zeros((D, LANES - H), BF16)], axis=1)
                elif name == "w_up":
                    full = jnp.moveaxis(g["w_up"], 0, 1).reshape(D, -1)
                else:
                    full = g[name].reshape(-1, D)
                self.full[(l, name)] = full
            return self.full[(l, name)]

        def gather_rider(self, l, names):
            return "gather", [shards[n][l] for n in names]

        def gathered(self, l, names, outs):
            self.got.setdefault(l, {}).update(zip(names, outs))

        def scatter_rider(self, name, l, g):
            if name == "w_in_perm":
                slabs = shard_cols(jnp.concatenate([g[:, :NQ], g[:, PR:PR + H], g[:, NQ:PR]], axis=1))
            elif name == "w_up":
                slabs = g
            else:
                slabs = shard_rows(g)
            return "scatter", [slabs]

        def scattered(self, name, l, out):
            self.stage[(name, l)] = out

    comm = MeshComm()
    loss_loc, dx, gr = _local_step(x[0], loss_target[0], ada, mix_norm_g, comm, b_forget, conf_w_full, conf_dw_b,
                                   conf_ln_g, conf_ln_b, sc_w_full, ffn_norm_g, ffn_w_full, ffn_dw_b, final_norm_g)
    loss = lax.psum(loss_loc, ("x", "y", "c"))

    small_names = ["ada", "mix_norm_g", "ffn_norm_g", "b_forget", "conf_dw_b", "conf_ln_g", "conf_ln_b",
                   "ffn_dw_b", "conf_dw_w", "sc_dw_w", "ffn_dw_w"]
    pk, lay = _pack([jnp.stack(gr[n]) for n in small_names] + [gr["final_norm_g"]])
    parts = _all_gather(pk, True, name="ag_small_bwd")
    tot = _unpack(_sum_slabs(parts, name="sum_small"), lay)
    g_small = dict(zip(small_names + ["final_norm_g"], tot))
    d_ada_all = _unpack(parts, lay, lead=(N_DEV,))[0]
    my_cols = lambda a, n: lax.dynamic_slice_in_dim(a, me * n, n, axis=a.ndim - 1)

    c_act_t = jnp.zeros((D, LANES), F32).at[:, :N_DEV].set(jnp.transpose(c_act)).astype(BF16)
    res = None
    for l in range(L):
        d_loc = jnp.zeros((LANES, ada_loc), F32).at[:N_DEV].set(my_cols(d_ada_all[:, l], ada_loc)).astype(BF16)
        g_l = _matmul(c_act_t, d_loc, F32, name="mm_dada")
        res = _adam_sum(g_l[None], ada_w, m_ada_w, v_ada_w, l, res, name="adam_ada_w")
    out_ada_w = res

    big = {}
    for nm, key, wq, mq, vq in (("w_down", "w_down", w_down, m_w_down, v_w_down), ("w_up", "w_up", w_up, m_w_up, v_w_up),
                                ("w_out", "w_out", w_out, m_w_out, v_w_out), ("w_in", "w_in_perm", w_in, m_w_in, v_w_in)):
        res = None
        for l in reversed(range(L)):
            res = _adam_sum(comm.stage[(key, l)], wq, mq, vq, l, res, name="adam_" + nm)
        big[nm] = res

    K31, K3 = conf_dw_w.shape[1], sc_dw_w.shape[1]
    sm = [("ada_b", ada_b, m_ada_b, v_ada_b, g_small["ada"]),
          ("mix_norm_g", mix_norm_g, m_mix_norm_g, v_mix_norm_g, g_small["mix_norm_g"]),
          ("b_forget", b_forget, m_b_forget, v_b_forget, g_small["b_forget"]),
          ("conf_dw_w", conf_dw_w, m_conf_dw_w, v_conf_dw_w, my_cols(g_small["conf_dw_w"], conf_dw_w.shape[2])),
          ("conf_dw_b", conf_dw_b, m_conf_dw_b, v_conf_dw_b, g_small["conf_dw_b"]),
          ("conf_ln_g", conf_ln_g, m_conf_ln_g, v_conf_ln_g, g_small["conf_ln_g"]),
          ("conf_ln_b", conf_ln_b, m_conf_ln_b, v_conf_ln_b, g_small["conf_ln_b"]),
          ("sc_dw_w", sc_dw_w, m_sc_dw_w, v_sc_dw_w, my_cols(g_small["sc_dw_w"], sc_dw_w.shape[2])),
          ("ffn_norm_g", ffn_norm_g, m_ffn_norm_g, v_ffn_norm_g, g_small["ffn_norm_g"]),
          ("ffn_dw_w", ffn_dw_w, m_ffn_dw_w, v_ffn_dw_w, my_cols(g_small["ffn_dw_w"], ffn_dw_w.shape[2])),
          ("ffn_dw_b", ffn_dw_b, m_ffn_dw_b, v_ffn_dw_b, g_small["ffn_dw_b"]),
          ("final_norm_g", final_norm_g, m_final_norm_g, v_final_norm_g, g_small["final_norm_g"])]
    pw, lay = _pack([t[1] for t in sm])
    pm, _ = _pack([t[2] for t in sm])
    pv, _ = _pack([t[3] for t in sm])
    pg, _ = _pack([t[4] for t in sm])
    sres = _adam_sum(pg[None], pw[None], pm[None], pv[None], 0, None, name="adam_small")
    s_g, s_d, s_m, s_v = [dict(zip([t[0] for t in sm], _unpack(r[0], lay))) for r in sres]

    def pick(idx, name):
        if name == "ada_w":
            return out_ada_w[idx]
        if name in big:
            return big[name][idx]
        return (s_g, s_d, s_m, s_v)[idx][name]

    order = ["ada_w", "ada_b", "mix_norm_g", "w_in", "b_forget", "conf_dw_w", "conf_dw_b", "conf_ln_g", "conf_ln_b",
             "sc_dw_w", "w_out", "ffn_norm_g", "w_up", "ffn_dw_w", "ffn_dw_b", "w_down", "final_norm_g"]
    outs = [loss, dx[None]]
    for idx in range(4):
        outs += [pick(idx, n) for n in order]
    return tuple(outs)
```
